```python
import jax
import jax.numpy as jnp
from jax import lax
import numpy as np

D_MODEL = 1024
BATCH = 8
SEQ = 2048
DEPTH = 1
DEC_BATCH = 128
DEC_SEQ = 1
PAST_LEN = 16384
PAGE_SIZE = 128

CONV_CH = 512
CONV_WIDTH = 31
N_HEADS = 4
HEAD_DIM = 128
DN_WIDTH = N_HEADS * HEAD_DIM
SHORT_CONV = 4
CHUNK = 64
N_EXPERTS = 256
TOP_K = 8
N_GROUPS = 8
TOPK_GROUPS = 4
D_EXPERT = 256
D_SHARED = 256
ROUTE_SCALE = 2.5
MOE_BLOCK = 64
EPS = 1e-6
IN_SIZES = (2 * CONV_CH, DN_WIDTH, DN_WIDTH, DN_WIDTH, DN_WIDTH, N_HEADS, N_HEADS, D_MODEL, D_MODEL)
IN_WIDTH = 2 * CONV_CH + 4 * DN_WIDTH + 2 * N_HEADS + 2 * D_MODEL

kernel_name = 'hybrid_conformer_gdn_moe_adaln_step'


def rms_norm(x, g):
    xf = x.astype(jnp.float32)
    y = xf * lax.rsqrt(jnp.mean(xf * xf, axis=-1, keepdims=True) + EPS)
    return (y * g.astype(jnp.float32)).astype(x.dtype)


def layer_norm(x, g, b):
    xf = x.astype(jnp.float32)
    mu = jnp.mean(xf, axis=-1, keepdims=True)
    xc = xf - mu
    var = jnp.mean(xc * xc, axis=-1, keepdims=True)
    y = xc * lax.rsqrt(var + EPS) * g.astype(jnp.float32) + b.astype(jnp.float32)
    return y.astype(x.dtype)


def l2_normalize(x):
    return x * lax.rsqrt(jnp.sum(x * x, axis=-1, keepdims=True) + 1e-6)


def causal_depthwise(xp, w):
    return lax.conv_general_dilated(
        xp, w[:, None, :].astype(xp.dtype), window_strides=(1,), padding='VALID',
        dimension_numbers=('NWC', 'WIO', 'NWC'), feature_group_count=xp.shape[-1])


def gated_delta_rule(q, k, v, log_alpha, beta, s0):
    bsz, seq_len, n_h, dk = q.shape
    dv = v.shape[-1]
    n_chunks = -(-seq_len // CHUNK)
    pad = n_chunks * CHUNK - seq_len

    def to_chunks(t):
        t = jnp.pad(t, [(0, 0), (0, pad)] + [(0, 0)] * (t.ndim - 2))
        t = t.reshape((bsz, n_chunks, CHUNK) + t.shape[2:])
        return jnp.moveaxis(t, 3, 1)

    qc, kc, vc, gc, bc = (to_chunks(t) for t in (q, k, v, log_alpha, beta))
    g_cum = jnp.cumsum(gc, axis=-1)
    idx = jnp.arange(CHUNK)
    incl = idx[:, None] >= idx[None, :]
    strict = idx[:, None] > idx[None, :]
    decay = jnp.exp(jnp.where(incl, g_cum[..., :, None] - g_cum[..., None, :], -jnp.inf))
    kb = kc * bc[..., None]
    vb = vc * bc[..., None]
    l_mat = jnp.where(strict, jnp.einsum('bhnid,bhnjd->bhnij', kb, kc) * decay, 0.0)
    a_mat = l_mat + jnp.eye(CHUNK, dtype=jnp.float32)
    rhs = jnp.concatenate([vb, kb * jnp.exp(g_cum)[..., None]], axis=-1)
    sol = lax.linalg.triangular_solve(a_mat, rhs, left_side=True, lower=True, unit_diagonal=True)
    u0, w = sol[..., :dv], sol[..., dv:]
    a_qk = jnp.einsum('bhnid,bhnjd->bhnij', qc, kc) * decay
    q_dec = qc * jnp.exp(g_cum)[..., None]
    k_dec = kc * jnp.exp(g_cum[..., -1:] - g_cum)[..., None]
    g_tot = jnp.exp(g_cum[..., -1])
    xs = tuple(jnp.moveaxis(t, 2, 0) for t in (u0, w, a_qk, q_dec, k_dec, g_tot))

    def step(s, inp):
        u0_n, w_n, aqk_n, qd_n, kd_n, gt_n = inp
        u = u0_n - jnp.einsum('bhck,bhkv->bhcv', w_n, s)
        o = jnp.einsum('bhck,bhkv->bhcv', qd_n, s) + jnp.einsum('bhcj,bhjv->bhcv', aqk_n, u)
        s = gt_n[..., None, None] * s + jnp.einsum('bhck,bhcv->bhkv', kd_n, u)
        return s, o

    s_fin, o = lax.scan(step, s0, xs)
    o = jnp.moveaxis(o, 0, 2).reshape(bsz, n_h, n_chunks * CHUNK, dv)[:, :, :seq_len]
    return jnp.transpose(o, (0, 2, 1, 3)), s_fin


def token_mixer(h, conv_buf, sc_buf, s0, w_in, conv_dw, conv_dw_b, conv_ln_g, conv_ln_b, w_pa,
                short_conv_w, a_log, dt_bias, dn_norm_g, w_pb, w_o):
    bsz, seq_len, _ = h.shape
    split_points = np.cumsum(IN_SIZES)[:-1].tolist()
    glu, q, k, v, z, a, b, gate_a, gate_b = jnp.split(h @ w_in, split_points, axis=-1)

    u = glu[..., :CONV_CH] * jax.nn.sigmoid(glu[..., CONV_CH:])
    u_hist = jnp.concatenate([conv_buf.astype(u.dtype), u], axis=1)
    y_a = causal_depthwise(u_hist, conv_dw) + conv_dw_b
    y_a = jax.nn.silu(layer_norm(y_a, conv_ln_g, conv_ln_b)) @ w_pa
    new_conv_buf = u_hist[:, u_hist.shape[1] - (CONV_WIDTH - 1):]

    qkv = jnp.concatenate([q, k, v], axis=-1)
    qkv_hist = jnp.concatenate([sc_buf.astype(qkv.dtype), qkv], axis=1)
    qkv_c = jax.nn.silu(causal_depthwise(qkv_hist, short_conv_w)).astype(jnp.float32)
    new_sc_buf = qkv_hist[:, qkv_hist.shape[1] - (SHORT_CONV - 1):]

    def heads(t):
        return t.reshape(bsz, seq_len, N_HEADS, HEAD_DIM)

    qh = l2_normalize(heads(qkv_c[..., :DN_WIDTH])) * (HEAD_DIM ** -0.5)
    kh = l2_normalize(heads(qkv_c[..., DN_WIDTH:2 * DN_WIDTH]))
    vh = heads(qkv_c[..., 2 * DN_WIDTH:])
    log_alpha = -jnp.exp(a_log.astype(jnp.float32)) * jax.nn.softplus(a.astype(jnp.float32) + dt_bias.astype(jnp.float32))
    beta = jax.nn.sigmoid(b.astype(jnp.float32))
    o, s_new = gated_delta_rule(qh, kh, vh, log_alpha, beta, s0.astype(jnp.float32))
    o = rms_norm(o, dn_norm_g) * jax.nn.silu(heads(z).astype(jnp.float32))
    y_b = o.reshape(bsz, seq_len, DN_WIDTH).astype(h.dtype) @ w_pb

    merged = jax.nn.sigmoid(gate_a) * y_a + jax.nn.sigmoid(gate_b) * y_b
    return merged @ w_o, new_conv_buf, new_sc_buf, s_new


def routed_experts(t, e_idx, e_w, w_gate, w_up, w_down):
    n_tok, dim = t.shape
    m = n_tok * TOP_K
    flat_e = e_idx.reshape(-1)
    order = jnp.argsort(flat_e)
    sorted_e = flat_e[order]
    counts = jnp.bincount(flat_e, length=N_EXPERTS)
    padded = (counts + MOE_BLOCK - 1) // MOE_BLOCK * MOE_BLOCK
    pad_end = jnp.cumsum(padded)
    pad_start = pad_end - padded
    cnt_start = jnp.cumsum(counts) - counts
    dest = pad_start[sorted_e] + jnp.arange(m) - cnt_start[sorted_e]
    n_blocks = -(-m // MOE_BLOCK) + N_EXPERTS
    n_rows = n_blocks * MOE_BLOCK
    row_tok = jnp.full((n_rows,), n_tok, jnp.int32).at[dest].set((order // TOP_K).astype(jnp.int32))
    row_w = jnp.zeros((n_rows,), t.dtype).at[dest].set(e_w.reshape(-1)[order])
    block_exp = jnp.minimum(jnp.searchsorted(pad_end, jnp.arange(n_blocks) * MOE_BLOCK, side='right'), N_EXPERTS - 1)
    t_pad = jnp.concatenate([t, jnp.zeros((1, dim), t.dtype)], axis=0)

    def block(args):
        rt, rw, e = args
        xb = t_pad[rt]
        hb = jax.nn.silu(xb @ w_gate[e]) * (xb @ w_up[e])
        return (hb @ w_down[e]) * rw[:, None]

    yb = lax.map(block, (row_tok.reshape(n_blocks, MOE_BLOCK), row_w.reshape(n_blocks, MOE_BLOCK), block_exp))
    out = jax.ops.segment_sum(yb.reshape(n_rows, dim), row_tok, num_segments=n_tok + 1)
    return out[:n_tok]


def moe(h, w_router, e_bias, w_gate, w_up, w_down, ws_gate, ws_up, ws_down):
    bsz, seq_len, dim = h.shape
    t = h.reshape(-1, dim)
    n_tok = t.shape[0]
    scores = jax.nn.sigmoid((t @ w_router).astype(jnp.float32))
    biased = scores + e_bias.astype(jnp.float32)
    grp = biased.reshape(n_tok, N_GROUPS, N_EXPERTS // N_GROUPS)
    grp_score = jnp.sum(lax.top_k(grp, 2)[0], axis=-1)
    _, g_idx = lax.top_k(grp_score, TOPK_GROUPS)
    g_mask = jnp.any(g_idx[..., None] == jnp.arange(N_GROUPS), axis=-2)
    e_mask = jnp.repeat(g_mask, N_EXPERTS // N_GROUPS, axis=1)
    _, e_idx = lax.top_k(jnp.where(e_mask, biased, -jnp.inf), TOP_K)
    e_w = jnp.take_along_axis(scores, e_idx, axis=-1)
    e_w = e_w / jnp.sum(e_w, axis=-1, keepdims=True) * ROUTE_SCALE
    routed = routed_experts(t, e_idx, e_w.astype(t.dtype), w_gate, w_up, w_down)
    shared = (jax.nn.silu(t @ ws_gate) * (t @ ws_up)) @ ws_down
    return (routed + shared).reshape(bsz, seq_len, dim)


def layer(x, c, conv_buf, sc_buf, s0, w_ada, b_ada, g_mix, w_in, conv_dw, conv_dw_b, conv_ln_g, conv_ln_b,
          w_pa, short_conv_w, a_log, dt_bias, dn_norm_g, w_pb, w_o, g_ffn, w_router, e_bias,
          w_gate, w_up, w_down, ws_gate, ws_up, ws_down):
    ada = (jax.nn.silu(c) @ w_ada + b_ada).reshape(c.shape[0], 6, 1, D_MODEL)
    shift1, scale1, gate1 = ada[:, 0], ada[:, 1], ada[:, 2]
    shift2, scale2, gate2 = ada[:, 3], ada[:, 4], ada[:, 5]
    h = rms_norm(x, g_mix) * (1.0 + scale1) + shift1
    mix, new_conv_buf, new_sc_buf, s_new = token_mixer(
        h, conv_buf, sc_buf, s0, w_in, conv_dw, conv_dw_b, conv_ln_g, conv_ln_b, w_pa,
        short_conv_w, a_log, dt_bias, dn_norm_g, w_pb, w_o)
    x = x + gate1 * mix
    h = rms_norm(x, g_ffn) * (1.0 + scale2) + shift2
    x = x + gate2 * moe(h, w_router, e_bias, w_gate, w_up, w_down, ws_gate, ws_up, ws_down)
    return x, new_conv_buf, new_sc_buf, s_new


def setup_inputs(seed: int = 0) -> dict:
    key = jax.random.key(seed)
    ks = jax.random.split(key, 40)
    f32 = jnp.float32

    def nrm(i, shape, scale):
        return jax.random.normal(ks[i], shape, f32) * scale

    def gain(i, shape):
        return 1.0 + 0.02 * jax.random.normal(ks[i], shape, f32)

    dt = jnp.exp(jax.random.uniform(ks[20], (DEPTH, N_HEADS), f32, minval=np.log(1e-3), maxval=np.log(1e-1)))
    return {
        'x_prompt': nrm(0, (BATCH, SEQ, D_MODEL), 1.0),
        'x_sample': nrm(1, (DEC_BATCH, DEC_SEQ, D_MODEL), 1.0),
        'c_prompt': nrm(2, (BATCH, D_MODEL), 1.0),
        'c_sample': nrm(3, (DEC_BATCH, D_MODEL), 1.0),
        'state_conv': nrm(4, (DEPTH, DEC_BATCH, CONV_WIDTH - 1, CONV_CH), 0.5),
        'state_short_conv': nrm(5, (DEPTH, DEC_BATCH, SHORT_CONV - 1, 3 * DN_WIDTH), 1.0),
        'state_delta': nrm(6, (DEPTH, DEC_BATCH, N_HEADS, HEAD_DIM, HEAD_DIM), 0.1),
        'w_ada': nrm(7, (DEPTH, D_MODEL, 6 * D_MODEL), 0.5 * D_MODEL ** -0.5),
        'b_ada': nrm(8, (DEPTH, 6 * D_MODEL), 0.02),
        'g_mix': gain(9, (DEPTH, D_MODEL)),
        'w_in': nrm(10, (DEPTH, D_MODEL, IN_WIDTH), D_MODEL ** -0.5),
        'conv_dw': nrm(11, (DEPTH, CONV_WIDTH, CONV_CH), CONV_WIDTH ** -0.5),
        'conv_dw_b': nrm(12, (DEPTH, CONV_CH), 0.02),
        'conv_ln_g': gain(13, (DEPTH, CONV_CH)),
        'conv_ln_b': nrm(14, (DEPTH, CONV_CH), 0.02),
        'w_pa': nrm(15, (DEPTH, CONV_CH, D_MODEL), CONV_CH ** -0.5),
        'short_conv_w': nrm(16, (DEPTH, SHORT_CONV, 3 * DN_WIDTH), SHORT_CONV ** -0.5),
        'a_log': jnp.log(jax.random.uniform(ks[17], (DEPTH, N_HEADS), f32, minval=1.0, maxval=16.0)),
        'dt_bias': dt + jnp.log(-jnp.expm1(-dt)),
        'dn_norm_g': gain(18, (DEPTH, HEAD_DIM)),
        'w_pb': nrm(19, (DEPTH, DN_WIDTH, D_MODEL), DN_WIDTH ** -0.5),
        'w_o': nrm(21, (DEPTH, D_MODEL, D_MODEL), D_MODEL ** -0.5),
        'g_ffn': gain(22, (DEPTH, D_MODEL)),
        'w_router': nrm(23, (DEPTH, D_MODEL, N_EXPERTS), D_MODEL ** -0.5),
        'e_bias': nrm(24, (DEPTH, N_EXPERTS), 0.01),
        'w_gate': nrm(25, (DEPTH, N_EXPERTS, D_MODEL, D_EXPERT), D_MODEL ** -0.5),
        'w_up': nrm(26, (DEPTH, N_EXPERTS, D_MODEL, D_EXPERT), D_MODEL ** -0.5),
        'w_down': nrm(27, (DEPTH, N_EXPERTS, D_EXPERT, D_MODEL), D_EXPERT ** -0.5),
        'ws_gate': nrm(28, (DEPTH, D_MODEL, D_SHARED), D_MODEL ** -0.5),
        'ws_up': nrm(29, (DEPTH, D_MODEL, D_SHARED), D_MODEL ** -0.5),
        'ws_down': nrm(30, (DEPTH, D_SHARED, D_MODEL), D_SHARED ** -0.5),
        'g_final': gain(31, (D_MODEL,)),
    }


def reference(x_prompt, x_sample, c_prompt, c_sample, state_conv, state_short_conv, state_delta,
              w_ada, b_ada, g_mix, w_in, conv_dw, conv_dw_b, conv_ln_g, conv_ln_b, w_pa, short_conv_w,
              a_log, dt_bias, dn_norm_g, w_pb, w_o, g_ffn, w_router, e_bias, w_gate, w_up, w_down,
              ws_gate, ws_up, ws_down, g_final):
    hp, hs = x_prompt, x_sample
    n_prompt = x_prompt.shape[0]
    p_conv, p_sc, p_delta, s_conv, s_sc, s_delta = [], [], [], [], [], []
    for l in range(DEPTH):
        params = (w_ada[l], b_ada[l], g_mix[l], w_in[l], conv_dw[l], conv_dw_b[l], conv_ln_g[l], conv_ln_b[l],
                  w_pa[l], short_conv_w[l], a_log[l], dt_bias[l], dn_norm_g[l], w_pb[l], w_o[l], g_ffn[l],
                  w_router[l], e_bias[l], w_gate[l], w_up[l], w_down[l], ws_gate[l], ws_up[l], ws_down[l])
        hp, c1, c2, c3 = layer(
            hp, c_prompt,
            jnp.zeros((n_prompt, CONV_WIDTH - 1, CONV_CH), hp.dtype),
            jnp.zeros((n_prompt, SHORT_CONV - 1, 3 * DN_WIDTH), hp.dtype),
            jnp.zeros((n_prompt, N_HEADS, HEAD_DIM, HEAD_DIM), jnp.float32),
            *params)
        p_conv.append(c1)
        p_sc.append(c2)
        p_delta.append(c3)
        hs, d1, d2, d3 = layer(hs, c_sample, state_conv[l], state_short_conv[l], state_delta[l], *params)
        s_conv.append(d1)
        s_sc.append(d2)
        s_delta.append(d3)
    y_prompt = rms_norm(hp, g_final)
    y_sample = rms_norm(hs, g_final)
    new_conv_prompt = jnp.stack(p_conv)
    new_short_conv_prompt = jnp.stack(p_sc)
    new_delta_prompt = jnp.stack(p_delta)
    new_conv_sample = jnp.stack(s_conv)
    new_short_conv_sample = jnp.stack(s_sc)
    new_delta_sample = jnp.stack(s_delta)
    return (y_prompt, y_sample, new_conv_prompt, new_short_conv_prompt, new_delta_prompt,
            new_conv_sample, new_short_conv_sample, new_delta_sample)
```

```python
import functools

import jax
import jax.numpy as jnp
from jax import lax
from jax.experimental import pallas as pl
from jax.experimental.pallas import tpu as pltpu

F32 = jnp.float32
BF16 = jnp.bfloat16
I32 = jnp.int32

EPS = 1e-6
D_MODEL = 1024
CONV_CH = 512
CONV_WIDTH = 31
N_HEADS = 4
HEAD_DIM = 128
DN_WIDTH = N_HEADS * HEAD_DIM
SHORT_CONV = 4
CHUNK = 64
N_EXPERTS = 256
TOP_K = 8
N_GROUPS = 8
GROUP_SIZE = N_EXPERTS // N_GROUPS
TOPK_GROUPS = 4
D_EXPERT = 256
D_SHARED = 256
ROUTE_SCALE = 2.5

LANES = 128
SUBLANES = 8
CONV_HALO = 32
SC_HALO = 8
VMEM_LIMIT = 48 * 1024 * 1024

HIGHEST = lax.Precision.HIGHEST
NT_DIMS = (((1,), (1,)), ((), ()))
TN_DIMS = (((0,), (0,)), ((), ()))


def _cparams(n_grid_dims=1):
    return pltpu.CompilerParams(
        dimension_semantics=("arbitrary",) * n_grid_dims,
        vmem_limit_bytes=VMEM_LIMIT)


def _sigmoid(x):
    return jax.nn.sigmoid(x)


def _silu(x):
    return x * jax.nn.sigmoid(x)


def _bdot(a, b):
    return jnp.dot(a.astype(BF16), b.astype(BF16), preferred_element_type=F32)


def _hdot(a, b):
    return jnp.dot(a, b, preferred_element_type=F32, precision=HIGHEST)


def _ada_kernel(c_ref, w_ref, b_ref, o_ref):
    o_ref[...] = _bdot(_silu(c_ref[...]), w_ref[...]) + b_ref[...]


def _ada(c_all, w_ada_bf, b_ada):
    n = c_all.shape[0]
    width = w_ada_bf.shape[1]
    tn = 1536
    return pl.pallas_call(
        _ada_kernel,
        grid=(width // tn,),
        in_specs=[pl.BlockSpec((n, D_MODEL), lambda j: (0, 0)),
                  pl.BlockSpec((D_MODEL, tn), lambda j: (0, j)),
                  pl.BlockSpec((1, tn), lambda j: (0, j))],
        out_specs=pl.BlockSpec((n, tn), lambda j: (0, j)),
        out_shape=jax.ShapeDtypeStruct((n, width), F32),
        compiler_params=_cparams(),
        name="ada",
    )(c_all, w_ada_bf, b_ada.reshape(1, width))


def _in_proj_kernel(x_ref, scale_ref, shift_ref, g_ref, wglu_ref, wqkv_ref, wz_ref, wab_ref,
                    wga_ref, wgb_ref, nea_ref, dtb_ref,
                    u_ref, qkv_ref, zs_ref, ab_ref, sga_ref, sgb_ref):
    x = x_ref[...]
    y = x * lax.rsqrt(jnp.mean(x * x, axis=-1, keepdims=True) + EPS) * g_ref[...]
    h = (y * (1.0 + scale_ref[...]) + shift_ref[...]).astype(BF16)
    glu = jnp.dot(h, wglu_ref[...], preferred_element_type=F32)
    u_ref[...] = glu[:, :CONV_CH] * _sigmoid(glu[:, CONV_CH:])
    qkv_ref[...] = jnp.dot(h, wqkv_ref[...], preferred_element_type=F32)
    zs_ref[...] = _silu(jnp.dot(h, wz_ref[...], preferred_element_type=F32))
    ab = jnp.dot(h, wab_ref[...], preferred_element_type=F32)
    sp = ab + dtb_ref[...]
    softplus = jnp.maximum(sp, 0.0) + jnp.log1p(jnp.exp(-jnp.abs(sp)))
    lane = lax.broadcasted_iota(I32, ab.shape, 1)
    ab_ref[...] = jnp.where(lane < N_HEADS, nea_ref[...] * softplus, _sigmoid(ab))
    sga_ref[...] = _sigmoid(jnp.dot(h, wga_ref[...], preferred_element_type=F32))
    sgb_ref[...] = _sigmoid(jnp.dot(h, wgb_ref[...], preferred_element_type=F32))


def _mod_spec(mod, tiles_per_mod):
    return pl.BlockSpec((None,) + mod.shape[1:], lambda i: (i // tiles_per_mod, 0, 0))


def _const_spec(a):
    nd = a.ndim
    return pl.BlockSpec(a.shape, lambda i: (0,) * nd)


def _in_proj(x2d, scale, shift, g_mix, w, nea, dtb, *, tm, tiles_per_mod):
    t = x2d.shape[0]
    row = lambda c: pl.BlockSpec((tm, c), lambda i: (i, 0))
    consts = [g_mix, w["glu"], w["qkv"], w["z"], w["ab"], w["ga"], w["gb"], nea, dtb]
    widths = [CONV_CH, 3 * DN_WIDTH, DN_WIDTH, LANES, D_MODEL, D_MODEL]
    return pl.pallas_call(
        _in_proj_kernel,
        grid=(t // tm,),
        in_specs=[row(D_MODEL), _mod_spec(scale, tiles_per_mod), _mod_spec(shift, tiles_per_mod)]
                 + [_const_spec(a) for a in consts],
        out_specs=[row(c) for c in widths],
        out_shape=[jax.ShapeDtypeStruct((t, c), F32) for c in widths],
        compiler_params=_cparams(),
        name="in_proj",
    )(x2d, scale, shift, *consts)


def _layer_norm_swish(y, g, b):
    mu = jnp.mean(y, axis=-1, keepdims=True)
    yc = y - mu
    var = jnp.mean(yc * yc, axis=-1, keepdims=True)
    return _silu(yc * lax.rsqrt(var + EPS) * g + b)


def _conv_kernel(u_ref, halo_ref, hist_ref, w_ref, b_ref, lg_ref, lb_ref, o_ref, win_ref,
                 *, tl, tiles_per_batch, rows_per_pass):
    i = pl.program_id(0)
    first = (i % tiles_per_batch) == 0

    @pl.when(first)
    def _():
        win_ref[0:CONV_HALO, :] = hist_ref[...]

    @pl.when(jnp.logical_not(first))
    def _():
        win_ref[0:CONV_HALO, :] = halo_ref[...]

    win_ref[CONV_HALO:CONV_HALO + tl, :] = u_ref[...]
    tap0 = CONV_HALO - (CONV_WIDTH - 1)
    for r in range(tl // rows_per_pass):
        base = r * rows_per_pass
        acc = jnp.zeros((rows_per_pass, CONV_CH), F32)
        for j in range(CONV_WIDTH):
            acc = acc + w_ref[j:j + 1, :] * win_ref[base + tap0 + j:base + tap0 + j + rows_per_pass, :]
        act = _layer_norm_swish(acc + b_ref[...], lg_ref[...], lb_ref[...])
        o_ref[base:base + rows_per_pass, :] = act.astype(o_ref.dtype)


def _conv_branch(u2d, hist, conv_dw, conv_dw_b, ln_g, ln_b, *, tl, tiles_per_batch):
    t = u2d.shape[0]
    halo_blocks = tl // CONV_HALO
    consts = [conv_dw, conv_dw_b, ln_g, ln_b]
    return pl.pallas_call(
        functools.partial(_conv_kernel, tl=tl, tiles_per_batch=tiles_per_batch, rows_per_pass=64),
        grid=(t // tl,),
        in_specs=[pl.BlockSpec((tl, CONV_CH), lambda i: (i, 0)),
                  pl.BlockSpec((CONV_HALO, CONV_CH), lambda i: (jnp.maximum(i * halo_blocks - 1, 0), 0)),
                  pl.BlockSpec((None, CONV_HALO, CONV_CH), lambda i: (i // tiles_per_batch, 0, 0))]
                 + [_const_spec(a) for a in consts],
        out_specs=pl.BlockSpec((tl, CONV_CH), lambda i: (i, 0)),
        out_shape=jax.ShapeDtypeStruct((t, CONV_CH), BF16),
        scratch_shapes=[pltpu.VMEM((CONV_HALO + tl, CONV_CH), F32)],
        compiler_params=_cparams(),
        name="conv_branch",
    )(u2d, u2d, hist, *consts)


def _l2n(x):
    return x * lax.rsqrt(jnp.sum(x * x, axis=-1, keepdims=True) + 1e-6)


def _gated_head_norm(o, g, zs):
    return o * lax.rsqrt(jnp.mean(o * o, axis=-1, keepdims=True) + EPS) * g * zs


def _delta_kernel(qkv_ref, halo_ref, hist_ref, scw_ref, ab_ref, zs_ref, s0_ref, dng_ref,
                  o_ref, s_ref, win_ref, c_ref, *, tc, tiles_per_batch):
    i = pl.program_id(0)
    first = (i % tiles_per_batch) == 0

    @pl.when(first)
    def _():
        win_ref[0:SC_HALO, :] = hist_ref[...]
        s_ref[...] = s0_ref[...]

    @pl.when(jnp.logical_not(first))
    def _():
        win_ref[0:SC_HALO, :] = halo_ref[...]

    win_ref[SC_HALO:SC_HALO + tc, :] = qkv_ref[...]
    tap0 = SC_HALO - (SHORT_CONV - 1)
    acc = jnp.zeros((tc, 3 * DN_WIDTH), F32)
    for j in range(SHORT_CONV):
        acc = acc + scw_ref[j:j + 1, :] * win_ref[tap0 + j:tap0 + j + tc, :]
    c_ref[...] = _silu(acc)

    row = lax.broadcasted_iota(I32, (CHUNK, CHUNK), 0)
    col = lax.broadcasted_iota(I32, (CHUNK, CHUNK), 1)
    incl = row >= col
    strict = row > col
    lower_ones = incl.astype(F32)
    lane_sel = lax.broadcasted_iota(I32, (CHUNK, LANES), 1)

    for ch in range(tc // CHUNK):
        r0 = ch * CHUNK
        ab = ab_ref[r0:r0 + CHUNK, :]
        gcum_all = _hdot(lower_ones, ab)
        for h in range(N_HEADS):
            lo = h * HEAD_DIM
            q = _l2n(c_ref[r0:r0 + CHUNK, lo:lo + HEAD_DIM]) * (HEAD_DIM ** -0.5)
            k = _l2n(c_ref[r0:r0 + CHUNK, DN_WIDTH + lo:DN_WIDTH + lo + HEAD_DIM])
            v = c_ref[r0:r0 + CHUNK, 2 * DN_WIDTH + lo:2 * DN_WIDTH + lo + HEAD_DIM]
            beta = ab[:, N_HEADS + h:N_HEADS + h + 1]
            gc = gcum_all[:, h:h + 1]
            g_last = gcum_all[CHUNK - 1:CHUNK, h:h + 1]
            gr = lax.dot_general((lane_sel == h).astype(F32), gcum_all, NT_DIMS,
                                 preferred_element_type=F32, precision=HIGHEST)
            decay = jnp.exp(jnp.where(incl, gc - gr, -jnp.inf))
            kb = k * beta
            kq = lax.dot_general(jnp.concatenate([kb, q], axis=0).astype(BF16), k.astype(BF16),
                                 NT_DIMS, preferred_element_type=F32)
            l_mat = jnp.where(strict, kq[:CHUNK] * decay, 0.0)
            a_qk = kq[CHUNK:] * decay
            e_gc = jnp.exp(gc)
            sol = jnp.concatenate([v * beta, kb * e_gc], axis=1)
            sol = sol - _hdot(l_mat, sol)
            pw = l_mat
            for _ in range(5):
                pw = _hdot(pw, pw)
                sol = sol + _hdot(pw, sol)
            u0 = sol[:, :HEAD_DIM]
            w = sol[:, HEAD_DIM:]
            s = s_ref[h]
            wq_s = _bdot(jnp.concatenate([w, q * e_gc], axis=0), s)
            u = u0 - wq_s[:CHUNK]
            o = wq_s[CHUNK:] + _bdot(a_qk, u)
            k_dec = k * jnp.exp(g_last - gc)
            s_ref[h] = jnp.exp(g_last) * s + lax.dot_general(
                k_dec.astype(BF16), u.astype(BF16), TN_DIMS, preferred_element_type=F32)
            o_ref[r0:r0 + CHUNK, lo:lo + HEAD_DIM] = _gated_head_norm(
                o, dng_ref[...], zs_ref[r0:r0 + CHUNK, lo:lo + HEAD_DIM]).astype(o_ref.dtype)


def _delta_branch(qkv2d, hist, s0, scw, ab, zs, dng, *, tc, tiles_per_batch):
    t = qkv2d.shape[0]
    nb = s0.shape[0]
    halo_blocks = tc // SC_HALO
    wqkv = 3 * DN_WIDTH
    state_spec = pl.BlockSpec((None, N_HEADS, HEAD_DIM, HEAD_DIM), lambda i: (i // tiles_per_batch, 0, 0, 0))
    return pl.pallas_call(
        functools.partial(_delta_kernel, tc=tc, tiles_per_batch=tiles_per_batch),
        grid=(t // tc,),
        in_specs=[pl.BlockSpec((tc, wqkv), lambda i: (i, 0)),
                  pl.BlockSpec((SC_HALO, wqkv), lambda i: (jnp.maximum(i * halo_blocks - 1, 0), 0)),
                  pl.BlockSpec((None, SC_HALO, wqkv), lambda i: (i // tiles_per_batch, 0, 0)),
                  _const_spec(scw),
                  pl.BlockSpec((tc, LANES), lambda i: (i, 0)),
                  pl.BlockSpec((tc, DN_WIDTH), lambda i: (i, 0)),
                  state_spec,
                  _const_spec(dng)],
        out_specs=[pl.BlockSpec((tc, DN_WIDTH), lambda i: (i, 0)), state_spec],
        out_shape=[jax.ShapeDtypeStruct((t, DN_WIDTH), BF16),
                   jax.ShapeDtypeStruct((nb, N_HEADS, HEAD_DIM, HEAD_DIM), F32)],
        scratch_shapes=[pltpu.VMEM((SC_HALO + tc, wqkv), F32), pltpu.VMEM((tc, wqkv), F32)],
        compiler_params=_cparams(),
        name="delta_branch",
    )(qkv2d, qkv2d, hist, scw, ab, zs, s0, dng)


def _step_front_kernel(u_ref, qkv_ref, ch_ref, sh_ref, cw_ref, cb_ref, lg_ref, lb_ref, scw_ref,
                       acta_ref, nch_ref, qkvp_ref, nsh_ref):
    hist_rows = CONV_WIDTH - 1
    u = u_ref[...]
    acc = cw_ref[hist_rows:hist_rows + 1, :] * u
    for j in range(hist_rows):
        acc = acc + cw_ref[j:j + 1, :] * ch_ref[:, j * CONV_CH:(j + 1) * CONV_CH]
    acta_ref[...] = _layer_norm_swish(acc + cb_ref[...], lg_ref[...], lb_ref[...]).astype(acta_ref.dtype)
    nch_ref[:, :(hist_rows - 1) * CONV_CH] = ch_ref[:, CONV_CH:]
    nch_ref[:, (hist_rows - 1) * CONV_CH:] = u

    wq = 3 * DN_WIDTH
    sc_rows = SHORT_CONV - 1
    qkv = qkv_ref[...]
    acc = scw_ref[sc_rows:sc_rows + 1, :] * qkv
    for j in range(sc_rows):
        acc = acc + scw_ref[j:j + 1, :] * sh_ref[:, j * wq:(j + 1) * wq]
    c = _silu(acc)
    for h in range(N_HEADS):
        lo = h * HEAD_DIM
        qkvp_ref[:, lo:lo + HEAD_DIM] = _l2n(c[:, lo:lo + HEAD_DIM]) * (HEAD_DIM ** -0.5)
        qkvp_ref[:, DN_WIDTH + lo:DN_WIDTH + lo + HEAD_DIM] = _l2n(c[:, DN_WIDTH + lo:DN_WIDTH + lo + HEAD_DIM])
    qkvp_ref[:, 2 * DN_WIDTH:] = c[:, 2 * DN_WIDTH:]
    nsh_ref[:, :(sc_rows - 1) * wq] = sh_ref[:, wq:]
    nsh_ref[:, (sc_rows - 1) * wq:] = qkv


def _step_front(u, qkv, conv_hist_flat, sc_hist_flat, conv_dw, conv_dw_b, ln_g, ln_b, scw, *, bt):
    n = u.shape[0]
    wq = 3 * DN_WIDTH
    consts = [conv_dw, conv_dw_b, ln_g, ln_b, scw]
    row = lambda c: pl.BlockSpec((bt, c), lambda i: (i, 0))
    widths = [CONV_CH, conv_hist_flat.shape[1], wq, sc_hist_flat.shape[1]]
    dtypes = [BF16, F32, F32, F32]
    return pl.pallas_call(
        _step_front_kernel,
        grid=(n // bt,),
        in_specs=[row(CONV_CH), row(wq), row(widths[1]), row(widths[3])] + [_const_spec(a) for a in consts],
        out_specs=[row(c) for c in widths],
        out_shape=[jax.ShapeDtypeStruct((n, c), d) for c, d in zip(widths, dtypes)],
        compiler_params=_cparams(),
        name="step_front",
    )(u, qkv, conv_hist_flat, sc_hist_flat, *consts)


def _rows_to_cols(x):
    pad = jnp.zeros((LANES - x.shape[0], LANES), x.dtype)
    return jnp.concatenate([x, pad], axis=0).T


def _step_delta_kernel(qkvp_ref, ab_ref, zs_ref, s_ref, dng_ref, o_ref, sn_ref, *, bt):
    ab = ab_ref[...]
    for h in range(N_HEADS):
        lo = h * HEAD_DIM
        q = qkvp_ref[:, lo:lo + HEAD_DIM]
        k = qkvp_ref[:, DN_WIDTH + lo:DN_WIDTH + lo + HEAD_DIM]
        v = qkvp_ref[:, 2 * DN_WIDTH + lo:2 * DN_WIDTH + lo + HEAD_DIM]
        q_cols = _rows_to_cols(q)
        k_cols = _rows_to_cols(k)
        qk = jnp.sum(q * k, axis=-1, keepdims=True)
        alpha = jnp.exp(ab[:, h:h + 1])
        beta = ab[:, N_HEADS + h:N_HEADS + h + 1]
        for j in range(bt):
            s = s_ref[j, h]
            kc = k_cols[:, j:j + 1]
            qc = q_cols[:, j:j + 1]
            a = alpha[j:j + 1, :]
            s_k = jnp.sum(s * kc, axis=0, keepdims=True)
            s_q = jnp.sum(s * qc, axis=0, keepdims=True)
            u = beta[j:j + 1, :] * (v[j:j + 1, :] - a * s_k)
            o = a * s_q + qk[j:j + 1, :] * u
            sn_ref[j, h] = a * s + kc * u
            o_ref[j:j + 1, lo:lo + HEAD_DIM] = _gated_head_norm(
                o, dng_ref[...], zs_ref[j:j + 1, lo:lo + HEAD_DIM]).astype(o_ref.dtype)


def _step_delta(qkvp, ab, zs, s0, dng, *, bt):
    n = qkvp.shape[0]
    row = lambda c: pl.BlockSpec((bt, c), lambda i: (i, 0))
    state_spec = pl.BlockSpec((bt, N_HEADS, HEAD_DIM, HEAD_DIM), lambda i: (i, 0, 0, 0))
    return pl.pallas_call(
        functools.partial(_step_delta_kernel, bt=bt),
        grid=(n // bt,),
        in_specs=[row(3 * DN_WIDTH), row(LANES), row(DN_WIDTH), state_spec, _const_spec(dng)],
        out_specs=[row(DN_WIDTH), state_spec],
        out_shape=[jax.ShapeDtypeStruct((n, DN_WIDTH), BF16), jax.ShapeDtypeStruct(s0.shape, F32)],
        compiler_params=_cparams(),
        name="step_delta",
    )(qkvp, ab, zs, s0, dng)


def _post_kernel(acta_ref, actb_ref, sga_ref, sgb_ref, x_ref, gate1_ref, scale2_ref, shift2_ref,
                 gate2_ref, cnt0_ref, gffn_ref, wpa_ref, wpb_ref, wo_ref, wrt_ref, ebias_ref,
                 wsgu_ref, wsd_ref,
                 xs_ref, h2_ref, eidx_ref, ew_ref, rank_ref, cnt_ref, *, tm):
    i = pl.program_id(0)

    @pl.when(i == 0)
    def _():
        cnt_ref[...] = cnt0_ref[...]

    y_a = jnp.dot(acta_ref[...], wpa_ref[...], preferred_element_type=F32)
    y_b = jnp.dot(actb_ref[...], wpb_ref[...], preferred_element_type=F32)
    merged = sga_ref[...] * y_a + sgb_ref[...] * y_b
    mix = _bdot(merged, wo_ref[...])
    x1 = x_ref[...] + gate1_ref[...] * mix
    y = x1 * lax.rsqrt(jnp.mean(x1 * x1, axis=-1, keepdims=True) + EPS) * gffn_ref[...]
    h2 = y * (1.0 + scale2_ref[...]) + shift2_ref[...]
    h2_ref[...] = h2
    h2b = h2.astype(BF16)

    gu = jnp.dot(h2b, wsgu_ref[...], preferred_element_type=F32)
    shared = _bdot(_silu(gu[:, :D_SHARED]) * gu[:, D_SHARED:], wsd_ref[...])
    xs_ref[...] = x1 + gate2_ref[...] * shared

    logits_t = lax.dot_general(wrt_ref[...], h2b, NT_DIMS, preferred_element_type=F32)
    scores = _sigmoid(logits_t)
    biased = scores + ebias_ref[...]
    neg = -jnp.inf
    big = jnp.int32(1 << 30)
    erow = lax.broadcasted_iota(I32, (N_EXPERTS, tm), 0)

    def first_argmax(vals, rows):
        m = jnp.max(vals, axis=0, keepdims=True)
        return m, jnp.min(jnp.where(vals == m, rows, big), axis=0, keepdims=True)

    group_scores = []
    for g in range(N_GROUPS):
        vals = biased[g * GROUP_SIZE:(g + 1) * GROUP_SIZE, :]
        rows = lax.broadcasted_iota(I32, (GROUP_SIZE, tm), 0) + g * GROUP_SIZE
        m1, i1 = first_argmax(vals, rows)
        m2 = jnp.max(jnp.where(rows == i1, neg, vals), axis=0, keepdims=True)
        group_scores.append(m1 + m2)
    group_sel = [jnp.zeros((1, tm), jnp.bool_)] * N_GROUPS
    for _ in range(TOPK_GROUPS):
        best = functools.reduce(jnp.maximum, group_scores)
        gi = functools.reduce(
            jnp.minimum, [jnp.where(group_scores[g] == best, jnp.int32(g), big) for g in range(N_GROUPS)])
        for g in range(N_GROUPS):
            hit = gi == g
            group_sel[g] = jnp.logical_or(group_sel[g], hit)
            group_scores[g] = jnp.where(hit, neg, group_scores[g])
    cand = jnp.concatenate(
        [jnp.where(group_sel[g], biased[g * GROUP_SIZE:(g + 1) * GROUP_SIZE, :], neg)
         for g in range(N_GROUPS)], axis=0)

    w_rows, hits = [], []
    picked = jnp.zeros((N_EXPERTS, tm), F32)
    for k in range(TOP_K):
        _, ei = first_argmax(cand, erow)
        hit = erow == ei
        eidx_ref[k:k + 1, :] = ei
        w_rows.append(jnp.sum(jnp.where(hit, scores, 0.0), axis=0, keepdims=True))
        hits.append(hit)
        picked = jnp.where(hit, 1.0, picked)
        cand = jnp.where(hit, neg, cand)
    w_sum = functools.reduce(jnp.add, w_rows)
    for k in range(TOP_K):
        ew_ref[k:k + 1, :] = w_rows[k] / w_sum * ROUTE_SCALE

    trow = lax.broadcasted_iota(I32, (tm, tm), 0)
    tcol = lax.broadcasted_iota(I32, (tm, tm), 1)
    before = (trow < tcol).astype(BF16)
    prefix = jnp.dot(picked.astype(BF16), before, preferred_element_type=F32) + cnt_ref[:, 0:1]
    for k in range(TOP_K):
        rank_ref[k:k + 1, :] = jnp.sum(jnp.where(hits[k], prefix, 0.0), axis=0, keepdims=True).astype(I32)
    cnt_ref[...] = cnt_ref[...] + jnp.sum(picked, axis=1, keepdims=True)


def _post(acta, actb, sga, sgb, x2d, gate1, scale2, shift2, gate2, cnt0, g_ffn, w, ebias_col,
          *, tm, tiles_per_mod):
    t = x2d.shape[0]
    row = lambda c: pl.BlockSpec((tm, c), lambda i: (i, 0))
    col = lambda r: pl.BlockSpec((r, tm), lambda i: (0, i))
    mods = [gate1, scale2, shift2, gate2]
    consts = [cnt0, g_ffn, w["pa"], w["pb"], w["o"], w["router_t"], ebias_col, w["s_gu"], w["s_down"]]
    return pl.pallas_call(
        functools.partial(_post_kernel, tm=tm),
        grid=(t // tm,),
        in_specs=[row(CONV_CH), row(DN_WIDTH), row(D_MODEL), row(D_MODEL), row(D_MODEL)]
                 + [_mod_spec(m, tiles_per_mod) for m in mods] + [_const_spec(a) for a in consts],
        out_specs=[row(D_MODEL), row(D_MODEL), col(TOP_K), col(TOP_K), col(TOP_K), _const_spec(cnt0)],
        out_shape=[jax.ShapeDtypeStruct((t, D_MODEL), F32), jax.ShapeDtypeStruct((t, D_MODEL), F32),
                   jax.ShapeDtypeStruct((TOP_K, t), I32), jax.ShapeDtypeStruct((TOP_K, t), F32),
                   jax.ShapeDtypeStruct((TOP_K, t), I32), jax.ShapeDtypeStruct(cnt0.shape, F32)],
        compiler_params=_cparams(),
        name="post_mixer",
    )(acta, actb, sga, sgb, x2d, *mods, *consts)


def _row_copy(src_ref, src_row, dst_ref, dst_row, sem):
    return pltpu.make_async_copy(src_ref.at[pl.ds(src_row, 1), :], dst_ref.at[pl.ds(dst_row, 1), :], sem)


def _dispatch_kernel(dest_ref, h_ref, xs_hbm, sem, *, tm):
    def issue(r, carry):
        for k in range(TOP_K):
            _row_copy(h_ref, r, xs_hbm, dest_ref[0, 0, r * TOP_K + k], sem).start()
        return carry

    lax.fori_loop(0, tm, issue, 0)

    def drain(r, carry):
        for k in range(TOP_K):
            _row_copy(h_ref, r, xs_hbm, dest_ref[0, 0, r * TOP_K + k], sem).wait()
        return carry

    lax.fori_loop(0, tm, drain, 0)


def _dispatch(h2_all, dest_tiles, n_rows, *, tm):
    t = h2_all.shape[0]
    return pl.pallas_call(
        functools.partial(_dispatch_kernel, tm=tm),
        grid=(t // tm,),
        in_specs=[pl.BlockSpec((1, 1, tm * TOP_K), lambda i: (i, 0, 0), memory_space=pltpu.SMEM),
                  pl.BlockSpec((tm, D_MODEL), lambda i: (i, 0))],
        out_specs=pl.BlockSpec(memory_space=pl.ANY),
        out_shape=jax.ShapeDtypeStruct((n_rows, D_MODEL), F32),
        scratch_shapes=[pltpu.SemaphoreType.DMA],
        compiler_params=_cparams(),
        name="moe_dispatch",
    )(dest_tiles, h2_all)


def _experts_kernel(sblk_ref, sexp_ref, slo_ref, shi_ref, x_ref, wg_ref, wu_ref, wd_ref, y_ref,
                    wgu_s, wd_s, loaded_ref, *, bm):
    p = pl.program_id(0)
    lo = slo_ref[p]
    expert = sexp_ref[p]

    @pl.when(p == 0)
    def _():
        loaded_ref[0] = -1

    @pl.when(shi_ref[p] > lo)
    def _():
        @pl.when(loaded_ref[0] != expert)
        def _():
            wgu_s[:, :D_EXPERT] = wg_ref[...].astype(BF16)
            wgu_s[:, D_EXPERT:] = wu_ref[...].astype(BF16)
            wd_s[...] = wd_ref[...].astype(BF16)
            loaded_ref[0] = expert

        gu = jnp.dot(x_ref[...].astype(BF16), wgu_s[...], preferred_element_type=F32)
        hb = (_silu(gu[:, :D_EXPERT]) * gu[:, D_EXPERT:]).astype(BF16)
        y = jnp.dot(hb, wd_s[...], preferred_element_type=F32)

        @pl.when(lo == 0)
        def _():
            y_ref[...] = y

        @pl.when(lo > 0)
        def _():
            rows = lax.broadcasted_iota(I32, (bm, 1), 0)
            y_ref[...] = jnp.where(rows >= lo, y, y_ref[...])


def _experts(xs, w_gate, w_up, w_down, seg_blk, seg_exp, seg_lo, seg_hi, *, bm):
    n_rows = xs.shape[0]
    n_seg = seg_blk.shape[0]
    grid_spec = pltpu.PrefetchScalarGridSpec(
        num_scalar_prefetch=4,
        grid=(n_seg,),
        in_specs=[pl.BlockSpec((bm, D_MODEL), lambda p, sb, se, sl, sh: (sb[p], 0)),
                  pl.BlockSpec((None, D_MODEL, D_EXPERT), lambda p, sb, se, sl, sh: (se[p], 0, 0)),
                  pl.BlockSpec((None, D_MODEL, D_EXPERT), lambda p, sb, se, sl, sh: (se[p], 0, 0)),
                  pl.BlockSpec((None, D_EXPERT, D_MODEL), lambda p, sb, se, sl, sh: (se[p], 0, 0))],
        out_specs=pl.BlockSpec((bm, D_MODEL), lambda p, sb, se, sl, sh: (sb[p], 0)),
        scratch_shapes=[pltpu.VMEM((D_MODEL, 2 * D_EXPERT), BF16), pltpu.VMEM((D_EXPERT, D_MODEL), BF16),
                        pltpu.SMEM((1,), I32)],
    )
    return pl.pallas_call(
        functools.partial(_experts_kernel, bm=bm),
        grid_spec=grid_spec,
        out_shape=jax.ShapeDtypeStruct((n_rows, D_MODEL), F32),
        compiler_params=_cparams(),
        name="moe_experts",
    )(seg_blk, seg_exp, seg_lo, seg_hi, xs, w_gate, w_up, w_down)


def _final_kernel(dest_ref, ew_ref, xs_ref, gate2_ref, gfin_ref, ys_hbm, o_ref, buf_ref, sem, *, tm):
    def issue(r, carry):
        for k in range(TOP_K):
            _row_copy(ys_hbm, dest_ref[0, 0, r * TOP_K + k], buf_ref.at[k], r, sem).start()
        return carry

    lax.fori_loop(0, tm, issue, 0)

    def drain(r, carry):
        for k in range(TOP_K):
            _row_copy(ys_hbm, dest_ref[0, 0, r * TOP_K + k], buf_ref.at[k], r, sem).wait()
        return carry

    lax.fori_loop(0, tm, drain, 0)

    ew = ew_ref[...]
    routed = ew[:, 0:1] * buf_ref[0]
    for k in range(1, TOP_K):
        routed = routed + ew[:, k:k + 1] * buf_ref[k]
    x2 = xs_ref[...] + gate2_ref[...] * routed
    o_ref[...] = x2 * lax.rsqrt(jnp.mean(x2 * x2, axis=-1, keepdims=True) + EPS) * gfin_ref[...]


def _final(dest_tiles, ew_tok, xs_all, gate2_rows, g_final, ys, *, tm):
    t = xs_all.shape[0]
    row = lambda c: pl.BlockSpec((tm, c), lambda i: (i, 0))
    return pl.pallas_call(
        functools.partial(_final_kernel, tm=tm),
        grid=(t // tm,),
        in_specs=[pl.BlockSpec((1, 1, tm * TOP_K), lambda i: (i, 0, 0), memory_space=pltpu.SMEM),
                  row(TOP_K), row(D_MODEL), row(D_MODEL), _const_spec(g_final),
                  pl.BlockSpec(memory_space=pl.ANY)],
        out_specs=row(D_MODEL),
        out_shape=jax.ShapeDtypeStruct((t, D_MODEL), F32),
        scratch_shapes=[pltpu.VMEM((TOP_K, tm, D_MODEL), F32), pltpu.SemaphoreType.DMA],
        compiler_params=_cparams(),
        name="moe_combine_final",
    )(dest_tiles, ew_tok, xs_all, gate2_rows, g_final, ys)


def _pick_tile(n, preferred):
    t = min(n, preferred)
    assert n % t == 0, (n, t)
    return t


def kernel(x_prompt, x_sample, c_prompt, c_sample, state_conv, state_short_conv, state_delta, w_ada, b_ada, g_mix, w_in, conv_dw, conv_dw_b, conv_ln_g, conv_ln_b, w_pa, short_conv_w, a_log, dt_bias, dn_norm_g, w_pb, w_o, g_ffn, w_router, e_bias, w_gate, w_up, w_down, ws_gate, ws_up, ws_down, g_final):
    nb, seq, d = x_prompt.shape
    ns = x_sample.shape[0]
    assert d == D_MODEL and w_ada.shape[0] == 1 and x_sample.shape[1] == 1
    assert seq % CHUNK == 0 and seq >= CONV_HALO and ns % SUBLANES == 0
    tp = nb * seq
    row1 = lambda a: a.reshape(1, -1)

    wi = w_in[0].astype(BF16)
    o_q = 2 * CONV_CH
    o_z = o_q + 3 * DN_WIDTH
    o_a = o_z + DN_WIDTH
    o_ga = o_a + 2 * N_HEADS
    w_proj = {
        "glu": wi[:, :o_q], "qkv": wi[:, o_q:o_z], "z": wi[:, o_z:o_a],
        "ab": jnp.pad(wi[:, o_a:o_ga], ((0, 0), (0, LANES - 2 * N_HEADS))),
        "ga": wi[:, o_ga:o_ga + D_MODEL], "gb": wi[:, o_ga + D_MODEL:],
    }
    nea = jnp.pad(-jnp.exp(a_log[0].astype(F32)), (0, LANES - N_HEADS)).reshape(1, LANES)
    dtb = jnp.pad(dt_bias[0].astype(F32), (0, LANES - N_HEADS)).reshape(1, LANES)
    w_post = {
        "pa": w_pa[0].astype(BF16), "pb": w_pb[0].astype(BF16), "o": w_o[0].astype(BF16),
        "router_t": w_router[0].T.astype(BF16),
        "s_gu": jnp.concatenate([ws_gate[0], ws_up[0]], axis=1).astype(BF16),
        "s_down": ws_down[0].astype(BF16),
    }
    ebias_col = e_bias[0].astype(F32).reshape(N_EXPERTS, 1)
    g_mix1, g_ffn1, g_fin1 = row1(g_mix[0]), row1(g_ffn[0]), row1(g_final)
    cdw, cdb, clg, clb = conv_dw[0], row1(conv_dw_b[0]), row1(conv_ln_g[0]), row1(conv_ln_b[0])
    scw, dng = short_conv_w[0], row1(dn_norm_g[0])

    ada = _ada(jnp.concatenate([c_prompt, c_sample], axis=0), w_ada[0].astype(BF16), b_ada[0])
    ada = ada.reshape(nb + ns, 6, D_MODEL)
    mod_p = [ada[:nb, m].reshape(nb, 1, D_MODEL) for m in range(6)]
    mod_s = [ada[nb:, m].reshape(1, ns, D_MODEL) for m in range(6)]
    shift1, scale1, gate1, shift2, scale2, gate2 = range(6)

    tm_p = _pick_tile(seq, 256)
    tpm_p = seq // tm_p
    xp = x_prompt.reshape(tp, D_MODEL)
    u_p, qkv_p, zs_p, ab_p, sga_p, sgb_p = _in_proj(
        xp, mod_p[scale1], mod_p[shift1], g_mix1, w_proj, nea, dtb, tm=tm_p, tiles_per_mod=tpm_p)
    tl = _pick_tile(seq, 256)
    acta_p = _conv_branch(u_p, jnp.zeros((nb, CONV_HALO, CONV_CH), F32), cdw, cdb, clg, clb,
                          tl=tl, tiles_per_batch=seq // tl)
    tc = _pick_tile(seq, 2 * CHUNK)
    actb_p, s_new_p = _delta_branch(
        qkv_p, jnp.zeros((nb, SC_HALO, 3 * DN_WIDTH), F32),
        jnp.zeros((nb, N_HEADS, HEAD_DIM, HEAD_DIM), F32), scw, ab_p, zs_p, dng,
        tc=tc, tiles_per_batch=seq // tc)

    xs_ = x_sample.reshape(ns, D_MODEL)
    u_s, qkv_s, zs_s, ab_s, sga_s, sgb_s = _in_proj(
        xs_, mod_s[scale1], mod_s[shift1], g_mix1, w_proj, nea, dtb, tm=ns, tiles_per_mod=1)
    bt = SUBLANES
    acta_s, nch_s, qkvp_s, nsh_s = _step_front(
        u_s, qkv_s, state_conv[0].reshape(ns, -1), state_short_conv[0].reshape(ns, -1),
        cdw, cdb, clg, clb, scw, bt=bt)
    actb_s, s_new_s = _step_delta(qkvp_s, ab_s, zs_s, state_delta[0].astype(F32), dng, bt=bt)

    cnt0 = jnp.zeros((N_EXPERTS, LANES), F32)
    xsh_p, h2_p, eidx_p, ew_p, rank_p, cnt_p = _post(
        acta_p, actb_p, sga_p, sgb_p, xp, mod_p[gate1], mod_p[scale2], mod_p[shift2], mod_p[gate2],
        cnt0, g_ffn1, w_post, ebias_col, tm=tm_p, tiles_per_mod=tpm_p)
    xsh_s, h2_s, eidx_s, ew_s, rank_s, cnt_all = _post(
        acta_s, actb_s, sga_s, sgb_s, xs_, mod_s[gate1], mod_s[scale2], mod_s[shift2], mod_s[gate2],
        cnt_p, g_ffn1, w_post, ebias_col, tm=ns, tiles_per_mod=1)

    bm = 256
    tmd = LANES
    t_all = tp + ns
    n_rows = t_all * TOP_K
    assert t_all % tmd == 0 and n_rows % bm == 0
    n_blocks = n_rows // bm
    counts = cnt_all[:, 0].astype(I32)
    cnt_end = jnp.cumsum(counts)
    cnt_start = cnt_end - counts
    e_idx = jnp.concatenate([eidx_p, eidx_s], axis=1)
    rank = jnp.concatenate([rank_p, rank_s], axis=1)
    dest = (cnt_start[e_idx] + rank).T
    dest_tiles = dest.reshape(t_all // tmd, 1, tmd * TOP_K)
    seg_start = jnp.sort(jnp.concatenate([jnp.arange(n_blocks, dtype=I32) * bm, cnt_start]))
    seg_end = jnp.concatenate([seg_start[1:], jnp.full((1,), n_rows, I32)])
    seg_blk = jnp.minimum(seg_start // bm, n_blocks - 1)
    seg_lo = seg_start - seg_blk * bm
    seg_hi = seg_end - seg_blk * bm
    seg_exp = jnp.minimum(jnp.searchsorted(cnt_end, seg_start, side="right"), N_EXPERTS - 1).astype(I32)
    seg_exp = lax.cummax(jnp.where(seg_hi > seg_lo, seg_exp, 0))

    h2_all = jnp.concatenate([h2_p, h2_s], axis=0)
    xs_sorted = _dispatch(h2_all, dest_tiles, n_rows, tm=tmd)
    ys = _experts(xs_sorted, w_gate[0], w_up[0], w_down[0], seg_blk, seg_exp, seg_lo, seg_hi, bm=bm)

    ew_tok = jnp.concatenate([ew_p, ew_s], axis=1).T
    gate2_rows = jnp.concatenate(
        [jnp.broadcast_to(mod_p[gate2], (nb, seq, D_MODEL)).reshape(tp, D_MODEL), mod_s[gate2][0]], axis=0)
    y_all = _final(dest_tiles, ew_tok, jnp.concatenate([xsh_p, xsh_s], axis=0), gate2_rows, g_fin1, ys, tm=tmd)

    hist_rows = CONV_WIDTH - 1
    sc_rows = SHORT_CONV - 1
    y_prompt = y_all[:tp].reshape(nb, seq, D_MODEL)
    y_sample = y_all[tp:].reshape(ns, 1, D_MODEL)
    new_conv_prompt = u_p.reshape(nb, seq, CONV_CH)[:, seq - hist_rows:][None]
    new_sc_prompt = qkv_p.reshape(nb, seq, 3 * DN_WIDTH)[:, seq - sc_rows:][None]
    new_conv_sample = nch_s.reshape(ns, hist_rows, CONV_CH)[None]
    new_sc_sample = nsh_s.reshape(ns, sc_rows, 3 * DN_WIDTH)[None]
    return (y_prompt, y_sample, new_conv_prompt, new_sc_prompt, s_new_p[None],
            new_conv_sample, new_sc_sample, s_new_s[None])
```

```python
import functools

import jax
import jax.numpy as jnp
from jax import lax
from jax.experimental import pallas as pl
from jax.experimental.pallas import tpu as pltpu

F32 = jnp.float32
BF16 = jnp.bfloat16
I32 = jnp.int32
U32 = jnp.uint32

EPS = 1e-6
D_MODEL = 1024
CONV_CH = 512
CONV_WIDTH = 31
N_HEADS = 4
HEAD_DIM = 128
DN_WIDTH = N_HEADS * HEAD_DIM
SHORT_CONV = 4
CHUNK = 64
CHUNK_SHIFT = CHUNK.bit_length() - 1
assert 1 << CHUNK_SHIFT == CHUNK
SUB = 16
SUB_SHIFT = SUB.bit_length() - 1
assert 1 << SUB_SHIFT == SUB and CHUNK % SUB == 0
N_EXPERTS = 256
TOP_K = 8
N_GROUPS = 8
GROUP_SIZE = N_EXPERTS // N_GROUPS
TOPK_GROUPS = 4
D_EXPERT = 256
D_SHARED = 256
ROUTE_SCALE = 2.5

LANES = 128
SUBLANES = 8
CONV_HALO = 32
SC_HALO = 8
VMEM_LIMIT = 48 * 1024 * 1024

NT_DIMS = (((1,), (1,)), ((), ()))
TN_DIMS = (((0,), (0,)), ((), ()))


def _cparams(n_grid_dims=1):
    return pltpu.CompilerParams(
        dimension_semantics=("arbitrary",) * n_grid_dims,
        vmem_limit_bytes=VMEM_LIMIT)


def _sigmoid(x):
    return jax.nn.sigmoid(x)


def _silu(x):
    return x * jax.nn.sigmoid(x)


def _bdot(a, b):
    return jnp.dot(a.astype(BF16), b.astype(BF16), preferred_element_type=F32)


def _pack_bf16_pairs(x):
    half = x.shape[1] // 2
    bits = lax.bitcast_convert_type(x.astype(BF16).astype(F32), U32)
    return bits[:, half:] | (bits[:, :half] >> 16)


def _unpack_bf16_pairs(w):
    low = lax.bitcast_convert_type(w << 16, F32).astype(BF16)
    high = lax.bitcast_convert_type(w & jnp.uint32(0xFFFF0000), F32).astype(BF16)
    return low, high


def _dot3(a, b):
    a_hi = a.astype(BF16)
    b_hi = b.astype(BF16)
    a_lo = (a - a_hi.astype(F32)).astype(BF16)
    b_lo = (b - b_hi.astype(F32)).astype(BF16)
    dot = lambda x, y: jnp.dot(x, y, preferred_element_type=F32)
    return dot(a_hi, b_hi) + (dot(a_hi, b_lo) + dot(a_lo, b_hi))


def _dot_exact_lhs(a_bf, b):
    hi = b.astype(BF16)
    r1 = b - hi.astype(F32)
    mid = r1.astype(BF16)
    low = (r1 - mid.astype(F32)).astype(BF16)
    dot = lambda p: jnp.dot(a_bf, p, preferred_element_type=F32)
    return dot(hi) + dot(mid) + dot(low)


def _ada_kernel(c_ref, w_ref, b_ref, o_ref):
    o_ref[...] = _bdot(_silu(c_ref[...]), w_ref[...]) + b_ref[...]


def _ada(c_all, w_ada_bf, b_ada):
    n = c_all.shape[0]
    width = w_ada_bf.shape[1]
    tn = 1536
    return pl.pallas_call(
        _ada_kernel,
        grid=(width // tn,),
        in_specs=[pl.BlockSpec((n, D_MODEL), lambda j: (0, 0)),
                  pl.BlockSpec((D_MODEL, tn), lambda j: (0, j)),
                  pl.BlockSpec((1, tn), lambda j: (0, j))],
        out_specs=pl.BlockSpec((n, tn), lambda j: (0, j)),
        out_shape=jax.ShapeDtypeStruct((n, width), F32),
        compiler_params=_cparams(),
        name="ada",
    )(c_all, w_ada_bf, b_ada.reshape(1, width))


def _in_proj_kernel(x_ref, scale_ref, shift_ref, g_ref, wglu_ref, wqkv_ref, wz_ref, wab_ref,
                    wga_ref, wgb_ref, nea_ref, dtb_ref,
                    u_ref, qkv_ref, zs_ref, ab_ref, sga_ref, sgb_ref):
    x = x_ref[...]
    y = x * lax.rsqrt(jnp.mean(x * x, axis=-1, keepdims=True) + EPS) * g_ref[...]
    h = (y * (1.0 + scale_ref[...]) + shift_ref[...]).astype(BF16)
    glu = jnp.dot(h, wglu_ref[...], preferred_element_type=F32)
    u_ref[...] = glu[:, :CONV_CH] * _sigmoid(glu[:, CONV_CH:])
    qkv_ref[...] = jnp.dot(h, wqkv_ref[...], preferred_element_type=F32)
    zs_ref[...] = _silu(jnp.dot(h, wz_ref[...], preferred_element_type=F32))
    ab = jnp.dot(h, wab_ref[...], preferred_element_type=F32)
    sp = ab + dtb_ref[...]
    softplus = jnp.maximum(sp, 0.0) + jnp.log1p(jnp.exp(-jnp.abs(sp)))
    lane = lax.broadcasted_iota(I32, ab.shape, 1)
    ab_ref[...] = jnp.where(lane < N_HEADS, nea_ref[...] * softplus, _sigmoid(ab))
    sga_ref[...] = _sigmoid(jnp.dot(h, wga_ref[...], preferred_element_type=F32))
    sgb_ref[...] = _sigmoid(jnp.dot(h, wgb_ref[...], preferred_element_type=F32))


def _mod_spec(mod, tiles_per_mod):
    return pl.BlockSpec((None,) + mod.shape[1:], lambda i: (i // tiles_per_mod, 0, 0))


def _const_spec(a):
    nd = a.ndim
    return pl.BlockSpec(a.shape, lambda i: (0,) * nd)


def _in_proj(x2d, scale, shift, g_mix, w, nea, dtb, *, tm, tiles_per_mod):
    t = x2d.shape[0]
    row = lambda c: pl.BlockSpec((tm, c), lambda i: (i, 0))
    consts = [g_mix, w["glu"], w["qkv"], w["z"], w["ab"], w["ga"], w["gb"], nea, dtb]
    widths = [CONV_CH, 3 * DN_WIDTH, DN_WIDTH, LANES, D_MODEL, D_MODEL]
    return pl.pallas_call(
        _in_proj_kernel,
        grid=(t // tm,),
        in_specs=[row(D_MODEL), _mod_spec(scale, tiles_per_mod), _mod_spec(shift, tiles_per_mod)]
                 + [_const_spec(a) for a in consts],
        out_specs=[row(c) for c in widths],
        out_shape=[jax.ShapeDtypeStruct((t, c), F32) for c in widths],
        compiler_params=_cparams(),
        name="in_proj",
    )(x2d, scale, shift, *consts)


def _layer_norm_swish(y, g, b):
    mu = jnp.mean(y, axis=-1, keepdims=True)
    yc = y - mu
    var = jnp.mean(yc * yc, axis=-1, keepdims=True)
    return _silu(yc * lax.rsqrt(var + EPS) * g + b)


def _conv_kernel(u_ref, halo_ref, hist_ref, w_ref, b_ref, lg_ref, lb_ref, o_ref, win_ref,
                 *, tl, tiles_per_batch, rows_per_pass):
    i = pl.program_id(0)
    first = (i % tiles_per_batch) == 0

    @pl.when(first)
    def _():
        win_ref[0:CONV_HALO, :] = hist_ref[...]

    @pl.when(jnp.logical_not(first))
    def _():
        win_ref[0:CONV_HALO, :] = halo_ref[...]

    win_ref[CONV_HALO:CONV_HALO + tl, :] = u_ref[...]
    tap0 = CONV_HALO - (CONV_WIDTH - 1)
    for r in range(tl // rows_per_pass):
        base = r * rows_per_pass
        acc = jnp.zeros((rows_per_pass, CONV_CH), F32)
        for j in range(CONV_WIDTH):
            acc = acc + w_ref[j:j + 1, :] * win_ref[base + tap0 + j:base + tap0 + j + rows_per_pass, :]
        act = _layer_norm_swish(acc + b_ref[...], lg_ref[...], lb_ref[...])
        o_ref[base:base + rows_per_pass, :] = act.astype(o_ref.dtype)


def _conv_branch(u2d, hist, conv_dw, conv_dw_b, ln_g, ln_b, *, tl, tiles_per_batch):
    t = u2d.shape[0]
    halo_blocks = tl // CONV_HALO
    consts = [conv_dw, conv_dw_b, ln_g, ln_b]
    return pl.pallas_call(
        functools.partial(_conv_kernel, tl=tl, tiles_per_batch=tiles_per_batch, rows_per_pass=64),
        grid=(t // tl,),
        in_specs=[pl.BlockSpec((tl, CONV_CH), lambda i: (i, 0)),
                  pl.BlockSpec((CONV_HALO, CONV_CH), lambda i: (jnp.maximum(i * halo_blocks - 1, 0), 0)),
                  pl.BlockSpec((None, CONV_HALO, CONV_CH), lambda i: (i // tiles_per_batch, 0, 0))]
                 + [_const_spec(a) for a in consts],
        out_specs=pl.BlockSpec((tl, CONV_CH), lambda i: (i, 0)),
        out_shape=jax.ShapeDtypeStruct((t, CONV_CH), BF16),
        scratch_shapes=[pltpu.VMEM((CONV_HALO + tl, CONV_CH), F32)],
        compiler_params=_cparams(),
        name="conv_branch",
    )(u2d, u2d, hist, *consts)


def _l2n(x):
    return x * lax.rsqrt(jnp.sum(x * x, axis=-1, keepdims=True) + 1e-6)


def _gated_head_norm(o, g, zs):
    return o * lax.rsqrt(jnp.mean(o * o, axis=-1, keepdims=True) + EPS) * g * zs


def _delta_kernel(qkv_ref, halo_ref, hist_ref, scw_ref, ab_ref, zs_ref, s0_ref, dng_ref,
                  o_ref, s_ref, win_ref, c_ref, *, tc, tiles_per_batch):
    i = pl.program_id(0)
    first = (i % tiles_per_batch) == 0

    @pl.when(first)
    def _():
        win_ref[0:SC_HALO, :] = hist_ref[...]
        s_ref[...] = s0_ref[...]

    @pl.when(jnp.logical_not(first))
    def _():
        win_ref[0:SC_HALO, :] = halo_ref[...]

    win_ref[SC_HALO:SC_HALO + tc, :] = qkv_ref[...]
    tap0 = SC_HALO - (SHORT_CONV - 1)
    acc = jnp.zeros((tc, 3 * DN_WIDTH), F32)
    for j in range(SHORT_CONV):
        acc = acc + scw_ref[j:j + 1, :] * win_ref[tap0 + j:tap0 + j + tc, :]
    c_ref[...] = _silu(acc)

    row = lax.broadcasted_iota(I32, (tc, tc), 0)
    col = lax.broadcasted_iota(I32, (tc, tc), 1)
    same_chunk = lax.shift_right_logical(row, CHUNK_SHIFT) == lax.shift_right_logical(col, CHUNK_SHIFT)
    incl = jnp.logical_and(same_chunk, row >= col)
    row_sub = lax.shift_right_logical(row, SUB_SHIFT)
    col_sub = lax.shift_right_logical(col, SUB_SHIFT)
    strict_sub = jnp.logical_and(row_sub == col_sub, row > col)
    below_sub = jnp.logical_and(same_chunk, row_sub > col_sub)
    eye = (row == col).astype(F32)
    ab = ab_ref[...]
    gcum = _dot_exact_lhs(incl.astype(BF16), ab)
    gcum_t = gcum.T
    n_ch = tc // CHUNK

    heads = range(N_HEADS)
    q, k, gc, kk, a_qk, rhs, qd = [], [], [], [], [], [], []
    for h in heads:
        lo = h * HEAD_DIM
        q_h = _l2n(c_ref[:, lo:lo + HEAD_DIM]) * (HEAD_DIM ** -0.5)
        k_h = _l2n(c_ref[:, DN_WIDTH + lo:DN_WIDTH + lo + HEAD_DIM])
        v_h = c_ref[:, 2 * DN_WIDTH + lo:2 * DN_WIDTH + lo + HEAD_DIM]
        beta = ab[:, N_HEADS + h:N_HEADS + h + 1]
        gc_h = gcum[:, h:h + 1]
        gr = gcum_t[h:h + 1, :]
        decay = jnp.exp(jnp.where(incl, gc_h - gr, -jnp.inf))
        kb = k_h * beta
        kq = lax.dot_general(jnp.concatenate([kb, q_h], axis=0).astype(BF16), k_h.astype(BF16),
                             NT_DIMS, preferred_element_type=F32)
        e_gc = jnp.exp(gc_h)
        q.append(q_h)
        k.append(k_h)
        gc.append(gc_h)
        kk.append(kq[:tc] * decay)
        a_qk.append(kq[tc:] * decay)
        rhs.append(jnp.concatenate([v_h * beta, kb * e_gc], axis=1))
        qd.append(q_h * e_gc)

    pw = [jnp.where(strict_sub, kk[h], 0.0) for h in heads]
    d_inv = [eye - pw[h] for h in heads]
    for _ in range(SUB_SHIFT - 1):
        pw = [_bdot(pw[h], pw[h]) for h in heads]
        d_inv = [d_inv[h] + _bdot(d_inv[h], pw[h]) for h in heads]
    c_rhs = [_bdot(d_inv[h], rhs[h]) for h in heads]
    m_off = [_bdot(d_inv[h], jnp.where(below_sub, kk[h], 0.0)) for h in heads]
    sol = c_rhs
    for _ in range(CHUNK // SUB - 1):
        sol = [c_rhs[h] - _bdot(m_off[h], sol[h]) for h in heads]

    s = [s_ref[h] for h in heads]
    u_parts = [[] for _ in heads]
    o_parts = [[] for _ in heads]
    for c in range(n_ch):
        r0 = c * CHUNK
        for h in heads:
            u0_c = sol[h][r0:r0 + CHUNK, :HEAD_DIM]
            w_c = sol[h][r0:r0 + CHUNK, HEAD_DIM:]
            wq_s = _bdot(jnp.concatenate([w_c, qd[h][r0:r0 + CHUNK]], axis=0), s[h])
            u_c = u0_c - wq_s[:CHUNK]
            g_last = gcum[r0 + CHUNK - 1:r0 + CHUNK, h:h + 1]
            k_dec = k[h][r0:r0 + CHUNK] * jnp.exp(g_last - gc[h][r0:r0 + CHUNK])
            s[h] = jnp.exp(g_last) * s[h] + lax.dot_general(
                k_dec.astype(BF16), u_c.astype(BF16), TN_DIMS, preferred_element_type=F32)
            u_parts[h].append(u_c)
            o_parts[h].append(wq_s[CHUNK:])
    for h in heads:
        lo = h * HEAD_DIM
        s_ref[h] = s[h]
        o = jnp.concatenate(o_parts[h], axis=0) + _bdot(a_qk[h], jnp.concatenate(u_parts[h], axis=0))
        o_ref[:, lo:lo + HEAD_DIM] = _gated_head_norm(
            o, dng_ref[...], zs_ref[:, lo:lo + HEAD_DIM]).astype(o_ref.dtype)


def _delta_branch(qkv2d, hist, s0, scw, ab, zs, dng, *, tc, tiles_per_batch):
    t = qkv2d.shape[0]
    nb = s0.shape[0]
    halo_blocks = tc // SC_HALO
    wqkv = 3 * DN_WIDTH
    state_spec = pl.BlockSpec((None, N_HEADS, HEAD_DIM, HEAD_DIM), lambda i: (i // tiles_per_batch, 0, 0, 0))
    return pl.pallas_call(
        functools.partial(_delta_kernel, tc=tc, tiles_per_batch=tiles_per_batch),
        grid=(t // tc,),
        in_specs=[pl.BlockSpec((tc, wqkv), lambda i: (i, 0)),
                  pl.BlockSpec((SC_HALO, wqkv), lambda i: (jnp.maximum(i * halo_blocks - 1, 0), 0)),
                  pl.BlockSpec((None, SC_HALO, wqkv), lambda i: (i // tiles_per_batch, 0, 0)),
                  _const_spec(scw),
                  pl.BlockSpec((tc, LANES), lambda i: (i, 0)),
                  pl.BlockSpec((tc, DN_WIDTH), lambda i: (i, 0)),
                  state_spec,
                  _const_spec(dng)],
        out_specs=[pl.BlockSpec((tc, DN_WIDTH), lambda i: (i, 0)), state_spec],
        out_shape=[jax.ShapeDtypeStruct((t, DN_WIDTH), BF16),
                   jax.ShapeDtypeStruct((nb, N_HEADS, HEAD_DIM, HEAD_DIM), F32)],
        scratch_shapes=[pltpu.VMEM((SC_HALO + tc, wqkv), F32), pltpu.VMEM((tc, wqkv), F32)],
        compiler_params=_cparams(),
        name="delta_branch",
    )(qkv2d, qkv2d, hist, scw, ab, zs, s0, dng)


def _step_front_kernel(u_ref, qkv_ref, ch_ref, sh_ref, cw_ref, cb_ref, lg_ref, lb_ref, scw_ref,
                       acta_ref, nch_ref, qkvp_ref, nsh_ref):
    hist_rows = CONV_WIDTH - 1
    u = u_ref[...]
    acc = cw_ref[hist_rows:hist_rows + 1, :] * u
    for j in range(hist_rows):
        acc = acc + cw_ref[j:j + 1, :] * ch_ref[:, j * CONV_CH:(j + 1) * CONV_CH]
    acta_ref[...] = _layer_norm_swish(acc + cb_ref[...], lg_ref[...], lb_ref[...]).astype(acta_ref.dtype)
    nch_ref[:, :(hist_rows - 1) * CONV_CH] = ch_ref[:, CONV_CH:]
    nch_ref[:, (hist_rows - 1) * CONV_CH:] = u

    wq = 3 * DN_WIDTH
    sc_rows = SHORT_CONV - 1
    qkv = qkv_ref[...]
    acc = scw_ref[sc_rows:sc_rows + 1, :] * qkv
    for j in range(sc_rows):
        acc = acc + scw_ref[j:j + 1, :] * sh_ref[:, j * wq:(j + 1) * wq]
    c = _silu(acc)
    for h in range(N_HEADS):
        lo = h * HEAD_DIM
        qkvp_ref[:, lo:lo + HEAD_DIM] = _l2n(c[:, lo:lo + HEAD_DIM]) * (HEAD_DIM ** -0.5)
        qkvp_ref[:, DN_WIDTH + lo:DN_WIDTH + lo + HEAD_DIM] = _l2n(c[:, DN_WIDTH + lo:DN_WIDTH + lo + HEAD_DIM])
    qkvp_ref[:, 2 * DN_WIDTH:] = c[:, 2 * DN_WIDTH:]
    nsh_ref[:, :(sc_rows - 1) * wq] = sh_ref[:, wq:]
    nsh_ref[:, (sc_rows - 1) * wq:] = qkv


def _step_front(u, qkv, conv_hist_flat, sc_hist_flat, conv_dw, conv_dw_b, ln_g, ln_b, scw, *, bt):
    n = u.shape[0]
    wq = 3 * DN_WIDTH
    consts = [conv_dw, conv_dw_b, ln_g, ln_b, scw]
    row = lambda c: pl.BlockSpec((bt, c), lambda i: (i, 0))
    widths = [CONV_CH, conv_hist_flat.shape[1], wq, sc_hist_flat.shape[1]]
    dtypes = [BF16, F32, F32, F32]
    return pl.pallas_call(
        _step_front_kernel,
        grid=(n // bt,),
        in_specs=[row(CONV_CH), row(wq), row(widths[1]), row(widths[3])] + [_const_spec(a) for a in consts],
        out_specs=[row(c) for c in widths],
        out_shape=[jax.ShapeDtypeStruct((n, c), d) for c, d in zip(widths, dtypes)],
        compiler_params=_cparams(),
        name="step_front",
    )(u, qkv, conv_hist_flat, sc_hist_flat, *consts)


def _rows_to_cols(x):
    pad = jnp.zeros((LANES - x.shape[0], LANES), x.dtype)
    return jnp.concatenate([x, pad], axis=0).T


def _step_delta_kernel(qkvp_ref, ab_ref, zs_ref, s_ref, dng_ref, o_ref, sn_ref, *, bt):
    ab = ab_ref[...]
    for h in range(N_HEADS):
        lo = h * HEAD_DIM
        q = qkvp_ref[:, lo:lo + HEAD_DIM]
        k = qkvp_ref[:, DN_WIDTH + lo:DN_WIDTH + lo + HEAD_DIM]
        v = qkvp_ref[:, 2 * DN_WIDTH + lo:2 * DN_WIDTH + lo + HEAD_DIM]
        q_cols = _rows_to_cols(q)
        k_cols = _rows_to_cols(k)
        qk = jnp.sum(q * k, axis=-1, keepdims=True)
        alpha = jnp.exp(ab[:, h:h + 1])
        beta = ab[:, N_HEADS + h:N_HEADS + h + 1]
        for j in range(bt):
            s = s_ref[j, h]
            kc = k_cols[:, j:j + 1]
            qc = q_cols[:, j:j + 1]
            a = alpha[j:j + 1, :]
            s_k = jnp.sum(s * kc, axis=0, keepdims=True)
            s_q = jnp.sum(s * qc, axis=0, keepdims=True)
            u = beta[j:j + 1, :] * (v[j:j + 1, :] - a * s_k)
            o = a * s_q + qk[j:j + 1, :] * u
            sn_ref[j, h] = a * s + kc * u
            o_ref[j:j + 1, lo:lo + HEAD_DIM] = _gated_head_norm(
                o, dng_ref[...], zs_ref[j:j + 1, lo:lo + HEAD_DIM]).astype(o_ref.dtype)


def _step_delta(qkvp, ab, zs, s0, dng, *, bt):
    n = qkvp.shape[0]
    row = lambda c: pl.BlockSpec((bt, c), lambda i: (i, 0))
    state_spec = pl.BlockSpec((bt, N_HEADS, HEAD_DIM, HEAD_DIM), lambda i: (i, 0, 0, 0))
    return pl.pallas_call(
        functools.partial(_step_delta_kernel, bt=bt),
        grid=(n // bt,),
        in_specs=[row(3 * DN_WIDTH), row(LANES), row(DN_WIDTH), state_spec, _const_spec(dng)],
        out_specs=[row(DN_WIDTH), state_spec],
        out_shape=[jax.ShapeDtypeStruct((n, DN_WIDTH), BF16), jax.ShapeDtypeStruct(s0.shape, F32)],
        compiler_params=_cparams(),
        name="step_delta",
    )(qkvp, ab, zs, s0, dng)


def _post_kernel(acta_ref, actb_ref, sga_ref, sgb_ref, x_ref, gate1_ref, scale2_ref, shift2_ref,
                 gate2_ref, cnt0_ref, gffn_ref, wpa_ref, wpb_ref, wo_ref, wrt_ref, ebias_ref,
                 wsgu_ref, wsd_ref,
                 xs_ref, h2_ref, eidx_ref, ew_ref, rank_ref, cnt_ref, *, tm):
    i = pl.program_id(0)

    @pl.when(i == 0)
    def _():
        cnt_ref[...] = cnt0_ref[...]

    y_a = jnp.dot(acta_ref[...], wpa_ref[...], preferred_element_type=F32)
    y_b = jnp.dot(actb_ref[...], wpb_ref[...], preferred_element_type=F32)
    merged = sga_ref[...] * y_a + sgb_ref[...] * y_b
    mix = _bdot(merged, wo_ref[...])
    x1 = x_ref[...] + gate1_ref[...] * mix
    y = x1 * lax.rsqrt(jnp.mean(x1 * x1, axis=-1, keepdims=True) + EPS) * gffn_ref[...]
    h2 = y * (1.0 + scale2_ref[...]) + shift2_ref[...]
    h2_ref[...] = _pack_bf16_pairs(h2)
    h2b = h2.astype(BF16)

    gu = jnp.dot(h2b, wsgu_ref[...], preferred_element_type=F32)
    shared = _bdot(_silu(gu[:, :D_SHARED]) * gu[:, D_SHARED:], wsd_ref[...])
    xs_ref[...] = x1 + gate2_ref[...] * shared

    logits_t = lax.dot_general(wrt_ref[...], h2b, NT_DIMS, preferred_element_type=F32)
    scores = _sigmoid(logits_t)
    biased = scores + ebias_ref[...]
    neg = -jnp.inf
    big = jnp.int32(1 << 30)
    erow = lax.broadcasted_iota(I32, (N_EXPERTS, tm), 0)

    def first_argmax(vals, rows):
        m = jnp.max(vals, axis=0, keepdims=True)
        return m, jnp.min(jnp.where(vals == m, rows, big), axis=0, keepdims=True)

    group_scores = []
    for g in range(N_GROUPS):
        vals = biased[g * GROUP_SIZE:(g + 1) * GROUP_SIZE, :]
        rows = lax.broadcasted_iota(I32, (GROUP_SIZE, tm), 0) + g * GROUP_SIZE
        m1, i1 = first_argmax(vals, rows)
        m2 = jnp.max(jnp.where(rows == i1, neg, vals), axis=0, keepdims=True)
        group_scores.append(m1 + m2)
    group_sel = [jnp.zeros((1, tm), jnp.bool_)] * N_GROUPS
    for _ in range(TOPK_GROUPS):
        best = functools.reduce(jnp.maximum, group_scores)
        gi = functools.reduce(
            jnp.minimum, [jnp.where(group_scores[g] == best, jnp.int32(g), big) for g in range(N_GROUPS)])
        for g in range(N_GROUPS):
            hit = gi == g
            group_sel[g] = jnp.logical_or(group_sel[g], hit)
            group_scores[g] = jnp.where(hit, neg, group_scores[g])
    cand = jnp.concatenate(
        [jnp.where(group_sel[g], biased[g * GROUP_SIZE:(g + 1) * GROUP_SIZE, :], neg)
         for g in range(N_GROUPS)], axis=0)

    w_rows, hits = [], []
    picked = jnp.zeros((N_EXPERTS, tm), F32)
    for k in range(TOP_K):
        _, ei = first_argmax(cand, erow)
        hit = erow == ei
        eidx_ref[k:k + 1, :] = ei
        w_rows.append(jnp.sum(jnp.where(hit, scores, 0.0), axis=0, keepdims=True))
        hits.append(hit)
        picked = jnp.where(hit, 1.0, picked)
        cand = jnp.where(hit, neg, cand)
    w_sum = functools.reduce(jnp.add, w_rows)
    for k in range(TOP_K):
        ew_ref[k:k + 1, :] = w_rows[k] / w_sum * ROUTE_SCALE

    trow = lax.broadcasted_iota(I32, (tm, tm), 0)
    tcol = lax.broadcasted_iota(I32, (tm, tm), 1)
    before = (trow < tcol).astype(BF16)
    prefix = jnp.dot(picked.astype(BF16), before, preferred_element_type=F32) + cnt_ref[:, 0:1]
    for k in range(TOP_K):
        rank_ref[k:k + 1, :] = jnp.sum(jnp.where(hits[k], prefix, 0.0), axis=0, keepdims=True).astype(I32)
    cnt_ref[...] = cnt_ref[...] + jnp.sum(picked, axis=1, keepdims=True)


def _post(acta, actb, sga, sgb, x2d, gate1, scale2, shift2, gate2, cnt0, g_ffn, w, ebias_col,
          *, tm, tiles_per_mod):
    t = x2d.shape[0]
    row = lambda c: pl.BlockSpec((tm, c), lambda i: (i, 0))
    col = lambda r: pl.BlockSpec((r, tm), lambda i: (0, i))
    mods = [gate1, scale2, shift2, gate2]
    consts = [cnt0, g_ffn, w["pa"], w["pb"], w["o"], w["router_t"], ebias_col, w["s_gu"], w["s_down"]]
    return pl.pallas_call(
        functools.partial(_post_kernel, tm=tm),
        grid=(t // tm,),
        in_specs=[row(CONV_CH), row(DN_WIDTH), row(D_MODEL), row(D_MODEL), row(D_MODEL)]
                 + [_mod_spec(m, tiles_per_mod) for m in mods] + [_const_spec(a) for a in consts],
        out_specs=[row(D_MODEL), row(D_MODEL // 2), col(TOP_K), col(TOP_K), col(TOP_K), _const_spec(cnt0)],
        out_shape=[jax.ShapeDtypeStruct((t, D_MODEL), F32), jax.ShapeDtypeStruct((t, D_MODEL // 2), U32),
                   jax.ShapeDtypeStruct((TOP_K, t), I32), jax.ShapeDtypeStruct((TOP_K, t), F32),
                   jax.ShapeDtypeStruct((TOP_K, t), I32), jax.ShapeDtypeStruct(cnt0.shape, F32)],
        compiler_params=_cparams(),
        name="post_mixer",
    )(acta, actb, sga, sgb, x2d, *mods, *consts)


def _dest_kernel(start_ref, eidx_ref, rank_ref, o_ref, *, width):
    step = SUBLANES * LANES
    for c0 in range(0, width, step):
        c1 = min(c0 + step, width)
        e = eidx_ref[:, c0:c1]
        base = lax.fori_loop(0, N_EXPERTS, lambda x, acc: jnp.where(e == x, start_ref[x], acc),
                             jnp.zeros(e.shape, I32), unroll=8)
        o_ref[:, c0:c1] = base + rank_ref[:, c0:c1]


def _dest(cnt_start, e_idx, rank):
    k, t = e_idx.shape
    full = pl.BlockSpec((k, t), lambda i, st: (0, 0))
    return pl.pallas_call(
        functools.partial(_dest_kernel, width=t),
        grid_spec=pltpu.PrefetchScalarGridSpec(
            num_scalar_prefetch=1, grid=(1,), in_specs=[full, full], out_specs=full),
        out_shape=jax.ShapeDtypeStruct((k, t), I32),
        compiler_params=_cparams(),
        name="moe_dest",
    )(cnt_start, e_idx, rank)


def _row_copy(src_ref, src_row, dst_ref, dst_row, sem):
    return pltpu.make_async_copy(src_ref.at[pl.ds(src_row, 1), :], dst_ref.at[pl.ds(dst_row, 1), :], sem)


def _dispatch_kernel(dest_ref, hp_ref, hs_ref, xs_hbm, sem, *, tm, n_prompt_tiles):
    def scatter_rows(h_ref):
        def issue(r, carry):
            for k in range(TOP_K):
                _row_copy(h_ref, r, xs_hbm, dest_ref[0, 0, k * tm + r], sem).start()
            return carry

        lax.fori_loop(0, tm, issue, 0)

        def drain(r, carry):
            for k in range(TOP_K):
                _row_copy(h_ref, r, xs_hbm, dest_ref[0, 0, k * tm + r], sem).wait()
            return carry

        lax.fori_loop(0, tm, drain, 0)

    is_prompt = pl.program_id(0) < n_prompt_tiles
    pl.when(is_prompt)(lambda: scatter_rows(hp_ref))
    pl.when(jnp.logical_not(is_prompt))(lambda: scatter_rows(hs_ref))


def _two_group_specs(tm, width, n_prompt_tiles):
    return [pl.BlockSpec((tm, width), lambda i: (jnp.minimum(i, n_prompt_tiles - 1), 0)),
            pl.BlockSpec((tm, width), lambda i: (jnp.maximum(i - n_prompt_tiles, 0), 0))]


def _dispatch(h2_p, h2_s, dest_tiles, n_rows, *, tm):
    n_prompt_tiles = h2_p.shape[0] // tm
    n_tiles = n_prompt_tiles + h2_s.shape[0] // tm
    return pl.pallas_call(
        functools.partial(_dispatch_kernel, tm=tm, n_prompt_tiles=n_prompt_tiles),
        grid=(n_tiles,),
        in_specs=[pl.BlockSpec((1, 1, tm * TOP_K), lambda i: (i, 0, 0), memory_space=pltpu.SMEM)]
                 + _two_group_specs(tm, h2_p.shape[1], n_prompt_tiles),
        out_specs=pl.BlockSpec(memory_space=pl.ANY),
        out_shape=jax.ShapeDtypeStruct((n_rows, h2_p.shape[1]), h2_p.dtype),
        scratch_shapes=[pltpu.SemaphoreType.DMA],
        compiler_params=_cparams(),
        name="moe_dispatch",
    )(dest_tiles, h2_p, h2_s)


def _experts_kernel(sblk_ref, sexp_ref, slo_ref, shi_ref, x_ref, wg_ref, wu_ref, wd_ref, y_ref,
                    wgu_s, wd_s, loaded_ref, *, bm):
    p = pl.program_id(0)
    lo = slo_ref[p]
    expert = sexp_ref[p]

    @pl.when(p == 0)
    def _():
        loaded_ref[0] = -1

    @pl.when(shi_ref[p] > lo)
    def _():
        @pl.when(loaded_ref[0] != expert)
        def _():
            wgu_s[:, :D_EXPERT] = wg_ref[...].astype(BF16)
            wgu_s[:, D_EXPERT:] = wu_ref[...].astype(BF16)
            wd_s[...] = wd_ref[...].astype(BF16)
            loaded_ref[0] = expert

        x_low, x_high = _unpack_bf16_pairs(x_ref[...])
        half = D_MODEL // 2
        gu = (jnp.dot(x_low, wgu_s[:half, :], preferred_element_type=F32)
              + jnp.dot(x_high, wgu_s[half:, :], preferred_element_type=F32))
        hb = (_silu(gu[:, :D_EXPERT]) * gu[:, D_EXPERT:]).astype(BF16)
        y = jnp.dot(hb, wd_s[...], preferred_element_type=F32)

        @pl.when(lo == 0)
        def _():
            y_ref[...] = y

        @pl.when(lo > 0)
        def _():
            rows = lax.broadcasted_iota(I32, (bm, 1), 0)
            y_ref[...] = jnp.where(rows >= lo, y, y_ref[...])


def _experts(xs, w_gate, w_up, w_down, seg_blk, seg_exp, seg_lo, seg_hi, *, bm):
    n_rows = xs.shape[0]
    n_seg = seg_blk.shape[0]
    grid_spec = pltpu.PrefetchScalarGridSpec(
        num_scalar_prefetch=4,
        grid=(n_seg,),
        in_specs=[pl.BlockSpec((bm, D_MODEL // 2), lambda p, sb, se, sl, sh: (sb[p], 0)),
                  pl.BlockSpec((None, D_MODEL, D_EXPERT), lambda p, sb, se, sl, sh: (se[p], 0, 0)),
                  pl.BlockSpec((None, D_MODEL, D_EXPERT), lambda p, sb, se, sl, sh: (se[p], 0, 0)),
                  pl.BlockSpec((None, D_EXPERT, D_MODEL), lambda p, sb, se, sl, sh: (se[p], 0, 0))],
        out_specs=pl.BlockSpec((bm, D_MODEL), lambda p, sb, se, sl, sh: (sb[p], 0)),
        scratch_shapes=[pltpu.VMEM((D_MODEL, 2 * D_EXPERT), BF16), pltpu.VMEM((D_EXPERT, D_MODEL), BF16),
                        pltpu.SMEM((1,), I32)],
    )
    return pl.pallas_call(
        functools.partial(_experts_kernel, bm=bm),
        grid_spec=grid_spec,
        out_shape=jax.ShapeDtypeStruct((n_rows, D_MODEL), F32),
        compiler_params=_cparams(),
        name="moe_experts",
    )(seg_blk, seg_exp, seg_lo, seg_hi, xs, w_gate, w_up, w_down)


def _final_kernel(dest_ref, ew_ref, xsp_ref, xss_ref, g2p_ref, g2s_ref, gfin_ref, ys_hbm,
                  op_ref, os_ref, buf_ref, sem, *, tm, n_prompt_tiles):
    def issue(r, carry):
        for k in range(TOP_K):
            _row_copy(ys_hbm, dest_ref[0, 0, k * tm + r], buf_ref.at[k], r, sem).start()
        return carry

    lax.fori_loop(0, tm, issue, 0)

    def drain(r, carry):
        for k in range(TOP_K):
            _row_copy(ys_hbm, dest_ref[0, 0, k * tm + r], buf_ref.at[k], r, sem).wait()
        return carry

    lax.fori_loop(0, tm, drain, 0)

    ew = ew_ref[...]
    routed = ew[:, 0:1] * buf_ref[0]
    for k in range(1, TOP_K):
        routed = routed + ew[:, k:k + 1] * buf_ref[k]

    def finish(xs_ref, gate2_ref, o_ref):
        x2 = xs_ref[...] + gate2_ref[...] * routed
        o_ref[...] = x2 * lax.rsqrt(jnp.mean(x2 * x2, axis=-1, keepdims=True) + EPS) * gfin_ref[...]

    is_prompt = pl.program_id(0) < n_prompt_tiles
    pl.when(is_prompt)(lambda: finish(xsp_ref, g2p_ref, op_ref))
    pl.when(jnp.logical_not(is_prompt))(lambda: finish(xss_ref, g2s_ref, os_ref))


def _final(dest_tiles, ew_tok, xs_p, xs_s, gate2_p, gate2_s, g_final, ys, *, tm, tiles_per_batch):
    n_prompt_tiles = xs_p.shape[0] // tm
    n_tiles = n_prompt_tiles + xs_s.shape[0] // tm
    nb = gate2_p.shape[0]
    return pl.pallas_call(
        functools.partial(_final_kernel, tm=tm, n_prompt_tiles=n_prompt_tiles),
        grid=(n_tiles,),
        in_specs=[pl.BlockSpec((1, 1, tm * TOP_K), lambda i: (i, 0, 0), memory_space=pltpu.SMEM),
                  pl.BlockSpec((tm, TOP_K), lambda i: (i, 0))]
                 + _two_group_specs(tm, D_MODEL, n_prompt_tiles)
                 + [pl.BlockSpec((None, 1, D_MODEL), lambda i: (jnp.minimum(i // tiles_per_batch, nb - 1), 0, 0)),
                    pl.BlockSpec((None, tm, D_MODEL), lambda i: (jnp.maximum(i - n_prompt_tiles, 0), 0, 0)),
                    _const_spec(g_final),
                    pl.BlockSpec(memory_space=pl.ANY)],
        out_specs=_two_group_specs(tm, D_MODEL, n_prompt_tiles),
        out_shape=[jax.ShapeDtypeStruct(xs_p.shape, F32), jax.ShapeDtypeStruct(xs_s.shape, F32)],
        scratch_shapes=[pltpu.VMEM((TOP_K, tm, D_MODEL), F32), pltpu.SemaphoreType.DMA],
        compiler_params=_cparams(),
        name="moe_combine_final",
    )(dest_tiles, ew_tok, xs_p, xs_s, gate2_p, gate2_s, g_final, ys)


def _pick_tile(n, preferred):
    t = min(n, preferred)
    assert n % t == 0, (n, t)
    return t


def kernel(x_prompt, x_sample, c_prompt, c_sample, state_conv, state_short_conv, state_delta, w_ada, b_ada, g_mix, w_in, conv_dw, conv_dw_b, conv_ln_g, conv_ln_b, w_pa, short_conv_w, a_log, dt_bias, dn_norm_g, w_pb, w_o, g_ffn, w_router, e_bias, w_gate, w_up, w_down, ws_gate, ws_up, ws_down, g_final):
    nb, seq, d = x_prompt.shape
    ns = x_sample.shape[0]
    assert d == D_MODEL and w_ada.shape[0] == 1 and x_sample.shape[1] == 1
    assert seq % CHUNK == 0 and seq >= CONV_HALO and ns % SUBLANES == 0
    tp = nb * seq
    row1 = lambda a: a.reshape(1, -1)

    wi = w_in[0].astype(BF16)
    o_q = 2 * CONV_CH
    o_z = o_q + 3 * DN_WIDTH
    o_a = o_z + DN_WIDTH
    o_ga = o_a + 2 * N_HEADS
    w_proj = {
        "glu": wi[:, :o_q], "qkv": wi[:, o_q:o_z], "z": wi[:, o_z:o_a],
        "ab": jnp.pad(wi[:, o_a:o_ga], ((0, 0), (0, LANES - 2 * N_HEADS))),
        "ga": wi[:, o_ga:o_ga + D_MODEL], "gb": wi[:, o_ga + D_MODEL:],
    }
    nea = jnp.pad(-jnp.exp(a_log[0].astype(F32)), (0, LANES - N_HEADS)).reshape(1, LANES)
    dtb = jnp.pad(dt_bias[0].astype(F32), (0, LANES - N_HEADS)).reshape(1, LANES)
    w_post = {
        "pa": w_pa[0].astype(BF16), "pb": w_pb[0].astype(BF16), "o": w_o[0].astype(BF16),
        "router_t": w_router[0].T.astype(BF16),
        "s_gu": jnp.concatenate([ws_gate[0], ws_up[0]], axis=1).astype(BF16),
        "s_down": ws_down[0].astype(BF16),
    }
    ebias_col = e_bias[0].astype(F32).reshape(N_EXPERTS, 1)
    g_mix1, g_ffn1, g_fin1 = row1(g_mix[0]), row1(g_ffn[0]), row1(g_final)
    cdw, cdb, clg, clb = conv_dw[0], row1(conv_dw_b[0]), row1(conv_ln_g[0]), row1(conv_ln_b[0])
    scw, dng = short_conv_w[0], row1(dn_norm_g[0])

    ada = _ada(jnp.concatenate([c_prompt, c_sample], axis=0), w_ada[0].astype(BF16), b_ada[0])
    ada = ada.reshape(nb + ns, 6, D_MODEL)
    mod_p = [ada[:nb, m].reshape(nb, 1, D_MODEL) for m in range(6)]
    mod_s = [ada[nb:, m].reshape(1, ns, D_MODEL) for m in range(6)]
    shift1, scale1, gate1, shift2, scale2, gate2 = range(6)

    tm_p = _pick_tile(seq, 256)
    tpm_p = seq // tm_p
    xp = x_prompt.reshape(tp, D_MODEL)
    u_p, qkv_p, zs_p, ab_p, sga_p, sgb_p = _in_proj(
        xp, mod_p[scale1], mod_p[shift1], g_mix1, w_proj, nea, dtb, tm=tm_p, tiles_per_mod=tpm_p)
    tl = _pick_tile(seq, 256)
    acta_p = _conv_branch(u_p, jnp.zeros((nb, CONV_HALO, CONV_CH), F32), cdw, cdb, clg, clb,
                          tl=tl, tiles_per_batch=seq // tl)
    tc = _pick_tile(seq, 4 * CHUNK)
    actb_p, s_new_p = _delta_branch(
        qkv_p, jnp.zeros((nb, SC_HALO, 3 * DN_WIDTH), F32),
        jnp.zeros((nb, N_HEADS, HEAD_DIM, HEAD_DIM), F32), scw, ab_p, zs_p, dng,
        tc=tc, tiles_per_batch=seq // tc)

    xs_ = x_sample.reshape(ns, D_MODEL)
    u_s, qkv_s, zs_s, ab_s, sga_s, sgb_s = _in_proj(
        xs_, mod_s[scale1], mod_s[shift1], g_mix1, w_proj, nea, dtb, tm=ns, tiles_per_mod=1)
    bt = SUBLANES
    acta_s, nch_s, qkvp_s, nsh_s = _step_front(
        u_s, qkv_s, state_conv[0].reshape(ns, -1), state_short_conv[0].reshape(ns, -1),
        cdw, cdb, clg, clb, scw, bt=bt)
    actb_s, s_new_s = _step_delta(qkvp_s, ab_s, zs_s, state_delta[0].astype(F32), dng, bt=bt)

    cnt0 = jnp.zeros((N_EXPERTS, LANES), F32)
    xsh_p, h2_p, eidx_p, ew_p, rank_p, cnt_p = _post(
        acta_p, actb_p, sga_p, sgb_p, xp, mod_p[gate1], mod_p[scale2], mod_p[shift2], mod_p[gate2],
        cnt0, g_ffn1, w_post, ebias_col, tm=tm_p, tiles_per_mod=tpm_p)
    xsh_s, h2_s, eidx_s, ew_s, rank_s, cnt_all = _post(
        acta_s, actb_s, sga_s, sgb_s, xs_, mod_s[gate1], mod_s[scale2], mod_s[shift2], mod_s[gate2],
        cnt_p, g_ffn1, w_post, ebias_col, tm=ns, tiles_per_mod=1)

    bm = 256
    tmd = LANES
    t_all = tp + ns
    n_rows = t_all * TOP_K
    assert seq % tmd == 0 and ns % tmd == 0 and n_rows % bm == 0
    n_blocks = n_rows // bm
    counts = cnt_all[:, 0].astype(I32)
    cnt_end = jnp.cumsum(counts)
    cnt_start = cnt_end - counts
    e_idx = jnp.concatenate([eidx_p, eidx_s], axis=1)
    rank = jnp.concatenate([rank_p, rank_s], axis=1)
    dest = _dest(cnt_start, e_idx, rank)
    dest_tiles = dest.reshape(TOP_K, t_all // tmd, tmd).transpose(1, 0, 2).reshape(t_all // tmd, 1, TOP_K * tmd)
    seg_start = jnp.sort(jnp.concatenate([jnp.arange(n_blocks, dtype=I32) * bm, cnt_start]))
    seg_end = jnp.concatenate([seg_start[1:], jnp.full((1,), n_rows, I32)])
    seg_blk = jnp.minimum(seg_start // bm, n_blocks - 1)
    seg_lo = seg_start - seg_blk * bm
    seg_hi = seg_end - seg_blk * bm
    seg_exp = jnp.minimum(jnp.searchsorted(cnt_end, seg_start, side="right"), N_EXPERTS - 1).astype(I32)
    seg_exp = lax.cummax(jnp.where(seg_hi > seg_lo, seg_exp, 0))

    xs_sorted = _dispatch(h2_p, h2_s, dest_tiles, n_rows, tm=tmd)
    ys = _experts(xs_sorted, w_gate[0], w_up[0], w_down[0], seg_blk, seg_exp, seg_lo, seg_hi, bm=bm)

    ew_tok = jnp.concatenate([ew_p, ew_s], axis=1).T
    y_p, y_s = _final(dest_tiles, ew_tok, xsh_p, xsh_s, mod_p[gate2], mod_s[gate2].reshape(ns // tmd, tmd, D_MODEL),
                      g_fin1, ys, tm=tmd, tiles_per_batch=seq // tmd)

    hist_rows = CONV_WIDTH - 1
    sc_rows = SHORT_CONV - 1
    y_prompt = y_p.reshape(nb, seq, D_MODEL)
    y_sample = y_s.reshape(ns, 1, D_MODEL)
    new_conv_prompt = u_p.reshape(nb, seq, CONV_CH)[:, seq - hist_rows:][None]
    new_sc_prompt = qkv_p.reshape(nb, seq, 3 * DN_WIDTH)[:, seq - sc_rows:][None]
    new_conv_sample = nch_s.reshape(ns, hist_rows, CONV_CH)[None]
    new_sc_sample = nsh_s.reshape(ns, sc_rows, 3 * DN_WIDTH)[None]
    return (y_prompt, y_sample, new_conv_prompt, new_sc_prompt, s_new_p[None],
            new_conv_sample, new_sc_sample, s_new_s[None])
```

```python
import functools

import jax
import jax.numpy as jnp
from jax import lax
from jax.experimental import pallas as pl
from jax.experimental.pallas import tpu as pltpu

F32 = jnp.float32
BF16 = jnp.bfloat16
I32 = jnp.int32
U32 = jnp.uint32

EPS = 1e-6
D_MODEL = 1024
CONV_CH = 512
CONV_WIDTH = 31
N_HEADS = 4
HEAD_DIM = 128
DN_WIDTH = N_HEADS * HEAD_DIM
SHORT_CONV = 4
CHUNK = 64
CHUNK_SHIFT = CHUNK.bit_length() - 1
assert 1 << CHUNK_SHIFT == CHUNK
SUB = 16
SUB_SHIFT = SUB.bit_length() - 1
assert 1 << SUB_SHIFT == SUB and CHUNK % SUB == 0
N_EXPERTS = 256
TOP_K = 8
N_GROUPS = 8
GROUP_SIZE = N_EXPERTS // N_GROUPS
TOPK_GROUPS = 4
D_EXPERT = 256
D_SHARED = 256
ROUTE_SCALE = 2.5

LANES = 128
SUBLANES = 8
CONV_HALO = 32
SC_HALO = 8
VMEM_LIMIT = 48 * 1024 * 1024
N_DMA_PRIORITIES = 2

NT_DIMS = (((1,), (1,)), ((), ()))
TN_DIMS = (((0,), (0,)), ((), ()))


def _cparams(n_grid_dims=1):
    return pltpu.CompilerParams(
        dimension_semantics=("arbitrary",) * n_grid_dims,
        vmem_limit_bytes=VMEM_LIMIT)


def _sigmoid(x):
    return jax.nn.sigmoid(x)


def _silu(x):
    return x * jax.nn.sigmoid(x)


def _bdot(a, b):
    return jnp.dot(a.astype(BF16), b.astype(BF16), preferred_element_type=F32)


def _pack_bf16_pairs(x):
    half = x.shape[1] // 2
    bits = lax.bitcast_convert_type(x.astype(BF16).astype(F32), U32)
    return bits[:, half:] | (bits[:, :half] >> 16)


def _unpack_bf16_pairs(w):
    low = lax.bitcast_convert_type(w << 16, F32).astype(BF16)
    high = lax.bitcast_convert_type(w & jnp.uint32(0xFFFF0000), F32).astype(BF16)
    return low, high


def _dot3(a, b):
    a_hi = a.astype(BF16)
    b_hi = b.astype(BF16)
    a_lo = (a - a_hi.astype(F32)).astype(BF16)
    b_lo = (b - b_hi.astype(F32)).astype(BF16)
    dot = lambda x, y: jnp.dot(x, y, preferred_element_type=F32)
    return dot(a_hi, b_hi) + (dot(a_hi, b_lo) + dot(a_lo, b_hi))


def _dot_exact_lhs(a_bf, b):
    hi = b.astype(BF16)
    r1 = b - hi.astype(F32)
    mid = r1.astype(BF16)
    low = (r1 - mid.astype(F32)).astype(BF16)
    dot = lambda p: jnp.dot(a_bf, p, preferred_element_type=F32)
    return dot(hi) + dot(mid) + dot(low)


def _ada_kernel(c_ref, w_ref, b_ref, o_ref):
    o_ref[...] = _bdot(_silu(c_ref[...]), w_ref[...]) + b_ref[...]


def _ada(c_all, w_ada_bf, b_ada):
    n = c_all.shape[0]
    width = w_ada_bf.shape[1]
    tn = 1536
    return pl.pallas_call(
        _ada_kernel,
        grid=(width // tn,),
        in_specs=[pl.BlockSpec((n, D_MODEL), lambda j: (0, 0)),
                  pl.BlockSpec((D_MODEL, tn), lambda j: (0, j)),
                  pl.BlockSpec((1, tn), lambda j: (0, j))],
        out_specs=pl.BlockSpec((n, tn), lambda j: (0, j)),
        out_shape=jax.ShapeDtypeStruct((n, width), F32),
        compiler_params=_cparams(),
        name="ada",
    )(c_all, w_ada_bf, b_ada.reshape(1, width))


def _in_proj_kernel(x_ref, scale_ref, shift_ref, g_ref, wglu_ref, wqkv_ref, wz_ref, wab_ref,
                    wga_ref, wgb_ref, nea_ref, dtb_ref,
                    u_ref, qkv_ref, zs_ref, ab_ref, sga_ref, sgb_ref):
    x = x_ref[...]
    y = x * lax.rsqrt(jnp.mean(x * x, axis=-1, keepdims=True) + EPS) * g_ref[...]
    h = (y * (1.0 + scale_ref[...]) + shift_ref[...]).astype(BF16)
    glu = jnp.dot(h, wglu_ref[...], preferred_element_type=F32)
    u_ref[...] = glu[:, :CONV_CH] * _sigmoid(glu[:, CONV_CH:])
    qkv_ref[...] = jnp.dot(h, wqkv_ref[...], preferred_element_type=F32)
    zs_ref[...] = _silu(jnp.dot(h, wz_ref[...], preferred_element_type=F32))
    ab = jnp.dot(h, wab_ref[...], preferred_element_type=F32)
    sp = ab + dtb_ref[...]
    softplus = jnp.maximum(sp, 0.0) + jnp.log1p(jnp.exp(-jnp.abs(sp)))
    lane = lax.broadcasted_iota(I32, ab.shape, 1)
    ab_ref[...] = jnp.where(lane < N_HEADS, nea_ref[...] * softplus, _sigmoid(ab))
    sga_ref[...] = _sigmoid(jnp.dot(h, wga_ref[...], preferred_element_type=F32))
    sgb_ref[...] = _sigmoid(jnp.dot(h, wgb_ref[...], preferred_element_type=F32))


def _mod_spec(mod, tiles_per_mod):
    return pl.BlockSpec((None,) + mod.shape[1:], lambda i: (i // tiles_per_mod, 0, 0))


def _const_spec(a):
    nd = a.ndim
    return pl.BlockSpec(a.shape, lambda i: (0,) * nd)


def _in_proj(x2d, scale, shift, g_mix, w, nea, dtb, *, tm, tiles_per_mod):
    t = x2d.shape[0]
    row = lambda c: pl.BlockSpec((tm, c), lambda i: (i, 0))
    consts = [g_mix, w["glu"], w["qkv"], w["z"], w["ab"], w["ga"], w["gb"], nea, dtb]
    widths = [CONV_CH, 3 * DN_WIDTH, DN_WIDTH, LANES, D_MODEL, D_MODEL]
    return pl.pallas_call(
        _in_proj_kernel,
        grid=(t // tm,),
        in_specs=[row(D_MODEL), _mod_spec(scale, tiles_per_mod), _mod_spec(shift, tiles_per_mod)]
                 + [_const_spec(a) for a in consts],
        out_specs=[row(c) for c in widths],
        out_shape=[jax.ShapeDtypeStruct((t, c), F32) for c in widths],
        compiler_params=_cparams(),
        name="in_proj",
    )(x2d, scale, shift, *consts)


def _layer_norm_swish(y, g, b):
    mu = jnp.mean(y, axis=-1, keepdims=True)
    yc = y - mu
    var = jnp.mean(yc * yc, axis=-1, keepdims=True)
    return _silu(yc * lax.rsqrt(var + EPS) * g + b)


def _conv_kernel(u_ref, halo_ref, hist_ref, w_ref, b_ref, lg_ref, lb_ref, o_ref, win_ref, shift_ref,
                 *, tl, tiles_per_batch, rows_per_pass):
    i = pl.program_id(0)
    first = (i % tiles_per_batch) == 0

    @pl.when(first)
    def _():
        win_ref[0:CONV_HALO, :] = hist_ref[...]

    @pl.when(jnp.logical_not(first))
    def _():
        win_ref[0:CONV_HALO, :] = halo_ref[...]

    win_ref[CONV_HALO:CONV_HALO + tl, :] = u_ref[...]
    tap0 = CONV_HALO - (CONV_WIDTH - 1)
    n_shifted = CONV_HALO + tl - SUBLANES
    for b in range(1, SUBLANES):
        shift_ref[b - 1] = win_ref[b:b + n_shifted, :]

    for r in range(tl // rows_per_pass):
        base = r * rows_per_pass
        acc = jnp.zeros((rows_per_pass, CONV_CH), F32)
        for j in range(CONV_WIDTH):
            a, b = divmod(tap0 + j, SUBLANES)
            start = base + a * SUBLANES
            rows = (win_ref[start:start + rows_per_pass, :] if b == 0
                    else shift_ref[b - 1, start:start + rows_per_pass, :])
            acc = acc + w_ref[j:j + 1, :] * rows
        act = _layer_norm_swish(acc + b_ref[...], lg_ref[...], lb_ref[...])
        o_ref[base:base + rows_per_pass, :] = act.astype(o_ref.dtype)


def _conv_branch(u2d, hist, conv_dw, conv_dw_b, ln_g, ln_b, *, tl, tiles_per_batch):
    t = u2d.shape[0]
    halo_blocks = tl // CONV_HALO
    consts = [conv_dw, conv_dw_b, ln_g, ln_b]
    return pl.pallas_call(
        functools.partial(_conv_kernel, tl=tl, tiles_per_batch=tiles_per_batch, rows_per_pass=64),
        grid=(t // tl,),
        in_specs=[pl.BlockSpec((tl, CONV_CH), lambda i: (i, 0)),
                  pl.BlockSpec((CONV_HALO, CONV_CH), lambda i: (jnp.maximum(i * halo_blocks - 1, 0), 0)),
                  pl.BlockSpec((None, CONV_HALO, CONV_CH), lambda i: (i // tiles_per_batch, 0, 0))]
                 + [_const_spec(a) for a in consts],
        out_specs=pl.BlockSpec((tl, CONV_CH), lambda i: (i, 0)),
        out_shape=jax.ShapeDtypeStruct((t, CONV_CH), BF16),
        scratch_shapes=[pltpu.VMEM((CONV_HALO + tl, CONV_CH), F32),
                        pltpu.VMEM((SUBLANES - 1, CONV_HALO + tl - SUBLANES, CONV_CH), F32)],
        compiler_params=_cparams(),
        name="conv_branch",
    )(u2d, u2d, hist, *consts)


def _l2n(x):
    return x * lax.rsqrt(jnp.sum(x * x, axis=-1, keepdims=True) + 1e-6)


def _gated_head_norm(o, g, zs):
    return o * lax.rsqrt(jnp.mean(o * o, axis=-1, keepdims=True) + EPS) * g * zs


def _delta_kernel(qkv_ref, halo_ref, hist_ref, scw_ref, ab_ref, zs_ref, s0_ref, dng_ref,
                  o_ref, s_ref, win_ref, c_ref, *, tc, tiles_per_batch):
    i = pl.program_id(0)
    first = (i % tiles_per_batch) == 0

    @pl.when(first)
    def _():
        win_ref[0:SC_HALO, :] = hist_ref[...]
        s_ref[...] = s0_ref[...]

    @pl.when(jnp.logical_not(first))
    def _():
        win_ref[0:SC_HALO, :] = halo_ref[...]

    win_ref[SC_HALO:SC_HALO + tc, :] = qkv_ref[...]
    tap0 = SC_HALO - (SHORT_CONV - 1)
    acc = jnp.zeros((tc, 3 * DN_WIDTH), F32)
    for j in range(SHORT_CONV):
        acc = acc + scw_ref[j:j + 1, :] * win_ref[tap0 + j:tap0 + j + tc, :]
    c_ref[...] = _silu(acc)

    row = lax.broadcasted_iota(I32, (tc, tc), 0)
    col = lax.broadcasted_iota(I32, (tc, tc), 1)
    same_chunk = lax.shift_right_logical(row, CHUNK_SHIFT) == lax.shift_right_logical(col, CHUNK_SHIFT)
    incl = jnp.logical_and(same_chunk, row >= col)
    row_sub = lax.shift_right_logical(row, SUB_SHIFT)
    col_sub = lax.shift_right_logical(col, SUB_SHIFT)
    strict_sub = jnp.logical_and(row_sub == col_sub, row > col)
    below_sub = jnp.logical_and(same_chunk, row_sub > col_sub)
    eye = (row == col).astype(F32)
    ab = ab_ref[...]
    gcum = _dot_exact_lhs(incl.astype(BF16), ab)
    gcum_t = gcum.T
    n_ch = tc // CHUNK

    heads = range(N_HEADS)
    q, k, gc, kk, a_qk, rhs, qd = [], [], [], [], [], [], []
    for h in heads:
        lo = h * HEAD_DIM
        q_h = _l2n(c_ref[:, lo:lo + HEAD_DIM]) * (HEAD_DIM ** -0.5)
        k_h = _l2n(c_ref[:, DN_WIDTH + lo:DN_WIDTH + lo + HEAD_DIM])
        v_h = c_ref[:, 2 * DN_WIDTH + lo:2 * DN_WIDTH + lo + HEAD_DIM]
        beta = ab[:, N_HEADS + h:N_HEADS + h + 1]
        gc_h = gcum[:, h:h + 1]
        gr = gcum_t[h:h + 1, :]
        decay = jnp.exp(jnp.where(incl, gc_h - gr, -jnp.inf))
        kb = k_h * beta
        kq = lax.dot_general(jnp.concatenate([kb, q_h], axis=0).astype(BF16), k_h.astype(BF16),
                             NT_DIMS, preferred_element_type=F32)
        e_gc = jnp.exp(gc_h)
        q.append(q_h)
        k.append(k_h)
        gc.append(gc_h)
        kk.append(kq[:tc] * decay)
        a_qk.append(kq[tc:] * decay)
        rhs.append(jnp.concatenate([v_h * beta, kb * e_gc], axis=1))
        qd.append(q_h * e_gc)

    pw = [jnp.where(strict_sub, kk[h], 0.0) for h in heads]
    d_inv = [eye - pw[h] for h in heads]
    for _ in range(SUB_SHIFT - 1):
        pw = [_bdot(pw[h], pw[h]) for h in heads]
        d_inv = [d_inv[h] + _bdot(d_inv[h], pw[h]) for h in heads]
    c_rhs = [_bdot(d_inv[h], rhs[h]) for h in heads]
    m_off = [_bdot(d_inv[h], jnp.where(below_sub, kk[h], 0.0)) for h in heads]
    sol = c_rhs
    for _ in range(CHUNK // SUB - 1):
        sol = [c_rhs[h] - _bdot(m_off[h], sol[h]) for h in heads]

    s = [s_ref[h] for h in heads]
    u_parts = [[] for _ in heads]
    o_parts = [[] for _ in heads]
    for c in range(n_ch):
        r0 = c * CHUNK
        for h in heads:
            u0_c = sol[h][r0:r0 + CHUNK, :HEAD_DIM]
            w_c = sol[h][r0:r0 + CHUNK, HEAD_DIM:]
            wq_s = _bdot(jnp.concatenate([w_c, qd[h][r0:r0 + CHUNK]], axis=0), s[h])
            u_c = u0_c - wq_s[:CHUNK]
            g_last = gcum[r0 + CHUNK - 1:r0 + CHUNK, h:h + 1]
            k_dec = k[h][r0:r0 + CHUNK] * jnp.exp(g_last - gc[h][r0:r0 + CHUNK])
            s[h] = jnp.exp(g_last) * s[h] + lax.dot_general(
                k_dec.astype(BF16), u_c.astype(BF16), TN_DIMS, preferred_element_type=F32)
            u_parts[h].append(u_c)
            o_parts[h].append(wq_s[CHUNK:])
    for h in heads:
        lo = h * HEAD_DIM
        s_ref[h] = s[h]
        o = jnp.concatenate(o_parts[h], axis=0) + _bdot(a_qk[h], jnp.concatenate(u_parts[h], axis=0))
        o_ref[:, lo:lo + HEAD_DIM] = _gated_head_norm(
            o, dng_ref[...], zs_ref[:, lo:lo + HEAD_DIM]).astype(o_ref.dtype)


def _delta_branch(qkv2d, hist, s0, scw, ab, zs, dng, *, tc, tiles_per_batch):
    t = qkv2d.shape[0]
    nb = s0.shape[0]
    halo_blocks = tc // SC_HALO
    wqkv = 3 * DN_WIDTH
    state_spec = pl.BlockSpec((None, N_HEADS, HEAD_DIM, HEAD_DIM), lambda i: (i // tiles_per_batch, 0, 0, 0))
    return pl.pallas_call(
        functools.partial(_delta_kernel, tc=tc, tiles_per_batch=tiles_per_batch),
        grid=(t // tc,),
        in_specs=[pl.BlockSpec((tc, wqkv), lambda i: (i, 0)),
                  pl.BlockSpec((SC_HALO, wqkv), lambda i: (jnp.maximum(i * halo_blocks - 1, 0), 0)),
                  pl.BlockSpec((None, SC_HALO, wqkv), lambda i: (i // tiles_per_batch, 0, 0)),
                  _const_spec(scw),
                  pl.BlockSpec((tc, LANES), lambda i: (i, 0)),
                  pl.BlockSpec((tc, DN_WIDTH), lambda i: (i, 0)),
                  state_spec,
                  _const_spec(dng)],
        out_specs=[pl.BlockSpec((tc, DN_WIDTH), lambda i: (i, 0)), state_spec],
        out_shape=[jax.ShapeDtypeStruct((t, DN_WIDTH), BF16),
                   jax.ShapeDtypeStruct((nb, N_HEADS, HEAD_DIM, HEAD_DIM), F32)],
        scratch_shapes=[pltpu.VMEM((SC_HALO + tc, wqkv), F32), pltpu.VMEM((tc, wqkv), F32)],
        compiler_params=_cparams(),
        name="delta_branch",
    )(qkv2d, qkv2d, hist, scw, ab, zs, s0, dng)


def _step_front_kernel(u_ref, qkv_ref, ch_ref, sh_ref, cw_ref, cb_ref, lg_ref, lb_ref, scw_ref,
                       acta_ref, nch_ref, qkvp_ref, nsh_ref):
    hist_rows = CONV_WIDTH - 1
    u = u_ref[...]
    acc = cw_ref[hist_rows:hist_rows + 1, :] * u
    for j in range(hist_rows):
        acc = acc + cw_ref[j:j + 1, :] * ch_ref[:, j * CONV_CH:(j + 1) * CONV_CH]
    acta_ref[...] = _layer_norm_swish(acc + cb_ref[...], lg_ref[...], lb_ref[...]).astype(acta_ref.dtype)
    nch_ref[:, :(hist_rows - 1) * CONV_CH] = ch_ref[:, CONV_CH:]
    nch_ref[:, (hist_rows - 1) * CONV_CH:] = u

    wq = 3 * DN_WIDTH
    sc_rows = SHORT_CONV - 1
    qkv = qkv_ref[...]
    acc = scw_ref[sc_rows:sc_rows + 1, :] * qkv
    for j in range(sc_rows):
        acc = acc + scw_ref[j:j + 1, :] * sh_ref[:, j * wq:(j + 1) * wq]
    c = _silu(acc)
    for h in range(N_HEADS):
        lo = h * HEAD_DIM
        qkvp_ref[:, lo:lo + HEAD_DIM] = _l2n(c[:, lo:lo + HEAD_DIM]) * (HEAD_DIM ** -0.5)
        qkvp_ref[:, DN_WIDTH + lo:DN_WIDTH + lo + HEAD_DIM] = _l2n(c[:, DN_WIDTH + lo:DN_WIDTH + lo + HEAD_DIM])
    qkvp_ref[:, 2 * DN_WIDTH:] = c[:, 2 * DN_WIDTH:]
    nsh_ref[:, :(sc_rows - 1) * wq] = sh_ref[:, wq:]
    nsh_ref[:, (sc_rows - 1) * wq:] = qkv


def _step_front(u, qkv, conv_hist_flat, sc_hist_flat, conv_dw, conv_dw_b, ln_g, ln_b, scw, *, bt):
    n = u.shape[0]
    wq = 3 * DN_WIDTH
    consts = [conv_dw, conv_dw_b, ln_g, ln_b, scw]
    row = lambda c: pl.BlockSpec((bt, c), lambda i: (i, 0))
    widths = [CONV_CH, conv_hist_flat.shape[1], wq, sc_hist_flat.shape[1]]
    dtypes = [BF16, F32, F32, F32]
    return pl.pallas_call(
        _step_front_kernel,
        grid=(n // bt,),
        in_specs=[row(CONV_CH), row(wq), row(widths[1]), row(widths[3])] + [_const_spec(a) for a in consts],
        out_specs=[row(c) for c in widths],
        out_shape=[jax.ShapeDtypeStruct((n, c), d) for c, d in zip(widths, dtypes)],
        compiler_params=_cparams(),
        name="step_front",
    )(u, qkv, conv_hist_flat, sc_hist_flat, *consts)


def _rows_to_cols(x):
    pad = jnp.zeros((LANES - x.shape[0], LANES), x.dtype)
    return jnp.concatenate([x, pad], axis=0).T


def _step_delta_kernel(qkvp_ref, ab_ref, zs_ref, s_ref, dng_ref, o_ref, sn_ref, *, bt):
    ab = ab_ref[...]
    for h in range(N_HEADS):
        lo = h * HEAD_DIM
        q = qkvp_ref[:, lo:lo + HEAD_DIM]
        k = qkvp_ref[:, DN_WIDTH + lo:DN_WIDTH + lo + HEAD_DIM]
        v = qkvp_ref[:, 2 * DN_WIDTH + lo:2 * DN_WIDTH + lo + HEAD_DIM]
        q_cols = _rows_to_cols(q)
        k_cols = _rows_to_cols(k)
        qk = jnp.sum(q * k, axis=-1, keepdims=True)
        alpha = jnp.exp(ab[:, h:h + 1])
        beta = ab[:, N_HEADS + h:N_HEADS + h + 1]
        for j in range(bt):
            s = s_ref[j, h]
            kc = k_cols[:, j:j + 1]
            qc = q_cols[:, j:j + 1]
            a = alpha[j:j + 1, :]
            s_k = jnp.sum(s * kc, axis=0, keepdims=True)
            s_q = jnp.sum(s * qc, axis=0, keepdims=True)
            u = beta[j:j + 1, :] * (v[j:j + 1, :] - a * s_k)
            o = a * s_q + qk[j:j + 1, :] * u
            sn_ref[j, h] = a * s + kc * u
            o_ref[j:j + 1, lo:lo + HEAD_DIM] = _gated_head_norm(
                o, dng_ref[...], zs_ref[j:j + 1, lo:lo + HEAD_DIM]).astype(o_ref.dtype)


def _step_delta(qkvp, ab, zs, s0, dng, *, bt):
    n = qkvp.shape[0]
    row = lambda c: pl.BlockSpec((bt, c), lambda i: (i, 0))
    state_spec = pl.BlockSpec((bt, N_HEADS, HEAD_DIM, HEAD_DIM), lambda i: (i, 0, 0, 0))
    return pl.pallas_call(
        functools.partial(_step_delta_kernel, bt=bt),
        grid=(n // bt,),
        in_specs=[row(3 * DN_WIDTH), row(LANES), row(DN_WIDTH), state_spec, _const_spec(dng)],
        out_specs=[row(DN_WIDTH), state_spec],
        out_shape=[jax.ShapeDtypeStruct((n, DN_WIDTH), BF16), jax.ShapeDtypeStruct(s0.shape, F32)],
        compiler_params=_cparams(),
        name="step_delta",
    )(qkvp, ab, zs, s0, dng)


def _post_kernel(acta_ref, actb_ref, sga_ref, sgb_ref, x_ref, gate1_ref, scale2_ref, shift2_ref,
                 gate2_ref, cnt0_ref, gffn_ref, wpa_ref, wpb_ref, wo_ref, wrt_ref, ebias_ref,
                 wsgu_ref, wsd_ref,
                 xs_ref, h2_ref, eidx_ref, ew_ref, rank_ref, cnt_ref, *, tm):
    i = pl.program_id(0)

    @pl.when(i == 0)
    def _():
        cnt_ref[...] = cnt0_ref[...]

    y_a = jnp.dot(acta_ref[...], wpa_ref[...], preferred_element_type=F32)
    y_b = jnp.dot(actb_ref[...], wpb_ref[...], preferred_element_type=F32)
    merged = sga_ref[...] * y_a + sgb_ref[...] * y_b
    mix = _bdot(merged, wo_ref[...])
    x1 = x_ref[...] + gate1_ref[...] * mix
    y = x1 * lax.rsqrt(jnp.mean(x1 * x1, axis=-1, keepdims=True) + EPS) * gffn_ref[...]
    h2 = y * (1.0 + scale2_ref[...]) + shift2_ref[...]
    h2_ref[...] = _pack_bf16_pairs(h2)
    h2b = h2.astype(BF16)

    gu = jnp.dot(h2b, wsgu_ref[...], preferred_element_type=F32)
    shared = _bdot(_silu(gu[:, :D_SHARED]) * gu[:, D_SHARED:], wsd_ref[...])
    xs_ref[...] = x1 + gate2_ref[...] * shared

    logits_t = lax.dot_general(wrt_ref[...], h2b, NT_DIMS, preferred_element_type=F32)
    scores = _sigmoid(logits_t)
    biased = scores + ebias_ref[...]
    neg = -jnp.inf
    big = jnp.int32(1 << 30)
    erow = lax.broadcasted_iota(I32, (N_EXPERTS, tm), 0)

    def first_argmax(vals, rows):
        m = jnp.max(vals, axis=0, keepdims=True)
        return m, jnp.min(jnp.where(vals == m, rows, big), axis=0, keepdims=True)

    group_scores = []
    for g in range(N_GROUPS):
        vals = biased[g * GROUP_SIZE:(g + 1) * GROUP_SIZE, :]
        rows = lax.broadcasted_iota(I32, (GROUP_SIZE, tm), 0) + g * GROUP_SIZE
        m1, i1 = first_argmax(vals, rows)
        m2 = jnp.max(jnp.where(rows == i1, neg, vals), axis=0, keepdims=True)
        group_scores.append(m1 + m2)
    group_sel = [jnp.zeros((1, tm), jnp.bool_)] * N_GROUPS
    for _ in range(TOPK_GROUPS):
        best = functools.reduce(jnp.maximum, group_scores)
        gi = functools.reduce(
            jnp.minimum, [jnp.where(group_scores[g] == best, jnp.int32(g), big) for g in range(N_GROUPS)])
        for g in range(N_GROUPS):
            hit = gi == g
            group_sel[g] = jnp.logical_or(group_sel[g], hit)
            group_scores[g] = jnp.where(hit, neg, group_scores[g])
    cand = jnp.concatenate(
        [jnp.where(group_sel[g], biased[g * GROUP_SIZE:(g + 1) * GROUP_SIZE, :], neg)
         for g in range(N_GROUPS)], axis=0)

    w_rows, hits = [], []
    picked = jnp.zeros((N_EXPERTS, tm), F32)
    for k in range(TOP_K):
        _, ei = first_argmax(cand, erow)
        hit = erow == ei
        eidx_ref[k:k + 1, :] = ei
        w_rows.append(jnp.sum(jnp.where(hit, scores, 0.0), axis=0, keepdims=True))
        hits.append(hit)
        picked = jnp.where(hit, 1.0, picked)
        cand = jnp.where(hit, neg, cand)
    w_sum = functools.reduce(jnp.add, w_rows)
    for k in range(TOP_K):
        ew_ref[k:k + 1, :] = w_rows[k] / w_sum * ROUTE_SCALE

    trow = lax.broadcasted_iota(I32, (tm, tm), 0)
    tcol = lax.broadcasted_iota(I32, (tm, tm), 1)
    before = (trow < tcol).astype(BF16)
    prefix = jnp.dot(picked.astype(BF16), before, preferred_element_type=F32) + cnt_ref[:, 0:1]
    for k in range(TOP_K):
        rank_ref[k:k + 1, :] = jnp.sum(jnp.where(hits[k], prefix, 0.0), axis=0, keepdims=True).astype(I32)
    cnt_ref[...] = cnt_ref[...] + jnp.sum(picked, axis=1, keepdims=True)


def _post(acta, actb, sga, sgb, x2d, gate1, scale2, shift2, gate2, cnt0, g_ffn, w, ebias_col,
          *, tm, tiles_per_mod):
    t = x2d.shape[0]
    row = lambda c: pl.BlockSpec((tm, c), lambda i: (i, 0))
    col = lambda r: pl.BlockSpec((r, tm), lambda i: (0, i))
    mods = [gate1, scale2, shift2, gate2]
    consts = [cnt0, g_ffn, w["pa"], w["pb"], w["o"], w["router_t"], ebias_col, w["s_gu"], w["s_down"]]
    return pl.pallas_call(
        functools.partial(_post_kernel, tm=tm),
        grid=(t // tm,),
        in_specs=[row(CONV_CH), row(DN_WIDTH), row(D_MODEL), row(D_MODEL), row(D_MODEL)]
                 + [_mod_spec(m, tiles_per_mod) for m in mods] + [_const_spec(a) for a in consts],
        out_specs=[row(D_MODEL), row(D_MODEL // 2), col(TOP_K), col(TOP_K), col(TOP_K), _const_spec(cnt0)],
        out_shape=[jax.ShapeDtypeStruct((t, D_MODEL), F32), jax.ShapeDtypeStruct((t, D_MODEL // 2), U32),
                   jax.ShapeDtypeStruct((TOP_K, t), I32), jax.ShapeDtypeStruct((TOP_K, t), F32),
                   jax.ShapeDtypeStruct((TOP_K, t), I32), jax.ShapeDtypeStruct(cnt0.shape, F32)],
        compiler_params=_cparams(),
        name="post_mixer",
    )(acta, actb, sga, sgb, x2d, *mods, *consts)


def _dest_kernel(start_ref, eidx_ref, rank_ref, o_ref, *, width):
    step = SUBLANES * LANES
    for c0 in range(0, width, step):
        c1 = min(c0 + step, width)
        e = eidx_ref[:, c0:c1]
        base = lax.fori_loop(0, N_EXPERTS, lambda x, acc: jnp.where(e == x, start_ref[x], acc),
                             jnp.zeros(e.shape, I32), unroll=8)
        o_ref[:, c0:c1] = base + rank_ref[:, c0:c1]


def _dest(cnt_start, e_idx, rank):
    k, t = e_idx.shape
    full = pl.BlockSpec((k, t), lambda i, st: (0, 0))
    return pl.pallas_call(
        functools.partial(_dest_kernel, width=t),
        grid_spec=pltpu.PrefetchScalarGridSpec(
            num_scalar_prefetch=1, grid=(1,), in_specs=[full, full], out_specs=full),
        out_shape=jax.ShapeDtypeStruct((k, t), I32),
        compiler_params=_cparams(),
        name="moe_dest",
    )(cnt_start, e_idx, rank)


def _row_copy(src_ref, src_row, dst_ref, dst_row, sem):
    return pltpu.make_async_copy(src_ref.at[pl.ds(src_row, 1), :], dst_ref.at[pl.ds(dst_row, 1), :], sem)


def _dispatch_kernel(dest_ref, hp_ref, hs_ref, xs_hbm, sem, *, tm, n_prompt_tiles):
    def scatter_rows(h_ref):
        def issue(r, carry):
            for k in range(TOP_K):
                _row_copy(h_ref, r, xs_hbm, dest_ref[0, 0, k * tm + r], sem).start(priority=k % N_DMA_PRIORITIES)
            return carry

        lax.fori_loop(0, tm, issue, 0)

        def drain(r, carry):
            for k in range(TOP_K):
                _row_copy(h_ref, r, xs_hbm, dest_ref[0, 0, k * tm + r], sem).wait()
            return carry

        lax.fori_loop(0, tm, drain, 0)

    is_prompt = pl.program_id(0) < n_prompt_tiles
    pl.when(is_prompt)(lambda: scatter_rows(hp_ref))
    pl.when(jnp.logical_not(is_prompt))(lambda: scatter_rows(hs_ref))


def _two_group_specs(tm, width, n_prompt_tiles):
    return [pl.BlockSpec((tm, width), lambda i: (jnp.minimum(i, n_prompt_tiles - 1), 0)),
            pl.BlockSpec((tm, width), lambda i: (jnp.maximum(i - n_prompt_tiles, 0), 0))]


def _dispatch(h2_p, h2_s, dest_tiles, n_rows, *, tm):
    n_prompt_tiles = h2_p.shape[0] // tm
    n_tiles = n_prompt_tiles + h2_s.shape[0] // tm
    return pl.pallas_call(
        functools.partial(_dispatch_kernel, tm=tm, n_prompt_tiles=n_prompt_tiles),
        grid=(n_tiles,),
        in_specs=[pl.BlockSpec((1, 1, tm * TOP_K), lambda i: (i, 0, 0), memory_space=pltpu.SMEM)]
                 + _two_group_specs(tm, h2_p.shape[1], n_prompt_tiles),
        out_specs=pl.BlockSpec(memory_space=pl.ANY),
        out_shape=jax.ShapeDtypeStruct((n_rows, h2_p.shape[1]), h2_p.dtype),
        scratch_shapes=[pltpu.SemaphoreType.DMA],
        compiler_params=_cparams(),
        name="moe_dispatch",
    )(dest_tiles, h2_p, h2_s)


def _experts_kernel(sblk_ref, sexp_ref, slo_ref, shi_ref, snext_ref, first_ref,
                    x_ref, wg_hbm, wu_hbm, wd_hbm, y_ref,
                    wg_buf, wu_buf, wd_buf, wgu_s, wd_s, state_ref, sems, *, bm):
    p = pl.program_id(0)
    lo = slo_ref[p]
    expert = sexp_ref[p]

    def weight_copies(e, slot):
        return [pltpu.make_async_copy(src.at[e], dst.at[slot], sems.at[slot])
                for src, dst in ((wg_hbm, wg_buf), (wu_hbm, wu_buf), (wd_hbm, wd_buf))]

    @pl.when(p == 0)
    def _():
        state_ref[0] = -1
        state_ref[1] = 0
        for c in weight_copies(first_ref[0], 0):
            c.start()

    @pl.when(shi_ref[p] > lo)
    def _():
        @pl.when(state_ref[0] != expert)
        def _():
            slot = state_ref[1]
            for c in weight_copies(expert, slot):
                c.wait()
            wgu_s[:, :D_EXPERT] = wg_buf[slot].astype(BF16)
            wgu_s[:, D_EXPERT:] = wu_buf[slot].astype(BF16)
            wd_s[...] = wd_buf[slot].astype(BF16)
            state_ref[0] = expert
            state_ref[1] = 1 - slot
            upcoming = snext_ref[p]

            @pl.when(upcoming >= 0)
            def _():
                for c in weight_copies(upcoming, 1 - slot):
                    c.start()

        x_low, x_high = _unpack_bf16_pairs(x_ref[...])
        half = D_MODEL // 2
        gu = (jnp.dot(x_low, wgu_s[:half, :], preferred_element_type=F32)
              + jnp.dot(x_high, wgu_s[half:, :], preferred_element_type=F32))
        hb = (_silu(gu[:, :D_EXPERT]) * gu[:, D_EXPERT:]).astype(BF16)
        y = jnp.dot(hb, wd_s[...], preferred_element_type=F32)

        @pl.when(lo == 0)
        def _():
            y_ref[...] = y

        @pl.when(lo > 0)
        def _():
            rows = lax.broadcasted_iota(I32, (bm, 1), 0)
            y_ref[...] = jnp.where(rows >= lo, y, y_ref[...])


def _experts(xs, w_gate, w_up, w_down, seg_blk, seg_exp, seg_lo, seg_hi, seg_next, first_exp, *, bm):
    n_rows = xs.shape[0]
    n_seg = seg_blk.shape[0]
    ring = 2
    hbm = pl.BlockSpec(memory_space=pl.ANY)
    grid_spec = pltpu.PrefetchScalarGridSpec(
        num_scalar_prefetch=6,
        grid=(n_seg,),
        in_specs=[pl.BlockSpec((bm, D_MODEL // 2), lambda p, sb, *_: (sb[p], 0)), hbm, hbm, hbm],
        out_specs=pl.BlockSpec((bm, D_MODEL), lambda p, sb, *_: (sb[p], 0)),
        scratch_shapes=[pltpu.VMEM((ring, D_MODEL, D_EXPERT), F32), pltpu.VMEM((ring, D_MODEL, D_EXPERT), F32),
                        pltpu.VMEM((ring, D_EXPERT, D_MODEL), F32),
                        pltpu.VMEM((D_MODEL, 2 * D_EXPERT), BF16), pltpu.VMEM((D_EXPERT, D_MODEL), BF16),
                        pltpu.SMEM((2,), I32), pltpu.SemaphoreType.DMA((ring,))],
    )
    return pl.pallas_call(
        functools.partial(_experts_kernel, bm=bm),
        grid_spec=grid_spec,
        out_shape=jax.ShapeDtypeStruct((n_rows, D_MODEL), F32),
        compiler_params=_cparams(),
        name="moe_experts",
    )(seg_blk, seg_exp, seg_lo, seg_hi, seg_next, first_exp, xs, w_gate, w_up, w_down)


def _final_kernel(dest_ref, ew_ref, xsp_ref, xss_ref, g2p_ref, g2s_ref, gfin_ref, ys_hbm,
                  op_ref, os_ref, buf_ref, sem, *, tm, n_prompt_tiles):
    def issue(r, carry):
        for k in range(TOP_K):
            _row_copy(ys_hbm, dest_ref[0, 0, k * tm + r], buf_ref.at[k], r, sem).start(priority=k % N_DMA_PRIORITIES)
        return carry

    lax.fori_loop(0, tm, issue, 0)

    def drain(r, carry):
        for k in range(TOP_K):
            _row_copy(ys_hbm, dest_ref[0, 0, k * tm + r], buf_ref.at[k], r, sem).wait()
        return carry

    lax.fori_loop(0, tm, drain, 0)

    ew = ew_ref[...]
    routed = ew[:, 0:1] * buf_ref[0]
    for k in range(1, TOP_K):
        routed = routed + ew[:, k:k + 1] * buf_ref[k]

    def finish(xs_ref, gate2_ref, o_ref):
        x2 = xs_ref[...] + gate2_ref[...] * routed
        o_ref[...] = x2 * lax.rsqrt(jnp.mean(x2 * x2, axis=-1, keepdims=True) + EPS) * gfin_ref[...]

    is_prompt = pl.program_id(0) < n_prompt_tiles
    pl.when(is_prompt)(lambda: finish(xsp_ref, g2p_ref, op_ref))
    pl.when(jnp.logical_not(is_prompt))(lambda: finish(xss_ref, g2s_ref, os_ref))


def _final(dest_tiles, ew_tok, xs_p, xs_s, gate2_p, gate2_s, g_final, ys, *, tm, tiles_per_batch):
    n_prompt_tiles = xs_p.shape[0] // tm
    n_tiles = n_prompt_tiles + xs_s.shape[0] // tm
    nb = gate2_p.shape[0]
    return pl.pallas_call(
        functools.partial(_final_kernel, tm=tm, n_prompt_tiles=n_prompt_tiles),
        grid=(n_tiles,),
        in_specs=[pl.BlockSpec((1, 1, tm * TOP_K), lambda i: (i, 0, 0), memory_space=pltpu.SMEM),
                  pl.BlockSpec((tm, TOP_K), lambda i: (i, 0))]
                 + _two_group_specs(tm, D_MODEL, n_prompt_tiles)
                 + [pl.BlockSpec((None, 1, D_MODEL), lambda i: (jnp.minimum(i // tiles_per_batch, nb - 1), 0, 0)),
                    pl.BlockSpec((None, tm, D_MODEL), lambda i: (jnp.maximum(i - n_prompt_tiles, 0), 0, 0)),
                    _const_spec(g_final),
                    pl.BlockSpec(memory_space=pl.ANY)],
        out_specs=_two_group_specs(tm, D_MODEL, n_prompt_tiles),
        out_shape=[jax.ShapeDtypeStruct(xs_p.shape, F32), jax.ShapeDtypeStruct(xs_s.shape, F32)],
        scratch_shapes=[pltpu.VMEM((TOP_K, tm, D_MODEL), F32), pltpu.SemaphoreType.DMA],
        compiler_params=_cparams(),
        name="moe_combine_final",
    )(dest_tiles, ew_tok, xs_p, xs_s, gate2_p, gate2_s, g_final, ys)


def _pick_tile(n, preferred):
    t = min(n, preferred)
    assert n % t == 0, (n, t)
    return t


def kernel(x_prompt, x_sample, c_prompt, c_sample, state_conv, state_short_conv, state_delta, w_ada, b_ada, g_mix, w_in, conv_dw, conv_dw_b, conv_ln_g, conv_ln_b, w_pa, short_conv_w, a_log, dt_bias, dn_norm_g, w_pb, w_o, g_ffn, w_router, e_bias, w_gate, w_up, w_down, ws_gate, ws_up, ws_down, g_final):
    nb, seq, d = x_prompt.shape
    ns = x_sample.shape[0]
    assert d == D_MODEL and w_ada.shape[0] == 1 and x_sample.shape[1] == 1
    assert seq % CHUNK == 0 and seq >= CONV_HALO and ns % SUBLANES == 0
    tp = nb * seq
    row1 = lambda a: a.reshape(1, -1)

    wi = w_in[0].astype(BF16)
    o_q = 2 * CONV_CH
    o_z = o_q + 3 * DN_WIDTH
    o_a = o_z + DN_WIDTH
    o_ga = o_a + 2 * N_HEADS
    w_proj = {
        "glu": wi[:, :o_q], "qkv": wi[:, o_q:o_z], "z": wi[:, o_z:o_a],
        "ab": jnp.pad(wi[:, o_a:o_ga], ((0, 0), (0, LANES - 2 * N_HEADS))),
        "ga": wi[:, o_ga:o_ga + D_MODEL], "gb": wi[:, o_ga + D_MODEL:],
    }
    nea = jnp.pad(-jnp.exp(a_log[0].astype(F32)), (0, LANES - N_HEADS)).reshape(1, LANES)
    dtb = jnp.pad(dt_bias[0].astype(F32), (0, LANES - N_HEADS)).reshape(1, LANES)
    w_post = {
        "pa": w_pa[0].astype(BF16), "pb": w_pb[0].astype(BF16), "o": w_o[0].astype(BF16),
        "router_t": w_router[0].T.astype(BF16),
        "s_gu": jnp.concatenate([ws_gate[0], ws_up[0]], axis=1).astype(BF16),
        "s_down": ws_down[0].astype(BF16),
    }
    ebias_col = e_bias[0].astype(F32).reshape(N_EXPERTS, 1)
    g_mix1, g_ffn1, g_fin1 = row1(g_mix[0]), row1(g_ffn[0]), row1(g_final)
    cdw, cdb, clg, clb = conv_dw[0], row1(conv_dw_b[0]), row1(conv_ln_g[0]), row1(conv_ln_b[0])
    scw, dng = short_conv_w[0], row1(dn_norm_g[0])

    ada = _ada(jnp.concatenate([c_prompt, c_sample], axis=0), w_ada[0].astype(BF16), b_ada[0])
    ada = ada.reshape(nb + ns, 6, D_MODEL)
    mod_p = [ada[:nb, m].reshape(nb, 1, D_MODEL) for m in range(6)]
    mod_s = [ada[nb:, m].reshape(1, ns, D_MODEL) for m in range(6)]
    shift1, scale1, gate1, shift2, scale2, gate2 = range(6)

    tm_p = _pick_tile(seq, 256)
    tpm_p = seq // tm_p
    xp = x_prompt.reshape(tp, D_MODEL)
    u_p, qkv_p, zs_p, ab_p, sga_p, sgb_p = _in_proj(
        xp, mod_p[scale1], mod_p[shift1], g_mix1, w_proj, nea, dtb, tm=tm_p, tiles_per_mod=tpm_p)
    tl = _pick_tile(seq, 256)
    acta_p = _conv_branch(u_p, jnp.zeros((nb, CONV_HALO, CONV_CH), F32), cdw, cdb, clg, clb,
                          tl=tl, tiles_per_batch=seq // tl)
    tc = _pick_tile(seq, 4 * CHUNK)
    actb_p, s_new_p = _delta_branch(
        qkv_p, jnp.zeros((nb, SC_HALO, 3 * DN_WIDTH), F32),
        jnp.zeros((nb, N_HEADS, HEAD_DIM, HEAD_DIM), F32), scw, ab_p, zs_p, dng,
        tc=tc, tiles_per_batch=seq // tc)

    xs_ = x_sample.reshape(ns, D_MODEL)
    u_s, qkv_s, zs_s, ab_s, sga_s, sgb_s = _in_proj(
        xs_, mod_s[scale1], mod_s[shift1], g_mix1, w_proj, nea, dtb, tm=ns, tiles_per_mod=1)
    bt = SUBLANES
    acta_s, nch_s, qkvp_s, nsh_s = _step_front(
        u_s, qkv_s, state_conv[0].reshape(ns, -1), state_short_conv[0].reshape(ns, -1),
        cdw, cdb, clg, clb, scw, bt=bt)
    actb_s, s_new_s = _step_delta(qkvp_s, ab_s, zs_s, state_delta[0].astype(F32), dng, bt=bt)

    cnt0 = jnp.zeros((N_EXPERTS, LANES), F32)
    xsh_p, h2_p, eidx_p, ew_p, rank_p, cnt_p = _post(
        acta_p, actb_p, sga_p, sgb_p, xp, mod_p[gate1], mod_p[scale2], mod_p[shift2], mod_p[gate2],
        cnt0, g_ffn1, w_post, ebias_col, tm=tm_p, tiles_per_mod=tpm_p)
    xsh_s, h2_s, eidx_s, ew_s, rank_s, cnt_all = _post(
        acta_s, actb_s, sga_s, sgb_s, xs_, mod_s[gate1], mod_s[scale2], mod_s[shift2], mod_s[gate2],
        cnt_p, g_ffn1, w_post, ebias_col, tm=ns, tiles_per_mod=1)

    bm = 256
    tmd = LANES
    t_all = tp + ns
    n_rows = t_all * TOP_K
    assert seq % tmd == 0 and ns % tmd == 0 and n_rows % bm == 0
    n_blocks = n_rows // bm
    counts = cnt_all[:, 0].astype(I32)
    cnt_end = jnp.cumsum(counts)
    cnt_start = cnt_end - counts
    e_idx = jnp.concatenate([eidx_p, eidx_s], axis=1)
    rank = jnp.concatenate([rank_p, rank_s], axis=1)
    dest = _dest(cnt_start, e_idx, rank)
    dest_tiles = dest.reshape(TOP_K, t_all // tmd, tmd).transpose(1, 0, 2).reshape(t_all // tmd, 1, TOP_K * tmd)
    seg_start = jnp.sort(jnp.concatenate([jnp.arange(n_blocks, dtype=I32) * bm, cnt_start]))
    seg_end = jnp.concatenate([seg_start[1:], jnp.full((1,), n_rows, I32)])
    seg_blk = jnp.minimum(seg_start // bm, n_blocks - 1)
    seg_lo = seg_start - seg_blk * bm
    seg_hi = seg_end - seg_blk * bm
    seg_exp = jnp.sum((cnt_end[None, :] <= seg_start[:, None]).astype(I32), axis=1)
    seg_exp = jnp.minimum(seg_exp, N_EXPERTS - 1)
    seg_exp = lax.cummax(jnp.where(seg_hi > seg_lo, seg_exp, 0))
    none = jnp.int32(N_EXPERTS)
    in_use_from = lax.cummin(jnp.where(counts > 0, jnp.arange(N_EXPERTS, dtype=I32), none), reverse=True)
    next_in_use = jnp.concatenate([in_use_from[1:], jnp.full((1,), none, I32)])
    next_in_use = jnp.where(next_in_use == none, -1, next_in_use)
    seg_next = next_in_use[seg_exp]
    first_exp = jnp.minimum(in_use_from[:1], N_EXPERTS - 1)

    xs_sorted = _dispatch(h2_p, h2_s, dest_tiles, n_rows, tm=tmd)
    ys = _experts(xs_sorted, w_gate[0], w_up[0], w_down[0], seg_blk, seg_exp, seg_lo, seg_hi, seg_next, first_exp,
                  bm=bm)

    ew_tok = jnp.concatenate([ew_p, ew_s], axis=1).T
    y_p, y_s = _final(dest_tiles, ew_tok, xsh_p, xsh_s, mod_p[gate2], mod_s[gate2].reshape(ns // tmd, tmd, D_MODEL),
                      g_fin1, ys, tm=tmd, tiles_per_batch=seq // tmd)

    hist_rows = CONV_WIDTH - 1
    sc_rows = SHORT_CONV - 1
    y_prompt = y_p.reshape(nb, seq, D_MODEL)
    y_sample = y_s.reshape(ns, 1, D_MODEL)
    new_conv_prompt = u_p.reshape(nb, seq, CONV_CH)[:, seq - hist_rows:][None]
    new_sc_prompt = qkv_p.reshape(nb, seq, 3 * DN_WIDTH)[:, seq - sc_rows:][None]
    new_conv_sample = nch_s.reshape(ns, hist_rows, CONV_CH)[None]
    new_sc_sample = nsh_s.reshape(ns, sc_rows, 3 * DN_WIDTH)[None]
    return (y_prompt, y_sample, new_conv_prompt, new_sc_prompt, s_new_p[None],
            new_conv_sample, new_sc_sample, s_new_s[None])
```

```python
import functools

import jax
import jax.numpy as jnp
from jax import lax
from jax.experimental import pallas as pl
from jax.experimental.pallas import tpu as pltpu
from jax.experimental.pallas import tpu_sc as plsc

F32 = jnp.float32
BF16 = jnp.bfloat16
I32 = jnp.int32
U32 = jnp.uint32

EPS = 1e-6
D_MODEL = 1024
CONV_CH = 512
CONV_WIDTH = 31
N_HEADS = 4
HEAD_DIM = 128
DN_WIDTH = N_HEADS * HEAD_DIM
SHORT_CONV = 4
CHUNK = 64
CHUNK_SHIFT = CHUNK.bit_length() - 1
assert 1 << CHUNK_SHIFT == CHUNK
SUB = 16
SUB_SHIFT = SUB.bit_length() - 1
assert 1 << SUB_SHIFT == SUB and CHUNK % SUB == 0
N_EXPERTS = 256
TOP_K = 8
N_GROUPS = 8
GROUP_SIZE = N_EXPERTS // N_GROUPS
TOPK_GROUPS = 4
D_EXPERT = 256
D_SHARED = 256
ROUTE_SCALE = 2.5

LANES = 128
SUBLANES = 8
CONV_HALO = 32
SC_HALO = 8
VMEM_LIMIT = 48 * 1024 * 1024
SC_WINDOW = 128
QUARTER_WORDS = D_MODEL // 4

NT_DIMS = (((1,), (1,)), ((), ()))
TN_DIMS = (((0,), (0,)), ((), ()))


def _cparams(n_grid_dims=1):
    return pltpu.CompilerParams(
        dimension_semantics=("arbitrary",) * n_grid_dims,
        vmem_limit_bytes=VMEM_LIMIT)


def _sigmoid(x):
    return jax.nn.sigmoid(x)


def _silu(x):
    return x * jax.nn.sigmoid(x)


def _bdot(a, b):
    return jnp.dot(a.astype(BF16), b.astype(BF16), preferred_element_type=F32)


def _pack_bf16_pairs(x):
    half = x.shape[1] // 2
    bits = lax.bitcast_convert_type(x.astype(BF16).astype(F32), U32)
    return bits[:, half:] | (bits[:, :half] >> 16)


def _unpack_bf16_pairs(w):
    low = lax.bitcast_convert_type(w << 16, F32).astype(BF16)
    high = lax.bitcast_convert_type(w & jnp.uint32(0xFFFF0000), F32).astype(BF16)
    return low, high


def _dot3(a, b):
    a_hi = a.astype(BF16)
    b_hi = b.astype(BF16)
    a_lo = (a - a_hi.astype(F32)).astype(BF16)
    b_lo = (b - b_hi.astype(F32)).astype(BF16)
    dot = lambda x, y: jnp.dot(x, y, preferred_element_type=F32)
    return dot(a_hi, b_hi) + (dot(a_hi, b_lo) + dot(a_lo, b_hi))


def _dot_exact_lhs(a_bf, b):
    hi = b.astype(BF16)
    r1 = b - hi.astype(F32)
    mid = r1.astype(BF16)
    low = (r1 - mid.astype(F32)).astype(BF16)
    dot = lambda p: jnp.dot(a_bf, p, preferred_element_type=F32)
    return dot(hi) + dot(mid) + dot(low)


def _ada_kernel(c_ref, w_ref, b_ref, o_ref):
    o_ref[...] = _bdot(_silu(c_ref[...]), w_ref[...]) + b_ref[...]


def _ada(c_all, w_ada_bf, b_ada):
    n = c_all.shape[0]
    width = w_ada_bf.shape[1]
    tn = 1536
    return pl.pallas_call(
        _ada_kernel,
        grid=(width // tn,),
        in_specs=[pl.BlockSpec((n, D_MODEL), lambda j: (0, 0)),
                  pl.BlockSpec((D_MODEL, tn), lambda j: (0, j)),
                  pl.BlockSpec((1, tn), lambda j: (0, j))],
        out_specs=pl.BlockSpec((n, tn), lambda j: (0, j)),
        out_shape=jax.ShapeDtypeStruct((n, width), F32),
        compiler_params=_cparams(),
        name="ada",
    )(c_all, w_ada_bf, b_ada.reshape(1, width))


def _in_proj_kernel(x_ref, scale_ref, shift_ref, g_ref, wglu_ref, wqkv_ref, wz_ref, wab_ref,
                    wga_ref, wgb_ref, nea_ref, dtb_ref,
                    u_ref, qkv_ref, zs_ref, ab_ref, sga_ref, sgb_ref):
    x = x_ref[...]
    y = x * lax.rsqrt(jnp.mean(x * x, axis=-1, keepdims=True) + EPS) * g_ref[...]
    h = (y * (1.0 + scale_ref[...]) + shift_ref[...]).astype(BF16)
    glu = jnp.dot(h, wglu_ref[...], preferred_element_type=F32)
    u_ref[...] = glu[:, :CONV_CH] * _sigmoid(glu[:, CONV_CH:])
    qkv_ref[...] = jnp.dot(h, wqkv_ref[...], preferred_element_type=F32)
    zs_ref[...] = _silu(jnp.dot(h, wz_ref[...], preferred_element_type=F32))
    ab = jnp.dot(h, wab_ref[...], preferred_element_type=F32)
    sp = ab + dtb_ref[...]
    softplus = jnp.maximum(sp, 0.0) + jnp.log1p(jnp.exp(-jnp.abs(sp)))
    lane = lax.broadcasted_iota(I32, ab.shape, 1)
    ab_ref[...] = jnp.where(lane < N_HEADS, nea_ref[...] * softplus, _sigmoid(ab))
    sga_ref[...] = _sigmoid(jnp.dot(h, wga_ref[...], preferred_element_type=F32))
    sgb_ref[...] = _sigmoid(jnp.dot(h, wgb_ref[...], preferred_element_type=F32))


def _mod_spec(mod, tiles_per_mod):
    return pl.BlockSpec((None,) + mod.shape[1:], lambda i: (i // tiles_per_mod, 0, 0))


def _const_spec(a):
    nd = a.ndim
    return pl.BlockSpec(a.shape, lambda i: (0,) * nd)


def _in_proj(x2d, scale, shift, g_mix, w, nea, dtb, *, tm, tiles_per_mod):
    t = x2d.shape[0]
    row = lambda c: pl.BlockSpec((tm, c), lambda i: (i, 0))
    consts = [g_mix, w["glu"], w["qkv"], w["z"], w["ab"], w["ga"], w["gb"], nea, dtb]
    widths = [CONV_CH, 3 * DN_WIDTH, DN_WIDTH, LANES, D_MODEL, D_MODEL]
    return pl.pallas_call(
        _in_proj_kernel,
        grid=(t // tm,),
        in_specs=[row(D_MODEL), _mod_spec(scale, tiles_per_mod), _mod_spec(shift, tiles_per_mod)]
                 + [_const_spec(a) for a in consts],
        out_specs=[row(c) for c in widths],
        out_shape=[jax.ShapeDtypeStruct((t, c), F32) for c in widths],
        compiler_params=_cparams(),
        name="in_proj",
    )(x2d, scale, shift, *consts)


def _layer_norm_swish(y, g, b):
    mu = jnp.mean(y, axis=-1, keepdims=True)
    yc = y - mu
    var = jnp.mean(yc * yc, axis=-1, keepdims=True)
    return _silu(yc * lax.rsqrt(var + EPS) * g + b)


def _conv_kernel(u_ref, halo_ref, hist_ref, w_ref, b_ref, lg_ref, lb_ref, o_ref, win_ref, shift_ref,
                 *, tl, tiles_per_batch, rows_per_pass):
    i = pl.program_id(0)
    first = (i % tiles_per_batch) == 0

    @pl.when(first)
    def _():
        win_ref[0:CONV_HALO, :] = hist_ref[...]

    @pl.when(jnp.logical_not(first))
    def _():
        win_ref[0:CONV_HALO, :] = halo_ref[...]

    win_ref[CONV_HALO:CONV_HALO + tl, :] = u_ref[...]
    tap0 = CONV_HALO - (CONV_WIDTH - 1)
    n_shifted = CONV_HALO + tl - SUBLANES
    for b in range(1, SUBLANES):
        shift_ref[b - 1] = win_ref[b:b + n_shifted, :]

    for r in range(tl // rows_per_pass):
        base = r * rows_per_pass
        acc = jnp.zeros((rows_per_pass, CONV_CH), F32)
        for j in range(CONV_WIDTH):
            a, b = divmod(tap0 + j, SUBLANES)
            start = base + a * SUBLANES
            rows = (win_ref[start:start + rows_per_pass, :] if b == 0
                    else shift_ref[b - 1, start:start + rows_per_pass, :])
            acc = acc + w_ref[j:j + 1, :] * rows
        act = _layer_norm_swish(acc + b_ref[...], lg_ref[...], lb_ref[...])
        o_ref[base:base + rows_per_pass, :] = act.astype(o_ref.dtype)


def _conv_branch(u2d, hist, conv_dw, conv_dw_b, ln_g, ln_b, *, tl, tiles_per_batch):
    t = u2d.shape[0]
    halo_blocks = tl // CONV_HALO
    consts = [conv_dw, conv_dw_b, ln_g, ln_b]
    return pl.pallas_call(
        functools.partial(_conv_kernel, tl=tl, tiles_per_batch=tiles_per_batch, rows_per_pass=64),
        grid=(t // tl,),
        in_specs=[pl.BlockSpec((tl, CONV_CH), lambda i: (i, 0)),
                  pl.BlockSpec((CONV_HALO, CONV_CH), lambda i: (jnp.maximum(i * halo_blocks - 1, 0), 0)),
                  pl.BlockSpec((None, CONV_HALO, CONV_CH), lambda i: (i // tiles_per_batch, 0, 0))]
                 + [_const_spec(a) for a in consts],
        out_specs=pl.BlockSpec((tl, CONV_CH), lambda i: (i, 0)),
        out_shape=jax.ShapeDtypeStruct((t, CONV_CH), BF16),
        scratch_shapes=[pltpu.VMEM((CONV_HALO + tl, CONV_CH), F32),
                        pltpu.VMEM((SUBLANES - 1, CONV_HALO + tl - SUBLANES, CONV_CH), F32)],
        compiler_params=_cparams(),
        name="conv_branch",
    )(u2d, u2d, hist, *consts)


def _l2n(x):
    return x * lax.rsqrt(jnp.sum(x * x, axis=-1, keepdims=True) + 1e-6)


def _gated_head_norm(o, g, zs):
    return o * lax.rsqrt(jnp.mean(o * o, axis=-1, keepdims=True) + EPS) * g * zs


def _delta_kernel(qkv_ref, halo_ref, hist_ref, scw_ref, ab_ref, zs_ref, s0_ref, dng_ref,
                  o_ref, s_ref, win_ref, c_ref, *, tc, tiles_per_batch):
    i = pl.program_id(0)
    first = (i % tiles_per_batch) == 0

    @pl.when(first)
    def _():
        win_ref[0:SC_HALO, :] = hist_ref[...]
        s_ref[...] = s0_ref[...]

    @pl.when(jnp.logical_not(first))
    def _():
        win_ref[0:SC_HALO, :] = halo_ref[...]

    win_ref[SC_HALO:SC_HALO + tc, :] = qkv_ref[...]
    tap0 = SC_HALO - (SHORT_CONV - 1)
    acc = jnp.zeros((tc, 3 * DN_WIDTH), F32)
    for j in range(SHORT_CONV):
        acc = acc + scw_ref[j:j + 1, :] * win_ref[tap0 + j:tap0 + j + tc, :]
    c_ref[...] = _silu(acc)

    row = lax.broadcasted_iota(I32, (tc, tc), 0)
    col = lax.broadcasted_iota(I32, (tc, tc), 1)
    same_chunk = lax.shift_right_logical(row, CHUNK_SHIFT) == lax.shift_right_logical(col, CHUNK_SHIFT)
    incl = jnp.logical_and(same_chunk, row >= col)
    row_sub = lax.shift_right_logical(row, SUB_SHIFT)
    col_sub = lax.shift_right_logical(col, SUB_SHIFT)
    strict_sub = jnp.logical_and(row_sub == col_sub, row > col)
    below_sub = jnp.logical_and(same_chunk, row_sub > col_sub)
    eye = (row == col).astype(F32)
    ab = ab_ref[...]
    gcum = _dot_exact_lhs(incl.astype(BF16), ab)
    gcum_t = gcum.T
    n_ch = tc // CHUNK

    heads = range(N_HEADS)
    q, k, gc, kk, a_qk, rhs, qd = [], [], [], [], [], [], []
    for h in heads:
        lo = h * HEAD_DIM
        q_h = _l2n(c_ref[:, lo:lo + HEAD_DIM]) * (HEAD_DIM ** -0.5)
        k_h = _l2n(c_ref[:, DN_WIDTH + lo:DN_WIDTH + lo + HEAD_DIM])
        v_h = c_ref[:, 2 * DN_WIDTH + lo:2 * DN_WIDTH + lo + HEAD_DIM]
        beta = ab[:, N_HEADS + h:N_HEADS + h + 1]
        gc_h = gcum[:, h:h + 1]
        gr = gcum_t[h:h + 1, :]
        decay = jnp.exp(jnp.where(incl, gc_h - gr, -jnp.inf))
        kb = k_h * beta
        kq = lax.dot_general(jnp.concatenate([kb, q_h], axis=0).astype(BF16), k_h.astype(BF16),
                             NT_DIMS, preferred_element_type=F32)
        e_gc = jnp.exp(gc_h)
        q.append(q_h)
        k.append(k_h)
        gc.append(gc_h)
        kk.append(kq[:tc] * decay)
        a_qk.append(kq[tc:] * decay)
        rhs.append(jnp.concatenate([v_h * beta, kb * e_gc], axis=1))
        qd.append(q_h * e_gc)

    pw = [jnp.where(strict_sub, kk[h], 0.0) for h in heads]
    d_inv = [eye - pw[h] for h in heads]
    for _ in range(SUB_SHIFT - 1):
        pw = [_bdot(pw[h], pw[h]) for h in heads]
        d_inv = [d_inv[h] + _bdot(d_inv[h], pw[h]) for h in heads]
    c_rhs = [_bdot(d_inv[h], rhs[h]) for h in heads]
    m_off = [_bdot(d_inv[h], jnp.where(below_sub, kk[h], 0.0)) for h in heads]
    sol = c_rhs
    for _ in range(CHUNK // SUB - 1):
        sol = [c_rhs[h] - _bdot(m_off[h], sol[h]) for h in heads]

    s = [s_ref[h] for h in heads]
    u_parts = [[] for _ in heads]
    o_parts = [[] for _ in heads]
    for c in range(n_ch):
        r0 = c * CHUNK
        for h in heads:
            u0_c = sol[h][r0:r0 + CHUNK, :HEAD_DIM]
            w_c = sol[h][r0:r0 + CHUNK, HEAD_DIM:]
            wq_s = _bdot(jnp.concatenate([w_c, qd[h][r0:r0 + CHUNK]], axis=0), s[h])
            u_c = u0_c - wq_s[:CHUNK]
            g_last = gcum[r0 + CHUNK - 1:r0 + CHUNK, h:h + 1]
            k_dec = k[h][r0:r0 + CHUNK] * jnp.exp(g_last - gc[h][r0:r0 + CHUNK])
            s[h] = jnp.exp(g_last) * s[h] + lax.dot_general(
                k_dec.astype(BF16), u_c.astype(BF16), TN_DIMS, preferred_element_type=F32)
            u_parts[h].append(u_c)
            o_parts[h].append(wq_s[CHUNK:])
    for h in heads:
        lo = h * HEAD_DIM
        s_ref[h] = s[h]
        o = jnp.concatenate(o_parts[h], axis=0) + _bdot(a_qk[h], jnp.concatenate(u_parts[h], axis=0))
        o_ref[:, lo:lo + HEAD_DIM] = _gated_head_norm(
            o, dng_ref[...], zs_ref[:, lo:lo + HEAD_DIM]).astype(o_ref.dtype)


def _delta_branch(qkv2d, hist, s0, scw, ab, zs, dng, *, tc, tiles_per_batch):
    t = qkv2d.shape[0]
    nb = s0.shape[0]
    halo_blocks = tc // SC_HALO
    wqkv = 3 * DN_WIDTH
    state_spec = pl.BlockSpec((None, N_HEADS, HEAD_DIM, HEAD_DIM), lambda i: (i // tiles_per_batch, 0, 0, 0))
    return pl.pallas_call(
        functools.partial(_delta_kernel, tc=tc, tiles_per_batch=tiles_per_batch),
        grid=(t // tc,),
        in_specs=[pl.BlockSpec((tc, wqkv), lambda i: (i, 0)),
                  pl.BlockSpec((SC_HALO, wqkv), lambda i: (jnp.maximum(i * halo_blocks - 1, 0), 0)),
                  pl.BlockSpec((None, SC_HALO, wqkv), lambda i: (i // tiles_per_batch, 0, 0)),
                  _const_spec(scw),
                  pl.BlockSpec((tc, LANES), lambda i: (i, 0)),
                  pl.BlockSpec((tc, DN_WIDTH), lambda i: (i, 0)),
                  state_spec,
                  _const_spec(dng)],
        out_specs=[pl.BlockSpec((tc, DN_WIDTH), lambda i: (i, 0)), state_spec],
        out_shape=[jax.ShapeDtypeStruct((t, DN_WIDTH), BF16),
                   jax.ShapeDtypeStruct((nb, N_HEADS, HEAD_DIM, HEAD_DIM), F32)],
        scratch_shapes=[pltpu.VMEM((SC_HALO + tc, wqkv), F32), pltpu.VMEM((tc, wqkv), F32)],
        compiler_params=_cparams(),
        name="delta_branch",
    )(qkv2d, qkv2d, hist, scw, ab, zs, s0, dng)


def _step_front_kernel(u_ref, qkv_ref, ch_ref, sh_ref, cw_ref, cb_ref, lg_ref, lb_ref, scw_ref,
                       acta_ref, nch_ref, qkvp_ref, nsh_ref):
    hist_rows = CONV_WIDTH - 1
    u = u_ref[...]
    acc = cw_ref[hist_rows:hist_rows + 1, :] * u
    for j in range(hist_rows):
        acc = acc + cw_ref[j:j + 1, :] * ch_ref[:, j * CONV_CH:(j + 1) * CONV_CH]
    acta_ref[...] = _layer_norm_swish(acc + cb_ref[...], lg_ref[...], lb_ref[...]).astype(acta_ref.dtype)
    nch_ref[:, :(hist_rows - 1) * CONV_CH] = ch_ref[:, CONV_CH:]
    nch_ref[:, (hist_rows - 1) * CONV_CH:] = u

    wq = 3 * DN_WIDTH
    sc_rows = SHORT_CONV - 1
    qkv = qkv_ref[...]
    acc = scw_ref[sc_rows:sc_rows + 1, :] * qkv
    for j in range(sc_rows):
        acc = acc + scw_ref[j:j + 1, :] * sh_ref[:, j * wq:(j + 1) * wq]
    c = _silu(acc)
    for h in range(N_HEADS):
        lo = h * HEAD_DIM
        qkvp_ref[:, lo:lo + HEAD_DIM] = _l2n(c[:, lo:lo + HEAD_DIM]) * (HEAD_DIM ** -0.5)
        qkvp_ref[:, DN_WIDTH + lo:DN_WIDTH + lo + HEAD_DIM] = _l2n(c[:, DN_WIDTH + lo:DN_WIDTH + lo + HEAD_DIM])
    qkvp_ref[:, 2 * DN_WIDTH:] = c[:, 2 * DN_WIDTH:]
    nsh_ref[:, :(sc_rows - 1) * wq] = sh_ref[:, wq:]
    nsh_ref[:, (sc_rows - 1) * wq:] = qkv


def _step_front(u, qkv, conv_hist_flat, sc_hist_flat, conv_dw, conv_dw_b, ln_g, ln_b, scw, *, bt):
    n = u.shape[0]
    wq = 3 * DN_WIDTH
    consts = [conv_dw, conv_dw_b, ln_g, ln_b, scw]
    row = lambda c: pl.BlockSpec((bt, c), lambda i: (i, 0))
    widths = [CONV_CH, conv_hist_flat.shape[1], wq, sc_hist_flat.shape[1]]
    dtypes = [BF16, F32, F32, F32]
    return pl.pallas_call(
        _step_front_kernel,
        grid=(n // bt,),
        in_specs=[row(CONV_CH), row(wq), row(widths[1]), row(widths[3])] + [_const_spec(a) for a in consts],
        out_specs=[row(c) for c in widths],
        out_shape=[jax.ShapeDtypeStruct((n, c), d) for c, d in zip(widths, dtypes)],
        compiler_params=_cparams(),
        name="step_front",
    )(u, qkv, conv_hist_flat, sc_hist_flat, *consts)


def _rows_to_cols(x):
    pad = jnp.zeros((LANES - x.shape[0], LANES), x.dtype)
    return jnp.concatenate([x, pad], axis=0).T


def _step_delta_kernel(qkvp_ref, ab_ref, zs_ref, s_ref, dng_ref, o_ref, sn_ref, *, bt):
    ab = ab_ref[...]
    for h in range(N_HEADS):
        lo = h * HEAD_DIM
        q = qkvp_ref[:, lo:lo + HEAD_DIM]
        k = qkvp_ref[:, DN_WIDTH + lo:DN_WIDTH + lo + HEAD_DIM]
        v = qkvp_ref[:, 2 * DN_WIDTH + lo:2 * DN_WIDTH + lo + HEAD_DIM]
        q_cols = _rows_to_cols(q)
        k_cols = _rows_to_cols(k)
        qk = jnp.sum(q * k, axis=-1, keepdims=True)
        alpha = jnp.exp(ab[:, h:h + 1])
        beta = ab[:, N_HEADS + h:N_HEADS + h + 1]
        for j in range(bt):
            s = s_ref[j, h]
            kc = k_cols[:, j:j + 1]
            qc = q_cols[:, j:j + 1]
            a = alpha[j:j + 1, :]
            s_k = jnp.sum(s * kc, axis=0, keepdims=True)
            s_q = jnp.sum(s * qc, axis=0, keepdims=True)
            u = beta[j:j + 1, :] * (v[j:j + 1, :] - a * s_k)
            o = a * s_q + qk[j:j + 1, :] * u
            sn_ref[j, h] = a * s + kc * u
            o_ref[j:j + 1, lo:lo + HEAD_DIM] = _gated_head_norm(
                o, dng_ref[...], zs_ref[j:j + 1, lo:lo + HEAD_DIM]).astype(o_ref.dtype)


def _step_delta(qkvp, ab, zs, s0, dng, *, bt):
    n = qkvp.shape[0]
    row = lambda c: pl.BlockSpec((bt, c), lambda i: (i, 0))
    state_spec = pl.BlockSpec((bt, N_HEADS, HEAD_DIM, HEAD_DIM), lambda i: (i, 0, 0, 0))
    return pl.pallas_call(
        functools.partial(_step_delta_kernel, bt=bt),
        grid=(n // bt,),
        in_specs=[row(3 * DN_WIDTH), row(LANES), row(DN_WIDTH), state_spec, _const_spec(dng)],
        out_specs=[row(DN_WIDTH), state_spec],
        out_shape=[jax.ShapeDtypeStruct((n, DN_WIDTH), BF16), jax.ShapeDtypeStruct(s0.shape, F32)],
        compiler_params=_cparams(),
        name="step_delta",
    )(qkvp, ab, zs, s0, dng)


def _post_kernel(acta_ref, actb_ref, sga_ref, sgb_ref, x_ref, gate1_ref, scale2_ref, shift2_ref,
                 gate2_ref, cnt0_ref, gffn_ref, wpa_ref, wpb_ref, wo_ref, wrt_ref, ebias_ref,
                 wsgu_ref, wsd_ref,
                 xs_ref, h2_ref, eidx_ref, ew_ref, rank_ref, cnt_ref, *, tm):
    i = pl.program_id(0)

    @pl.when(i == 0)
    def _():
        cnt_ref[...] = cnt0_ref[...]

    y_a = jnp.dot(acta_ref[...], wpa_ref[...], preferred_element_type=F32)
    y_b = jnp.dot(actb_ref[...], wpb_ref[...], preferred_element_type=F32)
    merged = sga_ref[...] * y_a + sgb_ref[...] * y_b
    mix = _bdot(merged, wo_ref[...])
    x1 = x_ref[...] + gate1_ref[...] * mix
    y = x1 * lax.rsqrt(jnp.mean(x1 * x1, axis=-1, keepdims=True) + EPS) * gffn_ref[...]
    h2 = y * (1.0 + scale2_ref[...]) + shift2_ref[...]
    h2_ref[...] = _pack_bf16_pairs(h2)
    h2b = h2.astype(BF16)

    gu = jnp.dot(h2b, wsgu_ref[...], preferred_element_type=F32)
    shared = _bdot(_silu(gu[:, :D_SHARED]) * gu[:, D_SHARED:], wsd_ref[...])
    xs_ref[...] = x1 + gate2_ref[...] * shared

    logits_t = lax.dot_general(wrt_ref[...], h2b, NT_DIMS, preferred_element_type=F32)
    scores = _sigmoid(logits_t)
    biased = scores + ebias_ref[...]
    neg = -jnp.inf
    big = jnp.int32(1 << 30)
    erow = lax.broadcasted_iota(I32, (N_EXPERTS, tm), 0)

    def first_argmax(vals, rows):
        m = jnp.max(vals, axis=0, keepdims=True)
        return m, jnp.min(jnp.where(vals == m, rows, big), axis=0, keepdims=True)

    group_scores = []
    for g in range(N_GROUPS):
        vals = biased[g * GROUP_SIZE:(g + 1) * GROUP_SIZE, :]
        rows = lax.broadcasted_iota(I32, (GROUP_SIZE, tm), 0) + g * GROUP_SIZE
        m1, i1 = first_argmax(vals, rows)
        m2 = jnp.max(jnp.where(rows == i1, neg, vals), axis=0, keepdims=True)
        group_scores.append(m1 + m2)
    group_sel = [jnp.zeros((1, tm), jnp.bool_)] * N_GROUPS
    for _ in range(TOPK_GROUPS):
        best = functools.reduce(jnp.maximum, group_scores)
        gi = functools.reduce(
            jnp.minimum, [jnp.where(group_scores[g] == best, jnp.int32(g), big) for g in range(N_GROUPS)])
        for g in range(N_GROUPS):
            hit = gi == g
            group_sel[g] = jnp.logical_or(group_sel[g], hit)
            group_scores[g] = jnp.where(hit, neg, group_scores[g])
    cand = jnp.concatenate(
        [jnp.where(group_sel[g], biased[g * GROUP_SIZE:(g + 1) * GROUP_SIZE, :], neg)
         for g in range(N_GROUPS)], axis=0)

    w_rows, hits = [], []
    picked = jnp.zeros((N_EXPERTS, tm), F32)
    for k in range(TOP_K):
        _, ei = first_argmax(cand, erow)
        hit = erow == ei
        eidx_ref[k:k + 1, :] = ei
        w_rows.append(jnp.sum(jnp.where(hit, scores, 0.0), axis=0, keepdims=True))
        hits.append(hit)
        picked = jnp.where(hit, 1.0, picked)
        cand = jnp.where(hit, neg, cand)
    w_sum = functools.reduce(jnp.add, w_rows)
    for k in range(TOP_K):
        ew_ref[k:k + 1, :] = w_rows[k] / w_sum * ROUTE_SCALE

    trow = lax.broadcasted_iota(I32, (tm, tm), 0)
    tcol = lax.broadcasted_iota(I32, (tm, tm), 1)
    before = (trow < tcol).astype(BF16)
    prefix = jnp.dot(picked.astype(BF16), before, preferred_element_type=F32) + cnt_ref[:, 0:1]
    for k in range(TOP_K):
        rank_ref[k:k + 1, :] = jnp.sum(jnp.where(hits[k], prefix, 0.0), axis=0, keepdims=True).astype(I32)
    cnt_ref[...] = cnt_ref[...] + jnp.sum(picked, axis=1, keepdims=True)


def _post(acta, actb, sga, sgb, x2d, gate1, scale2, shift2, gate2, cnt0, g_ffn, w, ebias_col,
          *, tm, tiles_per_mod):
    t = x2d.shape[0]
    row = lambda c: pl.BlockSpec((tm, c), lambda i: (i, 0))
    col = lambda r: pl.BlockSpec((r, tm), lambda i: (0, i))
    mods = [gate1, scale2, shift2, gate2]
    consts = [cnt0, g_ffn, w["pa"], w["pb"], w["o"], w["router_t"], ebias_col, w["s_gu"], w["s_down"]]
    return pl.pallas_call(
        functools.partial(_post_kernel, tm=tm),
        grid=(t // tm,),
        in_specs=[row(CONV_CH), row(DN_WIDTH), row(D_MODEL), row(D_MODEL), row(D_MODEL)]
                 + [_mod_spec(m, tiles_per_mod) for m in mods] + [_const_spec(a) for a in consts],
        out_specs=[row(D_MODEL), row(D_MODEL // 2), col(TOP_K), col(TOP_K), col(TOP_K), _const_spec(cnt0)],
        out_shape=[jax.ShapeDtypeStruct((t, D_MODEL), F32), jax.ShapeDtypeStruct((t, D_MODEL // 2), U32),
                   jax.ShapeDtypeStruct((TOP_K, t), I32), jax.ShapeDtypeStruct((TOP_K, t), F32),
                   jax.ShapeDtypeStruct((TOP_K, t), I32), jax.ShapeDtypeStruct(cnt0.shape, F32)],
        compiler_params=_cparams(),
        name="post_mixer",
    )(acta, actb, sga, sgb, x2d, *mods, *consts)


def _dest_kernel(start_ref, eidx_ref, rank_ref, o_ref, *, width):
    step = SUBLANES * LANES
    for c0 in range(0, width, step):
        c1 = min(c0 + step, width)
        e = eidx_ref[:, c0:c1]
        base = lax.fori_loop(0, N_EXPERTS, lambda x, acc: jnp.where(e == x, start_ref[x], acc),
                             jnp.zeros(e.shape, I32), unroll=8)
        o_ref[:, c0:c1] = base + rank_ref[:, c0:c1]


def _dest(cnt_start, e_idx, rank):
    k, t = e_idx.shape
    full = pl.BlockSpec((k, t), lambda i, st: (0, 0))
    return pl.pallas_call(
        functools.partial(_dest_kernel, width=t),
        grid_spec=pltpu.PrefetchScalarGridSpec(
            num_scalar_prefetch=1, grid=(1,), in_specs=[full, full], out_specs=full),
        out_shape=jax.ShapeDtypeStruct((k, t), I32),
        compiler_params=_cparams(),
        name="moe_dest",
    )(cnt_start, e_idx, rank)


def _sc_mesh():
    return plsc.VectorSubcoreMesh(core_axis_name="core", subcore_axis_name="subcore")


def _sc_pipeline(body, n_steps, in_specs, out_specs):
    return pltpu.emit_pipeline(body, grid=(n_steps,), in_specs=in_specs, out_specs=out_specs,
                               core_axis_name=("core", "subcore"), dimension_semantics=(pltpu.PARALLEL,))


def _sc_scatter_rows(x, half, dest_flat, n_rows):
    t = x.shape[0]
    tiles = t // SC_WINDOW
    n_steps = dest_flat.shape[1] // SC_WINDOW

    @functools.partial(pl.kernel, mesh=_sc_mesh(), scratch_types=[],
                       out_type=jax.ShapeDtypeStruct((n_rows, QUARTER_WORDS), x.dtype))
    def scatter(x_hbm, i_hbm, o_hbm):
        def body(x_vmem, i_vmem):
            pltpu.sync_copy(x_vmem, o_hbm.at[i_vmem.at[0]])

        _sc_pipeline(body, n_steps,
                     [pl.BlockSpec((SC_WINDOW, QUARTER_WORDS), lambda i: (i % tiles, half)),
                      pl.BlockSpec((1, SC_WINDOW), lambda i: (0, i))], [])(x_hbm, i_hbm)

    return scatter(x, dest_flat)


def _sc_gather_rows(table, idx_flat):
    n = idx_flat.shape[1]

    @functools.partial(pl.kernel, mesh=_sc_mesh(), scratch_types=[],
                       out_type=jax.ShapeDtypeStruct((n, table.shape[1]), table.dtype))
    def gather(t_hbm, i_hbm, o_hbm):
        def body(i_vmem, o_vmem):
            pltpu.sync_copy(t_hbm.at[i_vmem.at[0]], o_vmem)

        _sc_pipeline(body, n // SC_WINDOW,
                     [pl.BlockSpec((1, SC_WINDOW), lambda i: (0, i))],
                     [pl.BlockSpec((SC_WINDOW, table.shape[1]), lambda i: (i, 0))])(i_hbm, o_hbm)

    return gather(table, idx_flat)


def _two_group_specs(tm, width, n_prompt_tiles):
    return [pl.BlockSpec((tm, width), lambda i: (jnp.minimum(i, n_prompt_tiles - 1), 0)),
            pl.BlockSpec((tm, width), lambda i: (jnp.maximum(i - n_prompt_tiles, 0), 0))]


def _experts_kernel(sblk_ref, sexp_ref, slo_ref, shi_ref, snext_ref, first_ref,
                    xa_ref, xb_ref, wg_hbm, wu_hbm, wd_hbm, ya_ref, yb_ref,
                    wg_buf, wu_buf, wd_buf, wgu_s, wd_s, state_ref, sems, *, bm):
    p = pl.program_id(0)
    lo = slo_ref[p]
    expert = sexp_ref[p]

    def weight_copies(e, slot):
        return [pltpu.make_async_copy(src.at[e], dst.at[slot], sems.at[slot])
                for src, dst in ((wg_hbm, wg_buf), (wu_hbm, wu_buf), (wd_hbm, wd_buf))]

    @pl.when(p == 0)
    def _():
        state_ref[0] = -1
        state_ref[1] = 0
        for c in weight_copies(first_ref[0], 0):
            c.start()

    @pl.when(shi_ref[p] > lo)
    def _():
        @pl.when(state_ref[0] != expert)
        def _():
            slot = state_ref[1]
            for c in weight_copies(expert, slot):
                c.wait()
            wgu_s[:, :D_EXPERT] = wg_buf[slot].astype(BF16)
            wgu_s[:, D_EXPERT:] = wu_buf[slot].astype(BF16)
            wd_s[...] = wd_buf[slot].astype(BF16)
            state_ref[0] = expert
            state_ref[1] = 1 - slot
            upcoming = snext_ref[p]

            @pl.when(upcoming >= 0)
            def _():
                for c in weight_copies(upcoming, 1 - slot):
                    c.start()

        q = QUARTER_WORDS
        quarters = {}
        quarters[0], quarters[2] = _unpack_bf16_pairs(xa_ref[...])
        quarters[1], quarters[3] = _unpack_bf16_pairs(xb_ref[...])
        gu = functools.reduce(jnp.add, [
            jnp.dot(quarters[c], wgu_s[c * q:(c + 1) * q, :], preferred_element_type=F32) for c in range(4)])
        hb = (_silu(gu[:, :D_EXPERT]) * gu[:, D_EXPERT:]).astype(BF16)
        y = _pack_bf16_pairs(jnp.dot(hb, wd_s[...], preferred_element_type=F32))

        @pl.when(lo == 0)
        def _():
            ya_ref[...] = y[:, :q]
            yb_ref[...] = y[:, q:]

        @pl.when(lo > 0)
        def _():
            keep = lax.broadcasted_iota(I32, (bm, 1), 0) >= lo
            ya_ref[...] = jnp.where(keep, y[:, :q], ya_ref[...])
            yb_ref[...] = jnp.where(keep, y[:, q:], yb_ref[...])


def _experts(xs_a, xs_b, w_gate, w_up, w_down, seg_blk, seg_exp, seg_lo, seg_hi, seg_next, first_exp, *, bm):
    n_rows = xs_a.shape[0]
    n_seg = seg_blk.shape[0]
    ring = 2
    hbm = pl.BlockSpec(memory_space=pl.ANY)
    rows = pl.BlockSpec((bm, QUARTER_WORDS), lambda p, sb, *_: (sb[p], 0))
    grid_spec = pltpu.PrefetchScalarGridSpec(
        num_scalar_prefetch=6,
        grid=(n_seg,),
        in_specs=[rows, rows, hbm, hbm, hbm],
        out_specs=[rows, rows],
        scratch_shapes=[pltpu.VMEM((ring, D_MODEL, D_EXPERT), F32), pltpu.VMEM((ring, D_MODEL, D_EXPERT), F32),
                        pltpu.VMEM((ring, D_EXPERT, D_MODEL), F32),
                        pltpu.VMEM((D_MODEL, 2 * D_EXPERT), BF16), pltpu.VMEM((D_EXPERT, D_MODEL), BF16),
                        pltpu.SMEM((2,), I32), pltpu.SemaphoreType.DMA((ring,))],
    )
    return pl.pallas_call(
        functools.partial(_experts_kernel, bm=bm),
        grid_spec=grid_spec,
        out_shape=[jax.ShapeDtypeStruct((n_rows, QUARTER_WORDS), U32)] * 2,
        compiler_params=_cparams(),
        name="moe_experts",
    )(seg_blk, seg_exp, seg_lo, seg_hi, seg_next, first_exp, xs_a, xs_b, w_gate, w_up, w_down)


def _final_kernel(ew_ref, pa_ref, pb_ref, xsp_ref, xss_ref, g2p_ref, g2s_ref, gfin_ref, op_ref, os_ref,
                  *, tm, n_prompt_tiles):
    ew = ew_ref[...]
    quarters = [jnp.zeros((tm, QUARTER_WORDS), F32)] * 4
    for k in range(TOP_K):
        w_k = ew[:, k:k + 1]
        a_low, a_high = _unpack_bf16_pairs(pa_ref[k])
        b_low, b_high = _unpack_bf16_pairs(pb_ref[k])
        for c, part in enumerate((a_low, b_low, a_high, b_high)):
            quarters[c] = quarters[c] + w_k * part.astype(F32)
    routed = jnp.concatenate(quarters, axis=1)

    def finish(xs_ref, gate2_ref, o_ref):
        x2 = xs_ref[...] + gate2_ref[...] * routed
        o_ref[...] = x2 * lax.rsqrt(jnp.mean(x2 * x2, axis=-1, keepdims=True) + EPS) * gfin_ref[...]

    is_prompt = pl.program_id(0) < n_prompt_tiles
    pl.when(is_prompt)(lambda: finish(xsp_ref, g2p_ref, op_ref))
    pl.when(jnp.logical_not(is_prompt))(lambda: finish(xss_ref, g2s_ref, os_ref))


def _final(ew_tok, planes_a, planes_b, xs_p, xs_s, gate2_p, gate2_s, g_final, *, tm, tiles_per_batch):
    n_prompt_tiles = xs_p.shape[0] // tm
    n_tiles = n_prompt_tiles + xs_s.shape[0] // tm
    nb = gate2_p.shape[0]
    planes = pl.BlockSpec((TOP_K, tm, QUARTER_WORDS), lambda i: (0, i, 0))
    return pl.pallas_call(
        functools.partial(_final_kernel, tm=tm, n_prompt_tiles=n_prompt_tiles),
        grid=(n_tiles,),
        in_specs=[pl.BlockSpec((tm, TOP_K), lambda i: (i, 0)), planes, planes]
                 + _two_group_specs(tm, D_MODEL, n_prompt_tiles)
                 + [pl.BlockSpec((None, 1, D_MODEL), lambda i: (jnp.minimum(i // tiles_per_batch, nb - 1), 0, 0)),
                    pl.BlockSpec((None, tm, D_MODEL), lambda i: (jnp.maximum(i - n_prompt_tiles, 0), 0, 0)),
                    _const_spec(g_final)],
        out_specs=_two_group_specs(tm, D_MODEL, n_prompt_tiles),
        out_shape=[jax.ShapeDtypeStruct(xs_p.shape, F32), jax.ShapeDtypeStruct(xs_s.shape, F32)],
        compiler_params=_cparams(),
        name="moe_combine_final",
    )(ew_tok, planes_a, planes_b, xs_p, xs_s, gate2_p, gate2_s, g_final)


def _pick_tile(n, preferred):
    t = min(n, preferred)
    assert n % t == 0, (n, t)
    return t


def kernel(x_prompt, x_sample, c_prompt, c_sample, state_conv, state_short_conv, state_delta, w_ada, b_ada, g_mix, w_in, conv_dw, conv_dw_b, conv_ln_g, conv_ln_b, w_pa, short_conv_w, a_log, dt_bias, dn_norm_g, w_pb, w_o, g_ffn, w_router, e_bias, w_gate, w_up, w_down, ws_gate, ws_up, ws_down, g_final):
    nb, seq, d = x_prompt.shape
    ns = x_sample.shape[0]
    assert d == D_MODEL and w_ada.shape[0] == 1 and x_sample.shape[1] == 1
    assert seq % CHUNK == 0 and seq >= CONV_HALO and ns % SUBLANES == 0
    tp = nb * seq
    row1 = lambda a: a.reshape(1, -1)

    wi = w_in[0].astype(BF16)
    o_q = 2 * CONV_CH
    o_z = o_q + 3 * DN_WIDTH
    o_a = o_z + DN_WIDTH
    o_ga = o_a + 2 * N_HEADS
    w_proj = {
        "glu": wi[:, :o_q], "qkv": wi[:, o_q:o_z], "z": wi[:, o_z:o_a],
        "ab": jnp.pad(wi[:, o_a:o_ga], ((0, 0), (0, LANES - 2 * N_HEADS))),
        "ga": wi[:, o_ga:o_ga + D_MODEL], "gb": wi[:, o_ga + D_MODEL:],
    }
    nea = jnp.pad(-jnp.exp(a_log[0].astype(F32)), (0, LANES - N_HEADS)).reshape(1, LANES)
    dtb = jnp.pad(dt_bias[0].astype(F32), (0, LANES - N_HEADS)).reshape(1, LANES)
    w_post = {
        "pa": w_pa[0].astype(BF16), "pb": w_pb[0].astype(BF16), "o": w_o[0].astype(BF16),
        "router_t": w_router[0].T.astype(BF16),
        "s_gu": jnp.concatenate([ws_gate[0], ws_up[0]], axis=1).astype(BF16),
        "s_down": ws_down[0].astype(BF16),
    }
    ebias_col = e_bias[0].astype(F32).reshape(N_EXPERTS, 1)
    g_mix1, g_ffn1, g_fin1 = row1(g_mix[0]), row1(g_ffn[0]), row1(g_final)
    cdw, cdb, clg, clb = conv_dw[0], row1(conv_dw_b[0]), row1(conv_ln_g[0]), row1(conv_ln_b[0])
    scw, dng = short_conv_w[0], row1(dn_norm_g[0])

    ada = _ada(jnp.concatenate([c_prompt, c_sample], axis=0), w_ada[0].astype(BF16), b_ada[0])
    ada = ada.reshape(nb + ns, 6, D_MODEL)
    mod_p = [ada[:nb, m].reshape(nb, 1, D_MODEL) for m in range(6)]
    mod_s = [ada[nb:, m].reshape(1, ns, D_MODEL) for m in range(6)]
    shift1, scale1, gate1, shift2, scale2, gate2 = range(6)

    tm_p = _pick_tile(seq, 256)
    tpm_p = seq // tm_p
    xp = x_prompt.reshape(tp, D_MODEL)
    u_p, qkv_p, zs_p, ab_p, sga_p, sgb_p = _in_proj(
        xp, mod_p[scale1], mod_p[shift1], g_mix1, w_proj, nea, dtb, tm=tm_p, tiles_per_mod=tpm_p)
    tl = _pick_tile(seq, 256)
    acta_p = _conv_branch(u_p, jnp.zeros((nb, CONV_HALO, CONV_CH), F32), cdw, cdb, clg, clb,
                          tl=tl, tiles_per_batch=seq // tl)
    tc = _pick_tile(seq, 4 * CHUNK)
    actb_p, s_new_p = _delta_branch(
        qkv_p, jnp.zeros((nb, SC_HALO, 3 * DN_WIDTH), F32),
        jnp.zeros((nb, N_HEADS, HEAD_DIM, HEAD_DIM), F32), scw, ab_p, zs_p, dng,
        tc=tc, tiles_per_batch=seq // tc)

    xs_ = x_sample.reshape(ns, D_MODEL)
    u_s, qkv_s, zs_s, ab_s, sga_s, sgb_s = _in_proj(
        xs_, mod_s[scale1], mod_s[shift1], g_mix1, w_proj, nea, dtb, tm=ns, tiles_per_mod=1)
    bt = SUBLANES
    acta_s, nch_s, qkvp_s, nsh_s = _step_front(
        u_s, qkv_s, state_conv[0].reshape(ns, -1), state_short_conv[0].reshape(ns, -1),
        cdw, cdb, clg, clb, scw, bt=bt)
    actb_s, s_new_s = _step_delta(qkvp_s, ab_s, zs_s, state_delta[0].astype(F32), dng, bt=bt)

    cnt0 = jnp.zeros((N_EXPERTS, LANES), F32)
    xsh_p, h2_p, eidx_p, ew_p, rank_p, cnt_p = _post(
        acta_p, actb_p, sga_p, sgb_p, xp, mod_p[gate1], mod_p[scale2], mod_p[shift2], mod_p[gate2],
        cnt0, g_ffn1, w_post, ebias_col, tm=tm_p, tiles_per_mod=tpm_p)
    xsh_s, h2_s, eidx_s, ew_s, rank_s, cnt_all = _post(
        acta_s, actb_s, sga_s, sgb_s, xs_, mod_s[gate1], mod_s[scale2], mod_s[shift2], mod_s[gate2],
        cnt_p, g_ffn1, w_post, ebias_col, tm=ns, tiles_per_mod=1)

    bm = 256
    tmd = LANES
    t_all = tp + ns
    n_rows = t_all * TOP_K
    assert seq % tmd == 0 and ns % tmd == 0 and n_rows % bm == 0 and t_all % SC_WINDOW == 0
    n_blocks = n_rows // bm
    counts = cnt_all[:, 0].astype(I32)
    cnt_end = jnp.cumsum(counts)
    cnt_start = cnt_end - counts
    e_idx = jnp.concatenate([eidx_p, eidx_s], axis=1)
    rank = jnp.concatenate([rank_p, rank_s], axis=1)
    dest_flat = _dest(cnt_start, e_idx, rank).reshape(1, n_rows)
    seg_start = jnp.sort(jnp.concatenate([jnp.arange(n_blocks, dtype=I32) * bm, cnt_start]))
    seg_end = jnp.concatenate([seg_start[1:], jnp.full((1,), n_rows, I32)])
    seg_blk = jnp.minimum(seg_start // bm, n_blocks - 1)
    seg_lo = seg_start - seg_blk * bm
    seg_hi = seg_end - seg_blk * bm
    seg_exp = jnp.sum((cnt_end[None, :] <= seg_start[:, None]).astype(I32), axis=1)
    seg_exp = jnp.minimum(seg_exp, N_EXPERTS - 1)
    seg_exp = lax.cummax(jnp.where(seg_hi > seg_lo, seg_exp, 0))
    none = jnp.int32(N_EXPERTS)
    in_use_from = lax.cummin(jnp.where(counts > 0, jnp.arange(N_EXPERTS, dtype=I32), none), reverse=True)
    next_in_use = jnp.concatenate([in_use_from[1:], jnp.full((1,), none, I32)])
    next_in_use = jnp.where(next_in_use == none, -1, next_in_use)
    seg_next = next_in_use[seg_exp]
    first_exp = jnp.minimum(in_use_from[:1], N_EXPERTS - 1)

    h2_all = jnp.concatenate([h2_p, h2_s], axis=0)
    xs_a, xs_b = (_sc_scatter_rows(h2_all, half, dest_flat, n_rows) for half in range(2))
    ys_a, ys_b = _experts(xs_a, xs_b, w_gate[0], w_up[0], w_down[0], seg_blk, seg_exp, seg_lo, seg_hi, seg_next,
                          first_exp, bm=bm)

    planes_a, planes_b = (_sc_gather_rows(ys, dest_flat).reshape(TOP_K, t_all, QUARTER_WORDS) for ys in (ys_a, ys_b))
    ew_tok = jnp.concatenate([ew_p, ew_s], axis=1).T
    y_p, y_s = _final(ew_tok, planes_a, planes_b, xsh_p, xsh_s, mod_p[gate2],
                      mod_s[gate2].reshape(ns // tmd, tmd, D_MODEL), g_fin1, tm=tmd, tiles_per_batch=seq // tmd)

    hist_rows = CONV_WIDTH - 1
    sc_rows = SHORT_CONV - 1
    y_prompt = y_p.reshape(nb, seq, D_MODEL)
    y_sample = y_s.reshape(ns, 1, D_MODEL)
    new_conv_prompt = u_p.reshape(nb, seq, CONV_CH)[:, seq - hist_rows:][None]
    new_sc_prompt = qkv_p.reshape(nb, seq, 3 * DN_WIDTH)[:, seq - sc_rows:][None]
    new_conv_sample = nch_s.reshape(ns, hist_rows, CONV_CH)[None]
    new_sc_sample = nsh_s.reshape(ns, sc_rows, 3 * DN_WIDTH)[None]
    return (y_prompt, y_sample, new_conv_prompt, new_sc_prompt, s_new_p[None],
            new_conv_sample, new_sc_sample, s_new_s[None])
```

```python
import functools

import jax
import jax.numpy as jnp
from jax import lax
from jax.experimental import pallas as pl
from jax.experimental.pallas import tpu as pltpu
from jax.experimental.pallas import tpu_sc as plsc

F32 = jnp.float32
BF16 = jnp.bfloat16
I32 = jnp.int32
U32 = jnp.uint32

EPS = 1e-6
D_MODEL = 1024
CONV_CH = 512
CONV_WIDTH = 31
N_HEADS = 4
HEAD_DIM = 128
DN_WIDTH = N_HEADS * HEAD_DIM
SHORT_CONV = 4
CHUNK = 64
CHUNK_SHIFT = CHUNK.bit_length() - 1
assert 1 << CHUNK_SHIFT == CHUNK
SUB = 16
SUB_SHIFT = SUB.bit_length() - 1
assert 1 << SUB_SHIFT == SUB and CHUNK % SUB == 0
N_EXPERTS = 256
TOP_K = 8
N_GROUPS = 8
GROUP_SIZE = N_EXPERTS // N_GROUPS
TOPK_GROUPS = 4
D_EXPERT = 256
D_SHARED = 256
ROUTE_SCALE = 2.5

LANES = 128
SUBLANES = 8
CONV_HALO = 32
SC_HALO = 8
VMEM_LIMIT = 48 * 1024 * 1024
SC_WINDOW = 128
QUARTER_WORDS = D_MODEL // 4

NT_DIMS = (((1,), (1,)), ((), ()))
TN_DIMS = (((0,), (0,)), ((), ()))


def _cparams(n_grid_dims=1):
    return pltpu.CompilerParams(
        dimension_semantics=("arbitrary",) * n_grid_dims,
        vmem_limit_bytes=VMEM_LIMIT)


def _sigmoid(x):
    return jax.nn.sigmoid(x)


def _silu(x):
    return x * jax.nn.sigmoid(x)


def _bdot(a, b):
    return jnp.dot(a.astype(BF16), b.astype(BF16), preferred_element_type=F32)


def _pack_bf16_pairs(x):
    half = x.shape[1] // 2
    bits = lax.bitcast_convert_type(x.astype(BF16).astype(F32), U32)
    return bits[:, half:] | (bits[:, :half] >> 16)


def _unpack_bf16_pairs(w):
    low = lax.bitcast_convert_type(w << 16, F32).astype(BF16)
    high = lax.bitcast_convert_type(w & jnp.uint32(0xFFFF0000), F32).astype(BF16)
    return low, high


def _dot3(a, b):
    a_hi = a.astype(BF16)
    b_hi = b.astype(BF16)
    a_lo = (a - a_hi.astype(F32)).astype(BF16)
    b_lo = (b - b_hi.astype(F32)).astype(BF16)
    dot = lambda x, y: jnp.dot(x, y, preferred_element_type=F32)
    return dot(a_hi, b_hi) + (dot(a_hi, b_lo) + dot(a_lo, b_hi))


def _dot_exact_lhs(a_bf, b):
    hi = b.astype(BF16)
    r1 = b - hi.astype(F32)
    mid = r1.astype(BF16)
    low = (r1 - mid.astype(F32)).astype(BF16)
    dot = lambda p: jnp.dot(a_bf, p, preferred_element_type=F32)
    return dot(hi) + dot(mid) + dot(low)


def _ada_kernel(c_ref, w_ref, b_ref, o_ref):
    o_ref[...] = _bdot(_silu(c_ref[...]), w_ref[...]) + b_ref[...]


def _ada(c_all, w_ada_bf, b_ada):
    n = c_all.shape[0]
    width = w_ada_bf.shape[1]
    tn = 1536
    return pl.pallas_call(
        _ada_kernel,
        grid=(width // tn,),
        in_specs=[pl.BlockSpec((n, D_MODEL), lambda j: (0, 0)),
                  pl.BlockSpec((D_MODEL, tn), lambda j: (0, j)),
                  pl.BlockSpec((1, tn), lambda j: (0, j))],
        out_specs=pl.BlockSpec((n, tn), lambda j: (0, j)),
        out_shape=jax.ShapeDtypeStruct((n, width), F32),
        compiler_params=_cparams(),
        name="ada",
    )(c_all, w_ada_bf, b_ada.reshape(1, width))


def _in_proj_kernel(x_ref, scale_ref, shift_ref, g_ref, wglu_ref, wqkv_ref, wz_ref, wab_ref,
                    wga_ref, wgb_ref, nea_ref, dtb_ref,
                    u_ref, qkv_ref, zs_ref, ab_ref, sga_ref, sgb_ref):
    x = x_ref[...]
    y = x * lax.rsqrt(jnp.mean(x * x, axis=-1, keepdims=True) + EPS) * g_ref[...]
    h = (y * (1.0 + scale_ref[...]) + shift_ref[...]).astype(BF16)
    glu = jnp.dot(h, wglu_ref[...], preferred_element_type=F32)
    u_ref[...] = glu[:, :CONV_CH] * _sigmoid(glu[:, CONV_CH:])
    qkv_ref[...] = jnp.dot(h, wqkv_ref[...], preferred_element_type=F32)
    zs_ref[...] = _silu(jnp.dot(h, wz_ref[...], preferred_element_type=F32))
    ab = jnp.dot(h, wab_ref[...], preferred_element_type=F32)
    sp = ab + dtb_ref[...]
    softplus = jnp.maximum(sp, 0.0) + jnp.log1p(jnp.exp(-jnp.abs(sp)))
    lane = lax.broadcasted_iota(I32, ab.shape, 1)
    ab_ref[...] = jnp.where(lane < N_HEADS, nea_ref[...] * softplus, _sigmoid(ab))
    sga_ref[...] = _sigmoid(jnp.dot(h, wga_ref[...], preferred_element_type=F32))
    sgb_ref[...] = _sigmoid(jnp.dot(h, wgb_ref[...], preferred_element_type=F32))


def _mod_spec(mod, tiles_per_mod):
    return pl.BlockSpec((None,) + mod.shape[1:], lambda i: (i // tiles_per_mod, 0, 0))


def _const_spec(a):
    nd = a.ndim
    return pl.BlockSpec(a.shape, lambda i: (0,) * nd)


def _in_proj(x2d, scale, shift, g_mix, w, nea, dtb, *, tm, tiles_per_mod):
    t = x2d.shape[0]
    row = lambda c: pl.BlockSpec((tm, c), lambda i: (i, 0))
    consts = [g_mix, w["glu"], w["qkv"], w["z"], w["ab"], w["ga"], w["gb"], nea, dtb]
    widths = [CONV_CH, 3 * DN_WIDTH, DN_WIDTH, LANES, D_MODEL, D_MODEL]
    return pl.pallas_call(
        _in_proj_kernel,
        grid=(t // tm,),
        in_specs=[row(D_MODEL), _mod_spec(scale, tiles_per_mod), _mod_spec(shift, tiles_per_mod)]
                 + [_const_spec(a) for a in consts],
        out_specs=[row(c) for c in widths],
        out_shape=[jax.ShapeDtypeStruct((t, c), F32) for c in widths],
        compiler_params=_cparams(),
        name="in_proj",
    )(x2d, scale, shift, *consts)


def _layer_norm_swish(y, g, b):
    mu = jnp.mean(y, axis=-1, keepdims=True)
    yc = y - mu
    var = jnp.mean(yc * yc, axis=-1, keepdims=True)
    return _silu(yc * lax.rsqrt(var + EPS) * g + b)


def _conv_kernel(u_ref, halo_ref, hist_ref, w_ref, b_ref, lg_ref, lb_ref, o_ref, win_ref, shift_ref,
                 *, tl, tiles_per_batch, rows_per_pass):
    i = pl.program_id(0)
    first = (i % tiles_per_batch) == 0

    @pl.when(first)
    def _():
        win_ref[0:CONV_HALO, :] = hist_ref[...]

    @pl.when(jnp.logical_not(first))
    def _():
        win_ref[0:CONV_HALO, :] = halo_ref[...]

    win_ref[CONV_HALO:CONV_HALO + tl, :] = u_ref[...]
    tap0 = CONV_HALO - (CONV_WIDTH - 1)
    n_shifted = CONV_HALO + tl - SUBLANES
    for b in range(1, SUBLANES):
        shift_ref[b - 1] = win_ref[b:b + n_shifted, :]

    for r in range(tl // rows_per_pass):
        base = r * rows_per_pass
        acc = jnp.zeros((rows_per_pass, CONV_CH), F32)
        for j in range(CONV_WIDTH):
            a, b = divmod(tap0 + j, SUBLANES)
            start = base + a * SUBLANES
            rows = (win_ref[start:start + rows_per_pass, :] if b == 0
                    else shift_ref[b - 1, start:start + rows_per_pass, :])
            acc = acc + w_ref[j:j + 1, :] * rows
        act = _layer_norm_swish(acc + b_ref[...], lg_ref[...], lb_ref[...])
        o_ref[base:base + rows_per_pass, :] = act.astype(o_ref.dtype)


def _conv_branch(u2d, hist, conv_dw, conv_dw_b, ln_g, ln_b, *, tl, tiles_per_batch):
    t = u2d.shape[0]
    halo_blocks = tl // CONV_HALO
    consts = [conv_dw, conv_dw_b, ln_g, ln_b]
    return pl.pallas_call(
        functools.partial(_conv_kernel, tl=tl, tiles_per_batch=tiles_per_batch, rows_per_pass=64),
        grid=(t // tl,),
        in_specs=[pl.BlockSpec((tl, CONV_CH), lambda i: (i, 0)),
                  pl.BlockSpec((CONV_HALO, CONV_CH), lambda i: (jnp.maximum(i * halo_blocks - 1, 0), 0)),
                  pl.BlockSpec((None, CONV_HALO, CONV_CH), lambda i: (i // tiles_per_batch, 0, 0))]
                 + [_const_spec(a) for a in consts],
        out_specs=pl.BlockSpec((tl, CONV_CH), lambda i: (i, 0)),
        out_shape=jax.ShapeDtypeStruct((t, CONV_CH), BF16),
        scratch_shapes=[pltpu.VMEM((CONV_HALO + tl, CONV_CH), F32),
                        pltpu.VMEM((SUBLANES - 1, CONV_HALO + tl - SUBLANES, CONV_CH), F32)],
        compiler_params=_cparams(),
        name="conv_branch",
    )(u2d, u2d, hist, *consts)


def _l2n(x):
    return x * lax.rsqrt(jnp.sum(x * x, axis=-1, keepdims=True) + 1e-6)


def _gated_head_norm(o, g, zs):
    return o * lax.rsqrt(jnp.mean(o * o, axis=-1, keepdims=True) + EPS) * g * zs


def _delta_kernel(qkv_ref, halo_ref, hist_ref, scw_ref, ab_ref, zs_ref, s0_ref, dng_ref,
                  o_ref, s_ref, win_ref, c_ref, *, tc, tiles_per_batch):
    i = pl.program_id(0)
    first = (i % tiles_per_batch) == 0

    @pl.when(first)
    def _():
        win_ref[0:SC_HALO, :] = hist_ref[...]
        s_ref[...] = s0_ref[...]

    @pl.when(jnp.logical_not(first))
    def _():
        win_ref[0:SC_HALO, :] = halo_ref[...]

    win_ref[SC_HALO:SC_HALO + tc, :] = qkv_ref[...]
    tap0 = SC_HALO - (SHORT_CONV - 1)
    acc = jnp.zeros((tc, 3 * DN_WIDTH), F32)
    for j in range(SHORT_CONV):
        acc = acc + scw_ref[j:j + 1, :] * win_ref[tap0 + j:tap0 + j + tc, :]
    c_ref[...] = _silu(acc)

    row = lax.broadcasted_iota(I32, (tc, tc), 0)
    col = lax.broadcasted_iota(I32, (tc, tc), 1)
    same_chunk = lax.shift_right_logical(row, CHUNK_SHIFT) == lax.shift_right_logical(col, CHUNK_SHIFT)
    incl = jnp.logical_and(same_chunk, row >= col)
    row_sub = lax.shift_right_logical(row, SUB_SHIFT)
    col_sub = lax.shift_right_logical(col, SUB_SHIFT)
    strict_sub = jnp.logical_and(row_sub == col_sub, row > col)
    below_sub = jnp.logical_and(same_chunk, row_sub > col_sub)
    eye = (row == col).astype(F32)
    ab = ab_ref[...]
    gcum = _dot_exact_lhs(incl.astype(BF16), ab)
    gcum_t = gcum.T
    n_ch = tc // CHUNK

    heads = range(N_HEADS)
    q, k, gc, kk, a_qk, rhs, qd = [], [], [], [], [], [], []
    for h in heads:
        lo = h * HEAD_DIM
        q_h = _l2n(c_ref[:, lo:lo + HEAD_DIM]) * (HEAD_DIM ** -0.5)
        k_h = _l2n(c_ref[:, DN_WIDTH + lo:DN_WIDTH + lo + HEAD_DIM])
        v_h = c_ref[:, 2 * DN_WIDTH + lo:2 * DN_WIDTH + lo + HEAD_DIM]
        beta = ab[:, N_HEADS + h:N_HEADS + h + 1]
        gc_h = gcum[:, h:h + 1]
        gr = gcum_t[h:h + 1, :]
        decay = jnp.exp(jnp.where(incl, gc_h - gr, -jnp.inf))
        kb = k_h * beta
        kq = lax.dot_general(jnp.concatenate([kb, q_h], axis=0).astype(BF16), k_h.astype(BF16),
                             NT_DIMS, preferred_element_type=F32)
        e_gc = jnp.exp(gc_h)
        q.append(q_h)
        k.append(k_h)
        gc.append(gc_h)
        kk.append(kq[:tc] * decay)
        a_qk.append(kq[tc:] * decay)
        rhs.append(jnp.concatenate([v_h * beta, kb * e_gc], axis=1))
        qd.append(q_h * e_gc)

    pw = [jnp.where(strict_sub, kk[h], 0.0) for h in heads]
    d_inv = [eye - pw[h] for h in heads]
    for _ in range(SUB_SHIFT - 1):
        pw = [_bdot(pw[h], pw[h]) for h in heads]
        d_inv = [d_inv[h] + _bdot(d_inv[h], pw[h]) for h in heads]
    c_rhs = [_bdot(d_inv[h], rhs[h]) for h in heads]
    m_off = [_bdot(d_inv[h], jnp.where(below_sub, kk[h], 0.0)) for h in heads]
    sol = c_rhs
    for _ in range(CHUNK // SUB - 1):
        sol = [c_rhs[h] - _bdot(m_off[h], sol[h]) for h in heads]

    s = [s_ref[h] for h in heads]
    u_parts = [[] for _ in heads]
    o_parts = [[] for _ in heads]
    for c in range(n_ch):
        r0 = c * CHUNK
        for h in heads:
            u0_c = sol[h][r0:r0 + CHUNK, :HEAD_DIM]
            w_c = sol[h][r0:r0 + CHUNK, HEAD_DIM:]
            wq_s = _bdot(jnp.concatenate([w_c, qd[h][r0:r0 + CHUNK]], axis=0), s[h])
            u_c = u0_c - wq_s[:CHUNK]
            g_last = gcum[r0 + CHUNK - 1:r0 + CHUNK, h:h + 1]
            k_dec = k[h][r0:r0 + CHUNK] * jnp.exp(g_last - gc[h][r0:r0 + CHUNK])
            s[h] = jnp.exp(g_last) * s[h] + lax.dot_general(
                k_dec.astype(BF16), u_c.astype(BF16), TN_DIMS, preferred_element_type=F32)
            u_parts[h].append(u_c)
            o_parts[h].append(wq_s[CHUNK:])
    for h in heads:
        lo = h * HEAD_DIM
        s_ref[h] = s[h]
        o = jnp.concatenate(o_parts[h], axis=0) + _bdot(a_qk[h], jnp.concatenate(u_parts[h], axis=0))
        o_ref[:, lo:lo + HEAD_DIM] = _gated_head_norm(
            o, dng_ref[...], zs_ref[:, lo:lo + HEAD_DIM]).astype(o_ref.dtype)


def _delta_branch(qkv2d, hist, s0, scw, ab, zs, dng, *, tc, tiles_per_batch):
    t = qkv2d.shape[0]
    nb = s0.shape[0]
    halo_blocks = tc // SC_HALO
    wqkv = 3 * DN_WIDTH
    state_spec = pl.BlockSpec((None, N_HEADS, HEAD_DIM, HEAD_DIM), lambda i: (i // tiles_per_batch, 0, 0, 0))
    return pl.pallas_call(
        functools.partial(_delta_kernel, tc=tc, tiles_per_batch=tiles_per_batch),
        grid=(t // tc,),
        in_specs=[pl.BlockSpec((tc, wqkv), lambda i: (i, 0)),
                  pl.BlockSpec((SC_HALO, wqkv), lambda i: (jnp.maximum(i * halo_blocks - 1, 0), 0)),
                  pl.BlockSpec((None, SC_HALO, wqkv), lambda i: (i // tiles_per_batch, 0, 0)),
                  _const_spec(scw),
                  pl.BlockSpec((tc, LANES), lambda i: (i, 0)),
                  pl.BlockSpec((tc, DN_WIDTH), lambda i: (i, 0)),
                  state_spec,
                  _const_spec(dng)],
        out_specs=[pl.BlockSpec((tc, DN_WIDTH), lambda i: (i, 0)), state_spec],
        out_shape=[jax.ShapeDtypeStruct((t, DN_WIDTH), BF16),
                   jax.ShapeDtypeStruct((nb, N_HEADS, HEAD_DIM, HEAD_DIM), F32)],
        scratch_shapes=[pltpu.VMEM((SC_HALO + tc, wqkv), F32), pltpu.VMEM((tc, wqkv), F32)],
        compiler_params=_cparams(),
        name="delta_branch",
    )(qkv2d, qkv2d, hist, scw, ab, zs, s0, dng)


def _step_front_kernel(u_ref, qkv_ref, ch_ref, sh_ref, cw_ref, cb_ref, lg_ref, lb_ref, scw_ref,
                       acta_ref, nch_ref, qkvp_ref, nsh_ref):
    hist_rows = CONV_WIDTH - 1
    u = u_ref[...]
    acc = cw_ref[hist_rows:hist_rows + 1, :] * u
    for j in range(hist_rows):
        acc = acc + cw_ref[j:j + 1, :] * ch_ref[:, j * CONV_CH:(j + 1) * CONV_CH]
    acta_ref[...] = _layer_norm_swish(acc + cb_ref[...], lg_ref[...], lb_ref[...]).astype(acta_ref.dtype)
    nch_ref[:, :(hist_rows - 1) * CONV_CH] = ch_ref[:, CONV_CH:]
    nch_ref[:, (hist_rows - 1) * CONV_CH:] = u

    wq = 3 * DN_WIDTH
    sc_rows = SHORT_CONV - 1
    qkv = qkv_ref[...]
    acc = scw_ref[sc_rows:sc_rows + 1, :] * qkv
    for j in range(sc_rows):
        acc = acc + scw_ref[j:j + 1, :] * sh_ref[:, j * wq:(j + 1) * wq]
    c = _silu(acc)
    for h in range(N_HEADS):
        lo = h * HEAD_DIM
        qkvp_ref[:, lo:lo + HEAD_DIM] = _l2n(c[:, lo:lo + HEAD_DIM]) * (HEAD_DIM ** -0.5)
        qkvp_ref[:, DN_WIDTH + lo:DN_WIDTH + lo + HEAD_DIM] = _l2n(c[:, DN_WIDTH + lo:DN_WIDTH + lo + HEAD_DIM])
    qkvp_ref[:, 2 * DN_WIDTH:] = c[:, 2 * DN_WIDTH:]
    nsh_ref[:, :(sc_rows - 1) * wq] = sh_ref[:, wq:]
    nsh_ref[:, (sc_rows - 1) * wq:] = qkv


def _step_front(u, qkv, conv_hist_flat, sc_hist_flat, conv_dw, conv_dw_b, ln_g, ln_b, scw, *, bt):
    n = u.shape[0]
    wq = 3 * DN_WIDTH
    consts = [conv_dw, conv_dw_b, ln_g, ln_b, scw]
    row = lambda c: pl.BlockSpec((bt, c), lambda i: (i, 0))
    widths = [CONV_CH, conv_hist_flat.shape[1], wq, sc_hist_flat.shape[1]]
    dtypes = [BF16, F32, F32, F32]
    return pl.pallas_call(
        _step_front_kernel,
        grid=(n // bt,),
        in_specs=[row(CONV_CH), row(wq), row(widths[1]), row(widths[3])] + [_const_spec(a) for a in consts],
        out_specs=[row(c) for c in widths],
        out_shape=[jax.ShapeDtypeStruct((n, c), d) for c, d in zip(widths, dtypes)],
        compiler_params=_cparams(),
        name="step_front",
    )(u, qkv, conv_hist_flat, sc_hist_flat, *consts)


def _rows_to_cols(x):
    pad = jnp.zeros((LANES - x.shape[0], LANES), x.dtype)
    return jnp.concatenate([x, pad], axis=0).T


def _step_delta_kernel(qkvp_ref, ab_ref, zs_ref, s_ref, dng_ref, o_ref, sn_ref, *, bt):
    ab = ab_ref[...]
    for h in range(N_HEADS):
        lo = h * HEAD_DIM
        q = qkvp_ref[:, lo:lo + HEAD_DIM]
        k = qkvp_ref[:, DN_WIDTH + lo:DN_WIDTH + lo + HEAD_DIM]
        v = qkvp_ref[:, 2 * DN_WIDTH + lo:2 * DN_WIDTH + lo + HEAD_DIM]
        q_cols = _rows_to_cols(q)
        k_cols = _rows_to_cols(k)
        qk = jnp.sum(q * k, axis=-1, keepdims=True)
        alpha = jnp.exp(ab[:, h:h + 1])
        beta = ab[:, N_HEADS + h:N_HEADS + h + 1]
        seqs = range(bt)
        kc = [k_cols[:, j:j + 1] for j in seqs]
        a = [alpha[j:j + 1, :] for j in seqs]
        s_k = [jnp.sum(s_ref[j, h] * kc[j], axis=0, keepdims=True) for j in seqs]
        s_q = [jnp.sum(s_ref[j, h] * q_cols[:, j:j + 1], axis=0, keepdims=True) for j in seqs]
        u = [beta[j:j + 1, :] * (v[j:j + 1, :] - a[j] * s_k[j]) for j in seqs]
        for j in seqs:
            sn_ref[j, h] = a[j] * s_ref[j, h] + kc[j] * u[j]
        o = jnp.concatenate([a[j] * s_q[j] + qk[j:j + 1, :] * u[j] for j in seqs], axis=0)
        o_ref[:, lo:lo + HEAD_DIM] = _gated_head_norm(
            o, dng_ref[...], zs_ref[:, lo:lo + HEAD_DIM]).astype(o_ref.dtype)


def _step_delta(qkvp, ab, zs, s0, dng, *, bt):
    n = qkvp.shape[0]
    row = lambda c: pl.BlockSpec((bt, c), lambda i: (i, 0))
    state_spec = pl.BlockSpec((bt, N_HEADS, HEAD_DIM, HEAD_DIM), lambda i: (i, 0, 0, 0))
    return pl.pallas_call(
        functools.partial(_step_delta_kernel, bt=bt),
        grid=(n // bt,),
        in_specs=[row(3 * DN_WIDTH), row(LANES), row(DN_WIDTH), state_spec, _const_spec(dng)],
        out_specs=[row(DN_WIDTH), state_spec],
        out_shape=[jax.ShapeDtypeStruct((n, DN_WIDTH), BF16), jax.ShapeDtypeStruct(s0.shape, F32)],
        compiler_params=_cparams(),
        name="step_delta",
    )(qkvp, ab, zs, s0, dng)


def _post_kernel(acta_ref, actb_ref, sga_ref, sgb_ref, x_ref, gate1_ref, scale2_ref, shift2_ref,
                 gate2_ref, cnt0_ref, gffn_ref, wpa_ref, wpb_ref, wo_ref, wrt_ref, ebias_ref,
                 wsgu_ref, wsd_ref,
                 xs_ref, h2_ref, eidx_ref, ew_ref, rank_ref, cnt_ref, *, tm):
    i = pl.program_id(0)

    @pl.when(i == 0)
    def _():
        cnt_ref[...] = cnt0_ref[...]

    y_a = jnp.dot(acta_ref[...], wpa_ref[...], preferred_element_type=F32)
    y_b = jnp.dot(actb_ref[...], wpb_ref[...], preferred_element_type=F32)
    merged = sga_ref[...] * y_a + sgb_ref[...] * y_b
    mix = _bdot(merged, wo_ref[...])
    x1 = x_ref[...] + gate1_ref[...] * mix
    y = x1 * lax.rsqrt(jnp.mean(x1 * x1, axis=-1, keepdims=True) + EPS) * gffn_ref[...]
    h2 = y * (1.0 + scale2_ref[...]) + shift2_ref[...]
    h2_ref[...] = _pack_bf16_pairs(h2)
    h2b = h2.astype(BF16)

    gu = jnp.dot(h2b, wsgu_ref[...], preferred_element_type=F32)
    shared = _bdot(_silu(gu[:, :D_SHARED]) * gu[:, D_SHARED:], wsd_ref[...])
    xs_ref[...] = x1 + gate2_ref[...] * shared

    logits_t = lax.dot_general(wrt_ref[...], h2b, NT_DIMS, preferred_element_type=F32)
    scores = _sigmoid(logits_t)
    biased = scores + ebias_ref[...]
    neg = -jnp.inf
    big = jnp.int32(1 << 30)
    erow = lax.broadcasted_iota(I32, (N_EXPERTS, tm), 0)

    def first_argmax(vals, rows):
        m = jnp.max(vals, axis=0, keepdims=True)
        return m, jnp.min(jnp.where(vals == m, rows, big), axis=0, keepdims=True)

    group_scores = []
    for g in range(N_GROUPS):
        vals = biased[g * GROUP_SIZE:(g + 1) * GROUP_SIZE, :]
        rows = lax.broadcasted_iota(I32, (GROUP_SIZE, tm), 0) + g * GROUP_SIZE
        m1, i1 = first_argmax(vals, rows)
        m2 = jnp.max(jnp.where(rows == i1, neg, vals), axis=0, keepdims=True)
        group_scores.append(m1 + m2)
    group_sel = [jnp.zeros((1, tm), jnp.bool_)] * N_GROUPS
    for _ in range(TOPK_GROUPS):
        best = functools.reduce(jnp.maximum, group_scores)
        gi = functools.reduce(
            jnp.minimum, [jnp.where(group_scores[g] == best, jnp.int32(g), big) for g in range(N_GROUPS)])
        for g in range(N_GROUPS):
            hit = gi == g
            group_sel[g] = jnp.logical_or(group_sel[g], hit)
            group_scores[g] = jnp.where(hit, neg, group_scores[g])
    cand = jnp.concatenate(
        [jnp.where(group_sel[g], biased[g * GROUP_SIZE:(g + 1) * GROUP_SIZE, :], neg)
         for g in range(N_GROUPS)], axis=0)

    w_rows, hits = [], []
    picked = jnp.zeros((N_EXPERTS, tm), F32)
    for k in range(TOP_K):
        _, ei = first_argmax(cand, erow)
        hit = erow == ei
        eidx_ref[k:k + 1, :] = ei
        w_rows.append(jnp.sum(jnp.where(hit, scores, 0.0), axis=0, keepdims=True))
        hits.append(hit)
        picked = jnp.where(hit, 1.0, picked)
        cand = jnp.where(hit, neg, cand)
    w_sum = functools.reduce(jnp.add, w_rows)
    for k in range(TOP_K):
        ew_ref[k:k + 1, :] = w_rows[k] / w_sum * ROUTE_SCALE

    trow = lax.broadcasted_iota(I32, (tm, tm), 0)
    tcol = lax.broadcasted_iota(I32, (tm, tm), 1)
    before = (trow < tcol).astype(BF16)
    prefix = jnp.dot(picked.astype(BF16), before, preferred_element_type=F32) + cnt_ref[:, 0:1]
    for k in range(TOP_K):
        rank_ref[k:k + 1, :] = jnp.sum(jnp.where(hits[k], prefix, 0.0), axis=0, keepdims=True).astype(I32)
    cnt_ref[...] = cnt_ref[...] + jnp.sum(picked, axis=1, keepdims=True)


def _post(acta, actb, sga, sgb, x2d, gate1, scale2, shift2, gate2, cnt0, g_ffn, w, ebias_col,
          *, tm, tiles_per_mod):
    t = x2d.shape[0]
    row = lambda c: pl.BlockSpec((tm, c), lambda i: (i, 0))
    col = lambda r: pl.BlockSpec((r, tm), lambda i: (0, i))
    mods = [gate1, scale2, shift2, gate2]
    consts = [cnt0, g_ffn, w["pa"], w["pb"], w["o"], w["router_t"], ebias_col, w["s_gu"], w["s_down"]]
    return pl.pallas_call(
        functools.partial(_post_kernel, tm=tm),
        grid=(t // tm,),
        in_specs=[row(CONV_CH), row(DN_WIDTH), row(D_MODEL), row(D_MODEL), row(D_MODEL)]
                 + [_mod_spec(m, tiles_per_mod) for m in mods] + [_const_spec(a) for a in consts],
        out_specs=[row(D_MODEL), row(D_MODEL // 2), col(TOP_K), col(TOP_K), col(TOP_K), _const_spec(cnt0)],
        out_shape=[jax.ShapeDtypeStruct((t, D_MODEL), F32), jax.ShapeDtypeStruct((t, D_MODEL // 2), U32),
                   jax.ShapeDtypeStruct((TOP_K, t), I32), jax.ShapeDtypeStruct((TOP_K, t), F32),
                   jax.ShapeDtypeStruct((TOP_K, t), I32), jax.ShapeDtypeStruct(cnt0.shape, F32)],
        compiler_params=_cparams(),
        name="post_mixer",
    )(acta, actb, sga, sgb, x2d, *mods, *consts)


def _dest_kernel(start_ref, eidx_ref, rank_ref, o_ref, *, width):
    step = SUBLANES * LANES
    for c0 in range(0, width, step):
        c1 = min(c0 + step, width)
        e = eidx_ref[:, c0:c1]
        base = lax.fori_loop(0, N_EXPERTS, lambda x, acc: jnp.where(e == x, start_ref[x], acc),
                             jnp.zeros(e.shape, I32), unroll=8)
        o_ref[:, c0:c1] = base + rank_ref[:, c0:c1]


def _dest(cnt_start, e_idx, rank):
    k, t = e_idx.shape
    full = pl.BlockSpec((k, t), lambda i, st: (0, 0))
    return pl.pallas_call(
        functools.partial(_dest_kernel, width=t),
        grid_spec=pltpu.PrefetchScalarGridSpec(
            num_scalar_prefetch=1, grid=(1,), in_specs=[full, full], out_specs=full),
        out_shape=jax.ShapeDtypeStruct((k, t), I32),
        compiler_params=_cparams(),
        name="moe_dest",
    )(cnt_start, e_idx, rank)


def _sc_mesh():
    return plsc.VectorSubcoreMesh(core_axis_name="core", subcore_axis_name="subcore")


def _sc_pipeline(body, n_steps, in_specs, out_specs):
    return pltpu.emit_pipeline(body, grid=(n_steps,), in_specs=in_specs, out_specs=out_specs,
                               core_axis_name=("core", "subcore"), dimension_semantics=(pltpu.PARALLEL,))


def _sc_scatter_rows(x, half, dest_flat, n_rows):
    t = x.shape[0]
    tiles = t // SC_WINDOW
    n_steps = dest_flat.shape[1] // SC_WINDOW

    @functools.partial(pl.kernel, mesh=_sc_mesh(), scratch_types=[],
                       out_type=jax.ShapeDtypeStruct((n_rows, QUARTER_WORDS), x.dtype))
    def scatter(x_hbm, i_hbm, o_hbm):
        def body(x_vmem, i_vmem):
            pltpu.sync_copy(x_vmem, o_hbm.at[i_vmem.at[0]])

        _sc_pipeline(body, n_steps,
                     [pl.BlockSpec((SC_WINDOW, QUARTER_WORDS), lambda i: (i % tiles, half)),
                      pl.BlockSpec((1, SC_WINDOW), lambda i: (0, i))], [])(x_hbm, i_hbm)

    return scatter(x, dest_flat)


def _sc_gather_rows(table, idx_flat):
    n = idx_flat.shape[1]

    @functools.partial(pl.kernel, mesh=_sc_mesh(), scratch_types=[],
                       out_type=jax.ShapeDtypeStruct((n, table.shape[1]), table.dtype))
    def gather(t_hbm, i_hbm, o_hbm):
        def body(i_vmem, o_vmem):
            pltpu.sync_copy(t_hbm.at[i_vmem.at[0]], o_vmem)

        _sc_pipeline(body, n // SC_WINDOW,
                     [pl.BlockSpec((1, SC_WINDOW), lambda i: (0, i))],
                     [pl.BlockSpec((SC_WINDOW, table.shape[1]), lambda i: (i, 0))])(i_hbm, o_hbm)

    return gather(table, idx_flat)


def _two_group_specs(tm, width, n_prompt_tiles):
    return [pl.BlockSpec((tm, width), lambda i: (jnp.minimum(i, n_prompt_tiles - 1), 0)),
            pl.BlockSpec((tm, width), lambda i: (jnp.maximum(i - n_prompt_tiles, 0), 0))]


def _experts_kernel(sblk_ref, sexp_ref, slo_ref, shi_ref, snext_ref, first_ref,
                    xa_ref, xb_ref, wg_hbm, wu_hbm, wd_hbm, ya_ref, yb_ref,
                    wg_buf, wu_buf, wd_buf, wgu_s, wd_s, state_ref, sems, *, bm):
    p = pl.program_id(0)
    lo = slo_ref[p]
    expert = sexp_ref[p]

    def weight_copies(e, slot):
        return [pltpu.make_async_copy(src.at[e], dst.at[slot], sems.at[slot])
                for src, dst in ((wg_hbm, wg_buf), (wu_hbm, wu_buf), (wd_hbm, wd_buf))]

    @pl.when(p == 0)
    def _():
        state_ref[0] = -1
        state_ref[1] = 0
        for c in weight_copies(first_ref[0], 0):
            c.start()

    @pl.when(shi_ref[p] > lo)
    def _():
        @pl.when(state_ref[0] != expert)
        def _():
            slot = state_ref[1]
            for c in weight_copies(expert, slot):
                c.wait()
            wgu_s[:, :D_EXPERT] = wg_buf[slot].astype(BF16)
            wgu_s[:, D_EXPERT:] = wu_buf[slot].astype(BF16)
            wd_s[...] = wd_buf[slot].astype(BF16)
            state_ref[0] = expert
            state_ref[1] = 1 - slot
            upcoming = snext_ref[p]

            @pl.when(upcoming >= 0)
            def _():
                for c in weight_copies(upcoming, 1 - slot):
                    c.start()

        q = QUARTER_WORDS
        quarters = {}
        quarters[0], quarters[2] = _unpack_bf16_pairs(xa_ref[...])
        quarters[1], quarters[3] = _unpack_bf16_pairs(xb_ref[...])
        gu = functools.reduce(jnp.add, [
            jnp.dot(quarters[c], wgu_s[c * q:(c + 1) * q, :], preferred_element_type=F32) for c in range(4)])
        hb = (_silu(gu[:, :D_EXPERT]) * gu[:, D_EXPERT:]).astype(BF16)
        y = _pack_bf16_pairs(jnp.dot(hb, wd_s[...], preferred_element_type=F32))

        @pl.when(lo == 0)
        def _():
            ya_ref[...] = y[:, :q]
            yb_ref[...] = y[:, q:]

        @pl.when(lo > 0)
        def _():
            keep = lax.broadcasted_iota(I32, (bm, 1), 0) >= lo
            ya_ref[...] = jnp.where(keep, y[:, :q], ya_ref[...])
            yb_ref[...] = jnp.where(keep, y[:, q:], yb_ref[...])


def _experts(xs_a, xs_b, w_gate, w_up, w_down, seg_blk, seg_exp, seg_lo, seg_hi, seg_next, first_exp, *, bm):
    n_rows = xs_a.shape[0]
    n_seg = seg_blk.shape[0]
    ring = 2
    hbm = pl.BlockSpec(memory_space=pl.ANY)
    rows = pl.BlockSpec((bm, QUARTER_WORDS), lambda p, sb, *_: (sb[p], 0))
    grid_spec = pltpu.PrefetchScalarGridSpec(
        num_scalar_prefetch=6,
        grid=(n_seg,),
        in_specs=[rows, rows, hbm, hbm, hbm],
        out_specs=[rows, rows],
        scratch_shapes=[pltpu.VMEM((ring, D_MODEL, D_EXPERT), F32), pltpu.VMEM((ring, D_MODEL, D_EXPERT), F32),
                        pltpu.VMEM((ring, D_EXPERT, D_MODEL), F32),
                        pltpu.VMEM((D_MODEL, 2 * D_EXPERT), BF16), pltpu.VMEM((D_EXPERT, D_MODEL), BF16),
                        pltpu.SMEM((2,), I32), pltpu.SemaphoreType.DMA((ring,))],
    )
    return pl.pallas_call(
        functools.partial(_experts_kernel, bm=bm),
        grid_spec=grid_spec,
        out_shape=[jax.ShapeDtypeStruct((n_rows, QUARTER_WORDS), U32)] * 2,
        compiler_params=_cparams(),
        name="moe_experts",
    )(seg_blk, seg_exp, seg_lo, seg_hi, seg_next, first_exp, xs_a, xs_b, w_gate, w_up, w_down)


def _final_kernel(ew_ref, pa_ref, pb_ref, xsp_ref, xss_ref, g2p_ref, g2s_ref, gfin_ref, op_ref, os_ref,
                  *, tm, n_prompt_tiles):
    ew = ew_ref[...]
    quarters = [jnp.zeros((tm, QUARTER_WORDS), F32)] * 4
    for k in range(TOP_K):
        w_k = ew[:, k:k + 1]
        a_low, a_high = _unpack_bf16_pairs(pa_ref[k])
        b_low, b_high = _unpack_bf16_pairs(pb_ref[k])
        for c, part in enumerate((a_low, b_low, a_high, b_high)):
            quarters[c] = quarters[c] + w_k * part.astype(F32)
    routed = jnp.concatenate(quarters, axis=1)

    def finish(xs_ref, gate2_ref, o_ref):
        x2 = xs_ref[...] + gate2_ref[...] * routed
        o_ref[...] = x2 * lax.rsqrt(jnp.mean(x2 * x2, axis=-1, keepdims=True) + EPS) * gfin_ref[...]

    is_prompt = pl.program_id(0) < n_prompt_tiles
    pl.when(is_prompt)(lambda: finish(xsp_ref, g2p_ref, op_ref))
    pl.when(jnp.logical_not(is_prompt))(lambda: finish(xss_ref, g2s_ref, os_ref))


def _final(ew_tok, planes_a, planes_b, xs_p, xs_s, gate2_p, gate2_s, g_final, *, tm, tiles_per_batch):
    n_prompt_tiles = xs_p.shape[0] // tm
    n_tiles = n_prompt_tiles + xs_s.shape[0] // tm
    nb = gate2_p.shape[0]
    planes = pl.BlockSpec((TOP_K, tm, QUARTER_WORDS), lambda i: (0, i, 0))
    return pl.pallas_call(
        functools.partial(_final_kernel, tm=tm, n_prompt_tiles=n_prompt_tiles),
        grid=(n_tiles,),
        in_specs=[pl.BlockSpec((tm, TOP_K), lambda i: (i, 0)), planes, planes]
                 + _two_group_specs(tm, D_MODEL, n_prompt_tiles)
                 + [pl.BlockSpec((None, 1, D_MODEL), lambda i: (jnp.minimum(i // tiles_per_batch, nb - 1), 0, 0)),
                    pl.BlockSpec((None, tm, D_MODEL), lambda i: (jnp.maximum(i - n_prompt_tiles, 0), 0, 0)),
                    _const_spec(g_final)],
        out_specs=_two_group_specs(tm, D_MODEL, n_prompt_tiles),
        out_shape=[jax.ShapeDtypeStruct(xs_p.shape, F32), jax.ShapeDtypeStruct(xs_s.shape, F32)],
        compiler_params=_cparams(),
        name="moe_combine_final",
    )(ew_tok, planes_a, planes_b, xs_p, xs_s, gate2_p, gate2_s, g_final)


def _pick_tile(n, preferred):
    t = min(n, preferred)
    assert n % t == 0, (n, t)
    return t


def kernel(x_prompt, x_sample, c_prompt, c_sample, state_conv, state_short_conv, state_delta, w_ada, b_ada, g_mix, w_in, conv_dw, conv_dw_b, conv_ln_g, conv_ln_b, w_pa, short_conv_w, a_log, dt_bias, dn_norm_g, w_pb, w_o, g_ffn, w_router, e_bias, w_gate, w_up, w_down, ws_gate, ws_up, ws_down, g_final):
    nb, seq, d = x_prompt.shape
    ns = x_sample.shape[0]
    assert d == D_MODEL and w_ada.shape[0] == 1 and x_sample.shape[1] == 1
    assert seq % CHUNK == 0 and seq >= CONV_HALO and ns % SUBLANES == 0
    tp = nb * seq
    row1 = lambda a: a.reshape(1, -1)

    wi = w_in[0].astype(BF16)
    o_q = 2 * CONV_CH
    o_z = o_q + 3 * DN_WIDTH
    o_a = o_z + DN_WIDTH
    o_ga = o_a + 2 * N_HEADS
    w_proj = {
        "glu": wi[:, :o_q], "qkv": wi[:, o_q:o_z], "z": wi[:, o_z:o_a],
        "ab": jnp.pad(wi[:, o_a:o_ga], ((0, 0), (0, LANES - 2 * N_HEADS))),
        "ga": wi[:, o_ga:o_ga + D_MODEL], "gb": wi[:, o_ga + D_MODEL:],
    }
    nea = jnp.pad(-jnp.exp(a_log[0].astype(F32)), (0, LANES - N_HEADS)).reshape(1, LANES)
    dtb = jnp.pad(dt_bias[0].astype(F32), (0, LANES - N_HEADS)).reshape(1, LANES)
    w_post = {
        "pa": w_pa[0].astype(BF16), "pb": w_pb[0].astype(BF16), "o": w_o[0].astype(BF16),
        "router_t": w_router[0].T.astype(BF16),
        "s_gu": jnp.concatenate([ws_gate[0], ws_up[0]], axis=1).astype(BF16),
        "s_down": ws_down[0].astype(BF16),
    }
    ebias_col = e_bias[0].astype(F32).reshape(N_EXPERTS, 1)
    g_mix1, g_ffn1, g_fin1 = row1(g_mix[0]), row1(g_ffn[0]), row1(g_final)
    cdw, cdb, clg, clb = conv_dw[0], row1(conv_dw_b[0]), row1(conv_ln_g[0]), row1(conv_ln_b[0])
    scw, dng = short_conv_w[0], row1(dn_norm_g[0])

    ada = _ada(jnp.concatenate([c_prompt, c_sample], axis=0), w_ada[0].astype(BF16), b_ada[0])
    ada = ada.reshape(nb + ns, 6, D_MODEL)
    mod_p = [ada[:nb, m].reshape(nb, 1, D_MODEL) for m in range(6)]
    mod_s = [ada[nb:, m].reshape(1, ns, D_MODEL) for m in range(6)]
    shift1, scale1, gate1, shift2, scale2, gate2 = range(6)

    tm_p = _pick_tile(seq, 256)
    tpm_p = seq // tm_p
    xp = x_prompt.reshape(tp, D_MODEL)
    u_p, qkv_p, zs_p, ab_p, sga_p, sgb_p = _in_proj(
        xp, mod_p[scale1], mod_p[shift1], g_mix1, w_proj, nea, dtb, tm=tm_p, tiles_per_mod=tpm_p)
    tl = _pick_tile(seq, 256)
    acta_p = _conv_branch(u_p, jnp.zeros((nb, CONV_HALO, CONV_CH), F32), cdw, cdb, clg, clb,
                          tl=tl, tiles_per_batch=seq // tl)
    tc = _pick_tile(seq, 4 * CHUNK)
    actb_p, s_new_p = _delta_branch(
        qkv_p, jnp.zeros((nb, SC_HALO, 3 * DN_WIDTH), F32),
        jnp.zeros((nb, N_HEADS, HEAD_DIM, HEAD_DIM), F32), scw, ab_p, zs_p, dng,
        tc=tc, tiles_per_batch=seq // tc)

    xs_ = x_sample.reshape(ns, D_MODEL)
    u_s, qkv_s, zs_s, ab_s, sga_s, sgb_s = _in_proj(
        xs_, mod_s[scale1], mod_s[shift1], g_mix1, w_proj, nea, dtb, tm=ns, tiles_per_mod=1)
    bt = SUBLANES
    acta_s, nch_s, qkvp_s, nsh_s = _step_front(
        u_s, qkv_s, state_conv[0].reshape(ns, -1), state_short_conv[0].reshape(ns, -1),
        cdw, cdb, clg, clb, scw, bt=bt)
    actb_s, s_new_s = _step_delta(qkvp_s, ab_s, zs_s, state_delta[0].astype(F32), dng, bt=bt)

    cnt0 = jnp.zeros((N_EXPERTS, LANES), F32)
    xsh_p, h2_p, eidx_p, ew_p, rank_p, cnt_p = _post(
        acta_p, actb_p, sga_p, sgb_p, xp, mod_p[gate1], mod_p[scale2], mod_p[shift2], mod_p[gate2],
        cnt0, g_ffn1, w_post, ebias_col, tm=tm_p, tiles_per_mod=tpm_p)
    xsh_s, h2_s, eidx_s, ew_s, rank_s, cnt_all = _post(
        acta_s, actb_s, sga_s, sgb_s, xs_, mod_s[gate1], mod_s[scale2], mod_s[shift2], mod_s[gate2],
        cnt_p, g_ffn1, w_post, ebias_col, tm=ns, tiles_per_mod=1)

    bm = 256
    tmd = LANES
    t_all = tp + ns
    n_rows = t_all * TOP_K
    assert seq % tmd == 0 and ns % tmd == 0 and n_rows % bm == 0 and t_all % SC_WINDOW == 0
    n_blocks = n_rows // bm
    counts = cnt_all[:, 0].astype(I32)
    cnt_end = jnp.cumsum(counts)
    cnt_start = cnt_end - counts
    e_idx = jnp.concatenate([eidx_p, eidx_s], axis=1)
    rank = jnp.concatenate([rank_p, rank_s], axis=1)
    dest_flat = _dest(cnt_start, e_idx, rank).reshape(1, n_rows)
    seg_start = jnp.sort(jnp.concatenate([jnp.arange(n_blocks, dtype=I32) * bm, cnt_start]))
    seg_end = jnp.concatenate([seg_start[1:], jnp.full((1,), n_rows, I32)])
    seg_blk = jnp.minimum(seg_start // bm, n_blocks - 1)
    seg_lo = seg_start - seg_blk * bm
    seg_hi = seg_end - seg_blk * bm
    seg_exp = jnp.sum((cnt_end[None, :] <= seg_start[:, None]).astype(I32), axis=1)
    seg_exp = jnp.minimum(seg_exp, N_EXPERTS - 1)
    seg_exp = lax.cummax(jnp.where(seg_hi > seg_lo, seg_exp, 0))
    none = jnp.int32(N_EXPERTS)
    in_use_from = lax.cummin(jnp.where(counts > 0, jnp.arange(N_EXPERTS, dtype=I32), none), reverse=True)
    next_in_use = jnp.concatenate([in_use_from[1:], jnp.full((1,), none, I32)])
    next_in_use = jnp.where(next_in_use == none, -1, next_in_use)
    seg_next = next_in_use[seg_exp]
    first_exp = jnp.minimum(in_use_from[:1], N_EXPERTS - 1)

    h2_all = jnp.concatenate([h2_p, h2_s], axis=0)
    xs_a, xs_b = (_sc_scatter_rows(h2_all, half, dest_flat, n_rows) for half in range(2))
    ys_a, ys_b = _experts(xs_a, xs_b, w_gate[0], w_up[0], w_down[0], seg_blk, seg_exp, seg_lo, seg_hi, seg_next,
                          first_exp, bm=bm)

    planes_a, planes_b = (_sc_gather_rows(ys, dest_flat).reshape(TOP_K, t_all, QUARTER_WORDS) for ys in (ys_a, ys_b))
    ew_tok = jnp.concatenate([ew_p, ew_s], axis=1).T
    y_p, y_s = _final(ew_tok, planes_a, planes_b, xsh_p, xsh_s, mod_p[gate2],
                      mod_s[gate2].reshape(ns // tmd, tmd, D_MODEL), g_fin1, tm=tmd, tiles_per_batch=seq // tmd)

    hist_rows = CONV_WIDTH - 1
    sc_rows = SHORT_CONV - 1
    y_prompt = y_p.reshape(nb, seq, D_MODEL)
    y_sample = y_s.reshape(ns, 1, D_MODEL)
    new_conv_prompt = u_p.reshape(nb, seq, CONV_CH)[:, seq - hist_rows:][None]
    new_sc_prompt = qkv_p.reshape(nb, seq, 3 * DN_WIDTH)[:, seq - sc_rows:][None]
    new_conv_sample = nch_s.reshape(ns, hist_rows, CONV_CH)[None]
    new_sc_sample = nsh_s.reshape(ns, sc_rows, 3 * DN_WIDTH)[None]
    return (y_prompt, y_sample, new_conv_prompt, new_sc_prompt, s_new_p[None],
            new_conv_sample, new_sc_sample, s_new_s[None])
```

```python
import functools

import jax
import jax.numpy as jnp
from jax import lax
from jax.experimental import pallas as pl
from jax.experimental.pallas import tpu as pltpu
from jax.experimental.pallas import tpu_sc as plsc

F32 = jnp.float32
BF16 = jnp.bfloat16
I32 = jnp.int32
U32 = jnp.uint32

EPS = 1e-6
D_MODEL = 1024
CONV_CH = 512
CONV_WIDTH = 31
N_HEADS = 4
HEAD_DIM = 128
DN_WIDTH = N_HEADS * HEAD_DIM
SHORT_CONV = 4
CHUNK = 64
CHUNK_SHIFT = CHUNK.bit_length() - 1
assert 1 << CHUNK_SHIFT == CHUNK
SUB = 16
SUB_SHIFT = SUB.bit_length() - 1
assert 1 << SUB_SHIFT == SUB and CHUNK % SUB == 0
N_EXPERTS = 256
TOP_K = 8
N_GROUPS = 8
GROUP_SIZE = N_EXPERTS // N_GROUPS
TOPK_GROUPS = 4
D_EXPERT = 256
D_SHARED = 256
ROUTE_SCALE = 2.5

LANES = 128
SUBLANES = 8
CONV_HALO = 32
SC_HALO = 8
VMEM_LIMIT = 48 * 1024 * 1024
SC_WINDOW = 128
QUARTER_WORDS = D_MODEL // 4

NT_DIMS = (((1,), (1,)), ((), ()))
TN_DIMS = (((0,), (0,)), ((), ()))


def _cparams(n_grid_dims=1):
    return pltpu.CompilerParams(
        dimension_semantics=("arbitrary",) * n_grid_dims,
        vmem_limit_bytes=VMEM_LIMIT)


def _sigmoid(x):
    return jax.nn.sigmoid(x)


def _silu(x):
    return x * jax.nn.sigmoid(x)


def _bdot(a, b):
    return jnp.dot(a.astype(BF16), b.astype(BF16), preferred_element_type=F32)


def _pack_bf16_pairs(x):
    half = x.shape[1] // 2
    bits = lax.bitcast_convert_type(x.astype(BF16).astype(F32), U32)
    return bits[:, half:] | (bits[:, :half] >> 16)


def _unpack_bf16_pairs(w):
    low = lax.bitcast_convert_type(w << 16, F32).astype(BF16)
    high = lax.bitcast_convert_type(w & jnp.uint32(0xFFFF0000), F32).astype(BF16)
    return low, high


def _dot3(a, b):
    a_hi = a.astype(BF16)
    b_hi = b.astype(BF16)
    a_lo = (a - a_hi.astype(F32)).astype(BF16)
    b_lo = (b - b_hi.astype(F32)).astype(BF16)
    dot = lambda x, y: jnp.dot(x, y, preferred_element_type=F32)
    return dot(a_hi, b_hi) + (dot(a_hi, b_lo) + dot(a_lo, b_hi))


def _dot_exact_lhs(a_bf, b):
    hi = b.astype(BF16)
    r1 = b - hi.astype(F32)
    mid = r1.astype(BF16)
    low = (r1 - mid.astype(F32)).astype(BF16)
    dot = lambda p: jnp.dot(a_bf, p, preferred_element_type=F32)
    return dot(hi) + dot(mid) + dot(low)


def _ada_kernel(c_ref, w_ref, b_ref, o_ref):
    o_ref[...] = _bdot(_silu(c_ref[...]), w_ref[...]) + b_ref[...]


def _ada(c_all, w_ada_bf, b_ada):
    n = c_all.shape[0]
    width = w_ada_bf.shape[1]
    tn = 1536
    return pl.pallas_call(
        _ada_kernel,
        grid=(width // tn,),
        in_specs=[pl.BlockSpec((n, D_MODEL), lambda j: (0, 0)),
                  pl.BlockSpec((D_MODEL, tn), lambda j: (0, j)),
                  pl.BlockSpec((1, tn), lambda j: (0, j))],
        out_specs=pl.BlockSpec((n, tn), lambda j: (0, j)),
        out_shape=jax.ShapeDtypeStruct((n, width), F32),
        compiler_params=_cparams(),
        name="ada",
    )(c_all, w_ada_bf, b_ada.reshape(1, width))


def _in_proj_kernel(x_ref, scale_ref, shift_ref, g_ref, wglu_ref, wqkv_ref, wz_ref, wab_ref,
                    wga_ref, wgb_ref, nea_ref, dtb_ref,
                    u_ref, qkv_ref, zs_ref, ab_ref, sga_ref, sgb_ref):
    x = x_ref[...]
    y = x * lax.rsqrt(jnp.mean(x * x, axis=-1, keepdims=True) + EPS) * g_ref[...]
    h = (y * (1.0 + scale_ref[...]) + shift_ref[...]).astype(BF16)
    glu = jnp.dot(h, wglu_ref[...], preferred_element_type=F32)
    u_ref[...] = glu[:, :CONV_CH] * _sigmoid(glu[:, CONV_CH:])
    qkv_ref[...] = jnp.dot(h, wqkv_ref[...], preferred_element_type=F32)
    zs_ref[...] = _silu(jnp.dot(h, wz_ref[...], preferred_element_type=F32))
    ab = jnp.dot(h, wab_ref[...], preferred_element_type=F32)
    sp = ab + dtb_ref[...]
    softplus = jnp.maximum(sp, 0.0) + jnp.log1p(jnp.exp(-jnp.abs(sp)))
    lane = lax.broadcasted_iota(I32, ab.shape, 1)
    ab_ref[...] = jnp.where(lane < N_HEADS, nea_ref[...] * softplus, _sigmoid(ab))
    sga_ref[...] = _sigmoid(jnp.dot(h, wga_ref[...], preferred_element_type=F32))
    sgb_ref[...] = _sigmoid(jnp.dot(h, wgb_ref[...], preferred_element_type=F32))


def _mod_spec(mod, tiles_per_mod):
    return pl.BlockSpec((None,) + mod.shape[1:], lambda i: (i // tiles_per_mod, 0, 0))


def _const_spec(a):
    nd = a.ndim
    return pl.BlockSpec(a.shape, lambda i: (0,) * nd)


def _in_proj(x2d, scale, shift, g_mix, w, nea, dtb, *, tm, tiles_per_mod):
    t = x2d.shape[0]
    row = lambda c: pl.BlockSpec((tm, c), lambda i: (i, 0))
    consts = [g_mix, w["glu"], w["qkv"], w["z"], w["ab"], w["ga"], w["gb"], nea, dtb]
    widths = [CONV_CH, 3 * DN_WIDTH, DN_WIDTH, LANES, D_MODEL, D_MODEL]
    return pl.pallas_call(
        _in_proj_kernel,
        grid=(t // tm,),
        in_specs=[row(D_MODEL), _mod_spec(scale, tiles_per_mod), _mod_spec(shift, tiles_per_mod)]
                 + [_const_spec(a) for a in consts],
        out_specs=[row(c) for c in widths],
        out_shape=[jax.ShapeDtypeStruct((t, c), F32) for c in widths],
        compiler_params=_cparams(),
        name="in_proj",
    )(x2d, scale, shift, *consts)


def _layer_norm_swish(y, g, b):
    mu = jnp.mean(y, axis=-1, keepdims=True)
    yc = y - mu
    var = jnp.mean(yc * yc, axis=-1, keepdims=True)
    return _silu(yc * lax.rsqrt(var + EPS) * g + b)


def _conv_kernel(u_ref, halo_ref, hist_ref, w_ref, b_ref, lg_ref, lb_ref, o_ref, win_ref, shift_ref,
                 *, tl, tiles_per_batch, rows_per_pass):
    i = pl.program_id(0)
    first = (i % tiles_per_batch) == 0

    @pl.when(first)
    def _():
        win_ref[0:CONV_HALO, :] = hist_ref[...]

    @pl.when(jnp.logical_not(first))
    def _():
        win_ref[0:CONV_HALO, :] = halo_ref[...]

    win_ref[CONV_HALO:CONV_HALO + tl, :] = u_ref[...]
    tap0 = CONV_HALO - (CONV_WIDTH - 1)
    n_shifted = CONV_HALO + tl - SUBLANES
    for b in range(1, SUBLANES):
        shift_ref[b - 1] = win_ref[b:b + n_shifted, :]

    for r in range(tl // rows_per_pass):
        base = r * rows_per_pass
        acc = jnp.zeros((rows_per_pass, CONV_CH), F32)
        for j in range(CONV_WIDTH):
            a, b = divmod(tap0 + j, SUBLANES)
            start = base + a * SUBLANES
            rows = (win_ref[start:start + rows_per_pass, :] if b == 0
                    else shift_ref[b - 1, start:start + rows_per_pass, :])
            acc = acc + w_ref[j:j + 1, :] * rows
        act = _layer_norm_swish(acc + b_ref[...], lg_ref[...], lb_ref[...])
        o_ref[base:base + rows_per_pass, :] = act.astype(o_ref.dtype)


def _conv_branch(u2d, hist, conv_dw, conv_dw_b, ln_g, ln_b, *, tl, tiles_per_batch):
    t = u2d.shape[0]
    halo_blocks = tl // CONV_HALO
    consts = [conv_dw, conv_dw_b, ln_g, ln_b]
    return pl.pallas_call(
        functools.partial(_conv_kernel, tl=tl, tiles_per_batch=tiles_per_batch, rows_per_pass=64),
        grid=(t // tl,),
        in_specs=[pl.BlockSpec((tl, CONV_CH), lambda i: (i, 0)),
                  pl.BlockSpec((CONV_HALO, CONV_CH), lambda i: (jnp.maximum(i * halo_blocks - 1, 0), 0)),
                  pl.BlockSpec((None, CONV_HALO, CONV_CH), lambda i: (i // tiles_per_batch, 0, 0))]
                 + [_const_spec(a) for a in consts],
        out_specs=pl.BlockSpec((tl, CONV_CH), lambda i: (i, 0)),
        out_shape=jax.ShapeDtypeStruct((t, CONV_CH), BF16),
        scratch_shapes=[pltpu.VMEM((CONV_HALO + tl, CONV_CH), F32),
                        pltpu.VMEM((SUBLANES - 1, CONV_HALO + tl - SUBLANES, CONV_CH), F32)],
        compiler_params=_cparams(),
        name="conv_branch",
    )(u2d, u2d, hist, *consts)


def _l2n(x):
    return x * lax.rsqrt(jnp.sum(x * x, axis=-1, keepdims=True) + 1e-6)


def _gated_head_norm(o, g, zs):
    return o * lax.rsqrt(jnp.mean(o * o, axis=-1, keepdims=True) + EPS) * g * zs


def _delta_kernel(qkv_ref, halo_ref, hist_ref, scw_ref, ab_ref, zs_ref, s0_ref, dng_ref,
                  o_ref, s_ref, win_ref, c_ref, *, tc, tiles_per_batch):
    i = pl.program_id(0)
    first = (i % tiles_per_batch) == 0

    @pl.when(first)
    def _():
        win_ref[0:SC_HALO, :] = hist_ref[...]
        s_ref[...] = s0_ref[...]

    @pl.when(jnp.logical_not(first))
    def _():
        win_ref[0:SC_HALO, :] = halo_ref[...]

    win_ref[SC_HALO:SC_HALO + tc, :] = qkv_ref[...]
    tap0 = SC_HALO - (SHORT_CONV - 1)
    acc = jnp.zeros((tc, 3 * DN_WIDTH), F32)
    for j in range(SHORT_CONV):
        acc = acc + scw_ref[j:j + 1, :] * win_ref[tap0 + j:tap0 + j + tc, :]
    c_ref[...] = _silu(acc)

    row = lax.broadcasted_iota(I32, (tc, tc), 0)
    col = lax.broadcasted_iota(I32, (tc, tc), 1)
    same_chunk = lax.shift_right_logical(row, CHUNK_SHIFT) == lax.shift_right_logical(col, CHUNK_SHIFT)
    incl = jnp.logical_and(same_chunk, row >= col)
    row_sub = lax.shift_right_logical(row, SUB_SHIFT)
    col_sub = lax.shift_right_logical(col, SUB_SHIFT)
    strict_sub = jnp.logical_and(row_sub == col_sub, row > col)
    below_sub = jnp.logical_and(same_chunk, row_sub > col_sub)
    eye = (row == col).astype(F32)
    ab = ab_ref[...]
    gcum = _dot_exact_lhs(incl.astype(BF16), ab)
    gcum_t = gcum.T
    n_ch = tc // CHUNK

    heads = range(N_HEADS)
    q, k, gc, kk, a_qk, rhs, qd = [], [], [], [], [], [], []
    for h in heads:
        lo = h * HEAD_DIM
        q_h = _l2n(c_ref[:, lo:lo + HEAD_DIM]) * (HEAD_DIM ** -0.5)
        k_h = _l2n(c_ref[:, DN_WIDTH + lo:DN_WIDTH + lo + HEAD_DIM])
        v_h = c_ref[:, 2 * DN_WIDTH + lo:2 * DN_WIDTH + lo + HEAD_DIM]
        beta = ab[:, N_HEADS + h:N_HEADS + h + 1]
        gc_h = gcum[:, h:h + 1]
        gr = gcum_t[h:h + 1, :]
        decay = jnp.exp(jnp.where(incl, gc_h - gr, -jnp.inf))
        kb = k_h * beta
        kq = lax.dot_general(jnp.concatenate([kb, q_h], axis=0).astype(BF16), k_h.astype(BF16),
                             NT_DIMS, preferred_element_type=F32)
        e_gc = jnp.exp(gc_h)
        q.append(q_h)
        k.append(k_h)
        gc.append(gc_h)
        kk.append(kq[:tc] * decay)
        a_qk.append(kq[tc:] * decay)
        rhs.append(jnp.concatenate([v_h * beta, kb * e_gc], axis=1))
        qd.append(q_h * e_gc)

    pw = [jnp.where(strict_sub, kk[h], 0.0) for h in heads]
    d_inv = [eye - pw[h] for h in heads]
    for _ in range(SUB_SHIFT - 1):
        pw = [_bdot(pw[h], pw[h]) for h in heads]
        d_inv = [d_inv[h] + _bdot(d_inv[h], pw[h]) for h in heads]
    c_rhs = [_bdot(d_inv[h], rhs[h]) for h in heads]
    m_off = [_bdot(d_inv[h], jnp.where(below_sub, kk[h], 0.0)) for h in heads]
    sol = c_rhs
    for _ in range(CHUNK // SUB - 1):
        sol = [c_rhs[h] - _bdot(m_off[h], sol[h]) for h in heads]

    s = [s_ref[h] for h in heads]
    u_parts = [[] for _ in heads]
    o_parts = [[] for _ in heads]
    for c in range(n_ch):
        r0 = c * CHUNK
        for h in heads:
            u0_c = sol[h][r0:r0 + CHUNK, :HEAD_DIM]
            w_c = sol[h][r0:r0 + CHUNK, HEAD_DIM:]
            wq_s = _bdot(jnp.concatenate([w_c, qd[h][r0:r0 + CHUNK]], axis=0), s[h])
            u_c = u0_c - wq_s[:CHUNK]
            g_last = gcum[r0 + CHUNK - 1:r0 + CHUNK, h:h + 1]
            k_dec = k[h][r0:r0 + CHUNK] * jnp.exp(g_last - gc[h][r0:r0 + CHUNK])
            s[h] = jnp.exp(g_last) * s[h] + lax.dot_general(
                k_dec.astype(BF16), u_c.astype(BF16), TN_DIMS, preferred_element_type=F32)
            u_parts[h].append(u_c)
            o_parts[h].append(wq_s[CHUNK:])
    for h in heads:
        lo = h * HEAD_DIM
        s_ref[h] = s[h]
        o = jnp.concatenate(o_parts[h], axis=0) + _bdot(a_qk[h], jnp.concatenate(u_parts[h], axis=0))
        o_ref[:, lo:lo + HEAD_DIM] = _gated_head_norm(
            o, dng_ref[...], zs_ref[:, lo:lo + HEAD_DIM]).astype(o_ref.dtype)


def _delta_branch(qkv2d, hist, s0, scw, ab, zs, dng, *, tc, tiles_per_batch):
    t = qkv2d.shape[0]
    nb = s0.shape[0]
    halo_blocks = tc // SC_HALO
    wqkv = 3 * DN_WIDTH
    state_spec = pl.BlockSpec((None, N_HEADS, HEAD_DIM, HEAD_DIM), lambda i: (i // tiles_per_batch, 0, 0, 0))
    return pl.pallas_call(
        functools.partial(_delta_kernel, tc=tc, tiles_per_batch=tiles_per_batch),
        grid=(t // tc,),
        in_specs=[pl.BlockSpec((tc, wqkv), lambda i: (i, 0)),
                  pl.BlockSpec((SC_HALO, wqkv), lambda i: (jnp.maximum(i * halo_blocks - 1, 0), 0)),
                  pl.BlockSpec((None, SC_HALO, wqkv), lambda i: (i // tiles_per_batch, 0, 0)),
                  _const_spec(scw),
                  pl.BlockSpec((tc, LANES), lambda i: (i, 0)),
                  pl.BlockSpec((tc, DN_WIDTH), lambda i: (i, 0)),
                  state_spec,
                  _const_spec(dng)],
        out_specs=[pl.BlockSpec((tc, DN_WIDTH), lambda i: (i, 0)), state_spec],
        out_shape=[jax.ShapeDtypeStruct((t, DN_WIDTH), BF16),
                   jax.ShapeDtypeStruct((nb, N_HEADS, HEAD_DIM, HEAD_DIM), F32)],
        scratch_shapes=[pltpu.VMEM((SC_HALO + tc, wqkv), F32), pltpu.VMEM((tc, wqkv), F32)],
        compiler_params=_cparams(),
        name="delta_branch",
    )(qkv2d, qkv2d, hist, scw, ab, zs, s0, dng)


def _step_front_kernel(u_ref, qkv_ref, ch_ref, sh_ref, cw_ref, cb_ref, lg_ref, lb_ref, scw_ref,
                       acta_ref, nch_ref, qkvp_ref, nsh_ref):
    hist_rows = CONV_WIDTH - 1
    u = u_ref[...]
    acc = cw_ref[hist_rows:hist_rows + 1, :] * u
    for j in range(hist_rows):
        acc = acc + cw_ref[j:j + 1, :] * ch_ref[:, j * CONV_CH:(j + 1) * CONV_CH]
    acta_ref[...] = _layer_norm_swish(acc + cb_ref[...], lg_ref[...], lb_ref[...]).astype(acta_ref.dtype)
    nch_ref[:, :(hist_rows - 1) * CONV_CH] = ch_ref[:, CONV_CH:]
    nch_ref[:, (hist_rows - 1) * CONV_CH:] = u

    wq = 3 * DN_WIDTH
    sc_rows = SHORT_CONV - 1
    qkv = qkv_ref[...]
    acc = scw_ref[sc_rows:sc_rows + 1, :] * qkv
    for j in range(sc_rows):
        acc = acc + scw_ref[j:j + 1, :] * sh_ref[:, j * wq:(j + 1) * wq]
    c = _silu(acc)
    for h in range(N_HEADS):
        lo = h * HEAD_DIM
        qkvp_ref[:, lo:lo + HEAD_DIM] = _l2n(c[:, lo:lo + HEAD_DIM]) * (HEAD_DIM ** -0.5)
        qkvp_ref[:, DN_WIDTH + lo:DN_WIDTH + lo + HEAD_DIM] = _l2n(c[:, DN_WIDTH + lo:DN_WIDTH + lo + HEAD_DIM])
    qkvp_ref[:, 2 * DN_WIDTH:] = c[:, 2 * DN_WIDTH:]
    nsh_ref[:, :(sc_rows - 1) * wq] = sh_ref[:, wq:]
    nsh_ref[:, (sc_rows - 1) * wq:] = qkv


def _step_front(u, qkv, conv_hist_flat, sc_hist_flat, conv_dw, conv_dw_b, ln_g, ln_b, scw, *, bt):
    n = u.shape[0]
    wq = 3 * DN_WIDTH
    consts = [conv_dw, conv_dw_b, ln_g, ln_b, scw]
    row = lambda c: pl.BlockSpec((bt, c), lambda i: (i, 0))
    widths = [CONV_CH, conv_hist_flat.shape[1], wq, sc_hist_flat.shape[1]]
    dtypes = [BF16, F32, F32, F32]
    return pl.pallas_call(
        _step_front_kernel,
        grid=(n // bt,),
        in_specs=[row(CONV_CH), row(wq), row(widths[1]), row(widths[3])] + [_const_spec(a) for a in consts],
        out_specs=[row(c) for c in widths],
        out_shape=[jax.ShapeDtypeStruct((n, c), d) for c, d in zip(widths, dtypes)],
        compiler_params=_cparams(),
        name="step_front",
    )(u, qkv, conv_hist_flat, sc_hist_flat, *consts)


def _rows_to_cols(x):
    pad = jnp.zeros((LANES - x.shape[0], LANES), x.dtype)
    return jnp.concatenate([x, pad], axis=0).T


def _step_delta_kernel(qkvp_ref, ab_ref, zs_ref, s_ref, dng_ref, o_ref, sn_ref, *, bt):
    ab = ab_ref[...]
    for h in range(N_HEADS):
        lo = h * HEAD_DIM
        q = qkvp_ref[:, lo:lo + HEAD_DIM]
        k = qkvp_ref[:, DN_WIDTH + lo:DN_WIDTH + lo + HEAD_DIM]
        v = qkvp_ref[:, 2 * DN_WIDTH + lo:2 * DN_WIDTH + lo + HEAD_DIM]
        q_cols = _rows_to_cols(q)
        k_cols = _rows_to_cols(k)
        qk = jnp.sum(q * k, axis=-1, keepdims=True)
        alpha = jnp.exp(ab[:, h:h + 1])
        beta = ab[:, N_HEADS + h:N_HEADS + h + 1]
        seqs = range(bt)
        kc = [k_cols[:, j:j + 1] for j in seqs]
        a = [alpha[j:j + 1, :] for j in seqs]
        s_k = [jnp.sum(s_ref[j, h] * kc[j], axis=0, keepdims=True) for j in seqs]
        s_q = [jnp.sum(s_ref[j, h] * q_cols[:, j:j + 1], axis=0, keepdims=True) for j in seqs]
        u = [beta[j:j + 1, :] * (v[j:j + 1, :] - a[j] * s_k[j]) for j in seqs]
        for j in seqs:
            sn_ref[j, h] = a[j] * s_ref[j, h] + kc[j] * u[j]
        o = jnp.concatenate([a[j] * s_q[j] + qk[j:j + 1, :] * u[j] for j in seqs], axis=0)
        o_ref[:, lo:lo + HEAD_DIM] = _gated_head_norm(
            o, dng_ref[...], zs_ref[:, lo:lo + HEAD_DIM]).astype(o_ref.dtype)


def _step_delta(qkvp, ab, zs, s0, dng, *, bt):
    n = qkvp.shape[0]
    row = lambda c: pl.BlockSpec((bt, c), lambda i: (i, 0))
    state_spec = pl.BlockSpec((bt, N_HEADS, HEAD_DIM, HEAD_DIM), lambda i: (i, 0, 0, 0))
    return pl.pallas_call(
        functools.partial(_step_delta_kernel, bt=bt),
        grid=(n // bt,),
        in_specs=[row(3 * DN_WIDTH), row(LANES), row(DN_WIDTH), state_spec, _const_spec(dng)],
        out_specs=[row(DN_WIDTH), state_spec],
        out_shape=[jax.ShapeDtypeStruct((n, DN_WIDTH), BF16), jax.ShapeDtypeStruct(s0.shape, F32)],
        compiler_params=_cparams(),
        name="step_delta",
    )(qkvp, ab, zs, s0, dng)


def _post_kernel(acta_ref, actb_ref, sga_ref, sgb_ref, x_ref, gate1_ref, scale2_ref, shift2_ref,
                 gate2_ref, cnt0_ref, gffn_ref, wpa_ref, wpb_ref, wo_ref, wrt_ref, ebias_ref,
                 wsgu_ref, wsd_ref,
                 xs_ref, h2_ref, eidx_ref, ew_ref, rank_ref, cnt_ref, *, tm):
    i = pl.program_id(0)

    @pl.when(i == 0)
    def _():
        cnt_ref[...] = cnt0_ref[...]

    y_a = jnp.dot(acta_ref[...], wpa_ref[...], preferred_element_type=F32)
    y_b = jnp.dot(actb_ref[...], wpb_ref[...], preferred_element_type=F32)
    merged = sga_ref[...] * y_a + sgb_ref[...] * y_b
    mix = _bdot(merged, wo_ref[...])
    x1 = x_ref[...] + gate1_ref[...] * mix
    y = x1 * lax.rsqrt(jnp.mean(x1 * x1, axis=-1, keepdims=True) + EPS) * gffn_ref[...]
    h2 = y * (1.0 + scale2_ref[...]) + shift2_ref[...]
    h2_ref[...] = _pack_bf16_pairs(h2)
    h2b = h2.astype(BF16)

    gu = jnp.dot(h2b, wsgu_ref[...], preferred_element_type=F32)
    shared = _bdot(_silu(gu[:, :D_SHARED]) * gu[:, D_SHARED:], wsd_ref[...])
    xs_ref[...] = x1 + gate2_ref[...] * shared

    logits_t = lax.dot_general(wrt_ref[...], h2b, NT_DIMS, preferred_element_type=F32)
    scores = _sigmoid(logits_t)
    biased = scores + ebias_ref[...]
    neg = -jnp.inf
    big = jnp.int32(1 << 30)
    erow = lax.broadcasted_iota(I32, (N_EXPERTS, tm), 0)

    def first_argmax(vals, rows):
        m = jnp.max(vals, axis=0, keepdims=True)
        return m, jnp.min(jnp.where(vals == m, rows, big), axis=0, keepdims=True)

    group_scores = []
    for g in range(N_GROUPS):
        vals = biased[g * GROUP_SIZE:(g + 1) * GROUP_SIZE, :]
        rows = lax.broadcasted_iota(I32, (GROUP_SIZE, tm), 0) + g * GROUP_SIZE
        m1, i1 = first_argmax(vals, rows)
        m2 = jnp.max(jnp.where(rows == i1, neg, vals), axis=0, keepdims=True)
        group_scores.append(m1 + m2)
    group_sel = [jnp.zeros((1, tm), jnp.bool_)] * N_GROUPS
    for _ in range(TOPK_GROUPS):
        best = functools.reduce(jnp.maximum, group_scores)
        gi = functools.reduce(
            jnp.minimum, [jnp.where(group_scores[g] == best, jnp.int32(g), big) for g in range(N_GROUPS)])
        for g in range(N_GROUPS):
            hit = gi == g
            group_sel[g] = jnp.logical_or(group_sel[g], hit)
            group_scores[g] = jnp.where(hit, neg, group_scores[g])
    cand = jnp.concatenate(
        [jnp.where(group_sel[g], biased[g * GROUP_SIZE:(g + 1) * GROUP_SIZE, :], neg)
         for g in range(N_GROUPS)], axis=0)

    w_rows, hits = [], []
    picked = jnp.zeros((N_EXPERTS, tm), F32)
    for k in range(TOP_K):
        _, ei = first_argmax(cand, erow)
        hit = erow == ei
        eidx_ref[k:k + 1, :] = ei
        w_rows.append(jnp.sum(jnp.where(hit, scores, 0.0), axis=0, keepdims=True))
        hits.append(hit)
        picked = jnp.where(hit, 1.0, picked)
        cand = jnp.where(hit, neg, cand)
    w_sum = functools.reduce(jnp.add, w_rows)
    for k in range(TOP_K):
        ew_ref[k:k + 1, :] = w_rows[k] / w_sum * ROUTE_SCALE

    trow = lax.broadcasted_iota(I32, (tm, tm), 0)
    tcol = lax.broadcasted_iota(I32, (tm, tm), 1)
    before = (trow < tcol).astype(BF16)
    prefix = jnp.dot(picked.astype(BF16), before, preferred_element_type=F32) + cnt_ref[:, 0:1]
    for k in range(TOP_K):
        rank_ref[k:k + 1, :] = jnp.sum(jnp.where(hits[k], prefix, 0.0), axis=0, keepdims=True).astype(I32)
    cnt_ref[...] = cnt_ref[...] + jnp.sum(picked, axis=1, keepdims=True)


def _post(acta, actb, sga, sgb, x2d, gate1, scale2, shift2, gate2, cnt0, g_ffn, w, ebias_col,
          *, tm, tiles_per_mod):
    t = x2d.shape[0]
    row = lambda c: pl.BlockSpec((tm, c), lambda i: (i, 0))
    col = lambda r: pl.BlockSpec((r, tm), lambda i: (0, i))
    mods = [gate1, scale2, shift2, gate2]
    consts = [cnt0, g_ffn, w["pa"], w["pb"], w["o"], w["router_t"], ebias_col, w["s_gu"], w["s_down"]]
    return pl.pallas_call(
        functools.partial(_post_kernel, tm=tm),
        grid=(t // tm,),
        in_specs=[row(CONV_CH), row(DN_WIDTH), row(D_MODEL), row(D_MODEL), row(D_MODEL)]
                 + [_mod_spec(m, tiles_per_mod) for m in mods] + [_const_spec(a) for a in consts],
        out_specs=[row(D_MODEL), row(D_MODEL // 2), col(TOP_K), col(TOP_K), col(TOP_K), _const_spec(cnt0)],
        out_shape=[jax.ShapeDtypeStruct((t, D_MODEL), F32), jax.ShapeDtypeStruct((t, D_MODEL // 2), U32),
                   jax.ShapeDtypeStruct((TOP_K, t), I32), jax.ShapeDtypeStruct((TOP_K, t), F32),
                   jax.ShapeDtypeStruct((TOP_K, t), I32), jax.ShapeDtypeStruct(cnt0.shape, F32)],
        compiler_params=_cparams(),
        name="post_mixer",
    )(acta, actb, sga, sgb, x2d, *mods, *consts)


def _dest_kernel(start_ref, eidx_ref, rank_ref, o_ref, *, width):
    step = SUBLANES * LANES
    for c0 in range(0, width, step):
        c1 = min(c0 + step, width)
        e = eidx_ref[:, c0:c1]
        base = lax.fori_loop(0, N_EXPERTS, lambda x, acc: jnp.where(e == x, start_ref[x], acc),
                             jnp.zeros(e.shape, I32), unroll=8)
        o_ref[:, c0:c1] = base + rank_ref[:, c0:c1]


def _dest(cnt_start, e_idx, rank):
    k, t = e_idx.shape
    full = pl.BlockSpec((k, t), lambda i, st: (0, 0))
    return pl.pallas_call(
        functools.partial(_dest_kernel, width=t),
        grid_spec=pltpu.PrefetchScalarGridSpec(
            num_scalar_prefetch=1, grid=(1,), in_specs=[full, full], out_specs=full),
        out_shape=jax.ShapeDtypeStruct((k, t), I32),
        compiler_params=_cparams(),
        name="moe_dest",
    )(cnt_start, e_idx, rank)


def _sc_mesh():
    return plsc.VectorSubcoreMesh(core_axis_name="core", subcore_axis_name="subcore")


def _sc_pipeline(body, n_steps, in_specs, out_specs):
    return pltpu.emit_pipeline(body, grid=(n_steps,), in_specs=in_specs, out_specs=out_specs,
                               core_axis_name=("core", "subcore"), dimension_semantics=(pltpu.PARALLEL,))


def _sc_scatter_rows(x, half, dest_flat, n_rows):
    t = x.shape[0]
    tiles = t // SC_WINDOW
    n_steps = dest_flat.shape[1] // SC_WINDOW

    @functools.partial(pl.kernel, mesh=_sc_mesh(), scratch_types=[],
                       out_type=jax.ShapeDtypeStruct((n_rows, QUARTER_WORDS), x.dtype))
    def scatter(x_hbm, i_hbm, o_hbm):
        def body(x_vmem, i_vmem):
            pltpu.sync_copy(x_vmem, o_hbm.at[i_vmem.at[0]])

        _sc_pipeline(body, n_steps,
                     [pl.BlockSpec((SC_WINDOW, QUARTER_WORDS), lambda i: (i % tiles, half)),
                      pl.BlockSpec((1, SC_WINDOW), lambda i: (0, i))], [])(x_hbm, i_hbm)

    return scatter(x, dest_flat)


def _sc_gather_rows(table, idx_flat):
    n = idx_flat.shape[1]

    @functools.partial(pl.kernel, mesh=_sc_mesh(), scratch_types=[],
                       out_type=jax.ShapeDtypeStruct((n, table.shape[1]), table.dtype))
    def gather(t_hbm, i_hbm, o_hbm):
        def body(i_vmem, o_vmem):
            pltpu.sync_copy(t_hbm.at[i_vmem.at[0]], o_vmem)

        _sc_pipeline(body, n // SC_WINDOW,
                     [pl.BlockSpec((1, SC_WINDOW), lambda i: (0, i))],
                     [pl.BlockSpec((SC_WINDOW, table.shape[1]), lambda i: (i, 0))])(i_hbm, o_hbm)

    return gather(table, idx_flat)


def _two_group_specs(tm, width, n_prompt_tiles):
    return [pl.BlockSpec((tm, width), lambda i: (jnp.minimum(i, n_prompt_tiles - 1), 0)),
            pl.BlockSpec((tm, width), lambda i: (jnp.maximum(i - n_prompt_tiles, 0), 0))]


def _experts_kernel(sblk_ref, sexp_ref, slo_ref, shi_ref, snext_ref, first_ref,
                    xa_ref, xb_ref, wg_hbm, wu_hbm, wd_hbm, ya_ref, yb_ref,
                    wg_buf, wu_buf, wd_buf, wgu_s, wd_s, state_ref, sems, *, bm):
    p = pl.program_id(0)
    lo = slo_ref[p]
    expert = sexp_ref[p]

    def weight_copies(e, slot):
        return [pltpu.make_async_copy(src.at[e], dst.at[slot], sems.at[slot])
                for src, dst in ((wg_hbm, wg_buf), (wu_hbm, wu_buf), (wd_hbm, wd_buf))]

    @pl.when(p == 0)
    def _():
        state_ref[0] = -1
        state_ref[1] = 0
        for c in weight_copies(first_ref[0], 0):
            c.start()

    @pl.when(shi_ref[p] > lo)
    def _():
        @pl.when(state_ref[0] != expert)
        def _():
            slot = state_ref[1]
            for c in weight_copies(expert, slot):
                c.wait()
            wgu_s[:, :D_EXPERT] = wg_buf[slot].astype(BF16)
            wgu_s[:, D_EXPERT:] = wu_buf[slot].astype(BF16)
            wd_s[...] = wd_buf[slot].astype(BF16)
            state_ref[0] = expert
            state_ref[1] = 1 - slot
            upcoming = snext_ref[p]

            @pl.when(upcoming >= 0)
            def _():
                for c in weight_copies(upcoming, 1 - slot):
                    c.start()

        q = QUARTER_WORDS
        quarters = {}
        quarters[0], quarters[2] = _unpack_bf16_pairs(xa_ref[...])
        quarters[1], quarters[3] = _unpack_bf16_pairs(xb_ref[...])
        gu = functools.reduce(jnp.add, [
            jnp.dot(quarters[c], wgu_s[c * q:(c + 1) * q, :], preferred_element_type=F32) for c in range(4)])
        hb = (_silu(gu[:, :D_EXPERT]) * gu[:, D_EXPERT:]).astype(BF16)
        y = _pack_bf16_pairs(jnp.dot(hb, wd_s[...], preferred_element_type=F32))

        @pl.when(lo == 0)
        def _():
            ya_ref[...] = y[:, :q]
            yb_ref[...] = y[:, q:]

        @pl.when(lo > 0)
        def _():
            keep = lax.broadcasted_iota(I32, (bm, 1), 0) >= lo
            ya_ref[...] = jnp.where(keep, y[:, :q], ya_ref[...])
            yb_ref[...] = jnp.where(keep, y[:, q:], yb_ref[...])


def _experts(xs_a, xs_b, w_gate, w_up, w_down, seg_blk, seg_exp, seg_lo, seg_hi, seg_next, first_exp, *, bm):
    n_rows = xs_a.shape[0]
    n_seg = seg_blk.shape[0]
    ring = 2
    hbm = pl.BlockSpec(memory_space=pl.ANY)
    rows = pl.BlockSpec((bm, QUARTER_WORDS), lambda p, sb, *_: (sb[p], 0))
    grid_spec = pltpu.PrefetchScalarGridSpec(
        num_scalar_prefetch=6,
        grid=(n_seg,),
        in_specs=[rows, rows, hbm, hbm, hbm],
        out_specs=[rows, rows],
        scratch_shapes=[pltpu.VMEM((ring, D_MODEL, D_EXPERT), F32), pltpu.VMEM((ring, D_MODEL, D_EXPERT), F32),
                        pltpu.VMEM((ring, D_EXPERT, D_MODEL), F32),
                        pltpu.VMEM((D_MODEL, 2 * D_EXPERT), BF16), pltpu.VMEM((D_EXPERT, D_MODEL), BF16),
                        pltpu.SMEM((2,), I32), pltpu.SemaphoreType.DMA((ring,))],
    )
    return pl.pallas_call(
        functools.partial(_experts_kernel, bm=bm),
        grid_spec=grid_spec,
        out_shape=[jax.ShapeDtypeStruct((n_rows, QUARTER_WORDS), U32)] * 2,
        compiler_params=_cparams(),
        name="moe_experts",
    )(seg_blk, seg_exp, seg_lo, seg_hi, seg_next, first_exp, xs_a, xs_b, w_gate, w_up, w_down)


def _final_kernel(ew_ref, pa_ref, pb_ref, xsp_ref, xss_ref, g2p_ref, g2s_ref, gfin_ref, op_ref, os_ref,
                  *, tm, n_prompt_tiles):
    ew = ew_ref[...]
    quarters = [jnp.zeros((tm, QUARTER_WORDS), F32)] * 4
    for k in range(TOP_K):
        w_k = ew[:, k:k + 1]
        a_low, a_high = _unpack_bf16_pairs(pa_ref[k])
        b_low, b_high = _unpack_bf16_pairs(pb_ref[k])
        for c, part in enumerate((a_low, b_low, a_high, b_high)):
            quarters[c] = quarters[c] + w_k * part.astype(F32)
    routed = jnp.concatenate(quarters, axis=1)

    def finish(xs_ref, gate2_ref, o_ref):
        x2 = xs_ref[...] + gate2_ref[...] * routed
        o_ref[...] = x2 * lax.rsqrt(jnp.mean(x2 * x2, axis=-1, keepdims=True) + EPS) * gfin_ref[...]

    is_prompt = pl.program_id(0) < n_prompt_tiles
    pl.when(is_prompt)(lambda: finish(xsp_ref, g2p_ref, op_ref))
    pl.when(jnp.logical_not(is_prompt))(lambda: finish(xss_ref, g2s_ref, os_ref))


def _final(ew_tok, planes_a, planes_b, xs_p, xs_s, gate2_p, gate2_s, g_final, *, tm, tiles_per_batch):
    n_prompt_tiles = xs_p.shape[0] // tm
    n_tiles = n_prompt_tiles + xs_s.shape[0] // tm
    nb = gate2_p.shape[0]
    planes = pl.BlockSpec((TOP_K, tm, QUARTER_WORDS), lambda i: (0, i, 0))
    return pl.pallas_call(
        functools.partial(_final_kernel, tm=tm, n_prompt_tiles=n_prompt_tiles),
        grid=(n_tiles,),
        in_specs=[pl.BlockSpec((tm, TOP_K), lambda i: (i, 0)), planes, planes]
                 + _two_group_specs(tm, D_MODEL, n_prompt_tiles)
                 + [pl.BlockSpec((None, 1, D_MODEL), lambda i: (jnp.minimum(i // tiles_per_batch, nb - 1), 0, 0)),
                    pl.BlockSpec((None, tm, D_MODEL), lambda i: (jnp.maximum(i - n_prompt_tiles, 0), 0, 0)),
                    _const_spec(g_final)],
        out_specs=_two_group_specs(tm, D_MODEL, n_prompt_tiles),
        out_shape=[jax.ShapeDtypeStruct(xs_p.shape, F32), jax.ShapeDtypeStruct(xs_s.shape, F32)],
        compiler_params=_cparams(),
        name="moe_combine_final",
    )(ew_tok, planes_a, planes_b, xs_p, xs_s, gate2_p, gate2_s, g_final)


def _pick_tile(n, preferred):
    t = min(n, preferred)
    assert n % t == 0, (n, t)
    return t


def kernel(x_prompt, x_sample, c_prompt, c_sample, state_conv, state_short_conv, state_delta, w_ada, b_ada, g_mix, w_in, conv_dw, conv_dw_b, conv_ln_g, conv_ln_b, w_pa, short_conv_w, a_log, dt_bias, dn_norm_g, w_pb, w_o, g_ffn, w_router, e_bias, w_gate, w_up, w_down, ws_gate, ws_up, ws_down, g_final):
    nb, seq, d = x_prompt.shape
    ns = x_sample.shape[0]
    assert d == D_MODEL and w_ada.shape[0] == 1 and x_sample.shape[1] == 1
    assert seq % CHUNK == 0 and seq >= CONV_HALO and ns % SUBLANES == 0
    tp = nb * seq
    row1 = lambda a: a.reshape(1, -1)

    wi = w_in[0].astype(BF16)
    o_q = 2 * CONV_CH
    o_z = o_q + 3 * DN_WIDTH
    o_a = o_z + DN_WIDTH
    o_ga = o_a + 2 * N_HEADS
    w_proj = {
        "glu": wi[:, :o_q], "qkv": wi[:, o_q:o_z], "z": wi[:, o_z:o_a],
        "ab": jnp.pad(wi[:, o_a:o_ga], ((0, 0), (0, LANES - 2 * N_HEADS))),
        "ga": wi[:, o_ga:o_ga + D_MODEL], "gb": wi[:, o_ga + D_MODEL:],
    }
    nea = jnp.pad(-jnp.exp(a_log[0].astype(F32)), (0, LANES - N_HEADS)).reshape(1, LANES)
    dtb = jnp.pad(dt_bias[0].astype(F32), (0, LANES - N_HEADS)).reshape(1, LANES)
    w_post = {
        "pa": w_pa[0].astype(BF16), "pb": w_pb[0].astype(BF16), "o": w_o[0].astype(BF16),
        "router_t": w_router[0].T.astype(BF16),
        "s_gu": jnp.concatenate([ws_gate[0], ws_up[0]], axis=1).astype(BF16),
        "s_down": ws_down[0].astype(BF16),
    }
    ebias_col = e_bias[0].astype(F32).reshape(N_EXPERTS, 1)
    g_mix1, g_ffn1, g_fin1 = row1(g_mix[0]), row1(g_ffn[0]), row1(g_final)
    cdw, cdb, clg, clb = conv_dw[0], row1(conv_dw_b[0]), row1(conv_ln_g[0]), row1(conv_ln_b[0])
    scw, dng = short_conv_w[0], row1(dn_norm_g[0])

    ada = _ada(jnp.concatenate([c_prompt, c_sample], axis=0), w_ada[0].astype(BF16), b_ada[0])
    ada = ada.reshape(nb + ns, 6, D_MODEL)
    mod_p = [ada[:nb, m].reshape(nb, 1, D_MODEL) for m in range(6)]
    mod_s = [ada[nb:, m].reshape(1, ns, D_MODEL) for m in range(6)]
    shift1, scale1, gate1, shift2, scale2, gate2 = range(6)

    tm_p = _pick_tile(seq, 256)
    tpm_p = seq // tm_p
    xp = x_prompt.reshape(tp, D_MODEL)
    u_p, qkv_p, zs_p, ab_p, sga_p, sgb_p = _in_proj(
        xp, mod_p[scale1], mod_p[shift1], g_mix1, w_proj, nea, dtb, tm=tm_p, tiles_per_mod=tpm_p)
    tl = _pick_tile(seq, 256)
    acta_p = _conv_branch(u_p, jnp.zeros((nb, CONV_HALO, CONV_CH), F32), cdw, cdb, clg, clb,
                          tl=tl, tiles_per_batch=seq // tl)
    tc = _pick_tile(seq, 4 * CHUNK)
    actb_p, s_new_p = _delta_branch(
        qkv_p, jnp.zeros((nb, SC_HALO, 3 * DN_WIDTH), F32),
        jnp.zeros((nb, N_HEADS, HEAD_DIM, HEAD_DIM), F32), scw, ab_p, zs_p, dng,
        tc=tc, tiles_per_batch=seq // tc)

    xs_ = x_sample.reshape(ns, D_MODEL)
    u_s, qkv_s, zs_s, ab_s, sga_s, sgb_s = _in_proj(
        xs_, mod_s[scale1], mod_s[shift1], g_mix1, w_proj, nea, dtb, tm=ns, tiles_per_mod=1)
    bt = SUBLANES
    acta_s, nch_s, qkvp_s, nsh_s = _step_front(
        u_s, qkv_s, state_conv[0].reshape(ns, -1), state_short_conv[0].reshape(ns, -1),
        cdw, cdb, clg, clb, scw, bt=bt)
    actb_s, s_new_s = _step_delta(qkvp_s, ab_s, zs_s, state_delta[0].astype(F32), dng, bt=bt)

    cnt0 = jnp.zeros((N_EXPERTS, LANES), F32)
    xsh_p, h2_p, eidx_p, ew_p, rank_p, cnt_p = _post(
        acta_p, actb_p, sga_p, sgb_p, xp, mod_p[gate1], mod_p[scale2], mod_p[shift2], mod_p[gate2],
        cnt0, g_ffn1, w_post, ebias_col, tm=tm_p, tiles_per_mod=tpm_p)
    xsh_s, h2_s, eidx_s, ew_s, rank_s, cnt_all = _post(
        acta_s, actb_s, sga_s, sgb_s, xs_, mod_s[gate1], mod_s[scale2], mod_s[shift2], mod_s[gate2],
        cnt_p, g_ffn1, w_post, ebias_col, tm=ns, tiles_per_mod=1)

    bm = 512
    tmd = LANES
    t_all = tp + ns
    n_rows = t_all * TOP_K
    assert seq % tmd == 0 and ns % tmd == 0 and n_rows % bm == 0 and t_all % SC_WINDOW == 0
    n_blocks = n_rows // bm
    counts = cnt_all[:, 0].astype(I32)
    cnt_end = jnp.cumsum(counts)
    cnt_start = cnt_end - counts
    e_idx = jnp.concatenate([eidx_p, eidx_s], axis=1)
    rank = jnp.concatenate([rank_p, rank_s], axis=1)
    dest_flat = _dest(cnt_start, e_idx, rank).reshape(1, n_rows)
    seg_start = jnp.sort(jnp.concatenate([jnp.arange(n_blocks, dtype=I32) * bm, cnt_start]))
    seg_end = jnp.concatenate([seg_start[1:], jnp.full((1,), n_rows, I32)])
    seg_blk = jnp.minimum(seg_start // bm, n_blocks - 1)
    seg_lo = seg_start - seg_blk * bm
    seg_hi = seg_end - seg_blk * bm
    seg_exp = jnp.sum((cnt_end[None, :] <= seg_start[:, None]).astype(I32), axis=1)
    seg_exp = jnp.minimum(seg_exp, N_EXPERTS - 1)
    seg_exp = lax.cummax(jnp.where(seg_hi > seg_lo, seg_exp, 0))
    none = jnp.int32(N_EXPERTS)
    in_use_from = lax.cummin(jnp.where(counts > 0, jnp.arange(N_EXPERTS, dtype=I32), none), reverse=True)
    next_in_use = jnp.concatenate([in_use_from[1:], jnp.full((1,), none, I32)])
    next_in_use = jnp.where(next_in_use == none, -1, next_in_use)
    seg_next = next_in_use[seg_exp]
    first_exp = jnp.minimum(in_use_from[:1], N_EXPERTS - 1)

    h2_all = jnp.concatenate([h2_p, h2_s], axis=0)
    xs_a, xs_b = (_sc_scatter_rows(h2_all, half, dest_flat, n_rows) for half in range(2))
    ys_a, ys_b = _experts(xs_a, xs_b, w_gate[0], w_up[0], w_down[0], seg_blk, seg_exp, seg_lo, seg_hi, seg_next,
                          first_exp, bm=bm)

    planes_a, planes_b = (_sc_gather_rows(ys, dest_flat).reshape(TOP_K, t_all, QUARTER_WORDS) for ys in (ys_a, ys_b))
    ew_tok = jnp.concatenate([ew_p, ew_s], axis=1).T
    y_p, y_s = _final(ew_tok, planes_a, planes_b, xsh_p, xsh_s, mod_p[gate2],
                      mod_s[gate2].reshape(ns // tmd, tmd, D_MODEL), g_fin1, tm=tmd, tiles_per_batch=seq // tmd)

    hist_rows = CONV_WIDTH - 1
    sc_rows = SHORT_CONV - 1
    y_prompt = y_p.reshape(nb, seq, D_MODEL)
    y_sample = y_s.reshape(ns, 1, D_MODEL)
    new_conv_prompt = u_p.reshape(nb, seq, CONV_CH)[:, seq - hist_rows:][None]
    new_sc_prompt = qkv_p.reshape(nb, seq, 3 * DN_WIDTH)[:, seq - sc_rows:][None]
    new_conv_sample = nch_s.reshape(ns, hist_rows, CONV_CH)[None]
    new_sc_sample = nsh_s.reshape(ns, sc_rows, 3 * DN_WIDTH)[None]
    return (y_prompt, y_sample, new_conv_prompt, new_sc_prompt, s_new_p[None],
            new_conv_sample, new_sc_sample, s_new_s[None])
```

```python
import functools

import jax
import jax.numpy as jnp
from jax import lax
from jax.experimental import pallas as pl
from jax.experimental.pallas import tpu as pltpu
from jax.experimental.pallas import tpu_sc as plsc

F32 = jnp.float32
BF16 = jnp.bfloat16
I32 = jnp.int32
U32 = jnp.uint32

EPS = 1e-6
D_MODEL = 1024
CONV_CH = 512
CONV_WIDTH = 31
N_HEADS = 4
HEAD_DIM = 128
DN_WIDTH = N_HEADS * HEAD_DIM
SHORT_CONV = 4
CHUNK = 64
CHUNK_SHIFT = CHUNK.bit_length() - 1
assert 1 << CHUNK_SHIFT == CHUNK
SUB = 16
SUB_SHIFT = SUB.bit_length() - 1
assert 1 << SUB_SHIFT == SUB and CHUNK % SUB == 0
N_EXPERTS = 256
TOP_K = 8
N_GROUPS = 8
GROUP_SIZE = N_EXPERTS // N_GROUPS
TOPK_GROUPS = 4
D_EXPERT = 256
D_SHARED = 256
ROUTE_SCALE = 2.5

LANES = 128
SUBLANES = 8
CONV_HALO = 32
SC_HALO = 8
VMEM_LIMIT = 48 * 1024 * 1024
SC_WINDOW = 128
QUARTER_WORDS = D_MODEL // 4

NT_DIMS = (((1,), (1,)), ((), ()))
TN_DIMS = (((0,), (0,)), ((), ()))


def _cparams(n_grid_dims=1):
    return pltpu.CompilerParams(
        dimension_semantics=("arbitrary",) * n_grid_dims,
        vmem_limit_bytes=VMEM_LIMIT)


def _sigmoid(x):
    return jax.nn.sigmoid(x)


def _silu(x):
    return x * jax.nn.sigmoid(x)


def _bdot(a, b):
    return jnp.dot(a.astype(BF16), b.astype(BF16), preferred_element_type=F32)


def _pack_bf16_pairs(x):
    half = x.shape[1] // 2
    bits = lax.bitcast_convert_type(x.astype(BF16).astype(F32), U32)
    return bits[:, half:] | (bits[:, :half] >> 16)


def _unpack_bf16_pairs(w):
    low = lax.bitcast_convert_type(w << 16, F32).astype(BF16)
    high = lax.bitcast_convert_type(w & jnp.uint32(0xFFFF0000), F32).astype(BF16)
    return low, high


def _dot3(a, b):
    a_hi = a.astype(BF16)
    b_hi = b.astype(BF16)
    a_lo = (a - a_hi.astype(F32)).astype(BF16)
    b_lo = (b - b_hi.astype(F32)).astype(BF16)
    dot = lambda x, y: jnp.dot(x, y, preferred_element_type=F32)
    return dot(a_hi, b_hi) + (dot(a_hi, b_lo) + dot(a_lo, b_hi))


def _dot_exact_lhs(a_bf, b):
    hi = b.astype(BF16)
    r1 = b - hi.astype(F32)
    mid = r1.astype(BF16)
    low = (r1 - mid.astype(F32)).astype(BF16)
    dot = lambda p: jnp.dot(a_bf, p, preferred_element_type=F32)
    return dot(hi) + dot(mid) + dot(low)


def _ada_kernel(c_ref, w_ref, b_ref, o_ref):
    o_ref[...] = _bdot(_silu(c_ref[...]), w_ref[...]) + b_ref[...]


def _ada(c_all, w_ada_bf, b_ada):
    n = c_all.shape[0]
    width = w_ada_bf.shape[1]
    tn = 1536
    return pl.pallas_call(
        _ada_kernel,
        grid=(width // tn,),
        in_specs=[pl.BlockSpec((n, D_MODEL), lambda j: (0, 0)),
                  pl.BlockSpec((D_MODEL, tn), lambda j: (0, j)),
                  pl.BlockSpec((1, tn), lambda j: (0, j))],
        out_specs=pl.BlockSpec((n, tn), lambda j: (0, j)),
        out_shape=jax.ShapeDtypeStruct((n, width), F32),
        compiler_params=_cparams(),
        name="ada",
    )(c_all, w_ada_bf, b_ada.reshape(1, width))


def _in_proj_kernel(x_ref, scale_ref, shift_ref, g_ref, wglu_ref, wqkv_ref, wz_ref, wab_ref,
                    wga_ref, wgb_ref, nea_ref, dtb_ref,
                    u_ref, qkv_ref, zs_ref, ab_ref, sga_ref, sgb_ref):
    x = x_ref[...]
    y = x * lax.rsqrt(jnp.mean(x * x, axis=-1, keepdims=True) + EPS) * g_ref[...]
    h = (y * (1.0 + scale_ref[...]) + shift_ref[...]).astype(BF16)
    glu = jnp.dot(h, wglu_ref[...], preferred_element_type=F32)
    u_ref[...] = glu[:, :CONV_CH] * _sigmoid(glu[:, CONV_CH:])
    qkv_ref[...] = jnp.dot(h, wqkv_ref[...], preferred_element_type=F32)
    zs_ref[...] = _silu(jnp.dot(h, wz_ref[...], preferred_element_type=F32))
    ab = jnp.dot(h, wab_ref[...], preferred_element_type=F32)
    sp = ab + dtb_ref[...]
    softplus = jnp.maximum(sp, 0.0) + jnp.log1p(jnp.exp(-jnp.abs(sp)))
    lane = lax.broadcasted_iota(I32, ab.shape, 1)
    ab_ref[...] = jnp.where(lane < N_HEADS, nea_ref[...] * softplus, _sigmoid(ab))
    sga_ref[...] = _sigmoid(jnp.dot(h, wga_ref[...], preferred_element_type=F32))
    sgb_ref[...] = _sigmoid(jnp.dot(h, wgb_ref[...], preferred_element_type=F32))


def _mod_spec(mod, tiles_per_mod):
    return pl.BlockSpec((None,) + mod.shape[1:], lambda i: (i // tiles_per_mod, 0, 0))


def _const_spec(a):
    nd = a.ndim
    return pl.BlockSpec(a.shape, lambda i: (0,) * nd)


def _in_proj(x2d, scale, shift, g_mix, w, nea, dtb, *, tm, tiles_per_mod):
    t = x2d.shape[0]
    row = lambda c: pl.BlockSpec((tm, c), lambda i: (i, 0))
    consts = [g_mix, w["glu"], w["qkv"], w["z"], w["ab"], w["ga"], w["gb"], nea, dtb]
    widths = [CONV_CH, 3 * DN_WIDTH, DN_WIDTH, LANES, D_MODEL, D_MODEL]
    return pl.pallas_call(
        _in_proj_kernel,
        grid=(t // tm,),
        in_specs=[row(D_MODEL), _mod_spec(scale, tiles_per_mod), _mod_spec(shift, tiles_per_mod)]
                 + [_const_spec(a) for a in consts],
        out_specs=[row(c) for c in widths],
        out_shape=[jax.ShapeDtypeStruct((t, c), F32) for c in widths],
        compiler_params=_cparams(),
        name="in_proj",
    )(x2d, scale, shift, *consts)


def _layer_norm_swish(y, g, b):
    mu = jnp.mean(y, axis=-1, keepdims=True)
    yc = y - mu
    var = jnp.mean(yc * yc, axis=-1, keepdims=True)
    return _silu(yc * lax.rsqrt(var + EPS) * g + b)


def _conv_kernel(u_ref, halo_ref, hist_ref, w_ref, b_ref, lg_ref, lb_ref, o_ref, win_ref, shift_ref,
                 *, tl, tiles_per_batch, rows_per_pass):
    i = pl.program_id(0)
    first = (i % tiles_per_batch) == 0

    @pl.when(first)
    def _():
        win_ref[0:CONV_HALO, :] = hist_ref[...]

    @pl.when(jnp.logical_not(first))
    def _():
        win_ref[0:CONV_HALO, :] = halo_ref[...]

    win_ref[CONV_HALO:CONV_HALO + tl, :] = u_ref[...]
    tap0 = CONV_HALO - (CONV_WIDTH - 1)
    n_shifted = CONV_HALO + tl - SUBLANES
    for b in range(1, SUBLANES):
        shift_ref[b - 1] = win_ref[b:b + n_shifted, :]

    for r in range(tl // rows_per_pass):
        base = r * rows_per_pass
        acc = jnp.zeros((rows_per_pass, CONV_CH), F32)
        for j in range(CONV_WIDTH):
            a, b = divmod(tap0 + j, SUBLANES)
            start = base + a * SUBLANES
            rows = (win_ref[start:start + rows_per_pass, :] if b == 0
                    else shift_ref[b - 1, start:start + rows_per_pass, :])
            acc = acc + w_ref[j:j + 1, :] * rows
        act = _layer_norm_swish(acc + b_ref[...], lg_ref[...], lb_ref[...])
        o_ref[base:base + rows_per_pass, :] = act.astype(o_ref.dtype)


def _conv_branch(u2d, hist, conv_dw, conv_dw_b, ln_g, ln_b, *, tl, tiles_per_batch):
    t = u2d.shape[0]
    halo_blocks = tl // CONV_HALO
    consts = [conv_dw, conv_dw_b, ln_g, ln_b]
    return pl.pallas_call(
        functools.partial(_conv_kernel, tl=tl, tiles_per_batch=tiles_per_batch, rows_per_pass=64),
        grid=(t // tl,),
        in_specs=[pl.BlockSpec((tl, CONV_CH), lambda i: (i, 0)),
                  pl.BlockSpec((CONV_HALO, CONV_CH), lambda i: (jnp.maximum(i * halo_blocks - 1, 0), 0)),
                  pl.BlockSpec((None, CONV_HALO, CONV_CH), lambda i: (i // tiles_per_batch, 0, 0))]
                 + [_const_spec(a) for a in consts],
        out_specs=pl.BlockSpec((tl, CONV_CH), lambda i: (i, 0)),
        out_shape=jax.ShapeDtypeStruct((t, CONV_CH), BF16),
        scratch_shapes=[pltpu.VMEM((CONV_HALO + tl, CONV_CH), F32),
                        pltpu.VMEM((SUBLANES - 1, CONV_HALO + tl - SUBLANES, CONV_CH), F32)],
        compiler_params=_cparams(),
        name="conv_branch",
    )(u2d, u2d, hist, *consts)


def _l2n(x):
    return x * lax.rsqrt(jnp.sum(x * x, axis=-1, keepdims=True) + 1e-6)


def _gated_head_norm(o, g, zs):
    return o * lax.rsqrt(jnp.mean(o * o, axis=-1, keepdims=True) + EPS) * g * zs


def _delta_kernel(qkv_ref, halo_ref, hist_ref, scw_ref, ab_ref, zs_ref, s0_ref, dng_ref,
                  o_ref, s_ref, win_ref, c_ref, *, tc, tiles_per_batch):
    i = pl.program_id(0)
    first = (i % tiles_per_batch) == 0

    @pl.when(first)
    def _():
        win_ref[0:SC_HALO, :] = hist_ref[...]
        s_ref[...] = s0_ref[...]

    @pl.when(jnp.logical_not(first))
    def _():
        win_ref[0:SC_HALO, :] = halo_ref[...]

    win_ref[SC_HALO:SC_HALO + tc, :] = qkv_ref[...]
    tap0 = SC_HALO - (SHORT_CONV - 1)
    acc = jnp.zeros((tc, 3 * DN_WIDTH), F32)
    for j in range(SHORT_CONV):
        acc = acc + scw_ref[j:j + 1, :] * win_ref[tap0 + j:tap0 + j + tc, :]
    c_ref[...] = _silu(acc)

    row = lax.broadcasted_iota(I32, (tc, tc), 0)
    col = lax.broadcasted_iota(I32, (tc, tc), 1)
    same_chunk = lax.shift_right_logical(row, CHUNK_SHIFT) == lax.shift_right_logical(col, CHUNK_SHIFT)
    incl = jnp.logical_and(same_chunk, row >= col)
    row_sub = lax.shift_right_logical(row, SUB_SHIFT)
    col_sub = lax.shift_right_logical(col, SUB_SHIFT)
    strict_sub = jnp.logical_and(row_sub == col_sub, row > col)
    below_sub = jnp.logical_and(same_chunk, row_sub > col_sub)
    eye = (row == col).astype(F32)
    ab = ab_ref[...]
    gcum = _dot_exact_lhs(incl.astype(BF16), ab)
    gcum_t = gcum.T
    n_ch = tc // CHUNK

    heads = range(N_HEADS)
    q, k, gc, kk, a_qk, rhs, qd = [], [], [], [], [], [], []
    for h in heads:
        lo = h * HEAD_DIM
        q_h = _l2n(c_ref[:, lo:lo + HEAD_DIM]) * (HEAD_DIM ** -0.5)
        k_h = _l2n(c_ref[:, DN_WIDTH + lo:DN_WIDTH + lo + HEAD_DIM])
        v_h = c_ref[:, 2 * DN_WIDTH + lo:2 * DN_WIDTH + lo + HEAD_DIM]
        beta = ab[:, N_HEADS + h:N_HEADS + h + 1]
        gc_h = gcum[:, h:h + 1]
        gr = gcum_t[h:h + 1, :]
        decay = jnp.exp(jnp.where(incl, gc_h - gr, -jnp.inf))
        kb = k_h * beta
        kq = lax.dot_general(jnp.concatenate([kb, q_h], axis=0).astype(BF16), k_h.astype(BF16),
                             NT_DIMS, preferred_element_type=F32)
        e_gc = jnp.exp(gc_h)
        q.append(q_h)
        k.append(k_h)
        gc.append(gc_h)
        kk.append(kq[:tc] * decay)
        a_qk.append(kq[tc:] * decay)
        rhs.append(jnp.concatenate([v_h * beta, kb * e_gc], axis=1))
        qd.append(q_h * e_gc)

    pw = [jnp.where(strict_sub, kk[h], 0.0) for h in heads]
    d_inv = [eye - pw[h] for h in heads]
    for _ in range(SUB_SHIFT - 1):
        pw = [_bdot(pw[h], pw[h]) for h in heads]
        d_inv = [d_inv[h] + _bdot(d_inv[h], pw[h]) for h in heads]
    c_rhs = [_bdot(d_inv[h], rhs[h]) for h in heads]
    m_off = [_bdot(d_inv[h], jnp.where(below_sub, kk[h], 0.0)) for h in heads]
    sol = c_rhs
    for _ in range(CHUNK // SUB - 1):
        sol = [c_rhs[h] - _bdot(m_off[h], sol[h]) for h in heads]

    s = [s_ref[h] for h in heads]
    u_parts = [[] for _ in heads]
    o_parts = [[] for _ in heads]
    for c in range(n_ch):
        r0 = c * CHUNK
        for h in heads:
            u0_c = sol[h][r0:r0 + CHUNK, :HEAD_DIM]
            w_c = sol[h][r0:r0 + CHUNK, HEAD_DIM:]
            wq_s = _bdot(jnp.concatenate([w_c, qd[h][r0:r0 + CHUNK]], axis=0), s[h])
            u_c = u0_c - wq_s[:CHUNK]
            g_last = gcum[r0 + CHUNK - 1:r0 + CHUNK, h:h + 1]
            k_dec = k[h][r0:r0 + CHUNK] * jnp.exp(g_last - gc[h][r0:r0 + CHUNK])
            s[h] = jnp.exp(g_last) * s[h] + lax.dot_general(
                k_dec.astype(BF16), u_c.astype(BF16), TN_DIMS, preferred_element_type=F32)
            u_parts[h].append(u_c)
            o_parts[h].append(wq_s[CHUNK:])
    for h in heads:
        lo = h * HEAD_DIM
        s_ref[h] = s[h]
        o = jnp.concatenate(o_parts[h], axis=0) + _bdot(a_qk[h], jnp.concatenate(u_parts[h], axis=0))
        o_ref[:, lo:lo + HEAD_DIM] = _gated_head_norm(
            o, dng_ref[...], zs_ref[:, lo:lo + HEAD_DIM]).astype(o_ref.dtype)


def _delta_branch(qkv2d, hist, s0, scw, ab, zs, dng, *, tc, tiles_per_batch):
    t = qkv2d.shape[0]
    nb = s0.shape[0]
    halo_blocks = tc // SC_HALO
    wqkv = 3 * DN_WIDTH
    state_spec = pl.BlockSpec((None, N_HEADS, HEAD_DIM, HEAD_DIM), lambda i: (i // tiles_per_batch, 0, 0, 0))
    return pl.pallas_call(
        functools.partial(_delta_kernel, tc=tc, tiles_per_batch=tiles_per_batch),
        grid=(t // tc,),
        in_specs=[pl.BlockSpec((tc, wqkv), lambda i: (i, 0)),
                  pl.BlockSpec((SC_HALO, wqkv), lambda i: (jnp.maximum(i * halo_blocks - 1, 0), 0)),
                  pl.BlockSpec((None, SC_HALO, wqkv), lambda i: (i // tiles_per_batch, 0, 0)),
                  _const_spec(scw),
                  pl.BlockSpec((tc, LANES), lambda i: (i, 0)),
                  pl.BlockSpec((tc, DN_WIDTH), lambda i: (i, 0)),
                  state_spec,
                  _const_spec(dng)],
        out_specs=[pl.BlockSpec((tc, DN_WIDTH), lambda i: (i, 0)), state_spec],
        out_shape=[jax.ShapeDtypeStruct((t, DN_WIDTH), BF16),
                   jax.ShapeDtypeStruct((nb, N_HEADS, HEAD_DIM, HEAD_DIM), F32)],
        scratch_shapes=[pltpu.VMEM((SC_HALO + tc, wqkv), F32), pltpu.VMEM((tc, wqkv), F32)],
        compiler_params=_cparams(),
        name="delta_branch",
    )(qkv2d, qkv2d, hist, scw, ab, zs, s0, dng)


def _step_front_kernel(u_ref, qkv_ref, ch_ref, sh_ref, cw_ref, cb_ref, lg_ref, lb_ref, scw_ref,
                       acta_ref, nch_ref, qkvp_ref, nsh_ref):
    hist_rows = CONV_WIDTH - 1
    u = u_ref[...]
    acc = cw_ref[hist_rows:hist_rows + 1, :] * u
    for j in range(hist_rows):
        acc = acc + cw_ref[j:j + 1, :] * ch_ref[:, j * CONV_CH:(j + 1) * CONV_CH]
    acta_ref[...] = _layer_norm_swish(acc + cb_ref[...], lg_ref[...], lb_ref[...]).astype(acta_ref.dtype)
    nch_ref[:, :(hist_rows - 1) * CONV_CH] = ch_ref[:, CONV_CH:]
    nch_ref[:, (hist_rows - 1) * CONV_CH:] = u

    wq = 3 * DN_WIDTH
    sc_rows = SHORT_CONV - 1
    qkv = qkv_ref[...]
    acc = scw_ref[sc_rows:sc_rows + 1, :] * qkv
    for j in range(sc_rows):
        acc = acc + scw_ref[j:j + 1, :] * sh_ref[:, j * wq:(j + 1) * wq]
    c = _silu(acc)
    for h in range(N_HEADS):
        lo = h * HEAD_DIM
        qkvp_ref[:, lo:lo + HEAD_DIM] = _l2n(c[:, lo:lo + HEAD_DIM]) * (HEAD_DIM ** -0.5)
        qkvp_ref[:, DN_WIDTH + lo:DN_WIDTH + lo + HEAD_DIM] = _l2n(c[:, DN_WIDTH + lo:DN_WIDTH + lo + HEAD_DIM])
    qkvp_ref[:, 2 * DN_WIDTH:] = c[:, 2 * DN_WIDTH:]
    nsh_ref[:, :(sc_rows - 1) * wq] = sh_ref[:, wq:]
    nsh_ref[:, (sc_rows - 1) * wq:] = qkv


def _step_front(u, qkv, conv_hist_flat, sc_hist_flat, conv_dw, conv_dw_b, ln_g, ln_b, scw, *, bt):
    n = u.shape[0]
    wq = 3 * DN_WIDTH
    consts = [conv_dw, conv_dw_b, ln_g, ln_b, scw]
    row = lambda c: pl.BlockSpec((bt, c), lambda i: (i, 0))
    widths = [CONV_CH, conv_hist_flat.shape[1], wq, sc_hist_flat.shape[1]]
    dtypes = [BF16, F32, F32, F32]
    return pl.pallas_call(
        _step_front_kernel,
        grid=(n // bt,),
        in_specs=[row(CONV_CH), row(wq), row(widths[1]), row(widths[3])] + [_const_spec(a) for a in consts],
        out_specs=[row(c) for c in widths],
        out_shape=[jax.ShapeDtypeStruct((n, c), d) for c, d in zip(widths, dtypes)],
        compiler_params=_cparams(),
        name="step_front",
    )(u, qkv, conv_hist_flat, sc_hist_flat, *consts)


def _rows_to_cols(x):
    pad = jnp.zeros((LANES - x.shape[0], LANES), x.dtype)
    return jnp.concatenate([x, pad], axis=0).T


def _step_delta_kernel(qkvp_ref, ab_ref, zs_ref, s_ref, dng_ref, o_ref, sn_ref, *, bt):
    ab = ab_ref[...]
    for h in range(N_HEADS):
        lo = h * HEAD_DIM
        q = qkvp_ref[:, lo:lo + HEAD_DIM]
        k = qkvp_ref[:, DN_WIDTH + lo:DN_WIDTH + lo + HEAD_DIM]
        v = qkvp_ref[:, 2 * DN_WIDTH + lo:2 * DN_WIDTH + lo + HEAD_DIM]
        q_cols = _rows_to_cols(q)
        k_cols = _rows_to_cols(k)
        qk = jnp.sum(q * k, axis=-1, keepdims=True)
        alpha = jnp.exp(ab[:, h:h + 1])
        beta = ab[:, N_HEADS + h:N_HEADS + h + 1]
        seqs = range(bt)
        kc = [k_cols[:, j:j + 1] for j in seqs]
        a = [alpha[j:j + 1, :] for j in seqs]
        s_k = [jnp.sum(s_ref[j, h] * kc[j], axis=0, keepdims=True) for j in seqs]
        s_q = [jnp.sum(s_ref[j, h] * q_cols[:, j:j + 1], axis=0, keepdims=True) for j in seqs]
        u = [beta[j:j + 1, :] * (v[j:j + 1, :] - a[j] * s_k[j]) for j in seqs]
        for j in seqs:
            sn_ref[j, h] = a[j] * s_ref[j, h] + kc[j] * u[j]
        o = jnp.concatenate([a[j] * s_q[j] + qk[j:j + 1, :] * u[j] for j in seqs], axis=0)
        o_ref[:, lo:lo + HEAD_DIM] = _gated_head_norm(
            o, dng_ref[...], zs_ref[:, lo:lo + HEAD_DIM]).astype(o_ref.dtype)


def _step_delta(qkvp, ab, zs, s0, dng, *, bt):
    n = qkvp.shape[0]
    row = lambda c: pl.BlockSpec((bt, c), lambda i: (i, 0))
    state_spec = pl.BlockSpec((bt, N_HEADS, HEAD_DIM, HEAD_DIM), lambda i: (i, 0, 0, 0))
    return pl.pallas_call(
        functools.partial(_step_delta_kernel, bt=bt),
        grid=(n // bt,),
        in_specs=[row(3 * DN_WIDTH), row(LANES), row(DN_WIDTH), state_spec, _const_spec(dng)],
        out_specs=[row(DN_WIDTH), state_spec],
        out_shape=[jax.ShapeDtypeStruct((n, DN_WIDTH), BF16), jax.ShapeDtypeStruct(s0.shape, F32)],
        compiler_params=_cparams(),
        name="step_delta",
    )(qkvp, ab, zs, s0, dng)


def _post_kernel(acta_ref, actb_ref, sga_ref, sgb_ref, x_ref, gate1_ref, scale2_ref, shift2_ref,
                 gate2_ref, cnt0_ref, gffn_ref, wpa_ref, wpb_ref, wo_ref, wrt_ref, ebias_ref,
                 wsgu_ref, wsd_ref,
                 xs_ref, h2_ref, eidx_ref, ew_ref, rank_ref, cnt_ref, *, tm):
    i = pl.program_id(0)

    @pl.when(i == 0)
    def _():
        cnt_ref[...] = cnt0_ref[...]

    y_a = jnp.dot(acta_ref[...], wpa_ref[...], preferred_element_type=F32)
    y_b = jnp.dot(actb_ref[...], wpb_ref[...], preferred_element_type=F32)
    merged = sga_ref[...] * y_a + sgb_ref[...] * y_b
    mix = _bdot(merged, wo_ref[...])
    x1 = x_ref[...] + gate1_ref[...] * mix
    y = x1 * lax.rsqrt(jnp.mean(x1 * x1, axis=-1, keepdims=True) + EPS) * gffn_ref[...]
    h2 = y * (1.0 + scale2_ref[...]) + shift2_ref[...]
    h2_ref[...] = _pack_bf16_pairs(h2)
    h2b = h2.astype(BF16)

    gu = jnp.dot(h2b, wsgu_ref[...], preferred_element_type=F32)
    shared = _bdot(_silu(gu[:, :D_SHARED]) * gu[:, D_SHARED:], wsd_ref[...])
    xs_ref[...] = x1 + gate2_ref[...] * shared

    logits_t = lax.dot_general(wrt_ref[...], h2b, NT_DIMS, preferred_element_type=F32)
    scores = _sigmoid(logits_t)
    biased = scores + ebias_ref[...]
    neg = -jnp.inf
    big = jnp.int32(1 << 30)
    erow = lax.broadcasted_iota(I32, (N_EXPERTS, tm), 0)

    def first_argmax(vals, rows):
        m = jnp.max(vals, axis=0, keepdims=True)
        return m, jnp.min(jnp.where(vals == m, rows, big), axis=0, keepdims=True)

    group_scores = []
    for g in range(N_GROUPS):
        vals = biased[g * GROUP_SIZE:(g + 1) * GROUP_SIZE, :]
        rows = lax.broadcasted_iota(I32, (GROUP_SIZE, tm), 0) + g * GROUP_SIZE
        m1, i1 = first_argmax(vals, rows)
        m2 = jnp.max(jnp.where(rows == i1, neg, vals), axis=0, keepdims=True)
        group_scores.append(m1 + m2)
    group_sel = [jnp.zeros((1, tm), jnp.bool_)] * N_GROUPS
    for _ in range(TOPK_GROUPS):
        best = functools.reduce(jnp.maximum, group_scores)
        gi = functools.reduce(
            jnp.minimum, [jnp.where(group_scores[g] == best, jnp.int32(g), big) for g in range(N_GROUPS)])
        for g in range(N_GROUPS):
            hit = gi == g
            group_sel[g] = jnp.logical_or(group_sel[g], hit)
            group_scores[g] = jnp.where(hit, neg, group_scores[g])
    cand = jnp.concatenate(
        [jnp.where(group_sel[g], biased[g * GROUP_SIZE:(g + 1) * GROUP_SIZE, :], neg)
         for g in range(N_GROUPS)], axis=0)

    w_rows, hits = [], []
    picked = jnp.zeros((N_EXPERTS, tm), F32)
    for k in range(TOP_K):
        _, ei = first_argmax(cand, erow)
        hit = erow == ei
        eidx_ref[k:k + 1, :] = ei
        w_rows.append(jnp.sum(jnp.where(hit, scores, 0.0), axis=0, keepdims=True))
        hits.append(hit)
        picked = jnp.where(hit, 1.0, picked)
        cand = jnp.where(hit, neg, cand)
    w_sum = functools.reduce(jnp.add, w_rows)
    for k in range(TOP_K):
        ew_ref[k:k + 1, :] = w_rows[k] / w_sum * ROUTE_SCALE

    trow = lax.broadcasted_iota(I32, (tm, tm), 0)
    tcol = lax.broadcasted_iota(I32, (tm, tm), 1)
    before = (trow < tcol).astype(BF16)
    prefix = jnp.dot(picked.astype(BF16), before, preferred_element_type=F32) + cnt_ref[:, 0:1]
    for k in range(TOP_K):
        rank_ref[k:k + 1, :] = jnp.sum(jnp.where(hits[k], prefix, 0.0), axis=0, keepdims=True).astype(I32)
    cnt_ref[...] = cnt_ref[...] + jnp.sum(picked, axis=1, keepdims=True)


def _post(acta, actb, sga, sgb, x2d, gate1, scale2, shift2, gate2, cnt0, g_ffn, w, ebias_col,
          *, tm, tiles_per_mod):
    t = x2d.shape[0]
    row = lambda c: pl.BlockSpec((tm, c), lambda i: (i, 0))
    col = lambda r: pl.BlockSpec((r, tm), lambda i: (0, i))
    mods = [gate1, scale2, shift2, gate2]
    consts = [cnt0, g_ffn, w["pa"], w["pb"], w["o"], w["router_t"], ebias_col, w["s_gu"], w["s_down"]]
    return pl.pallas_call(
        functools.partial(_post_kernel, tm=tm),
        grid=(t // tm,),
        in_specs=[row(CONV_CH), row(DN_WIDTH), row(D_MODEL), row(D_MODEL), row(D_MODEL)]
                 + [_mod_spec(m, tiles_per_mod) for m in mods] + [_const_spec(a) for a in consts],
        out_specs=[row(D_MODEL), row(D_MODEL // 2), col(TOP_K), col(TOP_K), col(TOP_K), _const_spec(cnt0)],
        out_shape=[jax.ShapeDtypeStruct((t, D_MODEL), F32), jax.ShapeDtypeStruct((t, D_MODEL // 2), U32),
                   jax.ShapeDtypeStruct((TOP_K, t), I32), jax.ShapeDtypeStruct((TOP_K, t), F32),
                   jax.ShapeDtypeStruct((TOP_K, t), I32), jax.ShapeDtypeStruct(cnt0.shape, F32)],
        compiler_params=_cparams(),
        name="post_mixer",
    )(acta, actb, sga, sgb, x2d, *mods, *consts)


def _dest_kernel(start_ref, eidx_ref, rank_ref, o_ref, *, width):
    step = SUBLANES * LANES
    for c0 in range(0, width, step):
        c1 = min(c0 + step, width)
        e = eidx_ref[:, c0:c1]
        base = lax.fori_loop(0, N_EXPERTS, lambda x, acc: jnp.where(e == x, start_ref[x], acc),
                             jnp.zeros(e.shape, I32), unroll=8)
        o_ref[:, c0:c1] = base + rank_ref[:, c0:c1]


def _dest(cnt_start, e_idx, rank):
    k, t = e_idx.shape
    full = pl.BlockSpec((k, t), lambda i, st: (0, 0))
    return pl.pallas_call(
        functools.partial(_dest_kernel, width=t),
        grid_spec=pltpu.PrefetchScalarGridSpec(
            num_scalar_prefetch=1, grid=(1,), in_specs=[full, full], out_specs=full),
        out_shape=jax.ShapeDtypeStruct((k, t), I32),
        compiler_params=_cparams(),
        name="moe_dest",
    )(cnt_start, e_idx, rank)


def _sc_mesh():
    return plsc.VectorSubcoreMesh(core_axis_name="core", subcore_axis_name="subcore")


def _sc_pipeline(body, n_steps, in_specs, out_specs):
    return pltpu.emit_pipeline(body, grid=(n_steps,), in_specs=in_specs, out_specs=out_specs,
                               core_axis_name=("core", "subcore"), dimension_semantics=(pltpu.PARALLEL,))


def _sc_scatter_rows(x, half, dest_flat, n_rows):
    t = x.shape[0]
    tiles = t // SC_WINDOW
    n_steps = dest_flat.shape[1] // SC_WINDOW

    @functools.partial(pl.kernel, mesh=_sc_mesh(), scratch_types=[],
                       out_type=jax.ShapeDtypeStruct((n_rows, QUARTER_WORDS), x.dtype))
    def scatter(x_hbm, i_hbm, o_hbm):
        def body(x_vmem, i_vmem):
            pltpu.sync_copy(x_vmem, o_hbm.at[i_vmem.at[0]])

        _sc_pipeline(body, n_steps,
                     [pl.BlockSpec((SC_WINDOW, QUARTER_WORDS), lambda i: (i % tiles, half)),
                      pl.BlockSpec((1, SC_WINDOW), lambda i: (0, i))], [])(x_hbm, i_hbm)

    return scatter(x, dest_flat)


def _sc_gather_rows(table, idx_flat):
    n = idx_flat.shape[1]

    @functools.partial(pl.kernel, mesh=_sc_mesh(), scratch_types=[],
                       out_type=jax.ShapeDtypeStruct((n, table.shape[1]), table.dtype))
    def gather(t_hbm, i_hbm, o_hbm):
        def body(i_vmem, o_vmem):
            pltpu.sync_copy(t_hbm.at[i_vmem.at[0]], o_vmem)

        _sc_pipeline(body, n // SC_WINDOW,
                     [pl.BlockSpec((1, SC_WINDOW), lambda i: (0, i))],
                     [pl.BlockSpec((SC_WINDOW, table.shape[1]), lambda i: (i, 0))])(i_hbm, o_hbm)

    return gather(table, idx_flat)


def _experts_kernel(sblk_ref, sexp_ref, slo_ref, shi_ref, snext_ref, first_ref,
                    xa_ref, xb_ref, wg_hbm, wu_hbm, wd_hbm, ya_ref, yb_ref,
                    wg_buf, wu_buf, wd_buf, wgu_s, wd_s, state_ref, sems, *, bm):
    p = pl.program_id(0)
    lo = slo_ref[p]
    expert = sexp_ref[p]

    def weight_copies(e, slot):
        return [pltpu.make_async_copy(src.at[e], dst.at[slot], sems.at[slot])
                for src, dst in ((wg_hbm, wg_buf), (wu_hbm, wu_buf), (wd_hbm, wd_buf))]

    @pl.when(p == 0)
    def _():
        state_ref[0] = -1
        state_ref[1] = 0
        for c in weight_copies(first_ref[0], 0):
            c.start()

    @pl.when(shi_ref[p] > lo)
    def _():
        @pl.when(state_ref[0] != expert)
        def _():
            slot = state_ref[1]
            for c in weight_copies(expert, slot):
                c.wait()
            wgu_s[:, :D_EXPERT] = wg_buf[slot].astype(BF16)
            wgu_s[:, D_EXPERT:] = wu_buf[slot].astype(BF16)
            wd_s[...] = wd_buf[slot].astype(BF16)
            state_ref[0] = expert
            state_ref[1] = 1 - slot
            upcoming = snext_ref[p]

            @pl.when(upcoming >= 0)
            def _():
                for c in weight_copies(upcoming, 1 - slot):
                    c.start()

        q = QUARTER_WORDS
        quarters = {}
        quarters[0], quarters[2] = _unpack_bf16_pairs(xa_ref[...])
        quarters[1], quarters[3] = _unpack_bf16_pairs(xb_ref[...])
        gu = functools.reduce(jnp.add, [
            jnp.dot(quarters[c], wgu_s[c * q:(c + 1) * q, :], preferred_element_type=F32) for c in range(4)])
        hb = (_silu(gu[:, :D_EXPERT]) * gu[:, D_EXPERT:]).astype(BF16)
        y = _pack_bf16_pairs(jnp.dot(hb, wd_s[...], preferred_element_type=F32))

        @pl.when(lo == 0)
        def _():
            ya_ref[...] = y[:, :q]
            yb_ref[...] = y[:, q:]

        @pl.when(lo > 0)
        def _():
            keep = lax.broadcasted_iota(I32, (bm, 1), 0) >= lo
            ya_ref[...] = jnp.where(keep, y[:, :q], ya_ref[...])
            yb_ref[...] = jnp.where(keep, y[:, q:], yb_ref[...])


def _experts(xs_a, xs_b, w_gate, w_up, w_down, seg_blk, seg_exp, seg_lo, seg_hi, seg_next, first_exp, *, bm):
    n_rows = xs_a.shape[0]
    n_seg = seg_blk.shape[0]
    ring = 2
    hbm = pl.BlockSpec(memory_space=pl.ANY)
    rows = pl.BlockSpec((bm, QUARTER_WORDS), lambda p, sb, *_: (sb[p], 0))
    grid_spec = pltpu.PrefetchScalarGridSpec(
        num_scalar_prefetch=6,
        grid=(n_seg,),
        in_specs=[rows, rows, hbm, hbm, hbm],
        out_specs=[rows, rows],
        scratch_shapes=[pltpu.VMEM((ring, D_MODEL, D_EXPERT), F32), pltpu.VMEM((ring, D_MODEL, D_EXPERT), F32),
                        pltpu.VMEM((ring, D_EXPERT, D_MODEL), F32),
                        pltpu.VMEM((D_MODEL, 2 * D_EXPERT), BF16), pltpu.VMEM((D_EXPERT, D_MODEL), BF16),
                        pltpu.SMEM((2,), I32), pltpu.SemaphoreType.DMA((ring,))],
    )
    return pl.pallas_call(
        functools.partial(_experts_kernel, bm=bm),
        grid_spec=grid_spec,
        out_shape=[jax.ShapeDtypeStruct((n_rows, QUARTER_WORDS), U32)] * 2,
        compiler_params=_cparams(),
        name="moe_experts",
    )(seg_blk, seg_exp, seg_lo, seg_hi, seg_next, first_exp, xs_a, xs_b, w_gate, w_up, w_down)


def _final_kernel(*refs, tm, n_prompt_tiles, has_sample, has_prev):
    refs = list(refs)
    ew_ref, pa_ref, pb_ref, xsp_ref = refs[:4]
    del refs[:4]
    xss_ref = refs.pop(0) if has_sample else None
    g2p_ref = refs.pop(0)
    g2s_ref = refs.pop(0) if has_sample else None
    gfin_ref = refs.pop(0)
    if has_prev:
        refs.pop(0)
    op_ref = refs.pop(0)
    os_ref = refs.pop(0) if has_sample else None
    ew = ew_ref[...]
    quarters = [jnp.zeros((tm, QUARTER_WORDS), F32)] * 4
    for k in range(TOP_K):
        w_k = ew[:, k:k + 1]
        a_low, a_high = _unpack_bf16_pairs(pa_ref[k])
        b_low, b_high = _unpack_bf16_pairs(pb_ref[k])
        for c, part in enumerate((a_low, b_low, a_high, b_high)):
            quarters[c] = quarters[c] + w_k * part.astype(F32)
    routed = jnp.concatenate(quarters, axis=1)

    def finish(xs_ref, gate2_ref, o_ref):
        x2 = xs_ref[...] + gate2_ref[...] * routed
        o_ref[...] = x2 * lax.rsqrt(jnp.mean(x2 * x2, axis=-1, keepdims=True) + EPS) * gfin_ref[...]

    if not has_sample:
        finish(xsp_ref, g2p_ref, op_ref)
    else:
        is_prompt = pl.program_id(0) < n_prompt_tiles
        pl.when(is_prompt)(lambda: finish(xsp_ref, g2p_ref, op_ref))
        pl.when(jnp.logical_not(is_prompt))(lambda: finish(xss_ref, g2s_ref, os_ref))


def _final(ew_part, planes_a, planes_b, xs_p, gate2_p, g_final, *, tm, tiles_per_batch, tile_lo, n_prompt_tiles,
           sample=None, y_prev=None):
    prompt_tile = lambda i: tile_lo + jnp.minimum(i, n_prompt_tiles - 1)
    sample_tile = lambda i: jnp.maximum(i - n_prompt_tiles, 0)
    prompt_rows = pl.BlockSpec((tm, D_MODEL), lambda i: (prompt_tile(i), 0))
    sample_rows = pl.BlockSpec((tm, D_MODEL), lambda i: (sample_tile(i), 0))
    planes = pl.BlockSpec((TOP_K, tm, QUARTER_WORDS), lambda i: (0, i, 0))
    has_sample = sample is not None
    has_prev = y_prev is not None
    n_tiles = n_prompt_tiles + (sample[0].shape[0] // tm if has_sample else 0)
    args = [ew_part, planes_a, planes_b, xs_p]
    in_specs = [pl.BlockSpec((tm, TOP_K), lambda i: (i, 0)), planes, planes, prompt_rows]
    if has_sample:
        args.append(sample[0])
        in_specs.append(sample_rows)
    args.append(gate2_p)
    in_specs.append(pl.BlockSpec((None, 1, D_MODEL), lambda i: (prompt_tile(i) // tiles_per_batch, 0, 0)))
    if has_sample:
        args.append(sample[1])
        in_specs.append(pl.BlockSpec((None, tm, D_MODEL), lambda i: (sample_tile(i), 0, 0)))
    args.append(g_final)
    in_specs.append(_const_spec(g_final))
    if has_prev:
        args.append(y_prev)
        in_specs.append(pl.BlockSpec(memory_space=pl.ANY))
    out_specs = [prompt_rows] + ([sample_rows] if has_sample else [])
    out_shape = [jax.ShapeDtypeStruct(xs_p.shape, F32)] + (
        [jax.ShapeDtypeStruct(sample[0].shape, F32)] if has_sample else [])
    outs = pl.pallas_call(
        functools.partial(_final_kernel, tm=tm, n_prompt_tiles=n_prompt_tiles, has_sample=has_sample,
                          has_prev=has_prev),
        grid=(n_tiles,),
        in_specs=in_specs,
        out_specs=out_specs,
        out_shape=out_shape,
        input_output_aliases={len(args) - 1: 0} if has_prev else {},
        compiler_params=_cparams(),
        name="moe_combine_final",
    )(*args)
    return outs if has_sample else outs[0]


def _pick_tile(n, preferred):
    t = min(n, preferred)
    assert n % t == 0, (n, t)
    return t


def kernel(x_prompt, x_sample, c_prompt, c_sample, state_conv, state_short_conv, state_delta, w_ada, b_ada, g_mix, w_in, conv_dw, conv_dw_b, conv_ln_g, conv_ln_b, w_pa, short_conv_w, a_log, dt_bias, dn_norm_g, w_pb, w_o, g_ffn, w_router, e_bias, w_gate, w_up, w_down, ws_gate, ws_up, ws_down, g_final):
    nb, seq, d = x_prompt.shape
    ns = x_sample.shape[0]
    assert d == D_MODEL and w_ada.shape[0] == 1 and x_sample.shape[1] == 1
    assert seq % CHUNK == 0 and seq >= CONV_HALO and ns % SUBLANES == 0
    tp = nb * seq
    row1 = lambda a: a.reshape(1, -1)

    wi = w_in[0].astype(BF16)
    o_q = 2 * CONV_CH
    o_z = o_q + 3 * DN_WIDTH
    o_a = o_z + DN_WIDTH
    o_ga = o_a + 2 * N_HEADS
    w_proj = {
        "glu": wi[:, :o_q], "qkv": wi[:, o_q:o_z], "z": wi[:, o_z:o_a],
        "ab": jnp.pad(wi[:, o_a:o_ga], ((0, 0), (0, LANES - 2 * N_HEADS))),
        "ga": wi[:, o_ga:o_ga + D_MODEL], "gb": wi[:, o_ga + D_MODEL:],
    }
    nea = jnp.pad(-jnp.exp(a_log[0].astype(F32)), (0, LANES - N_HEADS)).reshape(1, LANES)
    dtb = jnp.pad(dt_bias[0].astype(F32), (0, LANES - N_HEADS)).reshape(1, LANES)
    w_post = {
        "pa": w_pa[0].astype(BF16), "pb": w_pb[0].astype(BF16), "o": w_o[0].astype(BF16),
        "router_t": w_router[0].T.astype(BF16),
        "s_gu": jnp.concatenate([ws_gate[0], ws_up[0]], axis=1).astype(BF16),
        "s_down": ws_down[0].astype(BF16),
    }
    ebias_col = e_bias[0].astype(F32).reshape(N_EXPERTS, 1)
    g_mix1, g_ffn1, g_fin1 = row1(g_mix[0]), row1(g_ffn[0]), row1(g_final)
    cdw, cdb, clg, clb = conv_dw[0], row1(conv_dw_b[0]), row1(conv_ln_g[0]), row1(conv_ln_b[0])
    scw, dng = short_conv_w[0], row1(dn_norm_g[0])

    ada = _ada(jnp.concatenate([c_prompt, c_sample], axis=0), w_ada[0].astype(BF16), b_ada[0])
    ada = ada.reshape(nb + ns, 6, D_MODEL)
    mod_p = [ada[:nb, m].reshape(nb, 1, D_MODEL) for m in range(6)]
    mod_s = [ada[nb:, m].reshape(1, ns, D_MODEL) for m in range(6)]
    shift1, scale1, gate1, shift2, scale2, gate2 = range(6)

    tm_p = _pick_tile(seq, 256)
    tpm_p = seq // tm_p
    xp = x_prompt.reshape(tp, D_MODEL)
    u_p, qkv_p, zs_p, ab_p, sga_p, sgb_p = _in_proj(
        xp, mod_p[scale1], mod_p[shift1], g_mix1, w_proj, nea, dtb, tm=tm_p, tiles_per_mod=tpm_p)
    tl = _pick_tile(seq, 256)
    acta_p = _conv_branch(u_p, jnp.zeros((nb, CONV_HALO, CONV_CH), F32), cdw, cdb, clg, clb,
                          tl=tl, tiles_per_batch=seq // tl)
    tc = _pick_tile(seq, 4 * CHUNK)
    actb_p, s_new_p = _delta_branch(
        qkv_p, jnp.zeros((nb, SC_HALO, 3 * DN_WIDTH), F32),
        jnp.zeros((nb, N_HEADS, HEAD_DIM, HEAD_DIM), F32), scw, ab_p, zs_p, dng,
        tc=tc, tiles_per_batch=seq // tc)

    xs_ = x_sample.reshape(ns, D_MODEL)
    u_s, qkv_s, zs_s, ab_s, sga_s, sgb_s = _in_proj(
        xs_, mod_s[scale1], mod_s[shift1], g_mix1, w_proj, nea, dtb, tm=ns, tiles_per_mod=1)
    bt = SUBLANES
    acta_s, nch_s, qkvp_s, nsh_s = _step_front(
        u_s, qkv_s, state_conv[0].reshape(ns, -1), state_short_conv[0].reshape(ns, -1),
        cdw, cdb, clg, clb, scw, bt=bt)
    actb_s, s_new_s = _step_delta(qkvp_s, ab_s, zs_s, state_delta[0].astype(F32), dng, bt=bt)

    cnt0 = jnp.zeros((N_EXPERTS, LANES), F32)
    xsh_p, h2_p, eidx_p, ew_p, rank_p, cnt_p = _post(
        acta_p, actb_p, sga_p, sgb_p, xp, mod_p[gate1], mod_p[scale2], mod_p[shift2], mod_p[gate2],
        cnt0, g_ffn1, w_post, ebias_col, tm=tm_p, tiles_per_mod=tpm_p)
    xsh_s, h2_s, eidx_s, ew_s, rank_s, cnt_all = _post(
        acta_s, actb_s, sga_s, sgb_s, xs_, mod_s[gate1], mod_s[scale2], mod_s[shift2], mod_s[gate2],
        cnt_p, g_ffn1, w_post, ebias_col, tm=ns, tiles_per_mod=1)

    bm = 512
    tmd = LANES
    t_all = tp + ns
    n_rows = t_all * TOP_K
    assert seq % tmd == 0 and ns % tmd == 0 and n_rows % bm == 0 and t_all % SC_WINDOW == 0
    n_blocks = n_rows // bm
    counts = cnt_all[:, 0].astype(I32)
    cnt_end = jnp.cumsum(counts)
    cnt_start = cnt_end - counts
    e_idx = jnp.concatenate([eidx_p, eidx_s], axis=1)
    rank = jnp.concatenate([rank_p, rank_s], axis=1)
    dest = _dest(cnt_start, e_idx, rank)
    dest_flat = dest.reshape(1, n_rows)
    seg_start = jnp.sort(jnp.concatenate([jnp.arange(n_blocks, dtype=I32) * bm, cnt_start]))
    seg_end = jnp.concatenate([seg_start[1:], jnp.full((1,), n_rows, I32)])
    seg_blk = jnp.minimum(seg_start // bm, n_blocks - 1)
    seg_lo = seg_start - seg_blk * bm
    seg_hi = seg_end - seg_blk * bm
    seg_exp = jnp.sum((cnt_end[None, :] <= seg_start[:, None]).astype(I32), axis=1)
    seg_exp = jnp.minimum(seg_exp, N_EXPERTS - 1)
    seg_exp = lax.cummax(jnp.where(seg_hi > seg_lo, seg_exp, 0))
    none = jnp.int32(N_EXPERTS)
    in_use_from = lax.cummin(jnp.where(counts > 0, jnp.arange(N_EXPERTS, dtype=I32), none), reverse=True)
    next_in_use = jnp.concatenate([in_use_from[1:], jnp.full((1,), none, I32)])
    next_in_use = jnp.where(next_in_use == none, -1, next_in_use)
    seg_next = next_in_use[seg_exp]
    first_exp = jnp.minimum(in_use_from[:1], N_EXPERTS - 1)

    h2_all = jnp.concatenate([h2_p, h2_s], axis=0)
    xs_a, xs_b = (_sc_scatter_rows(h2_all, half, dest_flat, n_rows) for half in range(2))
    ys_a, ys_b = _experts(xs_a, xs_b, w_gate[0], w_up[0], w_down[0], seg_blk, seg_exp, seg_lo, seg_hi, seg_next,
                          first_exp, bm=bm)

    ew_tok = jnp.concatenate([ew_p, ew_s], axis=1).T
    n_prompt_tiles = tp // tmd
    tiles_1 = n_prompt_tiles // 2
    t_1 = tiles_1 * tmd

    def gathered(lo, hi):
        idx = dest[:, lo:hi].reshape(1, -1)
        return [_sc_gather_rows(ys, idx).reshape(TOP_K, hi - lo, QUARTER_WORDS) for ys in (ys_a, ys_b)]

    final = functools.partial(_final, xs_p=xsh_p, gate2_p=mod_p[gate2], g_final=g_fin1, tm=tmd,
                              tiles_per_batch=seq // tmd)
    y_p = final(ew_tok[:t_1], *gathered(0, t_1), tile_lo=0, n_prompt_tiles=tiles_1)
    y_p, y_s = final(ew_tok[t_1:], *gathered(t_1, t_all), tile_lo=tiles_1, n_prompt_tiles=n_prompt_tiles - tiles_1,
                     sample=(xsh_s, mod_s[gate2].reshape(ns // tmd, tmd, D_MODEL)), y_prev=y_p)

    hist_rows = CONV_WIDTH - 1
    sc_rows = SHORT_CONV - 1
    y_prompt = y_p.reshape(nb, seq, D_MODEL)
    y_sample = y_s.reshape(ns, 1, D_MODEL)
    new_conv_prompt = u_p.reshape(nb, seq, CONV_CH)[:, seq - hist_rows:][None]
    new_sc_prompt = qkv_p.reshape(nb, seq, 3 * DN_WIDTH)[:, seq - sc_rows:][None]
    new_conv_sample = nch_s.reshape(ns, hist_rows, CONV_CH)[None]
    new_sc_sample = nsh_s.reshape(ns, sc_rows, 3 * DN_WIDTH)[None]
    return (y_prompt, y_sample, new_conv_prompt, new_sc_prompt, s_new_p[None],
            new_conv_sample, new_sc_sample, s_new_s[None])
```

```python
import functools

import jax
import jax.numpy as jnp
from jax import lax
from jax.experimental import pallas as pl
from jax.experimental.pallas import tpu as pltpu
from jax.experimental.pallas import tpu_sc as plsc

F32 = jnp.float32
BF16 = jnp.bfloat16
I32 = jnp.int32
U32 = jnp.uint32

EPS = 1e-6
D_MODEL = 1024
CONV_CH = 512
CONV_WIDTH = 31
N_HEADS = 4
HEAD_DIM = 128
DN_WIDTH = N_HEADS * HEAD_DIM
SHORT_CONV = 4
CHUNK = 64
CHUNK_SHIFT = CHUNK.bit_length() - 1
assert 1 << CHUNK_SHIFT == CHUNK
SUB = 16
SUB_SHIFT = SUB.bit_length() - 1
assert 1 << SUB_SHIFT == SUB and CHUNK % SUB == 0
N_EXPERTS = 256
TOP_K = 8
N_GROUPS = 8
GROUP_SIZE = N_EXPERTS // N_GROUPS
TOPK_GROUPS = 4
D_EXPERT = 256
D_SHARED = 256
ROUTE_SCALE = 2.5

LANES = 128
SUBLANES = 8
CONV_HALO = 32
SC_HALO = 8
VMEM_LIMIT = 48 * 1024 * 1024
SC_WINDOW = 128
QUARTER_WORDS = D_MODEL // 4

NT_DIMS = (((1,), (1,)), ((), ()))
TN_DIMS = (((0,), (0,)), ((), ()))


def _cparams(n_grid_dims=1):
    return pltpu.CompilerParams(
        dimension_semantics=("arbitrary",) * n_grid_dims,
        vmem_limit_bytes=VMEM_LIMIT)


def _sigmoid(x):
    return jax.nn.sigmoid(x)


def _silu(x):
    return x * jax.nn.sigmoid(x)


def _bdot(a, b):
    return jnp.dot(a.astype(BF16), b.astype(BF16), preferred_element_type=F32)


def _pack_bf16_pairs(x):
    half = x.shape[1] // 2
    bits = lax.bitcast_convert_type(x.astype(BF16).astype(F32), U32)
    return bits[:, half:] | (bits[:, :half] >> 16)


def _unpack_bf16_pairs(w):
    low = lax.bitcast_convert_type(w << 16, F32).astype(BF16)
    high = lax.bitcast_convert_type(w & jnp.uint32(0xFFFF0000), F32).astype(BF16)
    return low, high


def _dot3(a, b):
    a_hi = a.astype(BF16)
    b_hi = b.astype(BF16)
    a_lo = (a - a_hi.astype(F32)).astype(BF16)
    b_lo = (b - b_hi.astype(F32)).astype(BF16)
    dot = lambda x, y: jnp.dot(x, y, preferred_element_type=F32)
    return dot(a_hi, b_hi) + (dot(a_hi, b_lo) + dot(a_lo, b_hi))


def _dot_exact_lhs(a_bf, b):
    hi = b.astype(BF16)
    r1 = b - hi.astype(F32)
    mid = r1.astype(BF16)
    low = (r1 - mid.astype(F32)).astype(BF16)
    dot = lambda p: jnp.dot(a_bf, p, preferred_element_type=F32)
    return dot(hi) + dot(mid) + dot(low)


def _ada_kernel(c_ref, w_ref, b_ref, o_ref):
    o_ref[...] = _bdot(_silu(c_ref[...]), w_ref[...]) + b_ref[...]


def _ada(c_all, w_ada_bf, b_ada):
    n = c_all.shape[0]
    width = w_ada_bf.shape[1]
    tn = 1536
    return pl.pallas_call(
        _ada_kernel,
        grid=(width // tn,),
        in_specs=[pl.BlockSpec((n, D_MODEL), lambda j: (0, 0)),
                  pl.BlockSpec((D_MODEL, tn), lambda j: (0, j)),
                  pl.BlockSpec((1, tn), lambda j: (0, j))],
        out_specs=pl.BlockSpec((n, tn), lambda j: (0, j)),
        out_shape=jax.ShapeDtypeStruct((n, width), F32),
        compiler_params=_cparams(),
        name="ada",
    )(c_all, w_ada_bf, b_ada.reshape(1, width))


def _in_proj_kernel(x_ref, scale_ref, shift_ref, g_ref, wglu_ref, wqkv_ref, wz_ref, wab_ref,
                    wga_ref, wgb_ref, nea_ref, dtb_ref,
                    u_ref, qkv_ref, zs_ref, ab_ref, sga_ref, sgb_ref):
    x = x_ref[...]
    y = x * lax.rsqrt(jnp.mean(x * x, axis=-1, keepdims=True) + EPS) * g_ref[...]
    h = (y * (1.0 + scale_ref[...]) + shift_ref[...]).astype(BF16)
    glu = jnp.dot(h, wglu_ref[...], preferred_element_type=F32)
    u_ref[...] = glu[:, :CONV_CH] * _sigmoid(glu[:, CONV_CH:])
    qkv_ref[...] = jnp.dot(h, wqkv_ref[...], preferred_element_type=F32)
    zs_ref[...] = _silu(jnp.dot(h, wz_ref[...], preferred_element_type=F32))
    ab = jnp.dot(h, wab_ref[...], preferred_element_type=F32)
    sp = ab + dtb_ref[...]
    softplus = jnp.maximum(sp, 0.0) + jnp.log1p(jnp.exp(-jnp.abs(sp)))
    lane = lax.broadcasted_iota(I32, ab.shape, 1)
    ab_ref[...] = jnp.where(lane < N_HEADS, nea_ref[...] * softplus, _sigmoid(ab))
    sga_ref[...] = _sigmoid(jnp.dot(h, wga_ref[...], preferred_element_type=F32))
    sgb_ref[...] = _sigmoid(jnp.dot(h, wgb_ref[...], preferred_element_type=F32))


def _mod_spec(mod, tiles_per_mod):
    return pl.BlockSpec((None,) + mod.shape[1:], lambda i: (i // tiles_per_mod, 0, 0))


def _const_spec(a):
    nd = a.ndim
    return pl.BlockSpec(a.shape, lambda i: (0,) * nd)


def _in_proj(x2d, scale, shift, g_mix, w, nea, dtb, *, tm, tiles_per_mod):
    t = x2d.shape[0]
    row = lambda c: pl.BlockSpec((tm, c), lambda i: (i, 0))
    consts = [g_mix, w["glu"], w["qkv"], w["z"], w["ab"], w["ga"], w["gb"], nea, dtb]
    widths = [CONV_CH, 3 * DN_WIDTH, DN_WIDTH, LANES, D_MODEL, D_MODEL]
    return pl.pallas_call(
        _in_proj_kernel,
        grid=(t // tm,),
        in_specs=[row(D_MODEL), _mod_spec(scale, tiles_per_mod), _mod_spec(shift, tiles_per_mod)]
                 + [_const_spec(a) for a in consts],
        out_specs=[row(c) for c in widths],
        out_shape=[jax.ShapeDtypeStruct((t, c), F32) for c in widths],
        compiler_params=_cparams(),
        name="in_proj",
    )(x2d, scale, shift, *consts)


def _layer_norm_swish(y, g, b):
    mu = jnp.mean(y, axis=-1, keepdims=True)
    yc = y - mu
    var = jnp.mean(yc * yc, axis=-1, keepdims=True)
    return _silu(yc * lax.rsqrt(var + EPS) * g + b)


def _conv_kernel(u_ref, halo_ref, hist_ref, w_ref, b_ref, lg_ref, lb_ref, o_ref, win_ref, shift_ref,
                 *, tl, tiles_per_batch, rows_per_pass):
    i = pl.program_id(0)
    first = (i % tiles_per_batch) == 0

    @pl.when(first)
    def _():
        win_ref[0:CONV_HALO, :] = hist_ref[...]

    @pl.when(jnp.logical_not(first))
    def _():
        win_ref[0:CONV_HALO, :] = halo_ref[...]

    win_ref[CONV_HALO:CONV_HALO + tl, :] = u_ref[...]
    tap0 = CONV_HALO - (CONV_WIDTH - 1)
    n_shifted = CONV_HALO + tl - SUBLANES
    for b in range(1, SUBLANES):
        shift_ref[b - 1] = win_ref[b:b + n_shifted, :]

    for r in range(tl // rows_per_pass):
        base = r * rows_per_pass
        acc = jnp.zeros((rows_per_pass, CONV_CH), F32)
        for j in range(CONV_WIDTH):
            a, b = divmod(tap0 + j, SUBLANES)
            start = base + a * SUBLANES
            rows = (win_ref[start:start + rows_per_pass, :] if b == 0
                    else shift_ref[b - 1, start:start + rows_per_pass, :])
            acc = acc + w_ref[j:j + 1, :] * rows
        act = _layer_norm_swish(acc + b_ref[...], lg_ref[...], lb_ref[...])
        o_ref[base:base + rows_per_pass, :] = act.astype(o_ref.dtype)


def _conv_branch(u2d, hist, conv_dw, conv_dw_b, ln_g, ln_b, *, tl, tiles_per_batch):
    t = u2d.shape[0]
    halo_blocks = tl // CONV_HALO
    consts = [conv_dw, conv_dw_b, ln_g, ln_b]
    return pl.pallas_call(
        functools.partial(_conv_kernel, tl=tl, tiles_per_batch=tiles_per_batch, rows_per_pass=64),
        grid=(t // tl,),
        in_specs=[pl.BlockSpec((tl, CONV_CH), lambda i: (i, 0)),
                  pl.BlockSpec((CONV_HALO, CONV_CH), lambda i: (jnp.maximum(i * halo_blocks - 1, 0), 0)),
                  pl.BlockSpec((None, CONV_HALO, CONV_CH), lambda i: (i // tiles_per_batch, 0, 0))]
                 + [_const_spec(a) for a in consts],
        out_specs=pl.BlockSpec((tl, CONV_CH), lambda i: (i, 0)),
        out_shape=jax.ShapeDtypeStruct((t, CONV_CH), BF16),
        scratch_shapes=[pltpu.VMEM((CONV_HALO + tl, CONV_CH), F32),
                        pltpu.VMEM((SUBLANES - 1, CONV_HALO + tl - SUBLANES, CONV_CH), F32)],
        compiler_params=_cparams(),
        name="conv_branch",
    )(u2d, u2d, hist, *consts)


def _l2n(x):
    return x * lax.rsqrt(jnp.sum(x * x, axis=-1, keepdims=True) + 1e-6)


def _gated_head_norm(o, g, zs):
    return o * lax.rsqrt(jnp.mean(o * o, axis=-1, keepdims=True) + EPS) * g * zs


def _delta_kernel(qkv_ref, halo_ref, hist_ref, scw_ref, ab_ref, zs_ref, s0_ref, dng_ref,
                  o_ref, s_ref, win_ref, c_ref, *, tc, tiles_per_batch):
    i = pl.program_id(0)
    first = (i % tiles_per_batch) == 0

    @pl.when(first)
    def _():
        win_ref[0:SC_HALO, :] = hist_ref[...]
        s_ref[...] = s0_ref[...]

    @pl.when(jnp.logical_not(first))
    def _():
        win_ref[0:SC_HALO, :] = halo_ref[...]

    win_ref[SC_HALO:SC_HALO + tc, :] = qkv_ref[...]
    tap0 = SC_HALO - (SHORT_CONV - 1)
    acc = jnp.zeros((tc, 3 * DN_WIDTH), F32)
    for j in range(SHORT_CONV):
        acc = acc + scw_ref[j:j + 1, :] * win_ref[tap0 + j:tap0 + j + tc, :]
    c_ref[...] = _silu(acc)

    row = lax.broadcasted_iota(I32, (tc, tc), 0)
    col = lax.broadcasted_iota(I32, (tc, tc), 1)
    same_chunk = lax.shift_right_logical(row, CHUNK_SHIFT) == lax.shift_right_logical(col, CHUNK_SHIFT)
    incl = jnp.logical_and(same_chunk, row >= col)
    row_sub = lax.shift_right_logical(row, SUB_SHIFT)
    col_sub = lax.shift_right_logical(col, SUB_SHIFT)
    strict_sub = jnp.logical_and(row_sub == col_sub, row > col)
    below_sub = jnp.logical_and(same_chunk, row_sub > col_sub)
    eye = (row == col).astype(F32)
    ab = ab_ref[...]
    gcum = _dot_exact_lhs(incl.astype(BF16), ab)
    gcum_t = gcum.T
    n_ch = tc // CHUNK

    heads = range(N_HEADS)
    q, k, gc, kk, a_qk, rhs, qd = [], [], [], [], [], [], []
    for h in heads:
        lo = h * HEAD_DIM
        q_h = _l2n(c_ref[:, lo:lo + HEAD_DIM]) * (HEAD_DIM ** -0.5)
        k_h = _l2n(c_ref[:, DN_WIDTH + lo:DN_WIDTH + lo + HEAD_DIM])
        v_h = c_ref[:, 2 * DN_WIDTH + lo:2 * DN_WIDTH + lo + HEAD_DIM]
        beta = ab[:, N_HEADS + h:N_HEADS + h + 1]
        gc_h = gcum[:, h:h + 1]
        gr = gcum_t[h:h + 1, :]
        decay = jnp.exp(jnp.where(incl, gc_h - gr, -jnp.inf))
        kb = k_h * beta
        kq = lax.dot_general(jnp.concatenate([kb, q_h], axis=0).astype(BF16), k_h.astype(BF16),
                             NT_DIMS, preferred_element_type=F32)
        e_gc = jnp.exp(gc_h)
        q.append(q_h)
        k.append(k_h)
        gc.append(gc_h)
        kk.append(kq[:tc] * decay)
        a_qk.append(kq[tc:] * decay)
        rhs.append(jnp.concatenate([v_h * beta, kb * e_gc], axis=1))
        qd.append(q_h * e_gc)

    pw = [jnp.where(strict_sub, kk[h], 0.0) for h in heads]
    d_inv = [eye - pw[h] for h in heads]
    for _ in range(SUB_SHIFT - 1):
        pw = [_bdot(pw[h], pw[h]) for h in heads]
        d_inv = [d_inv[h] + _bdot(d_inv[h], pw[h]) for h in heads]
    c_rhs = [_bdot(d_inv[h], rhs[h]) for h in heads]
    m_off = [_bdot(d_inv[h], jnp.where(below_sub, kk[h], 0.0)) for h in heads]
    sol = c_rhs
    for _ in range(CHUNK // SUB - 1):
        sol = [c_rhs[h] - _bdot(m_off[h], sol[h]) for h in heads]

    s = [s_ref[h] for h in heads]
    u_parts = [[] for _ in heads]
    o_parts = [[] for _ in heads]
    for c in range(n_ch):
        r0 = c * CHUNK
        for h in heads:
            u0_c = sol[h][r0:r0 + CHUNK, :HEAD_DIM]
            w_c = sol[h][r0:r0 + CHUNK, HEAD_DIM:]
            wq_s = _bdot(jnp.concatenate([w_c, qd[h][r0:r0 + CHUNK]], axis=0), s[h])
            u_c = u0_c - wq_s[:CHUNK]
            g_last = gcum[r0 + CHUNK - 1:r0 + CHUNK, h:h + 1]
            k_dec = k[h][r0:r0 + CHUNK] * jnp.exp(g_last - gc[h][r0:r0 + CHUNK])
            s[h] = jnp.exp(g_last) * s[h] + lax.dot_general(
                k_dec.astype(BF16), u_c.astype(BF16), TN_DIMS, preferred_element_type=F32)
            u_parts[h].append(u_c)
            o_parts[h].append(wq_s[CHUNK:])
    for h in heads:
        lo = h * HEAD_DIM
        s_ref[h] = s[h]
        o = jnp.concatenate(o_parts[h], axis=0) + _bdot(a_qk[h], jnp.concatenate(u_parts[h], axis=0))
        o_ref[:, lo:lo + HEAD_DIM] = _gated_head_norm(
            o, dng_ref[...], zs_ref[:, lo:lo + HEAD_DIM]).astype(o_ref.dtype)


def _delta_branch(qkv2d, hist, s0, scw, ab, zs, dng, *, tc, tiles_per_batch):
    t = qkv2d.shape[0]
    nb = s0.shape[0]
    halo_blocks = tc // SC_HALO
    wqkv = 3 * DN_WIDTH
    state_spec = pl.BlockSpec((None, N_HEADS, HEAD_DIM, HEAD_DIM), lambda i: (i // tiles_per_batch, 0, 0, 0))
    return pl.pallas_call(
        functools.partial(_delta_kernel, tc=tc, tiles_per_batch=tiles_per_batch),
        grid=(t // tc,),
        in_specs=[pl.BlockSpec((tc, wqkv), lambda i: (i, 0)),
                  pl.BlockSpec((SC_HALO, wqkv), lambda i: (jnp.maximum(i * halo_blocks - 1, 0), 0)),
                  pl.BlockSpec((None, SC_HALO, wqkv), lambda i: (i // tiles_per_batch, 0, 0)),
                  _const_spec(scw),
                  pl.BlockSpec((tc, LANES), lambda i: (i, 0)),
                  pl.BlockSpec((tc, DN_WIDTH), lambda i: (i, 0)),
                  state_spec,
                  _const_spec(dng)],
        out_specs=[pl.BlockSpec((tc, DN_WIDTH), lambda i: (i, 0)), state_spec],
        out_shape=[jax.ShapeDtypeStruct((t, DN_WIDTH), BF16),
                   jax.ShapeDtypeStruct((nb, N_HEADS, HEAD_DIM, HEAD_DIM), F32)],
        scratch_shapes=[pltpu.VMEM((SC_HALO + tc, wqkv), F32), pltpu.VMEM((tc, wqkv), F32)],
        compiler_params=_cparams(),
        name="delta_branch",
    )(qkv2d, qkv2d, hist, scw, ab, zs, s0, dng)


def _step_front_kernel(u_ref, qkv_ref, ch_ref, sh_ref, cw_ref, cb_ref, lg_ref, lb_ref, scw_ref,
                       acta_ref, nch_ref, qkvp_ref, nsh_ref):
    hist_rows = CONV_WIDTH - 1
    u = u_ref[...]
    acc = cw_ref[hist_rows:hist_rows + 1, :] * u
    for j in range(hist_rows):
        acc = acc + cw_ref[j:j + 1, :] * ch_ref[:, j * CONV_CH:(j + 1) * CONV_CH]
    acta_ref[...] = _layer_norm_swish(acc + cb_ref[...], lg_ref[...], lb_ref[...]).astype(acta_ref.dtype)
    nch_ref[:, :(hist_rows - 1) * CONV_CH] = ch_ref[:, CONV_CH:]
    nch_ref[:, (hist_rows - 1) * CONV_CH:] = u

    wq = 3 * DN_WIDTH
    sc_rows = SHORT_CONV - 1
    qkv = qkv_ref[...]
    acc = scw_ref[sc_rows:sc_rows + 1, :] * qkv
    for j in range(sc_rows):
        acc = acc + scw_ref[j:j + 1, :] * sh_ref[:, j * wq:(j + 1) * wq]
    c = _silu(acc)
    for h in range(N_HEADS):
        lo = h * HEAD_DIM
        qkvp_ref[:, lo:lo + HEAD_DIM] = _l2n(c[:, lo:lo + HEAD_DIM]) * (HEAD_DIM ** -0.5)
        qkvp_ref[:, DN_WIDTH + lo:DN_WIDTH + lo + HEAD_DIM] = _l2n(c[:, DN_WIDTH + lo:DN_WIDTH + lo + HEAD_DIM])
    qkvp_ref[:, 2 * DN_WIDTH:] = c[:, 2 * DN_WIDTH:]
    nsh_ref[:, :(sc_rows - 1) * wq] = sh_ref[:, wq:]
    nsh_ref[:, (sc_rows - 1) * wq:] = qkv


def _step_front(u, qkv, conv_hist_flat, sc_hist_flat, conv_dw, conv_dw_b, ln_g, ln_b, scw, *, bt):
    n = u.shape[0]
    wq = 3 * DN_WIDTH
    consts = [conv_dw, conv_dw_b, ln_g, ln_b, scw]
    row = lambda c: pl.BlockSpec((bt, c), lambda i: (i, 0))
    widths = [CONV_CH, conv_hist_flat.shape[1], wq, sc_hist_flat.shape[1]]
    dtypes = [BF16, F32, F32, F32]
    return pl.pallas_call(
        _step_front_kernel,
        grid=(n // bt,),
        in_specs=[row(CONV_CH), row(wq), row(widths[1]), row(widths[3])] + [_const_spec(a) for a in consts],
        out_specs=[row(c) for c in widths],
        out_shape=[jax.ShapeDtypeStruct((n, c), d) for c, d in zip(widths, dtypes)],
        compiler_params=_cparams(),
        name="step_front",
    )(u, qkv, conv_hist_flat, sc_hist_flat, *consts)


def _rows_to_cols(x):
    pad = jnp.zeros((LANES - x.shape[0], LANES), x.dtype)
    return jnp.concatenate([x, pad], axis=0).T


def _step_delta_kernel(qkvp_ref, ab_ref, zs_ref, s_ref, dng_ref, o_ref, sn_ref, *, bt):
    ab = ab_ref[...]
    for h in range(N_HEADS):
        lo = h * HEAD_DIM
        q = qkvp_ref[:, lo:lo + HEAD_DIM]
        k = qkvp_ref[:, DN_WIDTH + lo:DN_WIDTH + lo + HEAD_DIM]
        v = qkvp_ref[:, 2 * DN_WIDTH + lo:2 * DN_WIDTH + lo + HEAD_DIM]
        q_cols = _rows_to_cols(q)
        k_cols = _rows_to_cols(k)
        qk = jnp.sum(q * k, axis=-1, keepdims=True)
        alpha = jnp.exp(ab[:, h:h + 1])
        beta = ab[:, N_HEADS + h:N_HEADS + h + 1]
        seqs = range(bt)
        kc = [k_cols[:, j:j + 1] for j in seqs]
        a = [alpha[j:j + 1, :] for j in seqs]
        s_k = [jnp.sum(s_ref[j, h] * kc[j], axis=0, keepdims=True) for j in seqs]
        s_q = [jnp.sum(s_ref[j, h] * q_cols[:, j:j + 1], axis=0, keepdims=True) for j in seqs]
        u = [beta[j:j + 1, :] * (v[j:j + 1, :] - a[j] * s_k[j]) for j in seqs]
        for j in seqs:
            sn_ref[j, h] = a[j] * s_ref[j, h] + kc[j] * u[j]
        o = jnp.concatenate([a[j] * s_q[j] + qk[j:j + 1, :] * u[j] for j in seqs], axis=0)
        o_ref[:, lo:lo + HEAD_DIM] = _gated_head_norm(
            o, dng_ref[...], zs_ref[:, lo:lo + HEAD_DIM]).astype(o_ref.dtype)


def _step_delta(qkvp, ab, zs, s0, dng, *, bt):
    n = qkvp.shape[0]
    row = lambda c: pl.BlockSpec((bt, c), lambda i: (i, 0))
    state_spec = pl.BlockSpec((bt, N_HEADS, HEAD_DIM, HEAD_DIM), lambda i: (i, 0, 0, 0))
    return pl.pallas_call(
        functools.partial(_step_delta_kernel, bt=bt),
        grid=(n // bt,),
        in_specs=[row(3 * DN_WIDTH), row(LANES), row(DN_WIDTH), state_spec, _const_spec(dng)],
        out_specs=[row(DN_WIDTH), state_spec],
        out_shape=[jax.ShapeDtypeStruct((n, DN_WIDTH), BF16), jax.ShapeDtypeStruct(s0.shape, F32)],
        compiler_params=_cparams(),
        name="step_delta",
    )(qkvp, ab, zs, s0, dng)


def _post_kernel(acta_ref, actb_ref, sga_ref, sgb_ref, x_ref, gate1_ref, scale2_ref, shift2_ref,
                 gate2_ref, cnt0_ref, gffn_ref, wpa_ref, wpb_ref, wo_ref, wrt_ref, ebias_ref,
                 wsgu_ref, wsd_ref,
                 xs_ref, h2_ref, eidx_ref, ew_ref, rank_ref, cnt_ref, scores_s, *, tm):
    i = pl.program_id(0)

    @pl.when(i == 0)
    def _():
        cnt_ref[...] = cnt0_ref[...]
        scores_s[...] = jnp.zeros(scores_s.shape, F32)

    neg = -jnp.inf
    big = jnp.int32(1 << 30)

    def project():
        y_a = jnp.dot(acta_ref[...], wpa_ref[...], preferred_element_type=F32)
        yield
        y_b = jnp.dot(actb_ref[...], wpb_ref[...], preferred_element_type=F32)
        merged = sga_ref[...] * y_a + sgb_ref[...] * y_b
        yield
        mix = _bdot(merged, wo_ref[...])
        yield
        x1 = x_ref[...] + gate1_ref[...] * mix
        y = x1 * lax.rsqrt(jnp.mean(x1 * x1, axis=-1, keepdims=True) + EPS) * gffn_ref[...]
        h2 = y * (1.0 + scale2_ref[...]) + shift2_ref[...]
        h2_ref[...] = _pack_bf16_pairs(h2)
        h2b = h2.astype(BF16)
        yield
        gu = jnp.dot(h2b, wsgu_ref[...], preferred_element_type=F32)
        yield
        shared = _bdot(_silu(gu[:, :D_SHARED]) * gu[:, D_SHARED:], wsd_ref[...])
        xs_ref[...] = x1 + gate2_ref[...] * shared
        yield
        logits_t = lax.dot_general(wrt_ref[...], h2b, NT_DIMS, preferred_element_type=F32)
        scores_s[...] = _sigmoid(logits_t)

    def first_argmax(vals, rows):
        m = jnp.max(vals, axis=0, keepdims=True)
        return m, jnp.min(jnp.where(vals == m, rows, big), axis=0, keepdims=True)

    def route(scores):
        biased = scores + ebias_ref[...]
        erow = lax.broadcasted_iota(I32, (N_EXPERTS, tm), 0)
        group_scores = []
        for g in range(N_GROUPS):
            vals = biased[g * GROUP_SIZE:(g + 1) * GROUP_SIZE, :]
            rows = lax.broadcasted_iota(I32, (GROUP_SIZE, tm), 0) + g * GROUP_SIZE
            m1, i1 = first_argmax(vals, rows)
            m2 = jnp.max(jnp.where(rows == i1, neg, vals), axis=0, keepdims=True)
            group_scores.append(m1 + m2)
            yield
        group_sel = [jnp.zeros((1, tm), jnp.bool_)] * N_GROUPS
        for _ in range(TOPK_GROUPS):
            best = functools.reduce(jnp.maximum, group_scores)
            gi = functools.reduce(
                jnp.minimum, [jnp.where(group_scores[g] == best, jnp.int32(g), big) for g in range(N_GROUPS)])
            for g in range(N_GROUPS):
                hit = gi == g
                group_sel[g] = jnp.logical_or(group_sel[g], hit)
                group_scores[g] = jnp.where(hit, neg, group_scores[g])
        cand = jnp.concatenate(
            [jnp.where(group_sel[g], biased[g * GROUP_SIZE:(g + 1) * GROUP_SIZE, :], neg)
             for g in range(N_GROUPS)], axis=0)
        yield
        w_rows, hits = [], []
        picked = jnp.zeros((N_EXPERTS, tm), F32)
        for k in range(TOP_K):
            _, ei = first_argmax(cand, erow)
            hit = erow == ei
            eidx_ref[k:k + 1, :] = ei
            w_rows.append(jnp.sum(jnp.where(hit, scores, 0.0), axis=0, keepdims=True))
            hits.append(hit)
            picked = jnp.where(hit, 1.0, picked)
            cand = jnp.where(hit, neg, cand)
            yield
        w_sum = functools.reduce(jnp.add, w_rows)
        for k in range(TOP_K):
            ew_ref[k:k + 1, :] = w_rows[k] / w_sum * ROUTE_SCALE
        yield
        trow = lax.broadcasted_iota(I32, (tm, tm), 0)
        tcol = lax.broadcasted_iota(I32, (tm, tm), 1)
        before = (trow < tcol).astype(BF16)
        prefix = jnp.dot(picked.astype(BF16), before, preferred_element_type=F32) + cnt_ref[:, 0:1]
        for k in range(TOP_K):
            rank_ref[k:k + 1, :] = jnp.sum(jnp.where(hits[k], prefix, 0.0), axis=0, keepdims=True).astype(I32)
            yield
        counted = jnp.where(i > 0, jnp.sum(picked, axis=1, keepdims=True), 0.0)
        cnt_ref[...] = cnt_ref[...] + counted

    jobs = [route(scores_s[...]), project()]
    shares = [4, 1]
    while jobs:
        for job, share in list(zip(jobs, shares)):
            for _ in range(share):
                if next(job, "done") == "done":
                    k = jobs.index(job)
                    del jobs[k], shares[k]
                    break


def _post(acta, actb, sga, sgb, x2d, gate1, scale2, shift2, gate2, cnt0, g_ffn, w, ebias_col,
          *, tm, tiles_per_mod):
    t = x2d.shape[0]
    n_tiles = t // tm
    tile = lambda i: jnp.minimum(i, n_tiles - 1)
    routed_tile = lambda i: jnp.maximum(i - 1, 0)
    row = lambda c: pl.BlockSpec((tm, c), lambda i: (tile(i), 0))
    col = lambda r: pl.BlockSpec((r, tm), lambda i: (0, routed_tile(i)))
    mod = lambda m: pl.BlockSpec((None,) + m.shape[1:], lambda i: (tile(i) // tiles_per_mod, 0, 0))
    mods = [gate1, scale2, shift2, gate2]
    consts = [cnt0, g_ffn, w["pa"], w["pb"], w["o"], w["router_t"], ebias_col, w["s_gu"], w["s_down"]]
    return pl.pallas_call(
        functools.partial(_post_kernel, tm=tm),
        grid=(n_tiles + 1,),
        in_specs=[row(CONV_CH), row(DN_WIDTH), row(D_MODEL), row(D_MODEL), row(D_MODEL)]
                 + [mod(m) for m in mods] + [_const_spec(a) for a in consts],
        out_specs=[row(D_MODEL), row(D_MODEL // 2), col(TOP_K), col(TOP_K), col(TOP_K), _const_spec(cnt0)],
        scratch_shapes=[pltpu.VMEM((N_EXPERTS, tm), F32)],
        out_shape=[jax.ShapeDtypeStruct((t, D_MODEL), F32), jax.ShapeDtypeStruct((t, D_MODEL // 2), U32),
                   jax.ShapeDtypeStruct((TOP_K, t), I32), jax.ShapeDtypeStruct((TOP_K, t), F32),
                   jax.ShapeDtypeStruct((TOP_K, t), I32), jax.ShapeDtypeStruct(cnt0.shape, F32)],
        compiler_params=_cparams(),
        name="post_mixer",
    )(acta, actb, sga, sgb, x2d, *mods, *consts)


def _dest_kernel(start_ref, eidx_ref, rank_ref, o_ref, *, width):
    step = SUBLANES * LANES
    for c0 in range(0, width, step):
        c1 = min(c0 + step, width)
        e = eidx_ref[:, c0:c1]
        base = lax.fori_loop(0, N_EXPERTS, lambda x, acc: jnp.where(e == x, start_ref[x], acc),
                             jnp.zeros(e.shape, I32), unroll=8)
        o_ref[:, c0:c1] = base + rank_ref[:, c0:c1]


def _dest(cnt_start, e_idx, rank):
    k, t = e_idx.shape
    full = pl.BlockSpec((k, t), lambda i, st: (0, 0))
    return pl.pallas_call(
        functools.partial(_dest_kernel, width=t),
        grid_spec=pltpu.PrefetchScalarGridSpec(
            num_scalar_prefetch=1, grid=(1,), in_specs=[full, full], out_specs=full),
        out_shape=jax.ShapeDtypeStruct((k, t), I32),
        compiler_params=_cparams(),
        name="moe_dest",
    )(cnt_start, e_idx, rank)


def _sc_mesh():
    return plsc.VectorSubcoreMesh(core_axis_name="core", subcore_axis_name="subcore")


def _sc_pipeline(body, n_steps, in_specs, out_specs):
    return pltpu.emit_pipeline(body, grid=(n_steps,), in_specs=in_specs, out_specs=out_specs,
                               core_axis_name=("core", "subcore"), dimension_semantics=(pltpu.PARALLEL,))


def _sc_scatter_rows(x, half, dest_flat, n_rows):
    t = x.shape[0]
    tiles = t // SC_WINDOW
    n_steps = dest_flat.shape[1] // SC_WINDOW

    @functools.partial(pl.kernel, mesh=_sc_mesh(), scratch_types=[],
                       out_type=jax.ShapeDtypeStruct((n_rows, QUARTER_WORDS), x.dtype))
    def scatter(x_hbm, i_hbm, o_hbm):
        def body(x_vmem, i_vmem):
            pltpu.sync_copy(x_vmem, o_hbm.at[i_vmem.at[0]])

        _sc_pipeline(body, n_steps,
                     [pl.BlockSpec((SC_WINDOW, QUARTER_WORDS), lambda i: (i % tiles, half)),
                      pl.BlockSpec((1, SC_WINDOW), lambda i: (0, i))], [])(x_hbm, i_hbm)

    return scatter(x, dest_flat)


def _sc_gather_rows(table, idx_flat):
    n = idx_flat.shape[1]

    @functools.partial(pl.kernel, mesh=_sc_mesh(), scratch_types=[],
                       out_type=jax.ShapeDtypeStruct((n, table.shape[1]), table.dtype))
    def gather(t_hbm, i_hbm, o_hbm):
        def body(i_vmem, o_vmem):
            pltpu.sync_copy(t_hbm.at[i_vmem.at[0]], o_vmem)

        _sc_pipeline(body, n // SC_WINDOW,
                     [pl.BlockSpec((1, SC_WINDOW), lambda i: (0, i))],
                     [pl.BlockSpec((SC_WINDOW, table.shape[1]), lambda i: (i, 0))])(i_hbm, o_hbm)

    return gather(table, idx_flat)


def _experts_kernel(sblk_ref, sexp_ref, slo_ref, shi_ref, snext_ref, first_ref,
                    xa_ref, xb_ref, wg_hbm, wu_hbm, wd_hbm, ya_ref, yb_ref,
                    wg_buf, wu_buf, wd_buf, wgu_s, wd_s, state_ref, sems, *, bm):
    p = pl.program_id(0)
    lo = slo_ref[p]
    expert = sexp_ref[p]

    def weight_copies(e, slot):
        return [pltpu.make_async_copy(src.at[e], dst.at[slot], sems.at[slot])
                for src, dst in ((wg_hbm, wg_buf), (wu_hbm, wu_buf), (wd_hbm, wd_buf))]

    @pl.when(p == 0)
    def _():
        state_ref[0] = -1
        state_ref[1] = 0
        for c in weight_copies(first_ref[0], 0):
            c.start()

    @pl.when(shi_ref[p] > lo)
    def _():
        @pl.when(state_ref[0] != expert)
        def _():
            slot = state_ref[1]
            for c in weight_copies(expert, slot):
                c.wait()
            wgu_s[:, :D_EXPERT] = wg_buf[slot].astype(BF16)
            wgu_s[:, D_EXPERT:] = wu_buf[slot].astype(BF16)
            wd_s[...] = wd_buf[slot].astype(BF16)
            state_ref[0] = expert
            state_ref[1] = 1 - slot
            upcoming = snext_ref[p]

            @pl.when(upcoming >= 0)
            def _():
                for c in weight_copies(upcoming, 1 - slot):
                    c.start()

        q = QUARTER_WORDS
        quarters = {}
        quarters[0], quarters[2] = _unpack_bf16_pairs(xa_ref[...])
        quarters[1], quarters[3] = _unpack_bf16_pairs(xb_ref[...])
        gu = functools.reduce(jnp.add, [
            jnp.dot(quarters[c], wgu_s[c * q:(c + 1) * q, :], preferred_element_type=F32) for c in range(4)])
        hb = (_silu(gu[:, :D_EXPERT]) * gu[:, D_EXPERT:]).astype(BF16)
        y = _pack_bf16_pairs(jnp.dot(hb, wd_s[...], preferred_element_type=F32))

        @pl.when(lo == 0)
        def _():
            ya_ref[...] = y[:, :q]
            yb_ref[...] = y[:, q:]

        @pl.when(lo > 0)
        def _():
            keep = lax.broadcasted_iota(I32, (bm, 1), 0) >= lo
            ya_ref[...] = jnp.where(keep, y[:, :q], ya_ref[...])
            yb_ref[...] = jnp.where(keep, y[:, q:], yb_ref[...])


def _experts(xs_a, xs_b, w_gate, w_up, w_down, seg_blk, seg_exp, seg_lo, seg_hi, seg_next, first_exp, *, bm):
    n_rows = xs_a.shape[0]
    n_seg = seg_blk.shape[0]
    ring = 2
    hbm = pl.BlockSpec(memory_space=pl.ANY)
    rows = pl.BlockSpec((bm, QUARTER_WORDS), lambda p, sb, *_: (sb[p], 0))
    grid_spec = pltpu.PrefetchScalarGridSpec(
        num_scalar_prefetch=6,
        grid=(n_seg,),
        in_specs=[rows, rows, hbm, hbm, hbm],
        out_specs=[rows, rows],
        scratch_shapes=[pltpu.VMEM((ring, D_MODEL, D_EXPERT), F32), pltpu.VMEM((ring, D_MODEL, D_EXPERT), F32),
                        pltpu.VMEM((ring, D_EXPERT, D_MODEL), F32),
                        pltpu.VMEM((D_MODEL, 2 * D_EXPERT), BF16), pltpu.VMEM((D_EXPERT, D_MODEL), BF16),
                        pltpu.SMEM((2,), I32), pltpu.SemaphoreType.DMA((ring,))],
    )
    return pl.pallas_call(
        functools.partial(_experts_kernel, bm=bm),
        grid_spec=grid_spec,
        out_shape=[jax.ShapeDtypeStruct((n_rows, QUARTER_WORDS), U32)] * 2,
        compiler_params=_cparams(),
        name="moe_experts",
    )(seg_blk, seg_exp, seg_lo, seg_hi, seg_next, first_exp, xs_a, xs_b, w_gate, w_up, w_down)


def _final_kernel(*refs, tm, n_prompt_tiles, has_sample, has_prev):
    refs = list(refs)
    ew_ref, pa_ref, pb_ref, xsp_ref = refs[:4]
    del refs[:4]
    xss_ref = refs.pop(0) if has_sample else None
    g2p_ref = refs.pop(0)
    g2s_ref = refs.pop(0) if has_sample else None
    gfin_ref = refs.pop(0)
    if has_prev:
        refs.pop(0)
    op_ref = refs.pop(0)
    os_ref = refs.pop(0) if has_sample else None
    ew = ew_ref[...]
    quarters = [jnp.zeros((tm, QUARTER_WORDS), F32)] * 4
    for k in range(TOP_K):
        w_k = ew[:, k:k + 1]
        a_low, a_high = _unpack_bf16_pairs(pa_ref[k])
        b_low, b_high = _unpack_bf16_pairs(pb_ref[k])
        for c, part in enumerate((a_low, b_low, a_high, b_high)):
            quarters[c] = quarters[c] + w_k * part.astype(F32)
    routed = jnp.concatenate(quarters, axis=1)

    def finish(xs_ref, gate2_ref, o_ref):
        x2 = xs_ref[...] + gate2_ref[...] * routed
        o_ref[...] = x2 * lax.rsqrt(jnp.mean(x2 * x2, axis=-1, keepdims=True) + EPS) * gfin_ref[...]

    if not has_sample:
        finish(xsp_ref, g2p_ref, op_ref)
    else:
        is_prompt = pl.program_id(0) < n_prompt_tiles
        pl.when(is_prompt)(lambda: finish(xsp_ref, g2p_ref, op_ref))
        pl.when(jnp.logical_not(is_prompt))(lambda: finish(xss_ref, g2s_ref, os_ref))


def _final(ew_part, planes_a, planes_b, xs_p, gate2_p, g_final, *, tm, tiles_per_batch, tile_lo, n_prompt_tiles,
           sample=None, y_prev=None):
    prompt_tile = lambda i: tile_lo + jnp.minimum(i, n_prompt_tiles - 1)
    sample_tile = lambda i: jnp.maximum(i - n_prompt_tiles, 0)
    prompt_rows = pl.BlockSpec((tm, D_MODEL), lambda i: (prompt_tile(i), 0))
    sample_rows = pl.BlockSpec((tm, D_MODEL), lambda i: (sample_tile(i), 0))
    planes = pl.BlockSpec((TOP_K, tm, QUARTER_WORDS), lambda i: (0, i, 0))
    has_sample = sample is not None
    has_prev = y_prev is not None
    n_tiles = n_prompt_tiles + (sample[0].shape[0] // tm if has_sample else 0)
    args = [ew_part, planes_a, planes_b, xs_p]
    in_specs = [pl.BlockSpec((tm, TOP_K), lambda i: (i, 0)), planes, planes, prompt_rows]
    if has_sample:
        args.append(sample[0])
        in_specs.append(sample_rows)
    args.append(gate2_p)
    in_specs.append(pl.BlockSpec((None, 1, D_MODEL), lambda i: (prompt_tile(i) // tiles_per_batch, 0, 0)))
    if has_sample:
        args.append(sample[1])
        in_specs.append(pl.BlockSpec((None, tm, D_MODEL), lambda i: (sample_tile(i), 0, 0)))
    args.append(g_final)
    in_specs.append(_const_spec(g_final))
    if has_prev:
        args.append(y_prev)
        in_specs.append(pl.BlockSpec(memory_space=pl.ANY))
    out_specs = [prompt_rows] + ([sample_rows] if has_sample else [])
    out_shape = [jax.ShapeDtypeStruct(xs_p.shape, F32)] + (
        [jax.ShapeDtypeStruct(sample[0].shape, F32)] if has_sample else [])
    outs = pl.pallas_call(
        functools.partial(_final_kernel, tm=tm, n_prompt_tiles=n_prompt_tiles, has_sample=has_sample,
                          has_prev=has_prev),
        grid=(n_tiles,),
        in_specs=in_specs,
        out_specs=out_specs,
        out_shape=out_shape,
        input_output_aliases={len(args) - 1: 0} if has_prev else {},
        compiler_params=_cparams(),
        name="moe_combine_final",
    )(*args)
    return outs if has_sample else outs[0]


def _pick_tile(n, preferred):
    t = min(n, preferred)
    assert n % t == 0, (n, t)
    return t


def kernel(x_prompt, x_sample, c_prompt, c_sample, state_conv, state_short_conv, state_delta, w_ada, b_ada, g_mix, w_in, conv_dw, conv_dw_b, conv_ln_g, conv_ln_b, w_pa, short_conv_w, a_log, dt_bias, dn_norm_g, w_pb, w_o, g_ffn, w_router, e_bias, w_gate, w_up, w_down, ws_gate, ws_up, ws_down, g_final):
    nb, seq, d = x_prompt.shape
    ns = x_sample.shape[0]
    assert d == D_MODEL and w_ada.shape[0] == 1 and x_sample.shape[1] == 1
    assert seq % CHUNK == 0 and seq >= CONV_HALO and ns % SUBLANES == 0
    tp = nb * seq
    row1 = lambda a: a.reshape(1, -1)

    wi = w_in[0].astype(BF16)
    o_q = 2 * CONV_CH
    o_z = o_q + 3 * DN_WIDTH
    o_a = o_z + DN_WIDTH
    o_ga = o_a + 2 * N_HEADS
    w_proj = {
        "glu": wi[:, :o_q], "qkv": wi[:, o_q:o_z], "z": wi[:, o_z:o_a],
        "ab": jnp.pad(wi[:, o_a:o_ga], ((0, 0), (0, LANES - 2 * N_HEADS))),
        "ga": wi[:, o_ga:o_ga + D_MODEL], "gb": wi[:, o_ga + D_MODEL:],
    }
    nea = jnp.pad(-jnp.exp(a_log[0].astype(F32)), (0, LANES - N_HEADS)).reshape(1, LANES)
    dtb = jnp.pad(dt_bias[0].astype(F32), (0, LANES - N_HEADS)).reshape(1, LANES)
    w_post = {
        "pa": w_pa[0].astype(BF16), "pb": w_pb[0].astype(BF16), "o": w_o[0].astype(BF16),
        "router_t": w_router[0].T.astype(BF16),
        "s_gu": jnp.concatenate([ws_gate[0], ws_up[0]], axis=1).astype(BF16),
        "s_down": ws_down[0].astype(BF16),
    }
    ebias_col = e_bias[0].astype(F32).reshape(N_EXPERTS, 1)
    g_mix1, g_ffn1, g_fin1 = row1(g_mix[0]), row1(g_ffn[0]), row1(g_final)
    cdw, cdb, clg, clb = conv_dw[0], row1(conv_dw_b[0]), row1(conv_ln_g[0]), row1(conv_ln_b[0])
    scw, dng = short_conv_w[0], row1(dn_norm_g[0])

    ada = _ada(jnp.concatenate([c_prompt, c_sample], axis=0), w_ada[0].astype(BF16), b_ada[0])
    ada = ada.reshape(nb + ns, 6, D_MODEL)
    mod_p = [ada[:nb, m].reshape(nb, 1, D_MODEL) for m in range(6)]
    mod_s = [ada[nb:, m].reshape(1, ns, D_MODEL) for m in range(6)]
    shift1, scale1, gate1, shift2, scale2, gate2 = range(6)

    tm_p = _pick_tile(seq, 256)
    tpm_p = seq // tm_p
    xp = x_prompt.reshape(tp, D_MODEL)
    u_p, qkv_p, zs_p, ab_p, sga_p, sgb_p = _in_proj(
        xp, mod_p[scale1], mod_p[shift1], g_mix1, w_proj, nea, dtb, tm=tm_p, tiles_per_mod=tpm_p)
    tl = _pick_tile(seq, 256)
    acta_p = _conv_branch(u_p, jnp.zeros((nb, CONV_HALO, CONV_CH), F32), cdw, cdb, clg, clb,
                          tl=tl, tiles_per_batch=seq // tl)
    tc = _pick_tile(seq, 4 * CHUNK)
    actb_p, s_new_p = _delta_branch(
        qkv_p, jnp.zeros((nb, SC_HALO, 3 * DN_WIDTH), F32),
        jnp.zeros((nb, N_HEADS, HEAD_DIM, HEAD_DIM), F32), scw, ab_p, zs_p, dng,
        tc=tc, tiles_per_batch=seq // tc)

    xs_ = x_sample.reshape(ns, D_MODEL)
    u_s, qkv_s, zs_s, ab_s, sga_s, sgb_s = _in_proj(
        xs_, mod_s[scale1], mod_s[shift1], g_mix1, w_proj, nea, dtb, tm=ns, tiles_per_mod=1)
    bt = SUBLANES
    acta_s, nch_s, qkvp_s, nsh_s = _step_front(
        u_s, qkv_s, state_conv[0].reshape(ns, -1), state_short_conv[0].reshape(ns, -1),
        cdw, cdb, clg, clb, scw, bt=bt)
    actb_s, s_new_s = _step_delta(qkvp_s, ab_s, zs_s, state_delta[0].astype(F32), dng, bt=bt)

    cnt0 = jnp.zeros((N_EXPERTS, LANES), F32)
    xsh_p, h2_p, eidx_p, ew_p, rank_p, cnt_p = _post(
        acta_p, actb_p, sga_p, sgb_p, xp, mod_p[gate1], mod_p[scale2], mod_p[shift2], mod_p[gate2],
        cnt0, g_ffn1, w_post, ebias_col, tm=tm_p, tiles_per_mod=tpm_p)
    xsh_s, h2_s, eidx_s, ew_s, rank_s, cnt_all = _post(
        acta_s, actb_s, sga_s, sgb_s, xs_, mod_s[gate1], mod_s[scale2], mod_s[shift2], mod_s[gate2],
        cnt_p, g_ffn1, w_post, ebias_col, tm=ns, tiles_per_mod=1)

    bm = 512
    tmd = LANES
    t_all = tp + ns
    n_rows = t_all * TOP_K
    assert seq % tmd == 0 and ns % tmd == 0 and n_rows % bm == 0 and t_all % SC_WINDOW == 0
    n_blocks = n_rows // bm
    counts = cnt_all[:, 0].astype(I32)
    cnt_end = jnp.cumsum(counts)
    cnt_start = cnt_end - counts
    e_idx = jnp.concatenate([eidx_p, eidx_s], axis=1)
    rank = jnp.concatenate([rank_p, rank_s], axis=1)
    dest = _dest(cnt_start, e_idx, rank)
    dest_flat = dest.reshape(1, n_rows)
    seg_start = jnp.sort(jnp.concatenate([jnp.arange(n_blocks, dtype=I32) * bm, cnt_start]))
    seg_end = jnp.concatenate([seg_start[1:], jnp.full((1,), n_rows, I32)])
    seg_blk = jnp.minimum(seg_start // bm, n_blocks - 1)
    seg_lo = seg_start - seg_blk * bm
    seg_hi = seg_end - seg_blk * bm
    seg_exp = jnp.sum((cnt_end[None, :] <= seg_start[:, None]).astype(I32), axis=1)
    seg_exp = jnp.minimum(seg_exp, N_EXPERTS - 1)
    seg_exp = lax.cummax(jnp.where(seg_hi > seg_lo, seg_exp, 0))
    none = jnp.int32(N_EXPERTS)
    in_use_from = lax.cummin(jnp.where(counts > 0, jnp.arange(N_EXPERTS, dtype=I32), none), reverse=True)
    next_in_use = jnp.concatenate([in_use_from[1:], jnp.full((1,), none, I32)])
    next_in_use = jnp.where(next_in_use == none, -1, next_in_use)
    seg_next = next_in_use[seg_exp]
    first_exp = jnp.minimum(in_use_from[:1], N_EXPERTS - 1)

    h2_all = jnp.concatenate([h2_p, h2_s], axis=0)
    xs_a, xs_b = (_sc_scatter_rows(h2_all, half, dest_flat, n_rows) for half in range(2))
    ys_a, ys_b = _experts(xs_a, xs_b, w_gate[0], w_up[0], w_down[0], seg_blk, seg_exp, seg_lo, seg_hi, seg_next,
                          first_exp, bm=bm)

    ew_tok = jnp.concatenate([ew_p, ew_s], axis=1).T
    n_prompt_tiles = tp // tmd
    tiles_1 = n_prompt_tiles // 2
    t_1 = tiles_1 * tmd

    def gathered(lo, hi):
        idx = dest[:, lo:hi].reshape(1, -1)
        return [_sc_gather_rows(ys, idx).reshape(TOP_K, hi - lo, QUARTER_WORDS) for ys in (ys_a, ys_b)]

    final = functools.partial(_final, xs_p=xsh_p, gate2_p=mod_p[gate2], g_final=g_fin1, tm=tmd,
                              tiles_per_batch=seq // tmd)
    y_p = final(ew_tok[:t_1], *gathered(0, t_1), tile_lo=0, n_prompt_tiles=tiles_1)
    y_p, y_s = final(ew_tok[t_1:], *gathered(t_1, t_all), tile_lo=tiles_1, n_prompt_tiles=n_prompt_tiles - tiles_1,
                     sample=(xsh_s, mod_s[gate2].reshape(ns // tmd, tmd, D_MODEL)), y_prev=y_p)

    hist_rows = CONV_WIDTH - 1
    sc_rows = SHORT_CONV - 1
    y_prompt = y_p.reshape(nb, seq, D_MODEL)
    y_sample = y_s.reshape(ns, 1, D_MODEL)
    new_conv_prompt = u_p.reshape(nb, seq, CONV_CH)[:, seq - hist_rows:][None]
    new_sc_prompt = qkv_p.reshape(nb, seq, 3 * DN_WIDTH)[:, seq - sc_rows:][None]
    new_conv_sample = nch_s.reshape(ns, hist_rows, CONV_CH)[None]
    new_sc_sample = nsh_s.reshape(ns, sc_rows, 3 * DN_WIDTH)[None]
    return (y_prompt, y_sample, new_conv_prompt, new_sc_prompt, s_new_p[None],
            new_conv_sample, new_sc_sample, s_new_s[None])
```

```python
import functools

import jax
import jax.numpy as jnp
from jax import lax
from jax.experimental import pallas as pl
from jax.experimental.pallas import tpu as pltpu
from jax.experimental.pallas import tpu_sc as plsc

F32 = jnp.float32
BF16 = jnp.bfloat16
I32 = jnp.int32
U32 = jnp.uint32

EPS = 1e-6
D_MODEL = 1024
CONV_CH = 512
CONV_WIDTH = 31
N_HEADS = 4
HEAD_DIM = 128
DN_WIDTH = N_HEADS * HEAD_DIM
SHORT_CONV = 4
CHUNK = 64
CHUNK_SHIFT = CHUNK.bit_length() - 1
assert 1 << CHUNK_SHIFT == CHUNK
SUB = 16
SUB_SHIFT = SUB.bit_length() - 1
assert 1 << SUB_SHIFT == SUB and CHUNK % SUB == 0
N_EXPERTS = 256
TOP_K = 8
N_GROUPS = 8
GROUP_SIZE = N_EXPERTS // N_GROUPS
TOPK_GROUPS = 4
D_EXPERT = 256
D_SHARED = 256
ROUTE_SCALE = 2.5

LANES = 128
SUBLANES = 8
CONV_HALO = 32
SC_HALO = 8
VMEM_LIMIT = 48 * 1024 * 1024
EXPERT_SUBBLOCKS = 4
SC_WINDOW = 128
QUARTER_WORDS = D_MODEL // 4

NT_DIMS = (((1,), (1,)), ((), ()))
TN_DIMS = (((0,), (0,)), ((), ()))


def _cparams(n_grid_dims=1):
    return pltpu.CompilerParams(
        dimension_semantics=("arbitrary",) * n_grid_dims,
        vmem_limit_bytes=VMEM_LIMIT)


def _sigmoid(x):
    return jax.nn.sigmoid(x)


def _silu(x):
    return x * jax.nn.sigmoid(x)


def _bdot(a, b):
    return jnp.dot(a.astype(BF16), b.astype(BF16), preferred_element_type=F32)


def _pack_bf16_pairs(x):
    half = x.shape[1] // 2
    bits = lax.bitcast_convert_type(x.astype(BF16).astype(F32), U32)
    return bits[:, half:] | (bits[:, :half] >> 16)


def _unpack_bf16_pairs(w):
    low = lax.bitcast_convert_type(w << 16, F32).astype(BF16)
    high = lax.bitcast_convert_type(w & jnp.uint32(0xFFFF0000), F32).astype(BF16)
    return low, high


def _dot3(a, b):
    a_hi = a.astype(BF16)
    b_hi = b.astype(BF16)
    a_lo = (a - a_hi.astype(F32)).astype(BF16)
    b_lo = (b - b_hi.astype(F32)).astype(BF16)
    dot = lambda x, y: jnp.dot(x, y, preferred_element_type=F32)
    return dot(a_hi, b_hi) + (dot(a_hi, b_lo) + dot(a_lo, b_hi))


def _dot_exact_lhs(a_bf, b):
    hi = b.astype(BF16)
    r1 = b - hi.astype(F32)
    mid = r1.astype(BF16)
    low = (r1 - mid.astype(F32)).astype(BF16)
    dot = lambda p: jnp.dot(a_bf, p, preferred_element_type=F32)
    return dot(hi) + dot(mid) + dot(low)


def _ada_kernel(c_ref, w_ref, b_ref, o_ref):
    o_ref[...] = _bdot(_silu(c_ref[...]), w_ref[...]) + b_ref[...]


def _ada(c_all, w_ada_bf, b_ada):
    n = c_all.shape[0]
    width = w_ada_bf.shape[1]
    tn = 1536
    return pl.pallas_call(
        _ada_kernel,
        grid=(width // tn,),
        in_specs=[pl.BlockSpec((n, D_MODEL), lambda j: (0, 0)),
                  pl.BlockSpec((D_MODEL, tn), lambda j: (0, j)),
                  pl.BlockSpec((1, tn), lambda j: (0, j))],
        out_specs=pl.BlockSpec((n, tn), lambda j: (0, j)),
        out_shape=jax.ShapeDtypeStruct((n, width), F32),
        compiler_params=_cparams(),
        name="ada",
    )(c_all, w_ada_bf, b_ada.reshape(1, width))


def _in_proj_kernel(x_ref, scale_ref, shift_ref, g_ref, wglu_ref, wqkv_ref, wz_ref, wab_ref,
                    wga_ref, wgb_ref, nea_ref, dtb_ref,
                    u_ref, qkv_ref, zs_ref, ab_ref, sga_ref, sgb_ref):
    x = x_ref[...]
    y = x * lax.rsqrt(jnp.mean(x * x, axis=-1, keepdims=True) + EPS) * g_ref[...]
    h = (y * (1.0 + scale_ref[...]) + shift_ref[...]).astype(BF16)
    glu = jnp.dot(h, wglu_ref[...], preferred_element_type=F32)
    u_ref[...] = glu[:, :CONV_CH] * _sigmoid(glu[:, CONV_CH:])
    qkv_ref[...] = jnp.dot(h, wqkv_ref[...], preferred_element_type=F32)
    zs_ref[...] = _silu(jnp.dot(h, wz_ref[...], preferred_element_type=F32))
    ab = jnp.dot(h, wab_ref[...], preferred_element_type=F32)
    sp = ab + dtb_ref[...]
    softplus = jnp.maximum(sp, 0.0) + jnp.log1p(jnp.exp(-jnp.abs(sp)))
    lane = lax.broadcasted_iota(I32, ab.shape, 1)
    ab_ref[...] = jnp.where(lane < N_HEADS, nea_ref[...] * softplus, _sigmoid(ab))
    sga_ref[...] = _sigmoid(jnp.dot(h, wga_ref[...], preferred_element_type=F32))
    sgb_ref[...] = _sigmoid(jnp.dot(h, wgb_ref[...], preferred_element_type=F32))


def _mod_spec(mod, tiles_per_mod):
    return pl.BlockSpec((None,) + mod.shape[1:], lambda i: (i // tiles_per_mod, 0, 0))


def _const_spec(a):
    nd = a.ndim
    return pl.BlockSpec(a.shape, lambda i: (0,) * nd)


def _in_proj(x2d, scale, shift, g_mix, w, nea, dtb, *, tm, tiles_per_mod):
    t = x2d.shape[0]
    row = lambda c: pl.BlockSpec((tm, c), lambda i: (i, 0))
    consts = [g_mix, w["glu"], w["qkv"], w["z"], w["ab"], w["ga"], w["gb"], nea, dtb]
    widths = [CONV_CH, 3 * DN_WIDTH, DN_WIDTH, LANES, D_MODEL, D_MODEL]
    return pl.pallas_call(
        _in_proj_kernel,
        grid=(t // tm,),
        in_specs=[row(D_MODEL), _mod_spec(scale, tiles_per_mod), _mod_spec(shift, tiles_per_mod)]
                 + [_const_spec(a) for a in consts],
        out_specs=[row(c) for c in widths],
        out_shape=[jax.ShapeDtypeStruct((t, c), F32) for c in widths],
        compiler_params=_cparams(),
        name="in_proj",
    )(x2d, scale, shift, *consts)


def _layer_norm_swish(y, g, b):
    mu = jnp.mean(y, axis=-1, keepdims=True)
    yc = y - mu
    var = jnp.mean(yc * yc, axis=-1, keepdims=True)
    return _silu(yc * lax.rsqrt(var + EPS) * g + b)


def _conv_kernel(u_ref, halo_ref, hist_ref, w_ref, b_ref, lg_ref, lb_ref, o_ref, win_ref, shift_ref,
                 *, tl, tiles_per_batch, rows_per_pass):
    i = pl.program_id(0)
    first = (i % tiles_per_batch) == 0

    @pl.when(first)
    def _():
        win_ref[0:CONV_HALO, :] = hist_ref[...]

    @pl.when(jnp.logical_not(first))
    def _():
        win_ref[0:CONV_HALO, :] = halo_ref[...]

    win_ref[CONV_HALO:CONV_HALO + tl, :] = u_ref[...]
    tap0 = CONV_HALO - (CONV_WIDTH - 1)
    n_shifted = CONV_HALO + tl - SUBLANES
    for b in range(1, SUBLANES):
        shift_ref[b - 1] = win_ref[b:b + n_shifted, :]

    for r in range(tl // rows_per_pass):
        base = r * rows_per_pass
        acc = jnp.zeros((rows_per_pass, CONV_CH), F32)
        for j in range(CONV_WIDTH):
            a, b = divmod(tap0 + j, SUBLANES)
            start = base + a * SUBLANES
            rows = (win_ref[start:start + rows_per_pass, :] if b == 0
                    else shift_ref[b - 1, start:start + rows_per_pass, :])
            acc = acc + w_ref[j:j + 1, :] * rows
        act = _layer_norm_swish(acc + b_ref[...], lg_ref[...], lb_ref[...])
        o_ref[base:base + rows_per_pass, :] = act.astype(o_ref.dtype)


def _conv_branch(u2d, hist, conv_dw, conv_dw_b, ln_g, ln_b, *, tl, tiles_per_batch):
    t = u2d.shape[0]
    halo_blocks = tl // CONV_HALO
    consts = [conv_dw, conv_dw_b, ln_g, ln_b]
    return pl.pallas_call(
        functools.partial(_conv_kernel, tl=tl, tiles_per_batch=tiles_per_batch, rows_per_pass=64),
        grid=(t // tl,),
        in_specs=[pl.BlockSpec((tl, CONV_CH), lambda i: (i, 0)),
                  pl.BlockSpec((CONV_HALO, CONV_CH), lambda i: (jnp.maximum(i * halo_blocks - 1, 0), 0)),
                  pl.BlockSpec((None, CONV_HALO, CONV_CH), lambda i: (i // tiles_per_batch, 0, 0))]
                 + [_const_spec(a) for a in consts],
        out_specs=pl.BlockSpec((tl, CONV_CH), lambda i: (i, 0)),
        out_shape=jax.ShapeDtypeStruct((t, CONV_CH), BF16),
        scratch_shapes=[pltpu.VMEM((CONV_HALO + tl, CONV_CH), F32),
                        pltpu.VMEM((SUBLANES - 1, CONV_HALO + tl - SUBLANES, CONV_CH), F32)],
        compiler_params=_cparams(),
        name="conv_branch",
    )(u2d, u2d, hist, *consts)


def _l2n(x):
    return x * lax.rsqrt(jnp.sum(x * x, axis=-1, keepdims=True) + 1e-6)


def _gated_head_norm(o, g, zs):
    return o * lax.rsqrt(jnp.mean(o * o, axis=-1, keepdims=True) + EPS) * g * zs


def _delta_kernel(qkv_ref, halo_ref, hist_ref, scw_ref, ab_ref, zs_ref, s0_ref, dng_ref,
                  o_ref, s_ref, win_ref, c_ref, *, tc, tiles_per_batch):
    i = pl.program_id(0)
    first = (i % tiles_per_batch) == 0

    @pl.when(first)
    def _():
        win_ref[0:SC_HALO, :] = hist_ref[...]
        s_ref[...] = s0_ref[...]

    @pl.when(jnp.logical_not(first))
    def _():
        win_ref[0:SC_HALO, :] = halo_ref[...]

    win_ref[SC_HALO:SC_HALO + tc, :] = qkv_ref[...]
    tap0 = SC_HALO - (SHORT_CONV - 1)
    acc = jnp.zeros((tc, 3 * DN_WIDTH), F32)
    for j in range(SHORT_CONV):
        acc = acc + scw_ref[j:j + 1, :] * win_ref[tap0 + j:tap0 + j + tc, :]
    c_ref[...] = _silu(acc)

    row = lax.broadcasted_iota(I32, (tc, tc), 0)
    col = lax.broadcasted_iota(I32, (tc, tc), 1)
    same_chunk = lax.shift_right_logical(row, CHUNK_SHIFT) == lax.shift_right_logical(col, CHUNK_SHIFT)
    incl = jnp.logical_and(same_chunk, row >= col)
    row_sub = lax.shift_right_logical(row, SUB_SHIFT)
    col_sub = lax.shift_right_logical(col, SUB_SHIFT)
    strict_sub = jnp.logical_and(row_sub == col_sub, row > col)
    below_sub = jnp.logical_and(same_chunk, row_sub > col_sub)
    eye = (row == col).astype(F32)
    ab = ab_ref[...]
    gcum = _dot_exact_lhs(incl.astype(BF16), ab)
    gcum_t = gcum.T
    n_ch = tc // CHUNK

    heads = range(N_HEADS)
    q, k, gc, kk, a_qk, rhs, qd = [], [], [], [], [], [], []
    for h in heads:
        lo = h * HEAD_DIM
        q_h = _l2n(c_ref[:, lo:lo + HEAD_DIM]) * (HEAD_DIM ** -0.5)
        k_h = _l2n(c_ref[:, DN_WIDTH + lo:DN_WIDTH + lo + HEAD_DIM])
        v_h = c_ref[:, 2 * DN_WIDTH + lo:2 * DN_WIDTH + lo + HEAD_DIM]
        beta = ab[:, N_HEADS + h:N_HEADS + h + 1]
        gc_h = gcum[:, h:h + 1]
        gr = gcum_t[h:h + 1, :]
        decay = jnp.exp(jnp.where(incl, gc_h - gr, -jnp.inf))
        kb = k_h * beta
        kq = lax.dot_general(jnp.concatenate([kb, q_h], axis=0).astype(BF16), k_h.astype(BF16),
                             NT_DIMS, preferred_element_type=F32)
        e_gc = jnp.exp(gc_h)
        q.append(q_h)
        k.append(k_h)
        gc.append(gc_h)
        kk.append(kq[:tc] * decay)
        a_qk.append(kq[tc:] * decay)
        rhs.append(jnp.concatenate([v_h * beta, kb * e_gc], axis=1))
        qd.append(q_h * e_gc)

    pw = [jnp.where(strict_sub, kk[h], 0.0) for h in heads]
    d_inv = [eye - pw[h] for h in heads]
    for _ in range(SUB_SHIFT - 1):
        pw = [_bdot(pw[h], pw[h]) for h in heads]
        d_inv = [d_inv[h] + _bdot(d_inv[h], pw[h]) for h in heads]
    c_rhs = [_bdot(d_inv[h], rhs[h]) for h in heads]
    m_off = [_bdot(d_inv[h], jnp.where(below_sub, kk[h], 0.0)) for h in heads]
    sol = c_rhs
    for _ in range(CHUNK // SUB - 1):
        sol = [c_rhs[h] - _bdot(m_off[h], sol[h]) for h in heads]

    s = [s_ref[h] for h in heads]
    u_parts = [[] for _ in heads]
    o_parts = [[] for _ in heads]
    for c in range(n_ch):
        r0 = c * CHUNK
        for h in heads:
            u0_c = sol[h][r0:r0 + CHUNK, :HEAD_DIM]
            w_c = sol[h][r0:r0 + CHUNK, HEAD_DIM:]
            wq_s = _bdot(jnp.concatenate([w_c, qd[h][r0:r0 + CHUNK]], axis=0), s[h])
            u_c = u0_c - wq_s[:CHUNK]
            g_last = gcum[r0 + CHUNK - 1:r0 + CHUNK, h:h + 1]
            k_dec = k[h][r0:r0 + CHUNK] * jnp.exp(g_last - gc[h][r0:r0 + CHUNK])
            s[h] = jnp.exp(g_last) * s[h] + lax.dot_general(
                k_dec.astype(BF16), u_c.astype(BF16), TN_DIMS, preferred_element_type=F32)
            u_parts[h].append(u_c)
            o_parts[h].append(wq_s[CHUNK:])
    for h in heads:
        lo = h * HEAD_DIM
        s_ref[h] = s[h]
        o = jnp.concatenate(o_parts[h], axis=0) + _bdot(a_qk[h], jnp.concatenate(u_parts[h], axis=0))
        o_ref[:, lo:lo + HEAD_DIM] = _gated_head_norm(
            o, dng_ref[...], zs_ref[:, lo:lo + HEAD_DIM]).astype(o_ref.dtype)


def _delta_branch(qkv2d, hist, s0, scw, ab, zs, dng, *, tc, tiles_per_batch):
    t = qkv2d.shape[0]
    nb = s0.shape[0]
    halo_blocks = tc // SC_HALO
    wqkv = 3 * DN_WIDTH
    state_spec = pl.BlockSpec((None, N_HEADS, HEAD_DIM, HEAD_DIM), lambda i: (i // tiles_per_batch, 0, 0, 0))
    return pl.pallas_call(
        functools.partial(_delta_kernel, tc=tc, tiles_per_batch=tiles_per_batch),
        grid=(t // tc,),
        in_specs=[pl.BlockSpec((tc, wqkv), lambda i: (i, 0)),
                  pl.BlockSpec((SC_HALO, wqkv), lambda i: (jnp.maximum(i * halo_blocks - 1, 0), 0)),
                  pl.BlockSpec((None, SC_HALO, wqkv), lambda i: (i // tiles_per_batch, 0, 0)),
                  _const_spec(scw),
                  pl.BlockSpec((tc, LANES), lambda i: (i, 0)),
                  pl.BlockSpec((tc, DN_WIDTH), lambda i: (i, 0)),
                  state_spec,
                  _const_spec(dng)],
        out_specs=[pl.BlockSpec((tc, DN_WIDTH), lambda i: (i, 0)), state_spec],
        out_shape=[jax.ShapeDtypeStruct((t, DN_WIDTH), BF16),
                   jax.ShapeDtypeStruct((nb, N_HEADS, HEAD_DIM, HEAD_DIM), F32)],
        scratch_shapes=[pltpu.VMEM((SC_HALO + tc, wqkv), F32), pltpu.VMEM((tc, wqkv), F32)],
        compiler_params=_cparams(),
        name="delta_branch",
    )(qkv2d, qkv2d, hist, scw, ab, zs, s0, dng)


def _step_front_kernel(u_ref, qkv_ref, ch_ref, sh_ref, cw_ref, cb_ref, lg_ref, lb_ref, scw_ref,
                       acta_ref, nch_ref, qkvp_ref, nsh_ref):
    hist_rows = CONV_WIDTH - 1
    u = u_ref[...]
    acc = cw_ref[hist_rows:hist_rows + 1, :] * u
    for j in range(hist_rows):
        acc = acc + cw_ref[j:j + 1, :] * ch_ref[:, j * CONV_CH:(j + 1) * CONV_CH]
    acta_ref[...] = _layer_norm_swish(acc + cb_ref[...], lg_ref[...], lb_ref[...]).astype(acta_ref.dtype)
    nch_ref[:, :(hist_rows - 1) * CONV_CH] = ch_ref[:, CONV_CH:]
    nch_ref[:, (hist_rows - 1) * CONV_CH:] = u

    wq = 3 * DN_WIDTH
    sc_rows = SHORT_CONV - 1
    qkv = qkv_ref[...]
    acc = scw_ref[sc_rows:sc_rows + 1, :] * qkv
    for j in range(sc_rows):
        acc = acc + scw_ref[j:j + 1, :] * sh_ref[:, j * wq:(j + 1) * wq]
    c = _silu(acc)
    for h in range(N_HEADS):
        lo = h * HEAD_DIM
        qkvp_ref[:, lo:lo + HEAD_DIM] = _l2n(c[:, lo:lo + HEAD_DIM]) * (HEAD_DIM ** -0.5)
        qkvp_ref[:, DN_WIDTH + lo:DN_WIDTH + lo + HEAD_DIM] = _l2n(c[:, DN_WIDTH + lo:DN_WIDTH + lo + HEAD_DIM])
    qkvp_ref[:, 2 * DN_WIDTH:] = c[:, 2 * DN_WIDTH:]
    nsh_ref[:, :(sc_rows - 1) * wq] = sh_ref[:, wq:]
    nsh_ref[:, (sc_rows - 1) * wq:] = qkv


def _step_front(u, qkv, conv_hist_flat, sc_hist_flat, conv_dw, conv_dw_b, ln_g, ln_b, scw, *, bt):
    n = u.shape[0]
    wq = 3 * DN_WIDTH
    consts = [conv_dw, conv_dw_b, ln_g, ln_b, scw]
    row = lambda c: pl.BlockSpec((bt, c), lambda i: (i, 0))
    widths = [CONV_CH, conv_hist_flat.shape[1], wq, sc_hist_flat.shape[1]]
    dtypes = [BF16, F32, F32, F32]
    return pl.pallas_call(
        _step_front_kernel,
        grid=(n // bt,),
        in_specs=[row(CONV_CH), row(wq), row(widths[1]), row(widths[3])] + [_const_spec(a) for a in consts],
        out_specs=[row(c) for c in widths],
        out_shape=[jax.ShapeDtypeStruct((n, c), d) for c, d in zip(widths, dtypes)],
        compiler_params=_cparams(),
        name="step_front",
    )(u, qkv, conv_hist_flat, sc_hist_flat, *consts)


def _rows_to_cols(x):
    pad = jnp.zeros((LANES - x.shape[0], LANES), x.dtype)
    return jnp.concatenate([x, pad], axis=0).T


def _step_delta_kernel(qkvp_ref, ab_ref, zs_ref, s_ref, dng_ref, o_ref, sn_ref, *, bt):
    ab = ab_ref[...]
    for h in range(N_HEADS):
        lo = h * HEAD_DIM
        q = qkvp_ref[:, lo:lo + HEAD_DIM]
        k = qkvp_ref[:, DN_WIDTH + lo:DN_WIDTH + lo + HEAD_DIM]
        v = qkvp_ref[:, 2 * DN_WIDTH + lo:2 * DN_WIDTH + lo + HEAD_DIM]
        q_cols = _rows_to_cols(q)
        k_cols = _rows_to_cols(k)
        qk = jnp.sum(q * k, axis=-1, keepdims=True)
        alpha = jnp.exp(ab[:, h:h + 1])
        beta = ab[:, N_HEADS + h:N_HEADS + h + 1]
        seqs = range(bt)
        kc = [k_cols[:, j:j + 1] for j in seqs]
        a = [alpha[j:j + 1, :] for j in seqs]
        s_k = [jnp.sum(s_ref[j, h] * kc[j], axis=0, keepdims=True) for j in seqs]
        s_q = [jnp.sum(s_ref[j, h] * q_cols[:, j:j + 1], axis=0, keepdims=True) for j in seqs]
        u = [beta[j:j + 1, :] * (v[j:j + 1, :] - a[j] * s_k[j]) for j in seqs]
        for j in seqs:
            sn_ref[j, h] = a[j] * s_ref[j, h] + kc[j] * u[j]
        o = jnp.concatenate([a[j] * s_q[j] + qk[j:j + 1, :] * u[j] for j in seqs], axis=0)
        o_ref[:, lo:lo + HEAD_DIM] = _gated_head_norm(
            o, dng_ref[...], zs_ref[:, lo:lo + HEAD_DIM]).astype(o_ref.dtype)


def _step_delta(qkvp, ab, zs, s0, dng, *, bt):
    n = qkvp.shape[0]
    row = lambda c: pl.BlockSpec((bt, c), lambda i: (i, 0))
    state_spec = pl.BlockSpec((bt, N_HEADS, HEAD_DIM, HEAD_DIM), lambda i: (i, 0, 0, 0))
    return pl.pallas_call(
        functools.partial(_step_delta_kernel, bt=bt),
        grid=(n // bt,),
        in_specs=[row(3 * DN_WIDTH), row(LANES), row(DN_WIDTH), state_spec, _const_spec(dng)],
        out_specs=[row(DN_WIDTH), state_spec],
        out_shape=[jax.ShapeDtypeStruct((n, DN_WIDTH), BF16), jax.ShapeDtypeStruct(s0.shape, F32)],
        compiler_params=_cparams(),
        name="step_delta",
    )(qkvp, ab, zs, s0, dng)


def _post_kernel(acta_ref, actb_ref, sga_ref, sgb_ref, x_ref, gate1_ref, scale2_ref, shift2_ref,
                 gate2_ref, cnt0_ref, gffn_ref, wpa_ref, wpb_ref, wo_ref, wrt_ref, ebias_ref,
                 wsgu_ref, wsd_ref,
                 xs_ref, h2_ref, eidx_ref, ew_ref, rank_ref, cnt_ref, scores_s, *, tm):
    i = pl.program_id(0)

    @pl.when(i == 0)
    def _():
        cnt_ref[...] = cnt0_ref[...]
        scores_s[...] = jnp.zeros(scores_s.shape, F32)

    neg = -jnp.inf
    big = jnp.int32(1 << 30)

    def project():
        y_a = jnp.dot(acta_ref[...], wpa_ref[...], preferred_element_type=F32)
        yield
        y_b = jnp.dot(actb_ref[...], wpb_ref[...], preferred_element_type=F32)
        merged = sga_ref[...] * y_a + sgb_ref[...] * y_b
        yield
        mix = _bdot(merged, wo_ref[...])
        yield
        x1 = x_ref[...] + gate1_ref[...] * mix
        y = x1 * lax.rsqrt(jnp.mean(x1 * x1, axis=-1, keepdims=True) + EPS) * gffn_ref[...]
        h2 = y * (1.0 + scale2_ref[...]) + shift2_ref[...]
        h2_ref[...] = _pack_bf16_pairs(h2)
        h2b = h2.astype(BF16)
        yield
        gu = jnp.dot(h2b, wsgu_ref[...], preferred_element_type=F32)
        yield
        shared = _bdot(_silu(gu[:, :D_SHARED]) * gu[:, D_SHARED:], wsd_ref[...])
        xs_ref[...] = x1 + gate2_ref[...] * shared
        yield
        logits_t = lax.dot_general(wrt_ref[...], h2b, NT_DIMS, preferred_element_type=F32)
        scores_s[...] = _sigmoid(logits_t)

    def first_argmax(vals, rows):
        m = jnp.max(vals, axis=0, keepdims=True)
        return m, jnp.min(jnp.where(vals == m, rows, big), axis=0, keepdims=True)

    def route(scores):
        biased = scores + ebias_ref[...]
        erow = lax.broadcasted_iota(I32, (N_EXPERTS, tm), 0)
        group_scores = []
        for g in range(N_GROUPS):
            vals = biased[g * GROUP_SIZE:(g + 1) * GROUP_SIZE, :]
            rows = lax.broadcasted_iota(I32, (GROUP_SIZE, tm), 0) + g * GROUP_SIZE
            m1, i1 = first_argmax(vals, rows)
            m2 = jnp.max(jnp.where(rows == i1, neg, vals), axis=0, keepdims=True)
            group_scores.append(m1 + m2)
            yield
        group_sel = [jnp.zeros((1, tm), jnp.bool_)] * N_GROUPS
        for _ in range(TOPK_GROUPS):
            best = functools.reduce(jnp.maximum, group_scores)
            gi = functools.reduce(
                jnp.minimum, [jnp.where(group_scores[g] == best, jnp.int32(g), big) for g in range(N_GROUPS)])
            for g in range(N_GROUPS):
                hit = gi == g
                group_sel[g] = jnp.logical_or(group_sel[g], hit)
                group_scores[g] = jnp.where(hit, neg, group_scores[g])
        cand = jnp.concatenate(
            [jnp.where(group_sel[g], biased[g * GROUP_SIZE:(g + 1) * GROUP_SIZE, :], neg)
             for g in range(N_GROUPS)], axis=0)
        yield
        w_rows, hits = [], []
        picked = jnp.zeros((N_EXPERTS, tm), F32)
        for k in range(TOP_K):
            _, ei = first_argmax(cand, erow)
            hit = erow == ei
            eidx_ref[k:k + 1, :] = ei
            w_rows.append(jnp.sum(jnp.where(hit, scores, 0.0), axis=0, keepdims=True))
            hits.append(hit)
            picked = jnp.where(hit, 1.0, picked)
            cand = jnp.where(hit, neg, cand)
            yield
        w_sum = functools.reduce(jnp.add, w_rows)
        for k in range(TOP_K):
            ew_ref[k:k + 1, :] = w_rows[k] / w_sum * ROUTE_SCALE
        yield
        trow = lax.broadcasted_iota(I32, (tm, tm), 0)
        tcol = lax.broadcasted_iota(I32, (tm, tm), 1)
        before = (trow < tcol).astype(BF16)
        prefix = jnp.dot(picked.astype(BF16), before, preferred_element_type=F32) + cnt_ref[:, 0:1]
        for k in range(TOP_K):
            rank_ref[k:k + 1, :] = jnp.sum(jnp.where(hits[k], prefix, 0.0), axis=0, keepdims=True).astype(I32)
            yield
        counted = jnp.where(i > 0, jnp.sum(picked, axis=1, keepdims=True), 0.0)
        cnt_ref[...] = cnt_ref[...] + counted

    jobs = [route(scores_s[...]), project()]
    shares = [4, 1]
    while jobs:
        for job, share in list(zip(jobs, shares)):
            for _ in range(share):
                if next(job, "done") == "done":
                    k = jobs.index(job)
                    del jobs[k], shares[k]
                    break


def _post(acta, actb, sga, sgb, x2d, gate1, scale2, shift2, gate2, cnt0, g_ffn, w, ebias_col,
          *, tm, tiles_per_mod):
    t = x2d.shape[0]
    n_tiles = t // tm
    tile = lambda i: jnp.minimum(i, n_tiles - 1)
    routed_tile = lambda i: jnp.maximum(i - 1, 0)
    row = lambda c: pl.BlockSpec((tm, c), lambda i: (tile(i), 0))
    col = lambda r: pl.BlockSpec((r, tm), lambda i: (0, routed_tile(i)))
    mod = lambda m: pl.BlockSpec((None,) + m.shape[1:], lambda i: (tile(i) // tiles_per_mod, 0, 0))
    mods = [gate1, scale2, shift2, gate2]
    consts = [cnt0, g_ffn, w["pa"], w["pb"], w["o"], w["router_t"], ebias_col, w["s_gu"], w["s_down"]]
    return pl.pallas_call(
        functools.partial(_post_kernel, tm=tm),
        grid=(n_tiles + 1,),
        in_specs=[row(CONV_CH), row(DN_WIDTH), row(D_MODEL), row(D_MODEL), row(D_MODEL)]
                 + [mod(m) for m in mods] + [_const_spec(a) for a in consts],
        out_specs=[row(D_MODEL), row(D_MODEL // 2), col(TOP_K), col(TOP_K), col(TOP_K), _const_spec(cnt0)],
        scratch_shapes=[pltpu.VMEM((N_EXPERTS, tm), F32)],
        out_shape=[jax.ShapeDtypeStruct((t, D_MODEL), F32), jax.ShapeDtypeStruct((t, D_MODEL // 2), U32),
                   jax.ShapeDtypeStruct((TOP_K, t), I32), jax.ShapeDtypeStruct((TOP_K, t), F32),
                   jax.ShapeDtypeStruct((TOP_K, t), I32), jax.ShapeDtypeStruct(cnt0.shape, F32)],
        compiler_params=_cparams(),
        name="post_mixer",
    )(acta, actb, sga, sgb, x2d, *mods, *consts)


def _dest_kernel(start_ref, eidx_ref, rank_ref, o_ref, *, width):
    step = SUBLANES * LANES
    for c0 in range(0, width, step):
        c1 = min(c0 + step, width)
        e = eidx_ref[:, c0:c1]
        base = lax.fori_loop(0, N_EXPERTS, lambda x, acc: jnp.where(e == x, start_ref[x], acc),
                             jnp.zeros(e.shape, I32), unroll=8)
        o_ref[:, c0:c1] = base + rank_ref[:, c0:c1]


def _dest(cnt_start, e_idx, rank):
    k, t = e_idx.shape
    full = pl.BlockSpec((k, t), lambda i, st: (0, 0))
    return pl.pallas_call(
        functools.partial(_dest_kernel, width=t),
        grid_spec=pltpu.PrefetchScalarGridSpec(
            num_scalar_prefetch=1, grid=(1,), in_specs=[full, full], out_specs=full),
        out_shape=jax.ShapeDtypeStruct((k, t), I32),
        compiler_params=_cparams(),
        name="moe_dest",
    )(cnt_start, e_idx, rank)


def _sc_mesh():
    return plsc.VectorSubcoreMesh(core_axis_name="core", subcore_axis_name="subcore")


def _sc_pipeline(body, n_steps, in_specs, out_specs):
    return pltpu.emit_pipeline(body, grid=(n_steps,), in_specs=in_specs, out_specs=out_specs,
                               core_axis_name=("core", "subcore"), dimension_semantics=(pltpu.PARALLEL,))


def _sc_scatter_rows(x, half, dest_flat, n_rows):
    t = x.shape[0]
    tiles = t // SC_WINDOW
    n_steps = dest_flat.shape[1] // SC_WINDOW

    @functools.partial(pl.kernel, mesh=_sc_mesh(), scratch_types=[],
                       out_type=jax.ShapeDtypeStruct((n_rows, QUARTER_WORDS), x.dtype))
    def scatter(x_hbm, i_hbm, o_hbm):
        def body(x_vmem, i_vmem):
            pltpu.sync_copy(x_vmem, o_hbm.at[i_vmem.at[0]])

        _sc_pipeline(body, n_steps,
                     [pl.BlockSpec((SC_WINDOW, QUARTER_WORDS), lambda i: (i % tiles, half)),
                      pl.BlockSpec((1, SC_WINDOW), lambda i: (0, i))], [])(x_hbm, i_hbm)

    return scatter(x, dest_flat)


def _sc_gather_rows(table, idx_flat):
    n = idx_flat.shape[1]

    @functools.partial(pl.kernel, mesh=_sc_mesh(), scratch_types=[],
                       out_type=jax.ShapeDtypeStruct((n, table.shape[1]), table.dtype))
    def gather(t_hbm, i_hbm, o_hbm):
        def body(i_vmem, o_vmem):
            pltpu.sync_copy(t_hbm.at[i_vmem.at[0]], o_vmem)

        _sc_pipeline(body, n // SC_WINDOW,
                     [pl.BlockSpec((1, SC_WINDOW), lambda i: (0, i))],
                     [pl.BlockSpec((SC_WINDOW, table.shape[1]), lambda i: (i, 0))])(i_hbm, o_hbm)

    return gather(table, idx_flat)


def _experts_kernel(sblk_ref, sexp_ref, slo_ref, shi_ref, snext_ref, first_ref,
                    xa_ref, xb_ref, wg_hbm, wu_hbm, wd_hbm, ya_ref, yb_ref,
                    wg_buf, wu_buf, wd_buf, wgu_s, wd_s, state_ref, sems, *, bm):
    p = pl.program_id(0)
    lo = slo_ref[p]
    expert = sexp_ref[p]

    def weight_copies(e, slot):
        return [pltpu.make_async_copy(src.at[e], dst.at[slot], sems.at[slot])
                for src, dst in ((wg_hbm, wg_buf), (wu_hbm, wu_buf), (wd_hbm, wd_buf))]

    @pl.when(p == 0)
    def _():
        state_ref[0] = -1
        state_ref[1] = 0
        for c in weight_copies(first_ref[0], 0):
            c.start()

    @pl.when(shi_ref[p] > lo)
    def _():
        @pl.when(state_ref[0] != expert)
        def _():
            slot = state_ref[1]
            for c in weight_copies(expert, slot):
                c.wait()
            wgu_s[:, :D_EXPERT] = wg_buf[slot].astype(BF16)
            wgu_s[:, D_EXPERT:] = wu_buf[slot].astype(BF16)
            wd_s[...] = wd_buf[slot].astype(BF16)
            state_ref[0] = expert
            state_ref[1] = 1 - slot
            upcoming = snext_ref[p]

            @pl.when(upcoming >= 0)
            def _():
                for c in weight_copies(upcoming, 1 - slot):
                    c.start()

        q = QUARTER_WORDS
        sub = bm // EXPERT_SUBBLOCKS
        first_sub = lo // sub
        last_sub = (shi_ref[p] - 1) // sub

        def run(r0, r1):
            quarters = {}
            quarters[0], quarters[2] = _unpack_bf16_pairs(xa_ref[r0:r1, :])
            quarters[1], quarters[3] = _unpack_bf16_pairs(xb_ref[r0:r1, :])
            gu = functools.reduce(jnp.add, [
                jnp.dot(quarters[c], wgu_s[c * q:(c + 1) * q, :], preferred_element_type=F32) for c in range(4)])
            hb = (_silu(gu[:, :D_EXPERT]) * gu[:, D_EXPERT:]).astype(BF16)
            y = _pack_bf16_pairs(jnp.dot(hb, wd_s[...], preferred_element_type=F32))

            @pl.when(lo <= r0)
            def _():
                ya_ref[r0:r1, :] = y[:, :q]
                yb_ref[r0:r1, :] = y[:, q:]

            @pl.when(lo > r0)
            def _():
                keep = lax.broadcasted_iota(I32, (r1 - r0, 1), 0) + r0 >= lo
                ya_ref[r0:r1, :] = jnp.where(keep, y[:, :q], ya_ref[r0:r1, :])
                yb_ref[r0:r1, :] = jnp.where(keep, y[:, q:], yb_ref[r0:r1, :])

        for a in range(EXPERT_SUBBLOCKS):
            for b in range(a, EXPERT_SUBBLOCKS):
                pl.when(jnp.logical_and(first_sub == a, last_sub == b))(
                    functools.partial(run, a * sub, (b + 1) * sub))


def _experts(xs_a, xs_b, w_gate, w_up, w_down, seg_blk, seg_exp, seg_lo, seg_hi, seg_next, first_exp, *, bm):
    n_rows = xs_a.shape[0]
    n_seg = seg_blk.shape[0]
    ring = 2
    hbm = pl.BlockSpec(memory_space=pl.ANY)
    rows = pl.BlockSpec((bm, QUARTER_WORDS), lambda p, sb, *_: (sb[p], 0))
    grid_spec = pltpu.PrefetchScalarGridSpec(
        num_scalar_prefetch=6,
        grid=(n_seg,),
        in_specs=[rows, rows, hbm, hbm, hbm],
        out_specs=[rows, rows],
        scratch_shapes=[pltpu.VMEM((ring, D_MODEL, D_EXPERT), F32), pltpu.VMEM((ring, D_MODEL, D_EXPERT), F32),
                        pltpu.VMEM((ring, D_EXPERT, D_MODEL), F32),
                        pltpu.VMEM((D_MODEL, 2 * D_EXPERT), BF16), pltpu.VMEM((D_EXPERT, D_MODEL), BF16),
                        pltpu.SMEM((2,), I32), pltpu.SemaphoreType.DMA((ring,))],
    )
    return pl.pallas_call(
        functools.partial(_experts_kernel, bm=bm),
        grid_spec=grid_spec,
        out_shape=[jax.ShapeDtypeStruct((n_rows, QUARTER_WORDS), U32)] * 2,
        compiler_params=_cparams(),
        name="moe_experts",
    )(seg_blk, seg_exp, seg_lo, seg_hi, seg_next, first_exp, xs_a, xs_b, w_gate, w_up, w_down)


def _final_kernel(*refs, tm, n_prompt_tiles, has_sample, has_prev):
    refs = list(refs)
    ew_ref, pa_ref, pb_ref, xsp_ref = refs[:4]
    del refs[:4]
    xss_ref = refs.pop(0) if has_sample else None
    g2p_ref = refs.pop(0)
    g2s_ref = refs.pop(0) if has_sample else None
    gfin_ref = refs.pop(0)
    if has_prev:
        refs.pop(0)
    op_ref = refs.pop(0)
    os_ref = refs.pop(0) if has_sample else None
    ew = ew_ref[...]
    quarters = [jnp.zeros((tm, QUARTER_WORDS), F32)] * 4
    for k in range(TOP_K):
        w_k = ew[:, k:k + 1]
        a_low, a_high = _unpack_bf16_pairs(pa_ref[k])
        b_low, b_high = _unpack_bf16_pairs(pb_ref[k])
        for c, part in enumerate((a_low, b_low, a_high, b_high)):
            quarters[c] = quarters[c] + w_k * part.astype(F32)
    routed = jnp.concatenate(quarters, axis=1)

    def finish(xs_ref, gate2_ref, o_ref):
        x2 = xs_ref[...] + gate2_ref[...] * routed
        o_ref[...] = x2 * lax.rsqrt(jnp.mean(x2 * x2, axis=-1, keepdims=True) + EPS) * gfin_ref[...]

    if not has_sample:
        finish(xsp_ref, g2p_ref, op_ref)
    else:
        is_prompt = pl.program_id(0) < n_prompt_tiles
        pl.when(is_prompt)(lambda: finish(xsp_ref, g2p_ref, op_ref))
        pl.when(jnp.logical_not(is_prompt))(lambda: finish(xss_ref, g2s_ref, os_ref))


def _final(ew_part, planes_a, planes_b, xs_p, gate2_p, g_final, *, tm, tiles_per_batch, tile_lo, n_prompt_tiles,
           sample=None, y_prev=None):
    prompt_tile = lambda i: tile_lo + jnp.minimum(i, n_prompt_tiles - 1)
    sample_tile = lambda i: jnp.maximum(i - n_prompt_tiles, 0)
    prompt_rows = pl.BlockSpec((tm, D_MODEL), lambda i: (prompt_tile(i), 0))
    sample_rows = pl.BlockSpec((tm, D_MODEL), lambda i: (sample_tile(i), 0))
    planes = pl.BlockSpec((TOP_K, tm, QUARTER_WORDS), lambda i: (0, i, 0))
    has_sample = sample is not None
    has_prev = y_prev is not None
    n_tiles = n_prompt_tiles + (sample[0].shape[0] // tm if has_sample else 0)
    args = [ew_part, planes_a, planes_b, xs_p]
    in_specs = [pl.BlockSpec((tm, TOP_K), lambda i: (i, 0)), planes, planes, prompt_rows]
    if has_sample:
        args.append(sample[0])
        in_specs.append(sample_rows)
    args.append(gate2_p)
    in_specs.append(pl.BlockSpec((None, 1, D_MODEL), lambda i: (prompt_tile(i) // tiles_per_batch, 0, 0)))
    if has_sample:
        args.append(sample[1])
        in_specs.append(pl.BlockSpec((None, tm, D_MODEL), lambda i: (sample_tile(i), 0, 0)))
    args.append(g_final)
    in_specs.append(_const_spec(g_final))
    if has_prev:
        args.append(y_prev)
        in_specs.append(pl.BlockSpec(memory_space=pl.ANY))
    out_specs = [prompt_rows] + ([sample_rows] if has_sample else [])
    out_shape = [jax.ShapeDtypeStruct(xs_p.shape, F32)] + (
        [jax.ShapeDtypeStruct(sample[0].shape, F32)] if has_sample else [])
    outs = pl.pallas_call(
        functools.partial(_final_kernel, tm=tm, n_prompt_tiles=n_prompt_tiles, has_sample=has_sample,
                          has_prev=has_prev),
        grid=(n_tiles,),
        in_specs=in_specs,
        out_specs=out_specs,
        out_shape=out_shape,
        input_output_aliases={len(args) - 1: 0} if has_prev else {},
        compiler_params=_cparams(),
        name="moe_combine_final",
    )(*args)
    return outs if has_sample else outs[0]


def _pick_tile(n, preferred):
    t = min(n, preferred)
    assert n % t == 0, (n, t)
    return t


def kernel(x_prompt, x_sample, c_prompt, c_sample, state_conv, state_short_conv, state_delta, w_ada, b_ada, g_mix, w_in, conv_dw, conv_dw_b, conv_ln_g, conv_ln_b, w_pa, short_conv_w, a_log, dt_bias, dn_norm_g, w_pb, w_o, g_ffn, w_router, e_bias, w_gate, w_up, w_down, ws_gate, ws_up, ws_down, g_final):
    nb, seq, d = x_prompt.shape
    ns = x_sample.shape[0]
    assert d == D_MODEL and w_ada.shape[0] == 1 and x_sample.shape[1] == 1
    assert seq % CHUNK == 0 and seq >= CONV_HALO and ns % SUBLANES == 0
    tp = nb * seq
    row1 = lambda a: a.reshape(1, -1)

    wi = w_in[0].astype(BF16)
    o_q = 2 * CONV_CH
    o_z = o_q + 3 * DN_WIDTH
    o_a = o_z + DN_WIDTH
    o_ga = o_a + 2 * N_HEADS
    w_proj = {
        "glu": wi[:, :o_q], "qkv": wi[:, o_q:o_z], "z": wi[:, o_z:o_a],
        "ab": jnp.pad(wi[:, o_a:o_ga], ((0, 0), (0, LANES - 2 * N_HEADS))),
        "ga": wi[:, o_ga:o_ga + D_MODEL], "gb": wi[:, o_ga + D_MODEL:],
    }
    nea = jnp.pad(-jnp.exp(a_log[0].astype(F32)), (0, LANES - N_HEADS)).reshape(1, LANES)
    dtb = jnp.pad(dt_bias[0].astype(F32), (0, LANES - N_HEADS)).reshape(1, LANES)
    w_post = {
        "pa": w_pa[0].astype(BF16), "pb": w_pb[0].astype(BF16), "o": w_o[0].astype(BF16),
        "router_t": w_router[0].T.astype(BF16),
        "s_gu": jnp.concatenate([ws_gate[0], ws_up[0]], axis=1).astype(BF16),
        "s_down": ws_down[0].astype(BF16),
    }
    ebias_col = e_bias[0].astype(F32).reshape(N_EXPERTS, 1)
    g_mix1, g_ffn1, g_fin1 = row1(g_mix[0]), row1(g_ffn[0]), row1(g_final)
    cdw, cdb, clg, clb = conv_dw[0], row1(conv_dw_b[0]), row1(conv_ln_g[0]), row1(conv_ln_b[0])
    scw, dng = short_conv_w[0], row1(dn_norm_g[0])

    ada = _ada(jnp.concatenate([c_prompt, c_sample], axis=0), w_ada[0].astype(BF16), b_ada[0])
    ada = ada.reshape(nb + ns, 6, D_MODEL)
    mod_p = [ada[:nb, m].reshape(nb, 1, D_MODEL) for m in range(6)]
    mod_s = [ada[nb:, m].reshape(1, ns, D_MODEL) for m in range(6)]
    shift1, scale1, gate1, shift2, scale2, gate2 = range(6)

    tm_p = _pick_tile(seq, 256)
    tpm_p = seq // tm_p
    xp = x_prompt.reshape(tp, D_MODEL)
    tm_in = _pick_tile(seq, 512)
    u_p, qkv_p, zs_p, ab_p, sga_p, sgb_p = _in_proj(
        xp, mod_p[scale1], mod_p[shift1], g_mix1, w_proj, nea, dtb, tm=tm_in, tiles_per_mod=seq // tm_in)
    tl = _pick_tile(seq, 256)
    acta_p = _conv_branch(u_p, jnp.zeros((nb, CONV_HALO, CONV_CH), F32), cdw, cdb, clg, clb,
                          tl=tl, tiles_per_batch=seq // tl)
    tc = _pick_tile(seq, 4 * CHUNK)
    actb_p, s_new_p = _delta_branch(
        qkv_p, jnp.zeros((nb, SC_HALO, 3 * DN_WIDTH), F32),
        jnp.zeros((nb, N_HEADS, HEAD_DIM, HEAD_DIM), F32), scw, ab_p, zs_p, dng,
        tc=tc, tiles_per_batch=seq // tc)

    xs_ = x_sample.reshape(ns, D_MODEL)
    u_s, qkv_s, zs_s, ab_s, sga_s, sgb_s = _in_proj(
        xs_, mod_s[scale1], mod_s[shift1], g_mix1, w_proj, nea, dtb, tm=ns, tiles_per_mod=1)
    bt = SUBLANES
    acta_s, nch_s, qkvp_s, nsh_s = _step_front(
        u_s, qkv_s, state_conv[0].reshape(ns, -1), state_short_conv[0].reshape(ns, -1),
        cdw, cdb, clg, clb, scw, bt=bt)
    actb_s, s_new_s = _step_delta(qkvp_s, ab_s, zs_s, state_delta[0].astype(F32), dng, bt=bt)

    cnt0 = jnp.zeros((N_EXPERTS, LANES), F32)
    xsh_p, h2_p, eidx_p, ew_p, rank_p, cnt_p = _post(
        acta_p, actb_p, sga_p, sgb_p, xp, mod_p[gate1], mod_p[scale2], mod_p[shift2], mod_p[gate2],
        cnt0, g_ffn1, w_post, ebias_col, tm=tm_p, tiles_per_mod=tpm_p)
    xsh_s, h2_s, eidx_s, ew_s, rank_s, cnt_all = _post(
        acta_s, actb_s, sga_s, sgb_s, xs_, mod_s[gate1], mod_s[scale2], mod_s[shift2], mod_s[gate2],
        cnt_p, g_ffn1, w_post, ebias_col, tm=ns, tiles_per_mod=1)

    bm = 512
    tmd = LANES
    t_all = tp + ns
    n_rows = t_all * TOP_K
    assert seq % tmd == 0 and ns % tmd == 0 and n_rows % bm == 0 and t_all % SC_WINDOW == 0
    n_blocks = n_rows // bm
    counts = cnt_all[:, 0].astype(I32)
    cnt_end = jnp.cumsum(counts)
    cnt_start = cnt_end - counts
    e_idx = jnp.concatenate([eidx_p, eidx_s], axis=1)
    rank = jnp.concatenate([rank_p, rank_s], axis=1)
    dest = _dest(cnt_start, e_idx, rank)
    dest_flat = dest.reshape(1, n_rows)
    seg_start = jnp.sort(jnp.concatenate([jnp.arange(n_blocks, dtype=I32) * bm, cnt_start]))
    seg_end = jnp.concatenate([seg_start[1:], jnp.full((1,), n_rows, I32)])
    seg_blk = jnp.minimum(seg_start // bm, n_blocks - 1)
    seg_lo = seg_start - seg_blk * bm
    seg_hi = seg_end - seg_blk * bm
    seg_exp = jnp.sum((cnt_end[None, :] <= seg_start[:, None]).astype(I32), axis=1)
    seg_exp = jnp.minimum(seg_exp, N_EXPERTS - 1)
    seg_exp = lax.cummax(jnp.where(seg_hi > seg_lo, seg_exp, 0))
    none = jnp.int32(N_EXPERTS)
    in_use_from = lax.cummin(jnp.where(counts > 0, jnp.arange(N_EXPERTS, dtype=I32), none), reverse=True)
    next_in_use = jnp.concatenate([in_use_from[1:], jnp.full((1,), none, I32)])
    next_in_use = jnp.where(next_in_use == none, -1, next_in_use)
    seg_next = next_in_use[seg_exp]
    first_exp = jnp.minimum(in_use_from[:1], N_EXPERTS - 1)

    h2_all = jnp.concatenate([h2_p, h2_s], axis=0)
    xs_a, xs_b = (_sc_scatter_rows(h2_all, half, dest_flat, n_rows) for half in range(2))
    ys_a, ys_b = _experts(xs_a, xs_b, w_gate[0], w_up[0], w_down[0], seg_blk, seg_exp, seg_lo, seg_hi, seg_next,
                          first_exp, bm=bm)

    ew_tok = jnp.concatenate([ew_p, ew_s], axis=1).T
    n_prompt_tiles = tp // tmd
    tiles_1 = n_prompt_tiles // 2
    t_1 = tiles_1 * tmd

    def gathered(lo, hi):
        idx = dest[:, lo:hi].reshape(1, -1)
        return [_sc_gather_rows(ys, idx).reshape(TOP_K, hi - lo, QUARTER_WORDS) for ys in (ys_a, ys_b)]

    final = functools.partial(_final, xs_p=xsh_p, gate2_p=mod_p[gate2], g_final=g_fin1, tm=tmd,
                              tiles_per_batch=seq // tmd)
    y_p = final(ew_tok[:t_1], *gathered(0, t_1), tile_lo=0, n_prompt_tiles=tiles_1)
    y_p, y_s = final(ew_tok[t_1:], *gathered(t_1, t_all), tile_lo=tiles_1, n_prompt_tiles=n_prompt_tiles - tiles_1,
                     sample=(xsh_s, mod_s[gate2].reshape(ns // tmd, tmd, D_MODEL)), y_prev=y_p)

    hist_rows = CONV_WIDTH - 1
    sc_rows = SHORT_CONV - 1
    y_prompt = y_p.reshape(nb, seq, D_MODEL)
    y_sample = y_s.reshape(ns, 1, D_MODEL)
    new_conv_prompt = u_p.reshape(nb, seq, CONV_CH)[:, seq - hist_rows:][None]
    new_sc_prompt = qkv_p.reshape(nb, seq, 3 * DN_WIDTH)[:, seq - sc_rows:][None]
    new_conv_sample = nch_s.reshape(ns, hist_rows, CONV_CH)[None]
    new_sc_sample = nsh_s.reshape(ns, sc_rows, 3 * DN_WIDTH)[None]
    return (y_prompt, y_sample, new_conv_prompt, new_sc_prompt, s_new_p[None],
            new_conv_sample, new_sc_sample, s_new_s[None])
```

```python
import functools

import jax
import jax.numpy as jnp
from jax import lax
from jax.experimental import pallas as pl
from jax.experimental.pallas import tpu as pltpu
from jax.experimental.pallas import tpu_sc as plsc

F32 = jnp.float32
BF16 = jnp.bfloat16
I32 = jnp.int32
U32 = jnp.uint32

EPS = 1e-6
D_MODEL = 1024
CONV_CH = 512
CONV_WIDTH = 31
N_HEADS = 4
HEAD_DIM = 128
DN_WIDTH = N_HEADS * HEAD_DIM
SHORT_CONV = 4
CHUNK = 64
CHUNK_SHIFT = CHUNK.bit_length() - 1
assert 1 << CHUNK_SHIFT == CHUNK
SUB = 16
SUB_SHIFT = SUB.bit_length() - 1
assert 1 << SUB_SHIFT == SUB and CHUNK % SUB == 0
N_EXPERTS = 256
TOP_K = 8
N_GROUPS = 8
GROUP_SIZE = N_EXPERTS // N_GROUPS
TOPK_GROUPS = 4
D_EXPERT = 256
D_SHARED = 256
ROUTE_SCALE = 2.5

LANES = 128
SUBLANES = 8
CONV_HALO = 32
SC_HALO = 8
VMEM_LIMIT = 48 * 1024 * 1024
EXPERT_SUBBLOCKS = 8
EXPERT_SUB_ROWS = 128
SC_WINDOW = 128
QUARTER_WORDS = D_MODEL // 4

NT_DIMS = (((1,), (1,)), ((), ()))
TN_DIMS = (((0,), (0,)), ((), ()))


def _cparams(n_grid_dims=1):
    return pltpu.CompilerParams(
        dimension_semantics=("arbitrary",) * n_grid_dims,
        vmem_limit_bytes=VMEM_LIMIT)


def _sigmoid(x):
    return jax.nn.sigmoid(x)


def _silu(x):
    return x * jax.nn.sigmoid(x)


def _bdot(a, b):
    return jnp.dot(a.astype(BF16), b.astype(BF16), preferred_element_type=F32)


def _pack_bf16_pairs(x):
    half = x.shape[1] // 2
    bits = lax.bitcast_convert_type(x.astype(BF16).astype(F32), U32)
    return bits[:, half:] | (bits[:, :half] >> 16)


def _unpack_bf16_pairs(w):
    low = lax.bitcast_convert_type(w << 16, F32).astype(BF16)
    high = lax.bitcast_convert_type(w & jnp.uint32(0xFFFF0000), F32).astype(BF16)
    return low, high


def _dot3(a, b):
    a_hi = a.astype(BF16)
    b_hi = b.astype(BF16)
    a_lo = (a - a_hi.astype(F32)).astype(BF16)
    b_lo = (b - b_hi.astype(F32)).astype(BF16)
    dot = lambda x, y: jnp.dot(x, y, preferred_element_type=F32)
    return dot(a_hi, b_hi) + (dot(a_hi, b_lo) + dot(a_lo, b_hi))


def _dot_exact_lhs(a_bf, b):
    hi = b.astype(BF16)
    r1 = b - hi.astype(F32)
    mid = r1.astype(BF16)
    low = (r1 - mid.astype(F32)).astype(BF16)
    dot = lambda p: jnp.dot(a_bf, p, preferred_element_type=F32)
    return dot(hi) + dot(mid) + dot(low)


def _ada_kernel(c_ref, w_ref, b_ref, o_ref):
    o_ref[...] = _bdot(_silu(c_ref[...]), w_ref[...]) + b_ref[...]


def _ada(c_all, w_ada_bf, b_ada):
    n = c_all.shape[0]
    width = w_ada_bf.shape[1]
    tn = 1536
    return pl.pallas_call(
        _ada_kernel,
        grid=(width // tn,),
        in_specs=[pl.BlockSpec((n, D_MODEL), lambda j: (0, 0)),
                  pl.BlockSpec((D_MODEL, tn), lambda j: (0, j)),
                  pl.BlockSpec((1, tn), lambda j: (0, j))],
        out_specs=pl.BlockSpec((n, tn), lambda j: (0, j)),
        out_shape=jax.ShapeDtypeStruct((n, width), F32),
        compiler_params=_cparams(),
        name="ada",
    )(c_all, w_ada_bf, b_ada.reshape(1, width))


def _in_proj_kernel(x_ref, scale_ref, shift_ref, g_ref, wglu_ref, wqkv_ref, wz_ref, wab_ref,
                    wga_ref, wgb_ref, nea_ref, dtb_ref,
                    u_ref, qkv_ref, zs_ref, ab_ref, sga_ref, sgb_ref):
    x = x_ref[...]
    y = x * lax.rsqrt(jnp.mean(x * x, axis=-1, keepdims=True) + EPS) * g_ref[...]
    h = (y * (1.0 + scale_ref[...]) + shift_ref[...]).astype(BF16)
    glu = jnp.dot(h, wglu_ref[...], preferred_element_type=F32)
    u_ref[...] = glu[:, :CONV_CH] * _sigmoid(glu[:, CONV_CH:])
    qkv_ref[...] = jnp.dot(h, wqkv_ref[...], preferred_element_type=F32)
    zs_ref[...] = _silu(jnp.dot(h, wz_ref[...], preferred_element_type=F32))
    ab = jnp.dot(h, wab_ref[...], preferred_element_type=F32)
    sp = ab + dtb_ref[...]
    softplus = jnp.maximum(sp, 0.0) + jnp.log1p(jnp.exp(-jnp.abs(sp)))
    lane = lax.broadcasted_iota(I32, ab.shape, 1)
    ab_ref[...] = jnp.where(lane < N_HEADS, nea_ref[...] * softplus, _sigmoid(ab))
    sga_ref[...] = _sigmoid(jnp.dot(h, wga_ref[...], preferred_element_type=F32))
    sgb_ref[...] = _sigmoid(jnp.dot(h, wgb_ref[...], preferred_element_type=F32))


def _mod_spec(mod, tiles_per_mod):
    return pl.BlockSpec((None,) + mod.shape[1:], lambda i: (i // tiles_per_mod, 0, 0))


def _const_spec(a):
    nd = a.ndim
    return pl.BlockSpec(a.shape, lambda i: (0,) * nd)


def _in_proj(x2d, scale, shift, g_mix, w, nea, dtb, *, tm, tiles_per_mod):
    t = x2d.shape[0]
    row = lambda c: pl.BlockSpec((tm, c), lambda i: (i, 0))
    consts = [g_mix, w["glu"], w["qkv"], w["z"], w["ab"], w["ga"], w["gb"], nea, dtb]
    widths = [CONV_CH, 3 * DN_WIDTH, DN_WIDTH, LANES, D_MODEL, D_MODEL]
    return pl.pallas_call(
        _in_proj_kernel,
        grid=(t // tm,),
        in_specs=[row(D_MODEL), _mod_spec(scale, tiles_per_mod), _mod_spec(shift, tiles_per_mod)]
                 + [_const_spec(a) for a in consts],
        out_specs=[row(c) for c in widths],
        out_shape=[jax.ShapeDtypeStruct((t, c), F32) for c in widths],
        compiler_params=_cparams(),
        name="in_proj",
    )(x2d, scale, shift, *consts)


def _layer_norm_swish(y, g, b):
    mu = jnp.mean(y, axis=-1, keepdims=True)
    yc = y - mu
    var = jnp.mean(yc * yc, axis=-1, keepdims=True)
    return _silu(yc * lax.rsqrt(var + EPS) * g + b)


def _conv_kernel(u_ref, halo_ref, hist_ref, w_ref, b_ref, lg_ref, lb_ref, o_ref, win_ref, shift_ref,
                 *, tl, tiles_per_batch, rows_per_pass):
    i = pl.program_id(0)
    first = (i % tiles_per_batch) == 0

    @pl.when(first)
    def _():
        win_ref[0:CONV_HALO, :] = hist_ref[...]

    @pl.when(jnp.logical_not(first))
    def _():
        win_ref[0:CONV_HALO, :] = halo_ref[...]

    win_ref[CONV_HALO:CONV_HALO + tl, :] = u_ref[...]
    tap0 = CONV_HALO - (CONV_WIDTH - 1)
    n_shifted = CONV_HALO + tl - SUBLANES
    for b in range(1, SUBLANES):
        shift_ref[b - 1] = win_ref[b:b + n_shifted, :]

    for r in range(tl // rows_per_pass):
        base = r * rows_per_pass
        acc = jnp.zeros((rows_per_pass, CONV_CH), F32)
        for j in range(CONV_WIDTH):
            a, b = divmod(tap0 + j, SUBLANES)
            start = base + a * SUBLANES
            rows = (win_ref[start:start + rows_per_pass, :] if b == 0
                    else shift_ref[b - 1, start:start + rows_per_pass, :])
            acc = acc + w_ref[j:j + 1, :] * rows
        act = _layer_norm_swish(acc + b_ref[...], lg_ref[...], lb_ref[...])
        o_ref[base:base + rows_per_pass, :] = act.astype(o_ref.dtype)


def _conv_branch(u2d, hist, conv_dw, conv_dw_b, ln_g, ln_b, *, tl, tiles_per_batch):
    t = u2d.shape[0]
    halo_blocks = tl // CONV_HALO
    consts = [conv_dw, conv_dw_b, ln_g, ln_b]
    return pl.pallas_call(
        functools.partial(_conv_kernel, tl=tl, tiles_per_batch=tiles_per_batch, rows_per_pass=64),
        grid=(t // tl,),
        in_specs=[pl.BlockSpec((tl, CONV_CH), lambda i: (i, 0)),
                  pl.BlockSpec((CONV_HALO, CONV_CH), lambda i: (jnp.maximum(i * halo_blocks - 1, 0), 0)),
                  pl.BlockSpec((None, CONV_HALO, CONV_CH), lambda i: (i // tiles_per_batch, 0, 0))]
                 + [_const_spec(a) for a in consts],
        out_specs=pl.BlockSpec((tl, CONV_CH), lambda i: (i, 0)),
        out_shape=jax.ShapeDtypeStruct((t, CONV_CH), BF16),
        scratch_shapes=[pltpu.VMEM((CONV_HALO + tl, CONV_CH), F32),
                        pltpu.VMEM((SUBLANES - 1, CONV_HALO + tl - SUBLANES, CONV_CH), F32)],
        compiler_params=_cparams(),
        name="conv_branch",
    )(u2d, u2d, hist, *consts)


def _l2n(x):
    return x * lax.rsqrt(jnp.sum(x * x, axis=-1, keepdims=True) + 1e-6)


def _gated_head_norm(o, g, zs):
    return o * lax.rsqrt(jnp.mean(o * o, axis=-1, keepdims=True) + EPS) * g * zs


def _delta_kernel(qkv_ref, halo_ref, hist_ref, scw_ref, ab_ref, zs_ref, s0_ref, dng_ref,
                  o_ref, s_ref, win_ref, c_ref, *, tc, tiles_per_batch):
    i = pl.program_id(0)
    first = (i % tiles_per_batch) == 0

    @pl.when(first)
    def _():
        win_ref[0:SC_HALO, :] = hist_ref[...]
        s_ref[...] = s0_ref[...]

    @pl.when(jnp.logical_not(first))
    def _():
        win_ref[0:SC_HALO, :] = halo_ref[...]

    win_ref[SC_HALO:SC_HALO + tc, :] = qkv_ref[...]
    tap0 = SC_HALO - (SHORT_CONV - 1)
    acc = jnp.zeros((tc, 3 * DN_WIDTH), F32)
    for j in range(SHORT_CONV):
        acc = acc + scw_ref[j:j + 1, :] * win_ref[tap0 + j:tap0 + j + tc, :]
    c_ref[...] = _silu(acc)

    row = lax.broadcasted_iota(I32, (tc, tc), 0)
    col = lax.broadcasted_iota(I32, (tc, tc), 1)
    same_chunk = lax.shift_right_logical(row, CHUNK_SHIFT) == lax.shift_right_logical(col, CHUNK_SHIFT)
    incl = jnp.logical_and(same_chunk, row >= col)
    row_sub = lax.shift_right_logical(row, SUB_SHIFT)
    col_sub = lax.shift_right_logical(col, SUB_SHIFT)
    strict_sub = jnp.logical_and(row_sub == col_sub, row > col)
    below_sub = jnp.logical_and(same_chunk, row_sub > col_sub)
    eye = (row == col).astype(F32)
    ab = ab_ref[...]
    gcum = _dot_exact_lhs(incl.astype(BF16), ab)
    gcum_t = gcum.T
    n_ch = tc // CHUNK

    heads = range(N_HEADS)
    q, k, gc, kk, a_qk, rhs, qd = [], [], [], [], [], [], []
    for h in heads:
        lo = h * HEAD_DIM
        q_h = _l2n(c_ref[:, lo:lo + HEAD_DIM]) * (HEAD_DIM ** -0.5)
        k_h = _l2n(c_ref[:, DN_WIDTH + lo:DN_WIDTH + lo + HEAD_DIM])
        v_h = c_ref[:, 2 * DN_WIDTH + lo:2 * DN_WIDTH + lo + HEAD_DIM]
        beta = ab[:, N_HEADS + h:N_HEADS + h + 1]
        gc_h = gcum[:, h:h + 1]
        gr = gcum_t[h:h + 1, :]
        decay = jnp.exp(jnp.where(incl, gc_h - gr, -jnp.inf))
        kb = k_h * beta
        kq = lax.dot_general(jnp.concatenate([kb, q_h], axis=0).astype(BF16), k_h.astype(BF16),
                             NT_DIMS, preferred_element_type=F32)
        e_gc = jnp.exp(gc_h)
        q.append(q_h)
        k.append(k_h)
        gc.append(gc_h)
        kk.append(kq[:tc] * decay)
        a_qk.append(kq[tc:] * decay)
        rhs.append(jnp.concatenate([v_h * beta, kb * e_gc], axis=1))
        qd.append(q_h * e_gc)

    pw = [jnp.where(strict_sub, kk[h], 0.0) for h in heads]
    d_inv = [eye - pw[h] for h in heads]
    for _ in range(SUB_SHIFT - 1):
        pw = [_bdot(pw[h], pw[h]) for h in heads]
        d_inv = [d_inv[h] + _bdot(d_inv[h], pw[h]) for h in heads]
    c_rhs = [_bdot(d_inv[h], rhs[h]) for h in heads]
    m_off = [_bdot(d_inv[h], jnp.where(below_sub, kk[h], 0.0)) for h in heads]
    sol = c_rhs
    for _ in range(CHUNK // SUB - 1):
        sol = [c_rhs[h] - _bdot(m_off[h], sol[h]) for h in heads]

    s = [s_ref[h] for h in heads]
    u_parts = [[] for _ in heads]
    o_parts = [[] for _ in heads]
    for c in range(n_ch):
        r0 = c * CHUNK
        for h in heads:
            u0_c = sol[h][r0:r0 + CHUNK, :HEAD_DIM]
            w_c = sol[h][r0:r0 + CHUNK, HEAD_DIM:]
            wq_s = _bdot(jnp.concatenate([w_c, qd[h][r0:r0 + CHUNK]], axis=0), s[h])
            u_c = u0_c - wq_s[:CHUNK]
            g_last = gcum[r0 + CHUNK - 1:r0 + CHUNK, h:h + 1]
            k_dec = k[h][r0:r0 + CHUNK] * jnp.exp(g_last - gc[h][r0:r0 + CHUNK])
            s[h] = jnp.exp(g_last) * s[h] + lax.dot_general(
                k_dec.astype(BF16), u_c.astype(BF16), TN_DIMS, preferred_element_type=F32)
            u_parts[h].append(u_c)
            o_parts[h].append(wq_s[CHUNK:])
    for h in heads:
        lo = h * HEAD_DIM
        s_ref[h] = s[h]
        o = jnp.concatenate(o_parts[h], axis=0) + _bdot(a_qk[h], jnp.concatenate(u_parts[h], axis=0))
        o_ref[:, lo:lo + HEAD_DIM] = _gated_head_norm(
            o, dng_ref[...], zs_ref[:, lo:lo + HEAD_DIM]).astype(o_ref.dtype)


def _delta_branch(qkv2d, hist, s0, scw, ab, zs, dng, *, tc, tiles_per_batch):
    t = qkv2d.shape[0]
    nb = s0.shape[0]
    halo_blocks = tc // SC_HALO
    wqkv = 3 * DN_WIDTH
    state_spec = pl.BlockSpec((None, N_HEADS, HEAD_DIM, HEAD_DIM), lambda i: (i // tiles_per_batch, 0, 0, 0))
    return pl.pallas_call(
        functools.partial(_delta_kernel, tc=tc, tiles_per_batch=tiles_per_batch),
        grid=(t // tc,),
        in_specs=[pl.BlockSpec((tc, wqkv), lambda i: (i, 0)),
                  pl.BlockSpec((SC_HALO, wqkv), lambda i: (jnp.maximum(i * halo_blocks - 1, 0), 0)),
                  pl.BlockSpec((None, SC_HALO, wqkv), lambda i: (i // tiles_per_batch, 0, 0)),
                  _const_spec(scw),
                  pl.BlockSpec((tc, LANES), lambda i: (i, 0)),
                  pl.BlockSpec((tc, DN_WIDTH), lambda i: (i, 0)),
                  state_spec,
                  _const_spec(dng)],
        out_specs=[pl.BlockSpec((tc, DN_WIDTH), lambda i: (i, 0)), state_spec],
        out_shape=[jax.ShapeDtypeStruct((t, DN_WIDTH), BF16),
                   jax.ShapeDtypeStruct((nb, N_HEADS, HEAD_DIM, HEAD_DIM), F32)],
        scratch_shapes=[pltpu.VMEM((SC_HALO + tc, wqkv), F32), pltpu.VMEM((tc, wqkv), F32)],
        compiler_params=_cparams(),
        name="delta_branch",
    )(qkv2d, qkv2d, hist, scw, ab, zs, s0, dng)


def _step_front_kernel(u_ref, qkv_ref, ch_ref, sh_ref, cw_ref, cb_ref, lg_ref, lb_ref, scw_ref,
                       acta_ref, nch_ref, qkvp_ref, nsh_ref):
    hist_rows = CONV_WIDTH - 1
    u = u_ref[...]
    acc = cw_ref[hist_rows:hist_rows + 1, :] * u
    for j in range(hist_rows):
        acc = acc + cw_ref[j:j + 1, :] * ch_ref[:, j * CONV_CH:(j + 1) * CONV_CH]
    acta_ref[...] = _layer_norm_swish(acc + cb_ref[...], lg_ref[...], lb_ref[...]).astype(acta_ref.dtype)
    nch_ref[:, :(hist_rows - 1) * CONV_CH] = ch_ref[:, CONV_CH:]
    nch_ref[:, (hist_rows - 1) * CONV_CH:] = u

    wq = 3 * DN_WIDTH
    sc_rows = SHORT_CONV - 1
    qkv = qkv_ref[...]
    acc = scw_ref[sc_rows:sc_rows + 1, :] * qkv
    for j in range(sc_rows):
        acc = acc + scw_ref[j:j + 1, :] * sh_ref[:, j * wq:(j + 1) * wq]
    c = _silu(acc)
    for h in range(N_HEADS):
        lo = h * HEAD_DIM
        qkvp_ref[:, lo:lo + HEAD_DIM] = _l2n(c[:, lo:lo + HEAD_DIM]) * (HEAD_DIM ** -0.5)
        qkvp_ref[:, DN_WIDTH + lo:DN_WIDTH + lo + HEAD_DIM] = _l2n(c[:, DN_WIDTH + lo:DN_WIDTH + lo + HEAD_DIM])
    qkvp_ref[:, 2 * DN_WIDTH:] = c[:, 2 * DN_WIDTH:]
    nsh_ref[:, :(sc_rows - 1) * wq] = sh_ref[:, wq:]
    nsh_ref[:, (sc_rows - 1) * wq:] = qkv


def _step_front(u, qkv, conv_hist_flat, sc_hist_flat, conv_dw, conv_dw_b, ln_g, ln_b, scw, *, bt):
    n = u.shape[0]
    wq = 3 * DN_WIDTH
    consts = [conv_dw, conv_dw_b, ln_g, ln_b, scw]
    row = lambda c: pl.BlockSpec((bt, c), lambda i: (i, 0))
    widths = [CONV_CH, conv_hist_flat.shape[1], wq, sc_hist_flat.shape[1]]
    dtypes = [BF16, F32, F32, F32]
    return pl.pallas_call(
        _step_front_kernel,
        grid=(n // bt,),
        in_specs=[row(CONV_CH), row(wq), row(widths[1]), row(widths[3])] + [_const_spec(a) for a in consts],
        out_specs=[row(c) for c in widths],
        out_shape=[jax.ShapeDtypeStruct((n, c), d) for c, d in zip(widths, dtypes)],
        compiler_params=_cparams(),
        name="step_front",
    )(u, qkv, conv_hist_flat, sc_hist_flat, *consts)


def _rows_to_cols(x):
    pad = jnp.zeros((LANES - x.shape[0], LANES), x.dtype)
    return jnp.concatenate([x, pad], axis=0).T


def _step_delta_kernel(qkvp_ref, ab_ref, zs_ref, s_ref, dng_ref, o_ref, sn_ref, *, bt):
    ab = ab_ref[...]
    for h in range(N_HEADS):
        lo = h * HEAD_DIM
        q = qkvp_ref[:, lo:lo + HEAD_DIM]
        k = qkvp_ref[:, DN_WIDTH + lo:DN_WIDTH + lo + HEAD_DIM]
        v = qkvp_ref[:, 2 * DN_WIDTH + lo:2 * DN_WIDTH + lo + HEAD_DIM]
        q_cols = _rows_to_cols(q)
        k_cols = _rows_to_cols(k)
        qk = jnp.sum(q * k, axis=-1, keepdims=True)
        alpha = jnp.exp(ab[:, h:h + 1])
        beta = ab[:, N_HEADS + h:N_HEADS + h + 1]
        seqs = range(bt)
        kc = [k_cols[:, j:j + 1] for j in seqs]
        a = [alpha[j:j + 1, :] for j in seqs]
        s_k = [jnp.sum(s_ref[j, h] * kc[j], axis=0, keepdims=True) for j in seqs]
        s_q = [jnp.sum(s_ref[j, h] * q_cols[:, j:j + 1], axis=0, keepdims=True) for j in seqs]
        u = [beta[j:j + 1, :] * (v[j:j + 1, :] - a[j] * s_k[j]) for j in seqs]
        for j in seqs:
            sn_ref[j, h] = a[j] * s_ref[j, h] + kc[j] * u[j]
        o = jnp.concatenate([a[j] * s_q[j] + qk[j:j + 1, :] * u[j] for j in seqs], axis=0)
        o_ref[:, lo:lo + HEAD_DIM] = _gated_head_norm(
            o, dng_ref[...], zs_ref[:, lo:lo + HEAD_DIM]).astype(o_ref.dtype)


def _step_delta(qkvp, ab, zs, s0, dng, *, bt):
    n = qkvp.shape[0]
    row = lambda c: pl.BlockSpec((bt, c), lambda i: (i, 0))
    state_spec = pl.BlockSpec((bt, N_HEADS, HEAD_DIM, HEAD_DIM), lambda i: (i, 0, 0, 0))
    return pl.pallas_call(
        functools.partial(_step_delta_kernel, bt=bt),
        grid=(n // bt,),
        in_specs=[row(3 * DN_WIDTH), row(LANES), row(DN_WIDTH), state_spec, _const_spec(dng)],
        out_specs=[row(DN_WIDTH), state_spec],
        out_shape=[jax.ShapeDtypeStruct((n, DN_WIDTH), BF16), jax.ShapeDtypeStruct(s0.shape, F32)],
        compiler_params=_cparams(),
        name="step_delta",
    )(qkvp, ab, zs, s0, dng)


def _post_kernel(acta_ref, actb_ref, sga_ref, sgb_ref, x_ref, gate1_ref, scale2_ref, shift2_ref,
                 gate2_ref, cnt0_ref, gffn_ref, wpa_ref, wpb_ref, wo_ref, wrt_ref, ebias_ref,
                 wsgu_ref, wsd_ref,
                 xs_ref, h2_ref, eidx_ref, ew_ref, rank_ref, cnt_ref, scores_s, *, tm):
    i = pl.program_id(0)

    @pl.when(i == 0)
    def _():
        cnt_ref[...] = cnt0_ref[...]
        scores_s[...] = jnp.zeros(scores_s.shape, F32)

    neg = -jnp.inf
    big = jnp.int32(1 << 30)

    def project():
        y_a = jnp.dot(acta_ref[...], wpa_ref[...], preferred_element_type=F32)
        yield
        y_b = jnp.dot(actb_ref[...], wpb_ref[...], preferred_element_type=F32)
        merged = sga_ref[...] * y_a + sgb_ref[...] * y_b
        yield
        mix = _bdot(merged, wo_ref[...])
        yield
        x1 = x_ref[...] + gate1_ref[...] * mix
        y = x1 * lax.rsqrt(jnp.mean(x1 * x1, axis=-1, keepdims=True) + EPS) * gffn_ref[...]
        h2 = y * (1.0 + scale2_ref[...]) + shift2_ref[...]
        h2_ref[...] = _pack_bf16_pairs(h2)
        h2b = h2.astype(BF16)
        yield
        gu = jnp.dot(h2b, wsgu_ref[...], preferred_element_type=F32)
        yield
        shared = _bdot(_silu(gu[:, :D_SHARED]) * gu[:, D_SHARED:], wsd_ref[...])
        xs_ref[...] = x1 + gate2_ref[...] * shared
        yield
        logits_t = lax.dot_general(wrt_ref[...], h2b, NT_DIMS, preferred_element_type=F32)
        scores_s[...] = _sigmoid(logits_t)

    def first_argmax(vals, rows):
        m = jnp.max(vals, axis=0, keepdims=True)
        return m, jnp.min(jnp.where(vals == m, rows, big), axis=0, keepdims=True)

    def route(scores):
        biased = scores + ebias_ref[...]
        erow = lax.broadcasted_iota(I32, (N_EXPERTS, tm), 0)
        group_scores = []
        for g in range(N_GROUPS):
            vals = biased[g * GROUP_SIZE:(g + 1) * GROUP_SIZE, :]
            rows = lax.broadcasted_iota(I32, (GROUP_SIZE, tm), 0) + g * GROUP_SIZE
            m1, i1 = first_argmax(vals, rows)
            m2 = jnp.max(jnp.where(rows == i1, neg, vals), axis=0, keepdims=True)
            group_scores.append(m1 + m2)
            yield
        group_sel = [jnp.zeros((1, tm), jnp.bool_)] * N_GROUPS
        for _ in range(TOPK_GROUPS):
            best = functools.reduce(jnp.maximum, group_scores)
            gi = functools.reduce(
                jnp.minimum, [jnp.where(group_scores[g] == best, jnp.int32(g), big) for g in range(N_GROUPS)])
            for g in range(N_GROUPS):
                hit = gi == g
                group_sel[g] = jnp.logical_or(group_sel[g], hit)
                group_scores[g] = jnp.where(hit, neg, group_scores[g])
        cand = jnp.concatenate(
            [jnp.where(group_sel[g], biased[g * GROUP_SIZE:(g + 1) * GROUP_SIZE, :], neg)
             for g in range(N_GROUPS)], axis=0)
        yield
        w_rows, hits = [], []
        picked = jnp.zeros((N_EXPERTS, tm), F32)
        for k in range(TOP_K):
            _, ei = first_argmax(cand, erow)
            hit = erow == ei
            eidx_ref[k:k + 1, :] = ei
            w_rows.append(jnp.sum(jnp.where(hit, scores, 0.0), axis=0, keepdims=True))
            hits.append(hit)
            picked = jnp.where(hit, 1.0, picked)
            cand = jnp.where(hit, neg, cand)
            yield
        w_sum = functools.reduce(jnp.add, w_rows)
        for k in range(TOP_K):
            ew_ref[k:k + 1, :] = w_rows[k] / w_sum * ROUTE_SCALE
        yield
        trow = lax.broadcasted_iota(I32, (tm, tm), 0)
        tcol = lax.broadcasted_iota(I32, (tm, tm), 1)
        before = (trow < tcol).astype(BF16)
        prefix = jnp.dot(picked.astype(BF16), before, preferred_element_type=F32) + cnt_ref[:, 0:1]
        for k in range(TOP_K):
            rank_ref[k:k + 1, :] = jnp.sum(jnp.where(hits[k], prefix, 0.0), axis=0, keepdims=True).astype(I32)
            yield
        counted = jnp.where(i > 0, jnp.sum(picked, axis=1, keepdims=True), 0.0)
        cnt_ref[...] = cnt_ref[...] + counted

    jobs = [route(scores_s[...]), project()]
    shares = [4, 1]
    while jobs:
        for job, share in list(zip(jobs, shares)):
            for _ in range(share):
                if next(job, "done") == "done":
                    k = jobs.index(job)
                    del jobs[k], shares[k]
                    break


def _post(acta, actb, sga, sgb, x2d, gate1, scale2, shift2, gate2, cnt0, g_ffn, w, ebias_col,
          *, tm, tiles_per_mod):
    t = x2d.shape[0]
    n_tiles = t // tm
    tile = lambda i: jnp.minimum(i, n_tiles - 1)
    routed_tile = lambda i: jnp.maximum(i - 1, 0)
    row = lambda c: pl.BlockSpec((tm, c), lambda i: (tile(i), 0))
    col = lambda r: pl.BlockSpec((r, tm), lambda i: (0, routed_tile(i)))
    mod = lambda m: pl.BlockSpec((None,) + m.shape[1:], lambda i: (tile(i) // tiles_per_mod, 0, 0))
    mods = [gate1, scale2, shift2, gate2]
    consts = [cnt0, g_ffn, w["pa"], w["pb"], w["o"], w["router_t"], ebias_col, w["s_gu"], w["s_down"]]
    return pl.pallas_call(
        functools.partial(_post_kernel, tm=tm),
        grid=(n_tiles + 1,),
        in_specs=[row(CONV_CH), row(DN_WIDTH), row(D_MODEL), row(D_MODEL), row(D_MODEL)]
                 + [mod(m) for m in mods] + [_const_spec(a) for a in consts],
        out_specs=[row(D_MODEL), row(D_MODEL // 2), col(TOP_K), col(TOP_K), col(TOP_K), _const_spec(cnt0)],
        scratch_shapes=[pltpu.VMEM((N_EXPERTS, tm), F32)],
        out_shape=[jax.ShapeDtypeStruct((t, D_MODEL), F32), jax.ShapeDtypeStruct((t, D_MODEL // 2), U32),
                   jax.ShapeDtypeStruct((TOP_K, t), I32), jax.ShapeDtypeStruct((TOP_K, t), F32),
                   jax.ShapeDtypeStruct((TOP_K, t), I32), jax.ShapeDtypeStruct(cnt0.shape, F32)],
        compiler_params=_cparams(),
        name="post_mixer",
    )(acta, actb, sga, sgb, x2d, *mods, *consts)


def _dest_kernel(start_ref, eidx_ref, rank_ref, o_ref, *, width):
    step = SUBLANES * LANES
    for c0 in range(0, width, step):
        c1 = min(c0 + step, width)
        e = eidx_ref[:, c0:c1]
        base = lax.fori_loop(0, N_EXPERTS, lambda x, acc: jnp.where(e == x, start_ref[x], acc),
                             jnp.zeros(e.shape, I32), unroll=8)
        o_ref[:, c0:c1] = base + rank_ref[:, c0:c1]


def _dest(cnt_start, e_idx, rank):
    k, t = e_idx.shape
    full = pl.BlockSpec((k, t), lambda i, st: (0, 0))
    return pl.pallas_call(
        functools.partial(_dest_kernel, width=t),
        grid_spec=pltpu.PrefetchScalarGridSpec(
            num_scalar_prefetch=1, grid=(1,), in_specs=[full, full], out_specs=full),
        out_shape=jax.ShapeDtypeStruct((k, t), I32),
        compiler_params=_cparams(),
        name="moe_dest",
    )(cnt_start, e_idx, rank)


def _sc_mesh():
    return plsc.VectorSubcoreMesh(core_axis_name="core", subcore_axis_name="subcore")


def _sc_pipeline(body, n_steps, in_specs, out_specs):
    return pltpu.emit_pipeline(body, grid=(n_steps,), in_specs=in_specs, out_specs=out_specs,
                               core_axis_name=("core", "subcore"), dimension_semantics=(pltpu.PARALLEL,))


def _sc_scatter_rows(x, half, dest_flat, n_rows):
    t = x.shape[0]
    tiles = t // SC_WINDOW
    n_steps = dest_flat.shape[1] // SC_WINDOW

    @functools.partial(pl.kernel, mesh=_sc_mesh(), scratch_types=[],
                       out_type=jax.ShapeDtypeStruct((n_rows, QUARTER_WORDS), x.dtype))
    def scatter(x_hbm, i_hbm, o_hbm):
        def body(x_vmem, i_vmem):
            pltpu.sync_copy(x_vmem, o_hbm.at[i_vmem.at[0]])

        _sc_pipeline(body, n_steps,
                     [pl.BlockSpec((SC_WINDOW, QUARTER_WORDS), lambda i: (i % tiles, half)),
                      pl.BlockSpec((1, SC_WINDOW), lambda i: (0, i))], [])(x_hbm, i_hbm)

    return scatter(x, dest_flat)


def _sc_gather_rows(table, idx_flat):
    n = idx_flat.shape[1]

    @functools.partial(pl.kernel, mesh=_sc_mesh(), scratch_types=[],
                       out_type=jax.ShapeDtypeStruct((n, table.shape[1]), table.dtype))
    def gather(t_hbm, i_hbm, o_hbm):
        def body(i_vmem, o_vmem):
            pltpu.sync_copy(t_hbm.at[i_vmem.at[0]], o_vmem)

        _sc_pipeline(body, n // SC_WINDOW,
                     [pl.BlockSpec((1, SC_WINDOW), lambda i: (0, i))],
                     [pl.BlockSpec((SC_WINDOW, table.shape[1]), lambda i: (i, 0))])(i_hbm, o_hbm)

    return gather(table, idx_flat)


def _experts_kernel(sblk_ref, sexp_ref, slo_ref, shi_ref, snext_ref, first_ref,
                    xa_ref, xb_ref, wg_hbm, wu_hbm, wd_hbm, ya_ref, yb_ref,
                    wg_buf, wu_buf, wd_buf, wgu_s, wd_s, state_ref, sems, *, bm):
    p = pl.program_id(0)
    lo = slo_ref[p]
    expert = sexp_ref[p]

    def weight_copies(e, slot):
        return [pltpu.make_async_copy(src.at[e], dst.at[slot], sems.at[slot])
                for src, dst in ((wg_hbm, wg_buf), (wu_hbm, wu_buf), (wd_hbm, wd_buf))]

    @pl.when(p == 0)
    def _():
        state_ref[0] = -1
        state_ref[1] = 0
        for c in weight_copies(first_ref[0], 0):
            c.start()

    @pl.when(shi_ref[p] > lo)
    def _():
        @pl.when(state_ref[0] != expert)
        def _():
            slot = state_ref[1]
            for c in weight_copies(expert, slot):
                c.wait()
            wgu_s[:, :D_EXPERT] = wg_buf[slot].astype(BF16)
            wgu_s[:, D_EXPERT:] = wu_buf[slot].astype(BF16)
            wd_s[...] = wd_buf[slot].astype(BF16)
            state_ref[0] = expert
            state_ref[1] = 1 - slot
            upcoming = snext_ref[p]

            @pl.when(upcoming >= 0)
            def _():
                for c in weight_copies(upcoming, 1 - slot):
                    c.start()

        q = QUARTER_WORDS
        sub = bm // EXPERT_SUBBLOCKS
        first_sub = lo // sub
        last_sub = (shi_ref[p] - 1) // sub

        def run(r0, r1):
            quarters = {}
            quarters[0], quarters[2] = _unpack_bf16_pairs(xa_ref[r0:r1, :])
            quarters[1], quarters[3] = _unpack_bf16_pairs(xb_ref[r0:r1, :])
            gu = functools.reduce(jnp.add, [
                jnp.dot(quarters[c], wgu_s[c * q:(c + 1) * q, :], preferred_element_type=F32) for c in range(4)])
            hb = (_silu(gu[:, :D_EXPERT]) * gu[:, D_EXPERT:]).astype(BF16)
            y = _pack_bf16_pairs(jnp.dot(hb, wd_s[...], preferred_element_type=F32))

            @pl.when(lo <= r0)
            def _():
                ya_ref[r0:r1, :] = y[:, :q]
                yb_ref[r0:r1, :] = y[:, q:]

            @pl.when(lo > r0)
            def _():
                keep = lax.broadcasted_iota(I32, (r1 - r0, 1), 0) + r0 >= lo
                ya_ref[r0:r1, :] = jnp.where(keep, y[:, :q], ya_ref[r0:r1, :])
                yb_ref[r0:r1, :] = jnp.where(keep, y[:, q:], yb_ref[r0:r1, :])

        for a in range(EXPERT_SUBBLOCKS):
            for b in range(a, EXPERT_SUBBLOCKS):
                pl.when(jnp.logical_and(first_sub == a, last_sub == b))(
                    functools.partial(run, a * sub, (b + 1) * sub))


def _experts(xs_a, xs_b, w_gate, w_up, w_down, seg_blk, seg_exp, seg_lo, seg_hi, seg_next, first_exp, *, bm):
    n_rows = xs_a.shape[0]
    n_seg = seg_blk.shape[0]
    ring = 2
    hbm = pl.BlockSpec(memory_space=pl.ANY)
    rows = pl.BlockSpec((bm, QUARTER_WORDS), lambda p, sb, *_: (sb[p], 0))
    grid_spec = pltpu.PrefetchScalarGridSpec(
        num_scalar_prefetch=6,
        grid=(n_seg,),
        in_specs=[rows, rows, hbm, hbm, hbm],
        out_specs=[rows, rows],
        scratch_shapes=[pltpu.VMEM((ring, D_MODEL, D_EXPERT), F32), pltpu.VMEM((ring, D_MODEL, D_EXPERT), F32),
                        pltpu.VMEM((ring, D_EXPERT, D_MODEL), F32),
                        pltpu.VMEM((D_MODEL, 2 * D_EXPERT), BF16), pltpu.VMEM((D_EXPERT, D_MODEL), BF16),
                        pltpu.SMEM((2,), I32), pltpu.SemaphoreType.DMA((ring,))],
    )
    return pl.pallas_call(
        functools.partial(_experts_kernel, bm=bm),
        grid_spec=grid_spec,
        out_shape=[jax.ShapeDtypeStruct((n_rows, QUARTER_WORDS), U32)] * 2,
        compiler_params=_cparams(),
        name="moe_experts",
    )(seg_blk, seg_exp, seg_lo, seg_hi, seg_next, first_exp, xs_a, xs_b, w_gate, w_up, w_down)


def _final_kernel(*refs, tm, n_prompt_tiles, has_sample, has_prev):
    refs = list(refs)
    ew_ref, pa_ref, pb_ref, xsp_ref = refs[:4]
    del refs[:4]
    xss_ref = refs.pop(0) if has_sample else None
    g2p_ref = refs.pop(0)
    g2s_ref = refs.pop(0) if has_sample else None
    gfin_ref = refs.pop(0)
    if has_prev:
        refs.pop(0)
    op_ref = refs.pop(0)
    os_ref = refs.pop(0) if has_sample else None
    ew = ew_ref[...]
    quarters = [jnp.zeros((tm, QUARTER_WORDS), F32)] * 4
    for k in range(TOP_K):
        w_k = ew[:, k:k + 1]
        a_low, a_high = _unpack_bf16_pairs(pa_ref[k])
        b_low, b_high = _unpack_bf16_pairs(pb_ref[k])
        for c, part in enumerate((a_low, b_low, a_high, b_high)):
            quarters[c] = quarters[c] + w_k * part.astype(F32)
    routed = jnp.concatenate(quarters, axis=1)

    def finish(xs_ref, gate2_ref, o_ref):
        x2 = xs_ref[...] + gate2_ref[...] * routed
        o_ref[...] = x2 * lax.rsqrt(jnp.mean(x2 * x2, axis=-1, keepdims=True) + EPS) * gfin_ref[...]

    if not has_sample:
        finish(xsp_ref, g2p_ref, op_ref)
    else:
        is_prompt = pl.program_id(0) < n_prompt_tiles
        pl.when(is_prompt)(lambda: finish(xsp_ref, g2p_ref, op_ref))
        pl.when(jnp.logical_not(is_prompt))(lambda: finish(xss_ref, g2s_ref, os_ref))


def _final(ew_part, planes_a, planes_b, xs_p, gate2_p, g_final, *, tm, tiles_per_batch, tile_lo, n_prompt_tiles,
           sample=None, y_prev=None):
    prompt_tile = lambda i: tile_lo + jnp.minimum(i, n_prompt_tiles - 1)
    sample_tile = lambda i: jnp.maximum(i - n_prompt_tiles, 0)
    prompt_rows = pl.BlockSpec((tm, D_MODEL), lambda i: (prompt_tile(i), 0))
    sample_rows = pl.BlockSpec((tm, D_MODEL), lambda i: (sample_tile(i), 0))
    planes = pl.BlockSpec((TOP_K, tm, QUARTER_WORDS), lambda i: (0, i, 0))
    has_sample = sample is not None
    has_prev = y_prev is not None
    n_tiles = n_prompt_tiles + (sample[0].shape[0] // tm if has_sample else 0)
    args = [ew_part, planes_a, planes_b, xs_p]
    in_specs = [pl.BlockSpec((tm, TOP_K), lambda i: (i, 0)), planes, planes, prompt_rows]
    if has_sample:
        args.append(sample[0])
        in_specs.append(sample_rows)
    args.append(gate2_p)
    in_specs.append(pl.BlockSpec((None, 1, D_MODEL), lambda i: (prompt_tile(i) // tiles_per_batch, 0, 0)))
    if has_sample:
        args.append(sample[1])
        in_specs.append(pl.BlockSpec((None, tm, D_MODEL), lambda i: (sample_tile(i), 0, 0)))
    args.append(g_final)
    in_specs.append(_const_spec(g_final))
    if has_prev:
        args.append(y_prev)
        in_specs.append(pl.BlockSpec(memory_space=pl.ANY))
    out_specs = [prompt_rows] + ([sample_rows] if has_sample else [])
    out_shape = [jax.ShapeDtypeStruct(xs_p.shape, F32)] + (
        [jax.ShapeDtypeStruct(sample[0].shape, F32)] if has_sample else [])
    outs = pl.pallas_call(
        functools.partial(_final_kernel, tm=tm, n_prompt_tiles=n_prompt_tiles, has_sample=has_sample,
                          has_prev=has_prev),
        grid=(n_tiles,),
        in_specs=in_specs,
        out_specs=out_specs,
        out_shape=out_shape,
        input_output_aliases={len(args) - 1: 0} if has_prev else {},
        compiler_params=_cparams(),
        name="moe_combine_final",
    )(*args)
    return outs if has_sample else outs[0]


def _pick_tile(n, preferred):
    t = min(n, preferred)
    assert n % t == 0, (n, t)
    return t


def kernel(x_prompt, x_sample, c_prompt, c_sample, state_conv, state_short_conv, state_delta, w_ada, b_ada, g_mix, w_in, conv_dw, conv_dw_b, conv_ln_g, conv_ln_b, w_pa, short_conv_w, a_log, dt_bias, dn_norm_g, w_pb, w_o, g_ffn, w_router, e_bias, w_gate, w_up, w_down, ws_gate, ws_up, ws_down, g_final):
    nb, seq, d = x_prompt.shape
    ns = x_sample.shape[0]
    assert d == D_MODEL and w_ada.shape[0] == 1 and x_sample.shape[1] == 1
    assert seq % CHUNK == 0 and seq >= CONV_HALO and ns % SUBLANES == 0
    tp = nb * seq
    row1 = lambda a: a.reshape(1, -1)

    wi = w_in[0].astype(BF16)
    o_q = 2 * CONV_CH
    o_z = o_q + 3 * DN_WIDTH
    o_a = o_z + DN_WIDTH
    o_ga = o_a + 2 * N_HEADS
    w_proj = {
        "glu": wi[:, :o_q], "qkv": wi[:, o_q:o_z], "z": wi[:, o_z:o_a],
        "ab": jnp.pad(wi[:, o_a:o_ga], ((0, 0), (0, LANES - 2 * N_HEADS))),
        "ga": wi[:, o_ga:o_ga + D_MODEL], "gb": wi[:, o_ga + D_MODEL:],
    }
    nea = jnp.pad(-jnp.exp(a_log[0].astype(F32)), (0, LANES - N_HEADS)).reshape(1, LANES)
    dtb = jnp.pad(dt_bias[0].astype(F32), (0, LANES - N_HEADS)).reshape(1, LANES)
    w_post = {
        "pa": w_pa[0].astype(BF16), "pb": w_pb[0].astype(BF16), "o": w_o[0].astype(BF16),
        "router_t": w_router[0].T.astype(BF16),
        "s_gu": jnp.concatenate([ws_gate[0], ws_up[0]], axis=1).astype(BF16),
        "s_down": ws_down[0].astype(BF16),
    }
    ebias_col = e_bias[0].astype(F32).reshape(N_EXPERTS, 1)
    g_mix1, g_ffn1, g_fin1 = row1(g_mix[0]), row1(g_ffn[0]), row1(g_final)
    cdw, cdb, clg, clb = conv_dw[0], row1(conv_dw_b[0]), row1(conv_ln_g[0]), row1(conv_ln_b[0])
    scw, dng = short_conv_w[0], row1(dn_norm_g[0])

    ada = _ada(jnp.concatenate([c_prompt, c_sample], axis=0), w_ada[0].astype(BF16), b_ada[0])
    ada = ada.reshape(nb + ns, 6, D_MODEL)
    mod_p = [ada[:nb, m].reshape(nb, 1, D_MODEL) for m in range(6)]
    mod_s = [ada[nb:, m].reshape(1, ns, D_MODEL) for m in range(6)]
    shift1, scale1, gate1, shift2, scale2, gate2 = range(6)

    tm_p = _pick_tile(seq, 256)
    tpm_p = seq // tm_p
    xp = x_prompt.reshape(tp, D_MODEL)
    u_p, qkv_p, zs_p, ab_p, sga_p, sgb_p = _in_proj(
        xp, mod_p[scale1], mod_p[shift1], g_mix1, w_proj, nea, dtb, tm=tm_p, tiles_per_mod=tpm_p)
    tl = _pick_tile(seq, 256)
    acta_p = _conv_branch(u_p, jnp.zeros((nb, CONV_HALO, CONV_CH), F32), cdw, cdb, clg, clb,
                          tl=tl, tiles_per_batch=seq // tl)
    tc = _pick_tile(seq, 4 * CHUNK)
    actb_p, s_new_p = _delta_branch(
        qkv_p, jnp.zeros((nb, SC_HALO, 3 * DN_WIDTH), F32),
        jnp.zeros((nb, N_HEADS, HEAD_DIM, HEAD_DIM), F32), scw, ab_p, zs_p, dng,
        tc=tc, tiles_per_batch=seq // tc)

    xs_ = x_sample.reshape(ns, D_MODEL)
    u_s, qkv_s, zs_s, ab_s, sga_s, sgb_s = _in_proj(
        xs_, mod_s[scale1], mod_s[shift1], g_mix1, w_proj, nea, dtb, tm=ns, tiles_per_mod=1)
    bt = SUBLANES
    acta_s, nch_s, qkvp_s, nsh_s = _step_front(
        u_s, qkv_s, state_conv[0].reshape(ns, -1), state_short_conv[0].reshape(ns, -1),
        cdw, cdb, clg, clb, scw, bt=bt)
    actb_s, s_new_s = _step_delta(qkvp_s, ab_s, zs_s, state_delta[0].astype(F32), dng, bt=bt)

    cnt0 = jnp.zeros((N_EXPERTS, LANES), F32)
    xsh_p, h2_p, eidx_p, ew_p, rank_p, cnt_p = _post(
        acta_p, actb_p, sga_p, sgb_p, xp, mod_p[gate1], mod_p[scale2], mod_p[shift2], mod_p[gate2],
        cnt0, g_ffn1, w_post, ebias_col, tm=tm_p, tiles_per_mod=tpm_p)
    xsh_s, h2_s, eidx_s, ew_s, rank_s, cnt_all = _post(
        acta_s, actb_s, sga_s, sgb_s, xs_, mod_s[gate1], mod_s[scale2], mod_s[shift2], mod_s[gate2],
        cnt_p, g_ffn1, w_post, ebias_col, tm=ns, tiles_per_mod=1)

    bm = EXPERT_SUBBLOCKS * EXPERT_SUB_ROWS
    tmd = LANES
    t_all = tp + ns
    n_rows = t_all * TOP_K
    assert seq % tmd == 0 and ns % tmd == 0 and n_rows % bm == 0 and t_all % SC_WINDOW == 0
    n_blocks = n_rows // bm
    counts = cnt_all[:, 0].astype(I32)
    cnt_end = jnp.cumsum(counts)
    cnt_start = cnt_end - counts
    e_idx = jnp.concatenate([eidx_p, eidx_s], axis=1)
    rank = jnp.concatenate([rank_p, rank_s], axis=1)
    dest = _dest(cnt_start, e_idx, rank)
    dest_flat = dest.reshape(1, n_rows)
    seg_start = jnp.sort(jnp.concatenate([jnp.arange(n_blocks, dtype=I32) * bm, cnt_start]))
    seg_end = jnp.concatenate([seg_start[1:], jnp.full((1,), n_rows, I32)])
    seg_blk = jnp.minimum(seg_start // bm, n_blocks - 1)
    seg_lo = seg_start - seg_blk * bm
    seg_hi = seg_end - seg_blk * bm
    seg_exp = jnp.sum((cnt_end[None, :] <= seg_start[:, None]).astype(I32), axis=1)
    seg_exp = jnp.minimum(seg_exp, N_EXPERTS - 1)
    seg_exp = lax.cummax(jnp.where(seg_hi > seg_lo, seg_exp, 0))
    none = jnp.int32(N_EXPERTS)
    in_use_from = lax.cummin(jnp.where(counts > 0, jnp.arange(N_EXPERTS, dtype=I32), none), reverse=True)
    next_in_use = jnp.concatenate([in_use_from[1:], jnp.full((1,), none, I32)])
    next_in_use = jnp.where(next_in_use == none, -1, next_in_use)
    seg_next = next_in_use[seg_exp]
    first_exp = jnp.minimum(in_use_from[:1], N_EXPERTS - 1)

    h2_all = jnp.concatenate([h2_p, h2_s], axis=0)
    xs_a, xs_b = (_sc_scatter_rows(h2_all, half, dest_flat, n_rows) for half in range(2))
    ys_a, ys_b = _experts(xs_a, xs_b, w_gate[0], w_up[0], w_down[0], seg_blk, seg_exp, seg_lo, seg_hi, seg_next,
                          first_exp, bm=bm)

    ew_tok = jnp.concatenate([ew_p, ew_s], axis=1).T
    n_prompt_tiles = tp // tmd
    tiles_1 = n_prompt_tiles // 2
    t_1 = tiles_1 * tmd

    def gathered(lo, hi):
        idx = dest[:, lo:hi].reshape(1, -1)
        return [_sc_gather_rows(ys, idx).reshape(TOP_K, hi - lo, QUARTER_WORDS) for ys in (ys_a, ys_b)]

    final = functools.partial(_final, xs_p=xsh_p, gate2_p=mod_p[gate2], g_final=g_fin1, tm=tmd,
                              tiles_per_batch=seq // tmd)
    y_p = final(ew_tok[:t_1], *gathered(0, t_1), tile_lo=0, n_prompt_tiles=tiles_1)
    y_p, y_s = final(ew_tok[t_1:], *gathered(t_1, t_all), tile_lo=tiles_1, n_prompt_tiles=n_prompt_tiles - tiles_1,
                     sample=(xsh_s, mod_s[gate2].reshape(ns // tmd, tmd, D_MODEL)), y_prev=y_p)

    hist_rows = CONV_WIDTH - 1
    sc_rows = SHORT_CONV - 1
    y_prompt = y_p.reshape(nb, seq, D_MODEL)
    y_sample = y_s.reshape(ns, 1, D_MODEL)
    new_conv_prompt = u_p.reshape(nb, seq, CONV_CH)[:, seq - hist_rows:][None]
    new_sc_prompt = qkv_p.reshape(nb, seq, 3 * DN_WIDTH)[:, seq - sc_rows:][None]
    new_conv_sample = nch_s.reshape(ns, hist_rows, CONV_CH)[None]
    new_sc_sample = nsh_s.reshape(ns, sc_rows, 3 * DN_WIDTH)[None]
    return (y_prompt, y_sample, new_conv_prompt, new_sc_prompt, s_new_p[None],
            new_conv_sample, new_sc_sample, s_new_s[None])
```

```python
import functools

import jax
import jax.numpy as jnp
from jax import lax
from jax.experimental import pallas as pl
from jax.experimental.pallas import tpu as pltpu
from jax.experimental.pallas import tpu_sc as plsc

F32 = jnp.float32
BF16 = jnp.bfloat16
I32 = jnp.int32
U32 = jnp.uint32

EPS = 1e-6
D_MODEL = 1024
CONV_CH = 512
CONV_WIDTH = 31
N_HEADS = 4
HEAD_DIM = 128
DN_WIDTH = N_HEADS * HEAD_DIM
SHORT_CONV = 4
CHUNK = 64
CHUNK_SHIFT = CHUNK.bit_length() - 1
assert 1 << CHUNK_SHIFT == CHUNK
SUB = 16
SUB_SHIFT = SUB.bit_length() - 1
assert 1 << SUB_SHIFT == SUB and CHUNK % SUB == 0
N_EXPERTS = 256
TOP_K = 8
N_GROUPS = 8
GROUP_SIZE = N_EXPERTS // N_GROUPS
TOPK_GROUPS = 4
D_EXPERT = 256
D_SHARED = 256
ROUTE_SCALE = 2.5

LANES = 128
SUBLANES = 8
CONV_HALO = 32
SC_HALO = 8
VMEM_LIMIT = 48 * 1024 * 1024
EXPERT_SUBBLOCKS = 8
EXPERT_SUB_ROWS = 128
SC_WINDOW = 128
QUARTER_WORDS = D_MODEL // 4

NT_DIMS = (((1,), (1,)), ((), ()))
TN_DIMS = (((0,), (0,)), ((), ()))


def _cparams(n_grid_dims=1):
    return pltpu.CompilerParams(
        dimension_semantics=("arbitrary",) * n_grid_dims,
        vmem_limit_bytes=VMEM_LIMIT)


def _sigmoid(x):
    return jax.nn.sigmoid(x)


def _silu(x):
    return x * jax.nn.sigmoid(x)


def _bdot(a, b):
    return jnp.dot(a.astype(BF16), b.astype(BF16), preferred_element_type=F32)


def _pack_bf16_pairs(x):
    half = x.shape[1] // 2
    bits = lax.bitcast_convert_type(x.astype(BF16).astype(F32), U32)
    return bits[:, half:] | (bits[:, :half] >> 16)


def _unpack_bf16_pairs(w):
    low = lax.bitcast_convert_type(w << 16, F32).astype(BF16)
    high = lax.bitcast_convert_type(w & jnp.uint32(0xFFFF0000), F32).astype(BF16)
    return low, high


def _dot3(a, b):
    a_hi = a.astype(BF16)
    b_hi = b.astype(BF16)
    a_lo = (a - a_hi.astype(F32)).astype(BF16)
    b_lo = (b - b_hi.astype(F32)).astype(BF16)
    dot = lambda x, y: jnp.dot(x, y, preferred_element_type=F32)
    return dot(a_hi, b_hi) + (dot(a_hi, b_lo) + dot(a_lo, b_hi))


def _dot_exact_lhs(a_bf, b):
    hi = b.astype(BF16)
    r1 = b - hi.astype(F32)
    mid = r1.astype(BF16)
    low = (r1 - mid.astype(F32)).astype(BF16)
    dot = lambda p: jnp.dot(a_bf, p, preferred_element_type=F32)
    return dot(hi) + dot(mid) + dot(low)


def _ada_kernel(c_ref, w_ref, b_ref, o_ref):
    o_ref[...] = _bdot(_silu(c_ref[...]), w_ref[...]) + b_ref[...]


def _ada(c_all, w_ada_bf, b_ada):
    n = c_all.shape[0]
    width = w_ada_bf.shape[1]
    tn = 1536
    return pl.pallas_call(
        _ada_kernel,
        grid=(width // tn,),
        in_specs=[pl.BlockSpec((n, D_MODEL), lambda j: (0, 0)),
                  pl.BlockSpec((D_MODEL, tn), lambda j: (0, j)),
                  pl.BlockSpec((1, tn), lambda j: (0, j))],
        out_specs=pl.BlockSpec((n, tn), lambda j: (0, j)),
        out_shape=jax.ShapeDtypeStruct((n, width), F32),
        compiler_params=_cparams(),
        name="ada",
    )(c_all, w_ada_bf, b_ada.reshape(1, width))


def _in_proj_kernel(x_ref, scale_ref, shift_ref, g_ref, wglu_ref, wqkv_ref, wz_ref, wab_ref,
                    wga_ref, wgb_ref, nea_ref, dtb_ref,
                    u_ref, qkv_ref, zs_ref, ab_ref, sga_ref, sgb_ref):
    x = x_ref[...]
    y = x * lax.rsqrt(jnp.mean(x * x, axis=-1, keepdims=True) + EPS) * g_ref[...]
    h = (y * (1.0 + scale_ref[...]) + shift_ref[...]).astype(BF16)
    glu = jnp.dot(h, wglu_ref[...], preferred_element_type=F32)
    u_ref[...] = glu[:, :CONV_CH] * _sigmoid(glu[:, CONV_CH:])
    qkv_ref[...] = jnp.dot(h, wqkv_ref[...], preferred_element_type=F32)
    zs_ref[...] = _silu(jnp.dot(h, wz_ref[...], preferred_element_type=F32))
    ab = jnp.dot(h, wab_ref[...], preferred_element_type=F32)
    sp = ab + dtb_ref[...]
    softplus = jnp.maximum(sp, 0.0) + jnp.log1p(jnp.exp(-jnp.abs(sp)))
    lane = lax.broadcasted_iota(I32, ab.shape, 1)
    ab_ref[...] = jnp.where(lane < N_HEADS, nea_ref[...] * softplus, _sigmoid(ab))
    sga_ref[...] = _sigmoid(jnp.dot(h, wga_ref[...], preferred_element_type=F32))
    sgb_ref[...] = _sigmoid(jnp.dot(h, wgb_ref[...], preferred_element_type=F32))


def _mod_spec(mod, tiles_per_mod):
    return pl.BlockSpec((None,) + mod.shape[1:], lambda i: (i // tiles_per_mod, 0, 0))


def _const_spec(a):
    nd = a.ndim
    return pl.BlockSpec(a.shape, lambda i: (0,) * nd)


def _in_proj(x2d, scale, shift, g_mix, w, nea, dtb, *, tm, tiles_per_mod):
    t = x2d.shape[0]
    row = lambda c: pl.BlockSpec((tm, c), lambda i: (i, 0))
    consts = [g_mix, w["glu"], w["qkv"], w["z"], w["ab"], w["ga"], w["gb"], nea, dtb]
    widths = [CONV_CH, 3 * DN_WIDTH, DN_WIDTH, LANES, D_MODEL, D_MODEL]
    return pl.pallas_call(
        _in_proj_kernel,
        grid=(t // tm,),
        in_specs=[row(D_MODEL), _mod_spec(scale, tiles_per_mod), _mod_spec(shift, tiles_per_mod)]
                 + [_const_spec(a) for a in consts],
        out_specs=[row(c) for c in widths],
        out_shape=[jax.ShapeDtypeStruct((t, c), F32) for c in widths],
        compiler_params=_cparams(),
        name="in_proj",
    )(x2d, scale, shift, *consts)


def _layer_norm_swish(y, g, b):
    mu = jnp.mean(y, axis=-1, keepdims=True)
    yc = y - mu
    var = jnp.mean(yc * yc, axis=-1, keepdims=True)
    return _silu(yc * lax.rsqrt(var + EPS) * g + b)


def _conv_kernel(u_ref, halo_ref, hist_ref, w_ref, b_ref, lg_ref, lb_ref, o_ref, win_ref, shift_ref,
                 *, tl, tiles_per_batch, rows_per_pass):
    i = pl.program_id(0)
    first = (i % tiles_per_batch) == 0

    @pl.when(first)
    def _():
        win_ref[0:CONV_HALO, :] = hist_ref[...]

    @pl.when(jnp.logical_not(first))
    def _():
        win_ref[0:CONV_HALO, :] = halo_ref[...]

    win_ref[CONV_HALO:CONV_HALO + tl, :] = u_ref[...]
    tap0 = CONV_HALO - (CONV_WIDTH - 1)
    n_shifted = CONV_HALO + tl - SUBLANES
    for b in range(1, SUBLANES):
        shift_ref[b - 1] = win_ref[b:b + n_shifted, :]

    for r in range(tl // rows_per_pass):
        base = r * rows_per_pass
        acc = jnp.zeros((rows_per_pass, CONV_CH), F32)
        for j in range(CONV_WIDTH):
            a, b = divmod(tap0 + j, SUBLANES)
            start = base + a * SUBLANES
            rows = (win_ref[start:start + rows_per_pass, :] if b == 0
                    else shift_ref[b - 1, start:start + rows_per_pass, :])
            acc = acc + w_ref[j:j + 1, :] * rows
        act = _layer_norm_swish(acc + b_ref[...], lg_ref[...], lb_ref[...])
        o_ref[base:base + rows_per_pass, :] = act.astype(o_ref.dtype)


def _conv_branch(u2d, hist, conv_dw, conv_dw_b, ln_g, ln_b, *, tl, tiles_per_batch):
    t = u2d.shape[0]
    halo_blocks = tl // CONV_HALO
    consts = [conv_dw, conv_dw_b, ln_g, ln_b]
    return pl.pallas_call(
        functools.partial(_conv_kernel, tl=tl, tiles_per_batch=tiles_per_batch, rows_per_pass=64),
        grid=(t // tl,),
        in_specs=[pl.BlockSpec((tl, CONV_CH), lambda i: (i, 0)),
                  pl.BlockSpec((CONV_HALO, CONV_CH), lambda i: (jnp.maximum(i * halo_blocks - 1, 0), 0)),
                  pl.BlockSpec((None, CONV_HALO, CONV_CH), lambda i: (i // tiles_per_batch, 0, 0))]
                 + [_const_spec(a) for a in consts],
        out_specs=pl.BlockSpec((tl, CONV_CH), lambda i: (i, 0)),
        out_shape=jax.ShapeDtypeStruct((t, CONV_CH), BF16),
        scratch_shapes=[pltpu.VMEM((CONV_HALO + tl, CONV_CH), F32),
                        pltpu.VMEM((SUBLANES - 1, CONV_HALO + tl - SUBLANES, CONV_CH), F32)],
        compiler_params=_cparams(),
        name="conv_branch",
    )(u2d, u2d, hist, *consts)


def _l2n(x):
    return x * lax.rsqrt(jnp.sum(x * x, axis=-1, keepdims=True) + 1e-6)


def _gated_head_norm(o, g, zs):
    return o * lax.rsqrt(jnp.mean(o * o, axis=-1, keepdims=True) + EPS) * g * zs


def _delta_kernel(qkv_ref, halo_ref, hist_ref, scw_ref, ab_ref, zs_ref, s0_ref, dng_ref,
                  o_ref, s_ref, win_ref, c_ref, *, tc, tiles_per_batch):
    i = pl.program_id(0)
    first = (i % tiles_per_batch) == 0

    @pl.when(first)
    def _():
        win_ref[0:SC_HALO, :] = hist_ref[...]
        s_ref[...] = s0_ref[...]

    @pl.when(jnp.logical_not(first))
    def _():
        win_ref[0:SC_HALO, :] = halo_ref[...]

    win_ref[SC_HALO:SC_HALO + tc, :] = qkv_ref[...]
    tap0 = SC_HALO - (SHORT_CONV - 1)
    acc = jnp.zeros((tc, 3 * DN_WIDTH), F32)
    for j in range(SHORT_CONV):
        acc = acc + scw_ref[j:j + 1, :] * win_ref[tap0 + j:tap0 + j + tc, :]
    c_ref[...] = _silu(acc)

    row = lax.broadcasted_iota(I32, (tc, tc), 0)
    col = lax.broadcasted_iota(I32, (tc, tc), 1)
    same_chunk = lax.shift_right_logical(row, CHUNK_SHIFT) == lax.shift_right_logical(col, CHUNK_SHIFT)
    incl = jnp.logical_and(same_chunk, row >= col)
    row_sub = lax.shift_right_logical(row, SUB_SHIFT)
    col_sub = lax.shift_right_logical(col, SUB_SHIFT)
    strict_sub = jnp.logical_and(row_sub == col_sub, row > col)
    below_sub = jnp.logical_and(same_chunk, row_sub > col_sub)
    eye = (row == col).astype(F32)
    ab = ab_ref[...]
    gcum = _dot_exact_lhs(incl.astype(BF16), ab)
    gcum_t = gcum.T
    n_ch = tc // CHUNK

    heads = range(N_HEADS)
    q, k, gc, kk, a_qk, rhs, qd = [], [], [], [], [], [], []
    for h in heads:
        lo = h * HEAD_DIM
        q_h = _l2n(c_ref[:, lo:lo + HEAD_DIM]) * (HEAD_DIM ** -0.5)
        k_h = _l2n(c_ref[:, DN_WIDTH + lo:DN_WIDTH + lo + HEAD_DIM])
        v_h = c_ref[:, 2 * DN_WIDTH + lo:2 * DN_WIDTH + lo + HEAD_DIM]
        beta = ab[:, N_HEADS + h:N_HEADS + h + 1]
        gc_h = gcum[:, h:h + 1]
        gr = gcum_t[h:h + 1, :]
        decay = jnp.exp(jnp.where(incl, gc_h - gr, -jnp.inf))
        kb = k_h * beta
        kq = lax.dot_general(jnp.concatenate([kb, q_h], axis=0).astype(BF16), k_h.astype(BF16),
                             NT_DIMS, preferred_element_type=F32)
        e_gc = jnp.exp(gc_h)
        q.append(q_h)
        k.append(k_h)
        gc.append(gc_h)
        kk.append(kq[:tc] * decay)
        a_qk.append(kq[tc:] * decay)
        rhs.append(jnp.concatenate([v_h * beta, kb * e_gc], axis=1))
        qd.append(q_h * e_gc)

    pw = [jnp.where(strict_sub, kk[h], 0.0) for h in heads]
    d_inv = [eye - pw[h] for h in heads]
    for _ in range(SUB_SHIFT - 1):
        pw = [_bdot(pw[h], pw[h]) for h in heads]
        d_inv = [d_inv[h] + _bdot(d_inv[h], pw[h]) for h in heads]
    c_rhs = [_bdot(d_inv[h], rhs[h]) for h in heads]
    m_off = [_bdot(d_inv[h], jnp.where(below_sub, kk[h], 0.0)) for h in heads]
    sol = c_rhs
    for _ in range(CHUNK // SUB - 1):
        sol = [c_rhs[h] - _bdot(m_off[h], sol[h]) for h in heads]

    s = [s_ref[h] for h in heads]
    u_parts = [[] for _ in heads]
    o_parts = [[] for _ in heads]
    for c in range(n_ch):
        r0 = c * CHUNK
        for h in heads:
            u0_c = sol[h][r0:r0 + CHUNK, :HEAD_DIM]
            w_c = sol[h][r0:r0 + CHUNK, HEAD_DIM:]
            wq_s = _bdot(jnp.concatenate([w_c, qd[h][r0:r0 + CHUNK]], axis=0), s[h])
            u_c = u0_c - wq_s[:CHUNK]
            g_last = gcum[r0 + CHUNK - 1:r0 + CHUNK, h:h + 1]
            k_dec = k[h][r0:r0 + CHUNK] * jnp.exp(g_last - gc[h][r0:r0 + CHUNK])
            s[h] = jnp.exp(g_last) * s[h] + lax.dot_general(
                k_dec.astype(BF16), u_c.astype(BF16), TN_DIMS, preferred_element_type=F32)
            u_parts[h].append(u_c)
            o_parts[h].append(wq_s[CHUNK:])
    for h in heads:
        lo = h * HEAD_DIM
        s_ref[h] = s[h]
        o = jnp.concatenate(o_parts[h], axis=0) + _bdot(a_qk[h], jnp.concatenate(u_parts[h], axis=0))
        o_ref[:, lo:lo + HEAD_DIM] = _gated_head_norm(
            o, dng_ref[...], zs_ref[:, lo:lo + HEAD_DIM]).astype(o_ref.dtype)


def _delta_branch(qkv2d, hist, s0, scw, ab, zs, dng, *, tc, tiles_per_batch):
    t = qkv2d.shape[0]
    nb = s0.shape[0]
    halo_blocks = tc // SC_HALO
    wqkv = 3 * DN_WIDTH
    state_spec = pl.BlockSpec((None, N_HEADS, HEAD_DIM, HEAD_DIM), lambda i: (i // tiles_per_batch, 0, 0, 0))
    return pl.pallas_call(
        functools.partial(_delta_kernel, tc=tc, tiles_per_batch=tiles_per_batch),
        grid=(t // tc,),
        in_specs=[pl.BlockSpec((tc, wqkv), lambda i: (i, 0)),
                  pl.BlockSpec((SC_HALO, wqkv), lambda i: (jnp.maximum(i * halo_blocks - 1, 0), 0)),
                  pl.BlockSpec((None, SC_HALO, wqkv), lambda i: (i // tiles_per_batch, 0, 0)),
                  _const_spec(scw),
                  pl.BlockSpec((tc, LANES), lambda i: (i, 0)),
                  pl.BlockSpec((tc, DN_WIDTH), lambda i: (i, 0)),
                  state_spec,
                  _const_spec(dng)],
        out_specs=[pl.BlockSpec((tc, DN_WIDTH), lambda i: (i, 0)), state_spec],
        out_shape=[jax.ShapeDtypeStruct((t, DN_WIDTH), BF16),
                   jax.ShapeDtypeStruct((nb, N_HEADS, HEAD_DIM, HEAD_DIM), F32)],
        scratch_shapes=[pltpu.VMEM((SC_HALO + tc, wqkv), F32), pltpu.VMEM((tc, wqkv), F32)],
        compiler_params=_cparams(),
        name="delta_branch",
    )(qkv2d, qkv2d, hist, scw, ab, zs, s0, dng)


def _step_front_kernel(u_ref, qkv_ref, ch_ref, sh_ref, cw_ref, cb_ref, lg_ref, lb_ref, scw_ref,
                       acta_ref, nch_ref, qkvp_ref, nsh_ref):
    hist_rows = CONV_WIDTH - 1
    u = u_ref[...]
    acc = cw_ref[hist_rows:hist_rows + 1, :] * u
    for j in range(hist_rows):
        acc = acc + cw_ref[j:j + 1, :] * ch_ref[:, j * CONV_CH:(j + 1) * CONV_CH]
    acta_ref[...] = _layer_norm_swish(acc + cb_ref[...], lg_ref[...], lb_ref[...]).astype(acta_ref.dtype)
    nch_ref[:, :(hist_rows - 1) * CONV_CH] = ch_ref[:, CONV_CH:]
    nch_ref[:, (hist_rows - 1) * CONV_CH:] = u

    wq = 3 * DN_WIDTH
    sc_rows = SHORT_CONV - 1
    qkv = qkv_ref[...]
    acc = scw_ref[sc_rows:sc_rows + 1, :] * qkv
    for j in range(sc_rows):
        acc = acc + scw_ref[j:j + 1, :] * sh_ref[:, j * wq:(j + 1) * wq]
    c = _silu(acc)
    for h in range(N_HEADS):
        lo = h * HEAD_DIM
        qkvp_ref[:, lo:lo + HEAD_DIM] = _l2n(c[:, lo:lo + HEAD_DIM]) * (HEAD_DIM ** -0.5)
        qkvp_ref[:, DN_WIDTH + lo:DN_WIDTH + lo + HEAD_DIM] = _l2n(c[:, DN_WIDTH + lo:DN_WIDTH + lo + HEAD_DIM])
    qkvp_ref[:, 2 * DN_WIDTH:] = c[:, 2 * DN_WIDTH:]
    nsh_ref[:, :(sc_rows - 1) * wq] = sh_ref[:, wq:]
    nsh_ref[:, (sc_rows - 1) * wq:] = qkv


def _step_front(u, qkv, conv_hist_flat, sc_hist_flat, conv_dw, conv_dw_b, ln_g, ln_b, scw, *, bt):
    n = u.shape[0]
    wq = 3 * DN_WIDTH
    consts = [conv_dw, conv_dw_b, ln_g, ln_b, scw]
    row = lambda c: pl.BlockSpec((bt, c), lambda i: (i, 0))
    widths = [CONV_CH, conv_hist_flat.shape[1], wq, sc_hist_flat.shape[1]]
    dtypes = [BF16, F32, F32, F32]
    return pl.pallas_call(
        _step_front_kernel,
        grid=(n // bt,),
        in_specs=[row(CONV_CH), row(wq), row(widths[1]), row(widths[3])] + [_const_spec(a) for a in consts],
        out_specs=[row(c) for c in widths],
        out_shape=[jax.ShapeDtypeStruct((n, c), d) for c, d in zip(widths, dtypes)],
        compiler_params=_cparams(),
        name="step_front",
    )(u, qkv, conv_hist_flat, sc_hist_flat, *consts)


def _rows_to_cols(x):
    pad = jnp.zeros((LANES - x.shape[0], LANES), x.dtype)
    return jnp.concatenate([x, pad], axis=0).T


def _step_delta_kernel(qkvp_ref, ab_ref, zs_ref, s_ref, dng_ref, o_ref, sn_ref, *, bt):
    ab = ab_ref[...]
    for h in range(N_HEADS):
        lo = h * HEAD_DIM
        q = qkvp_ref[:, lo:lo + HEAD_DIM]
        k = qkvp_ref[:, DN_WIDTH + lo:DN_WIDTH + lo + HEAD_DIM]
        v = qkvp_ref[:, 2 * DN_WIDTH + lo:2 * DN_WIDTH + lo + HEAD_DIM]
        q_cols = _rows_to_cols(q)
        k_cols = _rows_to_cols(k)
        qk = jnp.sum(q * k, axis=-1, keepdims=True)
        alpha = jnp.exp(ab[:, h:h + 1])
        beta = ab[:, N_HEADS + h:N_HEADS + h + 1]
        seqs = range(bt)
        kc = [k_cols[:, j:j + 1] for j in seqs]
        a = [alpha[j:j + 1, :] for j in seqs]
        s_k = [jnp.sum(s_ref[j, h] * kc[j], axis=0, keepdims=True) for j in seqs]
        s_q = [jnp.sum(s_ref[j, h] * q_cols[:, j:j + 1], axis=0, keepdims=True) for j in seqs]
        u = [beta[j:j + 1, :] * (v[j:j + 1, :] - a[j] * s_k[j]) for j in seqs]
        for j in seqs:
            sn_ref[j, h] = a[j] * s_ref[j, h] + kc[j] * u[j]
        o = jnp.concatenate([a[j] * s_q[j] + qk[j:j + 1, :] * u[j] for j in seqs], axis=0)
        o_ref[:, lo:lo + HEAD_DIM] = _gated_head_norm(
            o, dng_ref[...], zs_ref[:, lo:lo + HEAD_DIM]).astype(o_ref.dtype)


def _step_delta(qkvp, ab, zs, s0, dng, *, bt):
    n = qkvp.shape[0]
    row = lambda c: pl.BlockSpec((bt, c), lambda i: (i, 0))
    state_spec = pl.BlockSpec((bt, N_HEADS, HEAD_DIM, HEAD_DIM), lambda i: (i, 0, 0, 0))
    return pl.pallas_call(
        functools.partial(_step_delta_kernel, bt=bt),
        grid=(n // bt,),
        in_specs=[row(3 * DN_WIDTH), row(LANES), row(DN_WIDTH), state_spec, _const_spec(dng)],
        out_specs=[row(DN_WIDTH), state_spec],
        out_shape=[jax.ShapeDtypeStruct((n, DN_WIDTH), BF16), jax.ShapeDtypeStruct(s0.shape, F32)],
        compiler_params=_cparams(),
        name="step_delta",
    )(qkvp, ab, zs, s0, dng)


def _post_kernel(acta_ref, actb_ref, sga_ref, sgb_ref, x_ref, gate1_ref, scale2_ref, shift2_ref,
                 gate2_ref, cnt0_ref, gffn_ref, wpa_ref, wpb_ref, wo_ref, wrt_ref, ebias_ref,
                 wsgu_ref, wsd_ref,
                 xs_ref, h2_ref, eidx_ref, ew_ref, rank_ref, cnt_ref, scores_s, *, tm):
    i = pl.program_id(0)

    @pl.when(i == 0)
    def _():
        cnt_ref[...] = cnt0_ref[...]
        scores_s[...] = jnp.zeros(scores_s.shape, F32)

    neg = -jnp.inf
    big = jnp.int32(1 << 30)

    def project():
        y_a = jnp.dot(acta_ref[...], wpa_ref[...], preferred_element_type=F32)
        yield
        y_b = jnp.dot(actb_ref[...], wpb_ref[...], preferred_element_type=F32)
        merged = sga_ref[...] * y_a + sgb_ref[...] * y_b
        yield
        mix = _bdot(merged, wo_ref[...])
        yield
        x1 = x_ref[...] + gate1_ref[...] * mix
        y = x1 * lax.rsqrt(jnp.mean(x1 * x1, axis=-1, keepdims=True) + EPS) * gffn_ref[...]
        h2 = y * (1.0 + scale2_ref[...]) + shift2_ref[...]
        h2_ref[...] = _pack_bf16_pairs(h2)
        h2b = h2.astype(BF16)
        yield
        gu = jnp.dot(h2b, wsgu_ref[...], preferred_element_type=F32)
        yield
        shared = _bdot(_silu(gu[:, :D_SHARED]) * gu[:, D_SHARED:], wsd_ref[...])
        xs_ref[...] = x1 + gate2_ref[...] * shared
        yield
        logits_t = lax.dot_general(wrt_ref[...], h2b, NT_DIMS, preferred_element_type=F32)
        scores_s[...] = _sigmoid(logits_t)

    def first_argmax(vals, rows):
        m = jnp.max(vals, axis=0, keepdims=True)
        return m, jnp.min(jnp.where(vals == m, rows, big), axis=0, keepdims=True)

    def route(scores):
        biased = scores + ebias_ref[...]
        erow = lax.broadcasted_iota(I32, (N_EXPERTS, tm), 0)
        group_scores = []
        for g in range(N_GROUPS):
            vals = biased[g * GROUP_SIZE:(g + 1) * GROUP_SIZE, :]
            rows = lax.broadcasted_iota(I32, (GROUP_SIZE, tm), 0) + g * GROUP_SIZE
            m1, i1 = first_argmax(vals, rows)
            m2 = jnp.max(jnp.where(rows == i1, neg, vals), axis=0, keepdims=True)
            group_scores.append(m1 + m2)
            yield
        group_sel = [jnp.zeros((1, tm), jnp.bool_)] * N_GROUPS
        for _ in range(TOPK_GROUPS):
            best = functools.reduce(jnp.maximum, group_scores)
            gi = functools.reduce(
                jnp.minimum, [jnp.where(group_scores[g] == best, jnp.int32(g), big) for g in range(N_GROUPS)])
            for g in range(N_GROUPS):
                hit = gi == g
                group_sel[g] = jnp.logical_or(group_sel[g], hit)
                group_scores[g] = jnp.where(hit, neg, group_scores[g])
        cand = jnp.concatenate(
            [jnp.where(group_sel[g], biased[g * GROUP_SIZE:(g + 1) * GROUP_SIZE, :], neg)
             for g in range(N_GROUPS)], axis=0)
        yield
        w_rows, hits = [], []
        picked = jnp.zeros((N_EXPERTS, tm), F32)
        for k in range(TOP_K):
            _, ei = first_argmax(cand, erow)
            hit = erow == ei
            eidx_ref[k:k + 1, :] = ei
            w_rows.append(jnp.sum(jnp.where(hit, scores, 0.0), axis=0, keepdims=True))
            hits.append(hit)
            picked = jnp.where(hit, 1.0, picked)
            cand = jnp.where(hit, neg, cand)
            yield
        w_sum = functools.reduce(jnp.add, w_rows)
        for k in range(TOP_K):
            ew_ref[k:k + 1, :] = w_rows[k] / w_sum * ROUTE_SCALE
        yield
        trow = lax.broadcasted_iota(I32, (tm, tm), 0)
        tcol = lax.broadcasted_iota(I32, (tm, tm), 1)
        before = (trow < tcol).astype(BF16)
        prefix = jnp.dot(picked.astype(BF16), before, preferred_element_type=F32) + cnt_ref[:, 0:1]
        for k in range(TOP_K):
            rank_ref[k:k + 1, :] = jnp.sum(jnp.where(hits[k], prefix, 0.0), axis=0, keepdims=True).astype(I32)
            yield
        counted = jnp.where(i > 0, jnp.sum(picked, axis=1, keepdims=True), 0.0)
        cnt_ref[...] = cnt_ref[...] + counted

    jobs = [route(scores_s[...]), project()]
    shares = [4, 1]
    while jobs:
        for job, share in list(zip(jobs, shares)):
            for _ in range(share):
                if next(job, "done") == "done":
                    k = jobs.index(job)
                    del jobs[k], shares[k]
                    break


def _post(acta, actb, sga, sgb, x2d, gate1, scale2, shift2, gate2, cnt0, g_ffn, w, ebias_col,
          *, tm, tiles_per_mod):
    t = x2d.shape[0]
    n_tiles = t // tm
    tile = lambda i: jnp.minimum(i, n_tiles - 1)
    routed_tile = lambda i: jnp.maximum(i - 1, 0)
    row = lambda c: pl.BlockSpec((tm, c), lambda i: (tile(i), 0))
    col = lambda r: pl.BlockSpec((r, tm), lambda i: (0, routed_tile(i)))
    mod = lambda m: pl.BlockSpec((None,) + m.shape[1:], lambda i: (tile(i) // tiles_per_mod, 0, 0))
    mods = [gate1, scale2, shift2, gate2]
    consts = [cnt0, g_ffn, w["pa"], w["pb"], w["o"], w["router_t"], ebias_col, w["s_gu"], w["s_down"]]
    return pl.pallas_call(
        functools.partial(_post_kernel, tm=tm),
        grid=(n_tiles + 1,),
        in_specs=[row(CONV_CH), row(DN_WIDTH), row(D_MODEL), row(D_MODEL), row(D_MODEL)]
                 + [mod(m) for m in mods] + [_const_spec(a) for a in consts],
        out_specs=[row(D_MODEL), row(D_MODEL // 2), col(TOP_K), col(TOP_K), col(TOP_K), _const_spec(cnt0)],
        scratch_shapes=[pltpu.VMEM((N_EXPERTS, tm), F32)],
        out_shape=[jax.ShapeDtypeStruct((t, D_MODEL), F32), jax.ShapeDtypeStruct((t, D_MODEL // 2), U32),
                   jax.ShapeDtypeStruct((TOP_K, t), I32), jax.ShapeDtypeStruct((TOP_K, t), F32),
                   jax.ShapeDtypeStruct((TOP_K, t), I32), jax.ShapeDtypeStruct(cnt0.shape, F32)],
        compiler_params=_cparams(),
        name="post_mixer",
    )(acta, actb, sga, sgb, x2d, *mods, *consts)


def _dest_kernel(start_ref, eidx_ref, rank_ref, o_ref, *, width):
    step = SUBLANES * LANES
    for c0 in range(0, width, step):
        c1 = min(c0 + step, width)
        e = eidx_ref[:, c0:c1]
        base = lax.fori_loop(0, N_EXPERTS, lambda x, acc: jnp.where(e == x, start_ref[x], acc),
                             jnp.zeros(e.shape, I32), unroll=8)
        o_ref[:, c0:c1] = base + rank_ref[:, c0:c1]


def _dest(cnt_start, e_idx, rank):
    k, t = e_idx.shape
    full = pl.BlockSpec((k, t), lambda i, st: (0, 0))
    return pl.pallas_call(
        functools.partial(_dest_kernel, width=t),
        grid_spec=pltpu.PrefetchScalarGridSpec(
            num_scalar_prefetch=1, grid=(1,), in_specs=[full, full], out_specs=full),
        out_shape=jax.ShapeDtypeStruct((k, t), I32),
        compiler_params=_cparams(),
        name="moe_dest",
    )(cnt_start, e_idx, rank)


def _sc_mesh():
    return plsc.VectorSubcoreMesh(core_axis_name="core", subcore_axis_name="subcore")


def _sc_pipeline(body, n_steps, in_specs, out_specs):
    return pltpu.emit_pipeline(body, grid=(n_steps,), in_specs=in_specs, out_specs=out_specs,
                               core_axis_name=("core", "subcore"), dimension_semantics=(pltpu.PARALLEL,))


def _sc_scatter_rows(x_groups, half, dest_flat, n_rows):
    tiles_all = sum(x.shape[0] for x in x_groups) // SC_WINDOW
    n_k = dest_flat.shape[1] // (tiles_all * SC_WINDOW)

    @functools.partial(pl.kernel, mesh=_sc_mesh(), scratch_types=[],
                       out_type=jax.ShapeDtypeStruct((n_rows, QUARTER_WORDS), x_groups[0].dtype))
    def scatter(*refs):
        *x_hbms, i_hbm, o_hbm = refs

        def body(x_vmem, i_vmem):
            pltpu.sync_copy(x_vmem, o_hbm.at[i_vmem.at[0]])

        tile0 = 0
        for x_hbm in x_hbms:
            tiles = x_hbm.shape[0] // SC_WINDOW
            _sc_pipeline(body, n_k * tiles,
                         [pl.BlockSpec((SC_WINDOW, QUARTER_WORDS), lambda i, tiles=tiles: (i % tiles, half)),
                          pl.BlockSpec((1, SC_WINDOW),
                                       lambda i, tiles=tiles, tile0=tile0: (0, (i // tiles) * tiles_all + tile0 + i % tiles))],
                         [])(x_hbm, i_hbm)
            tile0 += tiles

    return scatter(*x_groups, dest_flat)


def _sc_gather_rows(table, idx_flat):
    n = idx_flat.shape[1]

    @functools.partial(pl.kernel, mesh=_sc_mesh(), scratch_types=[],
                       out_type=jax.ShapeDtypeStruct((n, table.shape[1]), table.dtype))
    def gather(t_hbm, i_hbm, o_hbm):
        def body(i_vmem, o_vmem):
            pltpu.sync_copy(t_hbm.at[i_vmem.at[0]], o_vmem)

        _sc_pipeline(body, n // SC_WINDOW,
                     [pl.BlockSpec((1, SC_WINDOW), lambda i: (0, i))],
                     [pl.BlockSpec((SC_WINDOW, table.shape[1]), lambda i: (i, 0))])(i_hbm, o_hbm)

    return gather(table, idx_flat)


def _experts_kernel(sblk_ref, sexp_ref, slo_ref, shi_ref, snext_ref, first_ref,
                    xa_ref, xb_ref, wg_hbm, wu_hbm, wd_hbm, ya_ref, yb_ref,
                    wg_buf, wu_buf, wd_buf, wgu_s, wd_s, state_ref, sems, *, bm):
    p = pl.program_id(0)
    lo = slo_ref[p]
    expert = sexp_ref[p]

    def weight_copies(e, slot):
        return [pltpu.make_async_copy(src.at[e], dst.at[slot], sems.at[slot])
                for src, dst in ((wg_hbm, wg_buf), (wu_hbm, wu_buf), (wd_hbm, wd_buf))]

    @pl.when(p == 0)
    def _():
        state_ref[0] = -1
        state_ref[1] = 0
        for c in weight_copies(first_ref[0], 0):
            c.start()

    @pl.when(shi_ref[p] > lo)
    def _():
        @pl.when(state_ref[0] != expert)
        def _():
            slot = state_ref[1]
            for c in weight_copies(expert, slot):
                c.wait()
            wgu_s[:, :D_EXPERT] = wg_buf[slot].astype(BF16)
            wgu_s[:, D_EXPERT:] = wu_buf[slot].astype(BF16)
            wd_s[...] = wd_buf[slot].astype(BF16)
            state_ref[0] = expert
            state_ref[1] = 1 - slot
            upcoming = snext_ref[p]

            @pl.when(upcoming >= 0)
            def _():
                for c in weight_copies(upcoming, 1 - slot):
                    c.start()

        q = QUARTER_WORDS
        sub = bm // EXPERT_SUBBLOCKS
        first_sub = lo // sub
        last_sub = (shi_ref[p] - 1) // sub

        def run(r0, r1):
            quarters = {}
            quarters[0], quarters[2] = _unpack_bf16_pairs(xa_ref[r0:r1, :])
            quarters[1], quarters[3] = _unpack_bf16_pairs(xb_ref[r0:r1, :])
            gu = functools.reduce(jnp.add, [
                jnp.dot(quarters[c], wgu_s[c * q:(c + 1) * q, :], preferred_element_type=F32) for c in range(4)])
            hb = (_silu(gu[:, :D_EXPERT]) * gu[:, D_EXPERT:]).astype(BF16)
            y = _pack_bf16_pairs(jnp.dot(hb, wd_s[...], preferred_element_type=F32))

            @pl.when(lo <= r0)
            def _():
                ya_ref[r0:r1, :] = y[:, :q]
                yb_ref[r0:r1, :] = y[:, q:]

            @pl.when(lo > r0)
            def _():
                keep = lax.broadcasted_iota(I32, (r1 - r0, 1), 0) + r0 >= lo
                ya_ref[r0:r1, :] = jnp.where(keep, y[:, :q], ya_ref[r0:r1, :])
                yb_ref[r0:r1, :] = jnp.where(keep, y[:, q:], yb_ref[r0:r1, :])

        for a in range(EXPERT_SUBBLOCKS):
            for b in range(a, EXPERT_SUBBLOCKS):
                pl.when(jnp.logical_and(first_sub == a, last_sub == b))(
                    functools.partial(run, a * sub, (b + 1) * sub))


def _experts(xs_a, xs_b, w_gate, w_up, w_down, seg_blk, seg_exp, seg_lo, seg_hi, seg_next, first_exp, *, bm):
    n_rows = xs_a.shape[0]
    n_seg = seg_blk.shape[0]
    ring = 2
    hbm = pl.BlockSpec(memory_space=pl.ANY)
    rows = pl.BlockSpec((bm, QUARTER_WORDS), lambda p, sb, *_: (sb[p], 0))
    grid_spec = pltpu.PrefetchScalarGridSpec(
        num_scalar_prefetch=6,
        grid=(n_seg,),
        in_specs=[rows, rows, hbm, hbm, hbm],
        out_specs=[rows, rows],
        scratch_shapes=[pltpu.VMEM((ring, D_MODEL, D_EXPERT), F32), pltpu.VMEM((ring, D_MODEL, D_EXPERT), F32),
                        pltpu.VMEM((ring, D_EXPERT, D_MODEL), F32),
                        pltpu.VMEM((D_MODEL, 2 * D_EXPERT), BF16), pltpu.VMEM((D_EXPERT, D_MODEL), BF16),
                        pltpu.SMEM((2,), I32), pltpu.SemaphoreType.DMA((ring,))],
    )
    return pl.pallas_call(
        functools.partial(_experts_kernel, bm=bm),
        grid_spec=grid_spec,
        out_shape=[jax.ShapeDtypeStruct((n_rows, QUARTER_WORDS), U32)] * 2,
        compiler_params=_cparams(),
        name="moe_experts",
    )(seg_blk, seg_exp, seg_lo, seg_hi, seg_next, first_exp, xs_a, xs_b, w_gate, w_up, w_down)


def _final_kernel(*refs, tm, n_prompt_tiles, has_sample, has_prev):
    refs = list(refs)
    ew_ref, pa_ref, pb_ref, xsp_ref = refs[:4]
    del refs[:4]
    xss_ref = refs.pop(0) if has_sample else None
    g2p_ref = refs.pop(0)
    g2s_ref = refs.pop(0) if has_sample else None
    gfin_ref = refs.pop(0)
    if has_prev:
        refs.pop(0)
    op_ref = refs.pop(0)
    os_ref = refs.pop(0) if has_sample else None
    ew = ew_ref[...]
    quarters = [jnp.zeros((tm, QUARTER_WORDS), F32)] * 4
    for k in range(TOP_K):
        w_k = ew[:, k:k + 1]
        a_low, a_high = _unpack_bf16_pairs(pa_ref[k])
        b_low, b_high = _unpack_bf16_pairs(pb_ref[k])
        for c, part in enumerate((a_low, b_low, a_high, b_high)):
            quarters[c] = quarters[c] + w_k * part.astype(F32)
    routed = jnp.concatenate(quarters, axis=1)

    def finish(xs_ref, gate2_ref, o_ref):
        x2 = xs_ref[...] + gate2_ref[...] * routed
        o_ref[...] = x2 * lax.rsqrt(jnp.mean(x2 * x2, axis=-1, keepdims=True) + EPS) * gfin_ref[...]

    if not has_sample:
        finish(xsp_ref, g2p_ref, op_ref)
    else:
        is_prompt = pl.program_id(0) < n_prompt_tiles
        pl.when(is_prompt)(lambda: finish(xsp_ref, g2p_ref, op_ref))
        pl.when(jnp.logical_not(is_prompt))(lambda: finish(xss_ref, g2s_ref, os_ref))


def _final(ew_part, planes_a, planes_b, xs_p, gate2_p, g_final, *, tm, tiles_per_batch, tile_lo, n_prompt_tiles,
           sample=None, y_prev=None):
    prompt_tile = lambda i: tile_lo + jnp.minimum(i, n_prompt_tiles - 1)
    sample_tile = lambda i: jnp.maximum(i - n_prompt_tiles, 0)
    prompt_rows = pl.BlockSpec((tm, D_MODEL), lambda i: (prompt_tile(i), 0))
    sample_rows = pl.BlockSpec((tm, D_MODEL), lambda i: (sample_tile(i), 0))
    planes = pl.BlockSpec((TOP_K, tm, QUARTER_WORDS), lambda i: (0, i, 0))
    has_sample = sample is not None
    has_prev = y_prev is not None
    n_tiles = n_prompt_tiles + (sample[0].shape[0] // tm if has_sample else 0)
    args = [ew_part, planes_a, planes_b, xs_p]
    in_specs = [pl.BlockSpec((tm, TOP_K), lambda i: (i, 0)), planes, planes, prompt_rows]
    if has_sample:
        args.append(sample[0])
        in_specs.append(sample_rows)
    args.append(gate2_p)
    in_specs.append(pl.BlockSpec((None, 1, D_MODEL), lambda i: (prompt_tile(i) // tiles_per_batch, 0, 0)))
    if has_sample:
        args.append(sample[1])
        in_specs.append(pl.BlockSpec((None, tm, D_MODEL), lambda i: (sample_tile(i), 0, 0)))
    args.append(g_final)
    in_specs.append(_const_spec(g_final))
    if has_prev:
        args.append(y_prev)
        in_specs.append(pl.BlockSpec(memory_space=pl.ANY))
    out_specs = [prompt_rows] + ([sample_rows] if has_sample else [])
    out_shape = [jax.ShapeDtypeStruct(xs_p.shape, F32)] + (
        [jax.ShapeDtypeStruct(sample[0].shape, F32)] if has_sample else [])
    outs = pl.pallas_call(
        functools.partial(_final_kernel, tm=tm, n_prompt_tiles=n_prompt_tiles, has_sample=has_sample,
                          has_prev=has_prev),
        grid=(n_tiles,),
        in_specs=in_specs,
        out_specs=out_specs,
        out_shape=out_shape,
        input_output_aliases={len(args) - 1: 0} if has_prev else {},
        compiler_params=_cparams(),
        name="moe_combine_final",
    )(*args)
    return outs if has_sample else outs[0]


def _pick_tile(n, preferred):
    t = min(n, preferred)
    assert n % t == 0, (n, t)
    return t


def kernel(x_prompt, x_sample, c_prompt, c_sample, state_conv, state_short_conv, state_delta, w_ada, b_ada, g_mix, w_in, conv_dw, conv_dw_b, conv_ln_g, conv_ln_b, w_pa, short_conv_w, a_log, dt_bias, dn_norm_g, w_pb, w_o, g_ffn, w_router, e_bias, w_gate, w_up, w_down, ws_gate, ws_up, ws_down, g_final):
    nb, seq, d = x_prompt.shape
    ns = x_sample.shape[0]
    assert d == D_MODEL and w_ada.shape[0] == 1 and x_sample.shape[1] == 1
    assert seq % CHUNK == 0 and seq >= CONV_HALO and ns % SUBLANES == 0
    tp = nb * seq
    row1 = lambda a: a.reshape(1, -1)

    wi = w_in[0].astype(BF16)
    o_q = 2 * CONV_CH
    o_z = o_q + 3 * DN_WIDTH
    o_a = o_z + DN_WIDTH
    o_ga = o_a + 2 * N_HEADS
    w_proj = {
        "glu": wi[:, :o_q], "qkv": wi[:, o_q:o_z], "z": wi[:, o_z:o_a],
        "ab": jnp.pad(wi[:, o_a:o_ga], ((0, 0), (0, LANES - 2 * N_HEADS))),
        "ga": wi[:, o_ga:o_ga + D_MODEL], "gb": wi[:, o_ga + D_MODEL:],
    }
    nea = jnp.pad(-jnp.exp(a_log[0].astype(F32)), (0, LANES - N_HEADS)).reshape(1, LANES)
    dtb = jnp.pad(dt_bias[0].astype(F32), (0, LANES - N_HEADS)).reshape(1, LANES)
    w_post = {
        "pa": w_pa[0].astype(BF16), "pb": w_pb[0].astype(BF16), "o": w_o[0].astype(BF16),
        "router_t": w_router[0].T.astype(BF16),
        "s_gu": jnp.concatenate([ws_gate[0], ws_up[0]], axis=1).astype(BF16),
        "s_down": ws_down[0].astype(BF16),
    }
    ebias_col = e_bias[0].astype(F32).reshape(N_EXPERTS, 1)
    g_mix1, g_ffn1, g_fin1 = row1(g_mix[0]), row1(g_ffn[0]), row1(g_final)
    cdw, cdb, clg, clb = conv_dw[0], row1(conv_dw_b[0]), row1(conv_ln_g[0]), row1(conv_ln_b[0])
    scw, dng = short_conv_w[0], row1(dn_norm_g[0])

    ada = _ada(jnp.concatenate([c_prompt, c_sample], axis=0), w_ada[0].astype(BF16), b_ada[0])
    ada = ada.reshape(nb + ns, 6, D_MODEL)
    mod_p = [ada[:nb, m].reshape(nb, 1, D_MODEL) for m in range(6)]
    mod_s = [ada[nb:, m].reshape(1, ns, D_MODEL) for m in range(6)]
    shift1, scale1, gate1, shift2, scale2, gate2 = range(6)

    tm_p = _pick_tile(seq, 256)
    tpm_p = seq // tm_p
    xp = x_prompt.reshape(tp, D_MODEL)
    u_p, qkv_p, zs_p, ab_p, sga_p, sgb_p = _in_proj(
        xp, mod_p[scale1], mod_p[shift1], g_mix1, w_proj, nea, dtb, tm=tm_p, tiles_per_mod=tpm_p)
    tl = _pick_tile(seq, 256)
    acta_p = _conv_branch(u_p, jnp.zeros((nb, CONV_HALO, CONV_CH), F32), cdw, cdb, clg, clb,
                          tl=tl, tiles_per_batch=seq // tl)
    tc = _pick_tile(seq, 4 * CHUNK)
    actb_p, s_new_p = _delta_branch(
        qkv_p, jnp.zeros((nb, SC_HALO, 3 * DN_WIDTH), F32),
        jnp.zeros((nb, N_HEADS, HEAD_DIM, HEAD_DIM), F32), scw, ab_p, zs_p, dng,
        tc=tc, tiles_per_batch=seq // tc)

    xs_ = x_sample.reshape(ns, D_MODEL)
    u_s, qkv_s, zs_s, ab_s, sga_s, sgb_s = _in_proj(
        xs_, mod_s[scale1], mod_s[shift1], g_mix1, w_proj, nea, dtb, tm=ns, tiles_per_mod=1)
    bt = SUBLANES
    acta_s, nch_s, qkvp_s, nsh_s = _step_front(
        u_s, qkv_s, state_conv[0].reshape(ns, -1), state_short_conv[0].reshape(ns, -1),
        cdw, cdb, clg, clb, scw, bt=bt)
    actb_s, s_new_s = _step_delta(qkvp_s, ab_s, zs_s, state_delta[0].astype(F32), dng, bt=bt)

    cnt0 = jnp.zeros((N_EXPERTS, LANES), F32)
    xsh_p, h2_p, eidx_p, ew_p, rank_p, cnt_p = _post(
        acta_p, actb_p, sga_p, sgb_p, xp, mod_p[gate1], mod_p[scale2], mod_p[shift2], mod_p[gate2],
        cnt0, g_ffn1, w_post, ebias_col, tm=tm_p, tiles_per_mod=tpm_p)
    xsh_s, h2_s, eidx_s, ew_s, rank_s, cnt_all = _post(
        acta_s, actb_s, sga_s, sgb_s, xs_, mod_s[gate1], mod_s[scale2], mod_s[shift2], mod_s[gate2],
        cnt_p, g_ffn1, w_post, ebias_col, tm=ns, tiles_per_mod=1)

    bm = EXPERT_SUBBLOCKS * EXPERT_SUB_ROWS
    tmd = LANES
    t_all = tp + ns
    n_rows = t_all * TOP_K
    assert seq % tmd == 0 and ns % tmd == 0 and n_rows % bm == 0 and t_all % SC_WINDOW == 0
    n_blocks = n_rows // bm
    counts = cnt_all[:, 0].astype(I32)
    cnt_end = jnp.cumsum(counts)
    cnt_start = cnt_end - counts
    e_idx = jnp.concatenate([eidx_p, eidx_s], axis=1)
    rank = jnp.concatenate([rank_p, rank_s], axis=1)
    dest = _dest(cnt_start, e_idx, rank)
    dest_flat = dest.reshape(1, n_rows)
    seg_start = jnp.sort(jnp.concatenate([jnp.arange(n_blocks, dtype=I32) * bm, cnt_start]))
    seg_end = jnp.concatenate([seg_start[1:], jnp.full((1,), n_rows, I32)])
    seg_blk = jnp.minimum(seg_start // bm, n_blocks - 1)
    seg_lo = seg_start - seg_blk * bm
    seg_hi = seg_end - seg_blk * bm
    seg_exp = jnp.sum((cnt_end[None, :] <= seg_start[:, None]).astype(I32), axis=1)
    seg_exp = jnp.minimum(seg_exp, N_EXPERTS - 1)
    seg_exp = lax.cummax(jnp.where(seg_hi > seg_lo, seg_exp, 0))
    none = jnp.int32(N_EXPERTS)
    in_use_from = lax.cummin(jnp.where(counts > 0, jnp.arange(N_EXPERTS, dtype=I32), none), reverse=True)
    next_in_use = jnp.concatenate([in_use_from[1:], jnp.full((1,), none, I32)])
    next_in_use = jnp.where(next_in_use == none, -1, next_in_use)
    seg_next = next_in_use[seg_exp]
    first_exp = jnp.minimum(in_use_from[:1], N_EXPERTS - 1)

    xs_a, xs_b = (_sc_scatter_rows((h2_p, h2_s), half, dest_flat, n_rows) for half in range(2))
    ys_a, ys_b = _experts(xs_a, xs_b, w_gate[0], w_up[0], w_down[0], seg_blk, seg_exp, seg_lo, seg_hi, seg_next,
                          first_exp, bm=bm)

    ew_tok = jnp.concatenate([ew_p, ew_s], axis=1).T
    n_prompt_tiles = tp // tmd
    tiles_1 = n_prompt_tiles // 2
    t_1 = tiles_1 * tmd

    def gathered(lo, hi):
        idx = dest[:, lo:hi].reshape(1, -1)
        return [_sc_gather_rows(ys, idx).reshape(TOP_K, hi - lo, QUARTER_WORDS) for ys in (ys_a, ys_b)]

    final = functools.partial(_final, xs_p=xsh_p, gate2_p=mod_p[gate2], g_final=g_fin1, tm=tmd,
                              tiles_per_batch=seq // tmd)
    y_p = final(ew_tok[:t_1], *gathered(0, t_1), tile_lo=0, n_prompt_tiles=tiles_1)
    y_p, y_s = final(ew_tok[t_1:], *gathered(t_1, t_all), tile_lo=tiles_1, n_prompt_tiles=n_prompt_tiles - tiles_1,
                     sample=(xsh_s, mod_s[gate2].reshape(ns // tmd, tmd, D_MODEL)), y_prev=y_p)

    hist_rows = CONV_WIDTH - 1
    sc_rows = SHORT_CONV - 1
    y_prompt = y_p.reshape(nb, seq, D_MODEL)
    y_sample = y_s.reshape(ns, 1, D_MODEL)
    new_conv_prompt = u_p.reshape(nb, seq, CONV_CH)[:, seq - hist_rows:][None]
    new_sc_prompt = qkv_p.reshape(nb, seq, 3 * DN_WIDTH)[:, seq - sc_rows:][None]
    new_conv_sample = nch_s.reshape(ns, hist_rows, CONV_CH)[None]
    new_sc_sample = nsh_s.reshape(ns, sc_rows, 3 * DN_WIDTH)[None]
    return (y_prompt, y_sample, new_conv_prompt, new_sc_prompt, s_new_p[None],
            new_conv_sample, new_sc_sample, s_new_s[None])
```

```python
import functools

import jax
import jax.numpy as jnp
from jax import lax
from jax.experimental import pallas as pl
from jax.experimental.pallas import tpu as pltpu
from jax.experimental.pallas import tpu_sc as plsc

F32 = jnp.float32
BF16 = jnp.bfloat16
I32 = jnp.int32
U32 = jnp.uint32

EPS = 1e-6
D_MODEL = 1024
CONV_CH = 512
CONV_WIDTH = 31
N_HEADS = 4
HEAD_DIM = 128
DN_WIDTH = N_HEADS * HEAD_DIM
SHORT_CONV = 4
CHUNK = 64
CHUNK_SHIFT = CHUNK.bit_length() - 1
assert 1 << CHUNK_SHIFT == CHUNK
SUB = 16
SUB_SHIFT = SUB.bit_length() - 1
assert 1 << SUB_SHIFT == SUB and CHUNK % SUB == 0
N_EXPERTS = 256
TOP_K = 8
N_GROUPS = 8
GROUP_SIZE = N_EXPERTS // N_GROUPS
TOPK_GROUPS = 4
D_EXPERT = 256
D_SHARED = 256
ROUTE_SCALE = 2.5

LANES = 128
SUBLANES = 8
CONV_HALO = 32
SC_HALO = 8
VMEM_LIMIT = 48 * 1024 * 1024
EXPERT_SUBBLOCKS = 8
EXPERT_SUB_ROWS = 128
SC_WINDOW = 128
QUARTER_WORDS = D_MODEL // 4

NT_DIMS = (((1,), (1,)), ((), ()))
TN_DIMS = (((0,), (0,)), ((), ()))


def _cparams(n_grid_dims=1):
    return pltpu.CompilerParams(
        dimension_semantics=("arbitrary",) * n_grid_dims,
        vmem_limit_bytes=VMEM_LIMIT)


def _sigmoid(x):
    return jax.nn.sigmoid(x)


def _silu(x):
    return x * jax.nn.sigmoid(x)


def _bdot(a, b):
    return jnp.dot(a.astype(BF16), b.astype(BF16), preferred_element_type=F32)


def _pack_bf16_pairs(x):
    half = x.shape[1] // 2
    bits = lax.bitcast_convert_type(x.astype(BF16).astype(F32), U32)
    return bits[:, half:] | (bits[:, :half] >> 16)


def _unpack_bf16_pairs(w):
    low = lax.bitcast_convert_type(w << 16, F32).astype(BF16)
    high = lax.bitcast_convert_type(w & jnp.uint32(0xFFFF0000), F32).astype(BF16)
    return low, high


def _dot3(a, b):
    a_hi = a.astype(BF16)
    b_hi = b.astype(BF16)
    a_lo = (a - a_hi.astype(F32)).astype(BF16)
    b_lo = (b - b_hi.astype(F32)).astype(BF16)
    dot = lambda x, y: jnp.dot(x, y, preferred_element_type=F32)
    return dot(a_hi, b_hi) + (dot(a_hi, b_lo) + dot(a_lo, b_hi))


def _dot_exact_lhs(a_bf, b):
    hi = b.astype(BF16)
    r1 = b - hi.astype(F32)
    mid = r1.astype(BF16)
    low = (r1 - mid.astype(F32)).astype(BF16)
    dot = lambda p: jnp.dot(a_bf, p, preferred_element_type=F32)
    return dot(hi) + dot(mid) + dot(low)


def _ada_kernel(c_ref, w_ref, b_ref, o_ref):
    o_ref[...] = _bdot(_silu(c_ref[...]), w_ref[...]) + b_ref[...]


def _ada(c_all, w_ada_bf, b_ada):
    n = c_all.shape[0]
    width = w_ada_bf.shape[1]
    tn = 1536
    return pl.pallas_call(
        _ada_kernel,
        grid=(width // tn,),
        in_specs=[pl.BlockSpec((n, D_MODEL), lambda j: (0, 0)),
                  pl.BlockSpec((D_MODEL, tn), lambda j: (0, j)),
                  pl.BlockSpec((1, tn), lambda j: (0, j))],
        out_specs=pl.BlockSpec((n, tn), lambda j: (0, j)),
        out_shape=jax.ShapeDtypeStruct((n, width), F32),
        compiler_params=_cparams(),
        name="ada",
    )(c_all, w_ada_bf, b_ada.reshape(1, width))


def _in_proj_kernel(x_ref, scale_ref, shift_ref, g_ref, wglu_ref, wqkv_ref, wz_ref, wab_ref,
                    wga_ref, wgb_ref, nea_ref, dtb_ref,
                    u_ref, qkv_ref, zs_ref, ab_ref, sga_ref, sgb_ref):
    x = x_ref[...]
    y = x * lax.rsqrt(jnp.mean(x * x, axis=-1, keepdims=True) + EPS) * g_ref[...]
    h = (y * (1.0 + scale_ref[...]) + shift_ref[...]).astype(BF16)
    glu = jnp.dot(h, wglu_ref[...], preferred_element_type=F32)
    u_ref[...] = glu[:, :CONV_CH] * _sigmoid(glu[:, CONV_CH:])
    qkv_ref[...] = jnp.dot(h, wqkv_ref[...], preferred_element_type=F32)
    zs_ref[...] = _silu(jnp.dot(h, wz_ref[...], preferred_element_type=F32))
    ab = jnp.dot(h, wab_ref[...], preferred_element_type=F32)
    sp = ab + dtb_ref[...]
    softplus = jnp.maximum(sp, 0.0) + jnp.log1p(jnp.exp(-jnp.abs(sp)))
    lane = lax.broadcasted_iota(I32, ab.shape, 1)
    ab_ref[...] = jnp.where(lane < N_HEADS, nea_ref[...] * softplus, _sigmoid(ab))
    sga_ref[...] = _sigmoid(jnp.dot(h, wga_ref[...], preferred_element_type=F32))
    sgb_ref[...] = _sigmoid(jnp.dot(h, wgb_ref[...], preferred_element_type=F32))


def _mod_spec(mod, tiles_per_mod):
    return pl.BlockSpec((None,) + mod.shape[1:], lambda i: (i // tiles_per_mod, 0, 0))


def _const_spec(a):
    nd = a.ndim
    return pl.BlockSpec(a.shape, lambda i: (0,) * nd)


def _in_proj(x2d, scale, shift, g_mix, w, nea, dtb, *, tm, tiles_per_mod):
    t = x2d.shape[0]
    row = lambda c: pl.BlockSpec((tm, c), lambda i: (i, 0))
    consts = [g_mix, w["glu"], w["qkv"], w["z"], w["ab"], w["ga"], w["gb"], nea, dtb]
    widths = [CONV_CH, 3 * DN_WIDTH, DN_WIDTH, LANES, D_MODEL, D_MODEL]
    return pl.pallas_call(
        _in_proj_kernel,
        grid=(t // tm,),
        in_specs=[row(D_MODEL), _mod_spec(scale, tiles_per_mod), _mod_spec(shift, tiles_per_mod)]
                 + [_const_spec(a) for a in consts],
        out_specs=[row(c) for c in widths],
        out_shape=[jax.ShapeDtypeStruct((t, c), F32) for c in widths],
        compiler_params=_cparams(),
        name="in_proj",
    )(x2d, scale, shift, *consts)


def _layer_norm_swish(y, g, b):
    mu = jnp.mean(y, axis=-1, keepdims=True)
    yc = y - mu
    var = jnp.mean(yc * yc, axis=-1, keepdims=True)
    return _silu(yc * lax.rsqrt(var + EPS) * g + b)


def _conv_kernel(u_ref, halo_ref, hist_ref, w_ref, b_ref, lg_ref, lb_ref, o_ref, win_ref, shift_ref,
                 *, tl, tiles_per_batch, rows_per_pass):
    i = pl.program_id(0)
    first = (i % tiles_per_batch) == 0

    @pl.when(first)
    def _():
        win_ref[0:CONV_HALO, :] = hist_ref[...]

    @pl.when(jnp.logical_not(first))
    def _():
        win_ref[0:CONV_HALO, :] = halo_ref[...]

    win_ref[CONV_HALO:CONV_HALO + tl, :] = u_ref[...]
    tap0 = CONV_HALO - (CONV_WIDTH - 1)
    n_shifted = CONV_HALO + tl - SUBLANES
    for b in range(1, SUBLANES):
        shift_ref[b - 1] = win_ref[b:b + n_shifted, :]

    for r in range(tl // rows_per_pass):
        base = r * rows_per_pass
        acc = jnp.zeros((rows_per_pass, CONV_CH), F32)
        for j in range(CONV_WIDTH):
            a, b = divmod(tap0 + j, SUBLANES)
            start = base + a * SUBLANES
            rows = (win_ref[start:start + rows_per_pass, :] if b == 0
                    else shift_ref[b - 1, start:start + rows_per_pass, :])
            acc = acc + w_ref[j:j + 1, :] * rows
        act = _layer_norm_swish(acc + b_ref[...], lg_ref[...], lb_ref[...])
        o_ref[base:base + rows_per_pass, :] = act.astype(o_ref.dtype)


def _conv_branch(u2d, hist, conv_dw, conv_dw_b, ln_g, ln_b, *, tl, tiles_per_batch):
    t = u2d.shape[0]
    halo_blocks = tl // CONV_HALO
    consts = [conv_dw, conv_dw_b, ln_g, ln_b]
    return pl.pallas_call(
        functools.partial(_conv_kernel, tl=tl, tiles_per_batch=tiles_per_batch, rows_per_pass=64),
        grid=(t // tl,),
        in_specs=[pl.BlockSpec((tl, CONV_CH), lambda i: (i, 0)),
                  pl.BlockSpec((CONV_HALO, CONV_CH), lambda i: (jnp.maximum(i * halo_blocks - 1, 0), 0)),
                  pl.BlockSpec((None, CONV_HALO, CONV_CH), lambda i: (i // tiles_per_batch, 0, 0))]
                 + [_const_spec(a) for a in consts],
        out_specs=pl.BlockSpec((tl, CONV_CH), lambda i: (i, 0)),
        out_shape=jax.ShapeDtypeStruct((t, CONV_CH), BF16),
        scratch_shapes=[pltpu.VMEM((CONV_HALO + tl, CONV_CH), F32),
                        pltpu.VMEM((SUBLANES - 1, CONV_HALO + tl - SUBLANES, CONV_CH), F32)],
        compiler_params=_cparams(),
        name="conv_branch",
    )(u2d, u2d, hist, *consts)


def _l2n(x):
    return x * lax.rsqrt(jnp.sum(x * x, axis=-1, keepdims=True) + 1e-6)


def _gated_head_norm(o, g, zs):
    return o * lax.rsqrt(jnp.mean(o * o, axis=-1, keepdims=True) + EPS) * g * zs


def _delta_kernel(qkv_ref, halo_ref, hist_ref, scw_ref, ab_ref, zs_ref, s0_ref, dng_ref,
                  o_ref, s_ref, win_ref, c_ref, *, tc, tiles_per_batch):
    i = pl.program_id(0)
    first = (i % tiles_per_batch) == 0

    @pl.when(first)
    def _():
        win_ref[0:SC_HALO, :] = hist_ref[...]
        s_ref[...] = s0_ref[...]

    @pl.when(jnp.logical_not(first))
    def _():
        win_ref[0:SC_HALO, :] = halo_ref[...]

    win_ref[SC_HALO:SC_HALO + tc, :] = qkv_ref[...]
    tap0 = SC_HALO - (SHORT_CONV - 1)
    acc = jnp.zeros((tc, 3 * DN_WIDTH), F32)
    for j in range(SHORT_CONV):
        acc = acc + scw_ref[j:j + 1, :] * win_ref[tap0 + j:tap0 + j + tc, :]
    c_ref[...] = _silu(acc)

    row = lax.broadcasted_iota(I32, (tc, tc), 0)
    col = lax.broadcasted_iota(I32, (tc, tc), 1)
    same_chunk = lax.shift_right_logical(row, CHUNK_SHIFT) == lax.shift_right_logical(col, CHUNK_SHIFT)
    incl = jnp.logical_and(same_chunk, row >= col)
    row_sub = lax.shift_right_logical(row, SUB_SHIFT)
    col_sub = lax.shift_right_logical(col, SUB_SHIFT)
    strict_sub = jnp.logical_and(row_sub == col_sub, row > col)
    below_sub = jnp.logical_and(same_chunk, row_sub > col_sub)
    eye = (row == col).astype(F32)
    ab = ab_ref[...]
    gcum = _dot_exact_lhs(incl.astype(BF16), ab)
    gcum_t = gcum.T
    n_ch = tc // CHUNK

    heads = range(N_HEADS)
    q, k, gc, kk, a_qk, rhs, qd = [], [], [], [], [], [], []
    for h in heads:
        lo = h * HEAD_DIM
        q_h = _l2n(c_ref[:, lo:lo + HEAD_DIM]) * (HEAD_DIM ** -0.5)
        k_h = _l2n(c_ref[:, DN_WIDTH + lo:DN_WIDTH + lo + HEAD_DIM])
        v_h = c_ref[:, 2 * DN_WIDTH + lo:2 * DN_WIDTH + lo + HEAD_DIM]
        beta = ab[:, N_HEADS + h:N_HEADS + h + 1]
        gc_h = gcum[:, h:h + 1]
        gr = gcum_t[h:h + 1, :]
        decay = jnp.exp(jnp.where(incl, gc_h - gr, -jnp.inf))
        kb = k_h * beta
        kq = lax.dot_general(jnp.concatenate([kb, q_h], axis=0).astype(BF16), k_h.astype(BF16),
                             NT_DIMS, preferred_element_type=F32)
        e_gc = jnp.exp(gc_h)
        q.append(q_h)
        k.append(k_h)
        gc.append(gc_h)
        kk.append(kq[:tc] * decay)
        a_qk.append(kq[tc:] * decay)
        rhs.append(jnp.concatenate([v_h * beta, kb * e_gc], axis=1))
        qd.append(q_h * e_gc)

    pw = [jnp.where(strict_sub, kk[h], 0.0) for h in heads]
    d_inv = [eye - pw[h] for h in heads]
    for _ in range(SUB_SHIFT - 1):
        pw = [_bdot(pw[h], pw[h]) for h in heads]
        d_inv = [d_inv[h] + _bdot(d_inv[h], pw[h]) for h in heads]
    c_rhs = [_bdot(d_inv[h], rhs[h]) for h in heads]
    m_off = [_bdot(d_inv[h], jnp.where(below_sub, kk[h], 0.0)) for h in heads]
    sol = c_rhs
    for _ in range(CHUNK // SUB - 1):
        sol = [c_rhs[h] - _bdot(m_off[h], sol[h]) for h in heads]

    s = [s_ref[h] for h in heads]
    u_parts = [[] for _ in heads]
    o_parts = [[] for _ in heads]
    for c in range(n_ch):
        r0 = c * CHUNK
        for h in heads:
            u0_c = sol[h][r0:r0 + CHUNK, :HEAD_DIM]
            w_c = sol[h][r0:r0 + CHUNK, HEAD_DIM:]
            wq_s = _bdot(jnp.concatenate([w_c, qd[h][r0:r0 + CHUNK]], axis=0), s[h])
            u_c = u0_c - wq_s[:CHUNK]
            g_last = gcum[r0 + CHUNK - 1:r0 + CHUNK, h:h + 1]
            k_dec = k[h][r0:r0 + CHUNK] * jnp.exp(g_last - gc[h][r0:r0 + CHUNK])
            s[h] = jnp.exp(g_last) * s[h] + lax.dot_general(
                k_dec.astype(BF16), u_c.astype(BF16), TN_DIMS, preferred_element_type=F32)
            u_parts[h].append(u_c)
            o_parts[h].append(wq_s[CHUNK:])
    for h in heads:
        lo = h * HEAD_DIM
        s_ref[h] = s[h]
        o = jnp.concatenate(o_parts[h], axis=0) + _bdot(a_qk[h], jnp.concatenate(u_parts[h], axis=0))
        o_ref[:, lo:lo + HEAD_DIM] = _gated_head_norm(
            o, dng_ref[...], zs_ref[:, lo:lo + HEAD_DIM]).astype(o_ref.dtype)


def _delta_branch(qkv2d, hist, s0, scw, ab, zs, dng, *, tc, tiles_per_batch):
    t = qkv2d.shape[0]
    nb = s0.shape[0]
    halo_blocks = tc // SC_HALO
    wqkv = 3 * DN_WIDTH
    state_spec = pl.BlockSpec((None, N_HEADS, HEAD_DIM, HEAD_DIM), lambda i: (i // tiles_per_batch, 0, 0, 0))
    return pl.pallas_call(
        functools.partial(_delta_kernel, tc=tc, tiles_per_batch=tiles_per_batch),
        grid=(t // tc,),
        in_specs=[pl.BlockSpec((tc, wqkv), lambda i: (i, 0)),
                  pl.BlockSpec((SC_HALO, wqkv), lambda i: (jnp.maximum(i * halo_blocks - 1, 0), 0)),
                  pl.BlockSpec((None, SC_HALO, wqkv), lambda i: (i // tiles_per_batch, 0, 0)),
                  _const_spec(scw),
                  pl.BlockSpec((tc, LANES), lambda i: (i, 0)),
                  pl.BlockSpec((tc, DN_WIDTH), lambda i: (i, 0)),
                  state_spec,
                  _const_spec(dng)],
        out_specs=[pl.BlockSpec((tc, DN_WIDTH), lambda i: (i, 0)), state_spec],
        out_shape=[jax.ShapeDtypeStruct((t, DN_WIDTH), BF16),
                   jax.ShapeDtypeStruct((nb, N_HEADS, HEAD_DIM, HEAD_DIM), F32)],
        scratch_shapes=[pltpu.VMEM((SC_HALO + tc, wqkv), F32), pltpu.VMEM((tc, wqkv), F32)],
        compiler_params=_cparams(),
        name="delta_branch",
    )(qkv2d, qkv2d, hist, scw, ab, zs, s0, dng)


def _step_front_kernel(u_ref, qkv_ref, ch_ref, sh_ref, cw_ref, cb_ref, lg_ref, lb_ref, scw_ref,
                       acta_ref, nch_ref, qkvp_ref, nsh_ref):
    hist_rows = CONV_WIDTH - 1
    u = u_ref[...]
    acc = cw_ref[hist_rows:hist_rows + 1, :] * u
    for j in range(hist_rows):
        acc = acc + cw_ref[j:j + 1, :] * ch_ref[:, j, :]
    acta_ref[...] = _layer_norm_swish(acc + cb_ref[...], lg_ref[...], lb_ref[...]).astype(acta_ref.dtype)
    for j in range(hist_rows - 1):
        nch_ref[:, j, :] = ch_ref[:, j + 1, :]
    nch_ref[:, hist_rows - 1, :] = u

    sc_rows = SHORT_CONV - 1
    qkv = qkv_ref[...]
    acc = scw_ref[sc_rows:sc_rows + 1, :] * qkv
    for j in range(sc_rows):
        acc = acc + scw_ref[j:j + 1, :] * sh_ref[:, j, :]
    c = _silu(acc)
    for h in range(N_HEADS):
        lo = h * HEAD_DIM
        qkvp_ref[:, lo:lo + HEAD_DIM] = _l2n(c[:, lo:lo + HEAD_DIM]) * (HEAD_DIM ** -0.5)
        qkvp_ref[:, DN_WIDTH + lo:DN_WIDTH + lo + HEAD_DIM] = _l2n(c[:, DN_WIDTH + lo:DN_WIDTH + lo + HEAD_DIM])
    qkvp_ref[:, 2 * DN_WIDTH:] = c[:, 2 * DN_WIDTH:]
    for j in range(sc_rows - 1):
        nsh_ref[:, j, :] = sh_ref[:, j + 1, :]
    nsh_ref[:, sc_rows - 1, :] = qkv


def _step_front(u, qkv, conv_hist, sc_hist, conv_dw, conv_dw_b, ln_g, ln_b, scw, *, bt):
    n = u.shape[0]
    wq = 3 * DN_WIDTH
    consts = [conv_dw, conv_dw_b, ln_g, ln_b, scw]
    row = lambda c: pl.BlockSpec((bt, c), lambda i: (i, 0))
    hist = lambda a: pl.BlockSpec((bt,) + a.shape[1:], lambda i: (i, 0, 0))
    return pl.pallas_call(
        _step_front_kernel,
        grid=(n // bt,),
        in_specs=[row(CONV_CH), row(wq), hist(conv_hist), hist(sc_hist)] + [_const_spec(a) for a in consts],
        out_specs=[row(CONV_CH), hist(conv_hist), row(wq), hist(sc_hist)],
        out_shape=[jax.ShapeDtypeStruct((n, CONV_CH), BF16), jax.ShapeDtypeStruct(conv_hist.shape, F32),
                   jax.ShapeDtypeStruct((n, wq), F32), jax.ShapeDtypeStruct(sc_hist.shape, F32)],
        compiler_params=_cparams(),
        name="step_front",
    )(u, qkv, conv_hist, sc_hist, *consts)


def _rows_to_cols(x):
    pad = jnp.zeros((LANES - x.shape[0], LANES), x.dtype)
    return jnp.concatenate([x, pad], axis=0).T


def _step_delta_kernel(qkvp_ref, ab_ref, zs_ref, s_ref, dng_ref, o_ref, sn_ref, *, bt):
    ab = ab_ref[...]
    for h in range(N_HEADS):
        lo = h * HEAD_DIM
        q = qkvp_ref[:, lo:lo + HEAD_DIM]
        k = qkvp_ref[:, DN_WIDTH + lo:DN_WIDTH + lo + HEAD_DIM]
        v = qkvp_ref[:, 2 * DN_WIDTH + lo:2 * DN_WIDTH + lo + HEAD_DIM]
        q_cols = _rows_to_cols(q)
        k_cols = _rows_to_cols(k)
        qk = jnp.sum(q * k, axis=-1, keepdims=True)
        alpha = jnp.exp(ab[:, h:h + 1])
        beta = ab[:, N_HEADS + h:N_HEADS + h + 1]
        seqs = range(bt)
        kc = [k_cols[:, j:j + 1] for j in seqs]
        a = [alpha[j:j + 1, :] for j in seqs]
        s_k = [jnp.sum(s_ref[j, h] * kc[j], axis=0, keepdims=True) for j in seqs]
        s_q = [jnp.sum(s_ref[j, h] * q_cols[:, j:j + 1], axis=0, keepdims=True) for j in seqs]
        u = [beta[j:j + 1, :] * (v[j:j + 1, :] - a[j] * s_k[j]) for j in seqs]
        for j in seqs:
            sn_ref[j, h] = a[j] * s_ref[j, h] + kc[j] * u[j]
        o = jnp.concatenate([a[j] * s_q[j] + qk[j:j + 1, :] * u[j] for j in seqs], axis=0)
        o_ref[:, lo:lo + HEAD_DIM] = _gated_head_norm(
            o, dng_ref[...], zs_ref[:, lo:lo + HEAD_DIM]).astype(o_ref.dtype)


def _step_delta(qkvp, ab, zs, s0, dng, *, bt):
    n = qkvp.shape[0]
    row = lambda c: pl.BlockSpec((bt, c), lambda i: (i, 0))
    state_spec = pl.BlockSpec((bt, N_HEADS, HEAD_DIM, HEAD_DIM), lambda i: (i, 0, 0, 0))
    return pl.pallas_call(
        functools.partial(_step_delta_kernel, bt=bt),
        grid=(n // bt,),
        in_specs=[row(3 * DN_WIDTH), row(LANES), row(DN_WIDTH), state_spec, _const_spec(dng)],
        out_specs=[row(DN_WIDTH), state_spec],
        out_shape=[jax.ShapeDtypeStruct((n, DN_WIDTH), BF16), jax.ShapeDtypeStruct(s0.shape, F32)],
        compiler_params=_cparams(),
        name="step_delta",
    )(qkvp, ab, zs, s0, dng)


def _post_kernel(acta_ref, actb_ref, sga_ref, sgb_ref, x_ref, gate1_ref, scale2_ref, shift2_ref,
                 gate2_ref, cnt0_ref, gffn_ref, wpa_ref, wpb_ref, wo_ref, wrt_ref, ebias_ref,
                 wsgu_ref, wsd_ref,
                 xs_ref, h2_ref, eidx_ref, ew_ref, rank_ref, cnt_ref, scores_s, *, tm):
    i = pl.program_id(0)

    @pl.when(i == 0)
    def _():
        cnt_ref[...] = cnt0_ref[...]
        scores_s[...] = jnp.zeros(scores_s.shape, F32)

    neg = -jnp.inf
    big = jnp.int32(1 << 30)

    def project():
        y_a = jnp.dot(acta_ref[...], wpa_ref[...], preferred_element_type=F32)
        yield
        y_b = jnp.dot(actb_ref[...], wpb_ref[...], preferred_element_type=F32)
        merged = sga_ref[...] * y_a + sgb_ref[...] * y_b
        yield
        mix = _bdot(merged, wo_ref[...])
        yield
        x1 = x_ref[...] + gate1_ref[...] * mix
        y = x1 * lax.rsqrt(jnp.mean(x1 * x1, axis=-1, keepdims=True) + EPS) * gffn_ref[...]
        h2 = y * (1.0 + scale2_ref[...]) + shift2_ref[...]
        h2_ref[...] = _pack_bf16_pairs(h2)
        h2b = h2.astype(BF16)
        yield
        gu = jnp.dot(h2b, wsgu_ref[...], preferred_element_type=F32)
        yield
        shared = _bdot(_silu(gu[:, :D_SHARED]) * gu[:, D_SHARED:], wsd_ref[...])
        xs_ref[...] = x1 + gate2_ref[...] * shared
        yield
        logits_t = lax.dot_general(wrt_ref[...], h2b, NT_DIMS, preferred_element_type=F32)
        scores_s[...] = _sigmoid(logits_t)

    def first_argmax(vals, rows):
        m = jnp.max(vals, axis=0, keepdims=True)
        return m, jnp.min(jnp.where(vals == m, rows, big), axis=0, keepdims=True)

    def route(scores):
        biased = scores + ebias_ref[...]
        erow = lax.broadcasted_iota(I32, (N_EXPERTS, tm), 0)
        group_scores = []
        for g in range(N_GROUPS):
            vals = biased[g * GROUP_SIZE:(g + 1) * GROUP_SIZE, :]
            rows = lax.broadcasted_iota(I32, (GROUP_SIZE, tm), 0) + g * GROUP_SIZE
            m1, i1 = first_argmax(vals, rows)
            m2 = jnp.max(jnp.where(rows == i1, neg, vals), axis=0, keepdims=True)
            group_scores.append(m1 + m2)
            yield
        group_sel = [jnp.zeros((1, tm), jnp.bool_)] * N_GROUPS
        for _ in range(TOPK_GROUPS):
            best = functools.reduce(jnp.maximum, group_scores)
            gi = functools.reduce(
                jnp.minimum, [jnp.where(group_scores[g] == best, jnp.int32(g), big) for g in range(N_GROUPS)])
            for g in range(N_GROUPS):
                hit = gi == g
                group_sel[g] = jnp.logical_or(group_sel[g], hit)
                group_scores[g] = jnp.where(hit, neg, group_scores[g])
        cand = jnp.concatenate(
            [jnp.where(group_sel[g], biased[g * GROUP_SIZE:(g + 1) * GROUP_SIZE, :], neg)
             for g in range(N_GROUPS)], axis=0)
        yield
        w_rows, hits = [], []
        picked = jnp.zeros((N_EXPERTS, tm), F32)
        for k in range(TOP_K):
            _, ei = first_argmax(cand, erow)
            hit = erow == ei
            eidx_ref[k:k + 1, :] = ei
            w_rows.append(jnp.sum(jnp.where(hit, scores, 0.0), axis=0, keepdims=True))
            hits.append(hit)
            picked = jnp.where(hit, 1.0, picked)
            cand = jnp.where(hit, neg, cand)
            yield
        w_sum = functools.reduce(jnp.add, w_rows)
        for k in range(TOP_K):
            ew_ref[k:k + 1, :] = w_rows[k] / w_sum * ROUTE_SCALE
        yield
        trow = lax.broadcasted_iota(I32, (tm, tm), 0)
        tcol = lax.broadcasted_iota(I32, (tm, tm), 1)
        before = (trow < tcol).astype(BF16)
        prefix = jnp.dot(picked.astype(BF16), before, preferred_element_type=F32) + cnt_ref[:, 0:1]
        for k in range(TOP_K):
            rank_ref[k:k + 1, :] = jnp.sum(jnp.where(hits[k], prefix, 0.0), axis=0, keepdims=True).astype(I32)
            yield
        counted = jnp.where(i > 0, jnp.sum(picked, axis=1, keepdims=True), 0.0)
        cnt_ref[...] = cnt_ref[...] + counted

    jobs = [route(scores_s[...]), project()]
    shares = [4, 1]
    while jobs:
        for job, share in list(zip(jobs, shares)):
            for _ in range(share):
                if next(job, "done") == "done":
                    k = jobs.index(job)
                    del jobs[k], shares[k]
                    break


def _post(acta, actb, sga, sgb, x2d, gate1, scale2, shift2, gate2, cnt0, g_ffn, w, ebias_col,
          *, tm, tiles_per_mod):
    t = x2d.shape[0]
    n_tiles = t // tm
    tile = lambda i: jnp.minimum(i, n_tiles - 1)
    routed_tile = lambda i: jnp.maximum(i - 1, 0)
    row = lambda c: pl.BlockSpec((tm, c), lambda i: (tile(i), 0))
    col = lambda r: pl.BlockSpec((r, tm), lambda i: (0, routed_tile(i)))
    mod = lambda m: pl.BlockSpec((None,) + m.shape[1:], lambda i: (tile(i) // tiles_per_mod, 0, 0))
    mods = [gate1, scale2, shift2, gate2]
    consts = [cnt0, g_ffn, w["pa"], w["pb"], w["o"], w["router_t"], ebias_col, w["s_gu"], w["s_down"]]
    return pl.pallas_call(
        functools.partial(_post_kernel, tm=tm),
        grid=(n_tiles + 1,),
        in_specs=[row(CONV_CH), row(DN_WIDTH), row(D_MODEL), row(D_MODEL), row(D_MODEL)]
                 + [mod(m) for m in mods] + [_const_spec(a) for a in consts],
        out_specs=[row(D_MODEL), row(D_MODEL // 2), col(TOP_K), col(TOP_K), col(TOP_K), _const_spec(cnt0)],
        scratch_shapes=[pltpu.VMEM((N_EXPERTS, tm), F32)],
        out_shape=[jax.ShapeDtypeStruct((t, D_MODEL), F32), jax.ShapeDtypeStruct((t, D_MODEL // 2), U32),
                   jax.ShapeDtypeStruct((TOP_K, t), I32), jax.ShapeDtypeStruct((TOP_K, t), F32),
                   jax.ShapeDtypeStruct((TOP_K, t), I32), jax.ShapeDtypeStruct(cnt0.shape, F32)],
        compiler_params=_cparams(),
        name="post_mixer",
    )(acta, actb, sga, sgb, x2d, *mods, *consts)


def _dest_kernel(start_ref, eidx_ref, rank_ref, o_ref, *, width):
    step = SUBLANES * LANES
    for c0 in range(0, width, step):
        c1 = min(c0 + step, width)
        e = eidx_ref[:, c0:c1]
        base = lax.fori_loop(0, N_EXPERTS, lambda x, acc: jnp.where(e == x, start_ref[x], acc),
                             jnp.zeros(e.shape, I32), unroll=8)
        o_ref[:, c0:c1] = base + rank_ref[:, c0:c1]


def _dest(cnt_start, e_idx, rank):
    k, t = e_idx.shape
    full = pl.BlockSpec((k, t), lambda i, st: (0, 0))
    return pl.pallas_call(
        functools.partial(_dest_kernel, width=t),
        grid_spec=pltpu.PrefetchScalarGridSpec(
            num_scalar_prefetch=1, grid=(1,), in_specs=[full, full], out_specs=full),
        out_shape=jax.ShapeDtypeStruct((k, t), I32),
        compiler_params=_cparams(),
        name="moe_dest",
    )(cnt_start, e_idx, rank)


def _sc_mesh():
    return plsc.VectorSubcoreMesh(core_axis_name="core", subcore_axis_name="subcore")


def _sc_pipeline(body, n_steps, in_specs, out_specs):
    return pltpu.emit_pipeline(body, grid=(n_steps,), in_specs=in_specs, out_specs=out_specs,
                               core_axis_name=("core", "subcore"), dimension_semantics=(pltpu.PARALLEL,))


def _sc_scatter_rows(x_groups, half, dest_flat, n_rows):
    tiles_all = sum(x.shape[0] for x in x_groups) // SC_WINDOW
    n_k = dest_flat.shape[1] // (tiles_all * SC_WINDOW)

    @functools.partial(pl.kernel, mesh=_sc_mesh(), scratch_types=[],
                       out_type=jax.ShapeDtypeStruct((n_rows, QUARTER_WORDS), x_groups[0].dtype))
    def scatter(*refs):
        *x_hbms, i_hbm, o_hbm = refs

        def body(x_vmem, i_vmem):
            pltpu.sync_copy(x_vmem, o_hbm.at[i_vmem.at[0]])

        tile0 = 0
        for x_hbm in x_hbms:
            tiles = x_hbm.shape[0] // SC_WINDOW
            _sc_pipeline(body, n_k * tiles,
                         [pl.BlockSpec((SC_WINDOW, QUARTER_WORDS), lambda i, tiles=tiles: (i % tiles, half)),
                          pl.BlockSpec((1, SC_WINDOW),
                                       lambda i, tiles=tiles, tile0=tile0: (0, (i // tiles) * tiles_all + tile0 + i % tiles))],
                         [])(x_hbm, i_hbm)
            tile0 += tiles

    return scatter(*x_groups, dest_flat)


def _sc_gather_rows(table, idx_flat):
    n = idx_flat.shape[1]

    @functools.partial(pl.kernel, mesh=_sc_mesh(), scratch_types=[],
                       out_type=jax.ShapeDtypeStruct((n, table.shape[1]), table.dtype))
    def gather(t_hbm, i_hbm, o_hbm):
        def body(i_vmem, o_vmem):
            pltpu.sync_copy(t_hbm.at[i_vmem.at[0]], o_vmem)

        _sc_pipeline(body, n // SC_WINDOW,
                     [pl.BlockSpec((1, SC_WINDOW), lambda i: (0, i))],
                     [pl.BlockSpec((SC_WINDOW, table.shape[1]), lambda i: (i, 0))])(i_hbm, o_hbm)

    return gather(table, idx_flat)


def _experts_kernel(sblk_ref, sexp_ref, slo_ref, shi_ref, snext_ref, first_ref,
                    xa_ref, xb_ref, wg_hbm, wu_hbm, wd_hbm, ya_ref, yb_ref,
                    wg_buf, wu_buf, wd_buf, wgu_s, wd_s, state_ref, sems, *, bm):
    p = pl.program_id(0)
    lo = slo_ref[p]
    expert = sexp_ref[p]

    def weight_copies(e, slot):
        return [pltpu.make_async_copy(src.at[e], dst.at[slot], sems.at[slot])
                for src, dst in ((wg_hbm, wg_buf), (wu_hbm, wu_buf), (wd_hbm, wd_buf))]

    @pl.when(p == 0)
    def _():
        state_ref[0] = -1
        state_ref[1] = 0
        for c in weight_copies(first_ref[0], 0):
            c.start()

    @pl.when(shi_ref[p] > lo)
    def _():
        @pl.when(state_ref[0] != expert)
        def _():
            slot = state_ref[1]
            for c in weight_copies(expert, slot):
                c.wait()
            wgu_s[:, :D_EXPERT] = wg_buf[slot].astype(BF16)
            wgu_s[:, D_EXPERT:] = wu_buf[slot].astype(BF16)
            wd_s[...] = wd_buf[slot].astype(BF16)
            state_ref[0] = expert
            state_ref[1] = 1 - slot
            upcoming = snext_ref[p]

            @pl.when(upcoming >= 0)
            def _():
                for c in weight_copies(upcoming, 1 - slot):
                    c.start()

        q = QUARTER_WORDS
        sub = bm // EXPERT_SUBBLOCKS
        first_sub = lo // sub
        last_sub = (shi_ref[p] - 1) // sub

        def run(r0, r1):
            quarters = {}
            quarters[0], quarters[2] = _unpack_bf16_pairs(xa_ref[r0:r1, :])
            quarters[1], quarters[3] = _unpack_bf16_pairs(xb_ref[r0:r1, :])
            gu = functools.reduce(jnp.add, [
                jnp.dot(quarters[c], wgu_s[c * q:(c + 1) * q, :], preferred_element_type=F32) for c in range(4)])
            hb = (_silu(gu[:, :D_EXPERT]) * gu[:, D_EXPERT:]).astype(BF16)
            y = _pack_bf16_pairs(jnp.dot(hb, wd_s[...], preferred_element_type=F32))

            @pl.when(lo <= r0)
            def _():
                ya_ref[r0:r1, :] = y[:, :q]
                yb_ref[r0:r1, :] = y[:, q:]

            @pl.when(lo > r0)
            def _():
                keep = lax.broadcasted_iota(I32, (r1 - r0, 1), 0) + r0 >= lo
                ya_ref[r0:r1, :] = jnp.where(keep, y[:, :q], ya_ref[r0:r1, :])
                yb_ref[r0:r1, :] = jnp.where(keep, y[:, q:], yb_ref[r0:r1, :])

        for a in range(EXPERT_SUBBLOCKS):
            for b in range(a, EXPERT_SUBBLOCKS):
                pl.when(jnp.logical_and(first_sub == a, last_sub == b))(
                    functools.partial(run, a * sub, (b + 1) * sub))


def _experts(xs_a, xs_b, w_gate, w_up, w_down, seg_blk, seg_exp, seg_lo, seg_hi, seg_next, first_exp, *, bm):
    n_rows = xs_a.shape[0]
    n_seg = seg_blk.shape[0]
    ring = 2
    hbm = pl.BlockSpec(memory_space=pl.ANY)
    rows = pl.BlockSpec((bm, QUARTER_WORDS), lambda p, sb, *_: (sb[p], 0))
    grid_spec = pltpu.PrefetchScalarGridSpec(
        num_scalar_prefetch=6,
        grid=(n_seg,),
        in_specs=[rows, rows, hbm, hbm, hbm],
        out_specs=[rows, rows],
        scratch_shapes=[pltpu.VMEM((ring, D_MODEL, D_EXPERT), F32), pltpu.VMEM((ring, D_MODEL, D_EXPERT), F32),
                        pltpu.VMEM((ring, D_EXPERT, D_MODEL), F32),
                        pltpu.VMEM((D_MODEL, 2 * D_EXPERT), BF16), pltpu.VMEM((D_EXPERT, D_MODEL), BF16),
                        pltpu.SMEM((2,), I32), pltpu.SemaphoreType.DMA((ring,))],
    )
    return pl.pallas_call(
        functools.partial(_experts_kernel, bm=bm),
        grid_spec=grid_spec,
        out_shape=[jax.ShapeDtypeStruct((n_rows, QUARTER_WORDS), U32)] * 2,
        compiler_params=_cparams(),
        name="moe_experts",
    )(seg_blk, seg_exp, seg_lo, seg_hi, seg_next, first_exp, xs_a, xs_b, w_gate, w_up, w_down)


def _final_kernel(*refs, tm, n_prompt_tiles, has_sample, has_prev):
    refs = list(refs)
    ew_ref, pa_ref, pb_ref, xsp_ref = refs[:4]
    del refs[:4]
    xss_ref = refs.pop(0) if has_sample else None
    g2p_ref = refs.pop(0)
    g2s_ref = refs.pop(0) if has_sample else None
    gfin_ref = refs.pop(0)
    if has_prev:
        refs.pop(0)
    op_ref = refs.pop(0)
    os_ref = refs.pop(0) if has_sample else None
    ew = ew_ref[...]
    quarters = [jnp.zeros((tm, QUARTER_WORDS), F32)] * 4
    for k in range(TOP_K):
        w_k = ew[:, k:k + 1]
        a_low, a_high = _unpack_bf16_pairs(pa_ref[k])
        b_low, b_high = _unpack_bf16_pairs(pb_ref[k])
        for c, part in enumerate((a_low, b_low, a_high, b_high)):
            quarters[c] = quarters[c] + w_k * part.astype(F32)
    routed = jnp.concatenate(quarters, axis=1)

    def finish(xs_ref, gate2_ref, o_ref):
        x2 = xs_ref[...] + gate2_ref[...] * routed
        o_ref[...] = x2 * lax.rsqrt(jnp.mean(x2 * x2, axis=-1, keepdims=True) + EPS) * gfin_ref[...]

    if not has_sample:
        finish(xsp_ref, g2p_ref, op_ref)
    else:
        is_prompt = pl.program_id(0) < n_prompt_tiles
        pl.when(is_prompt)(lambda: finish(xsp_ref, g2p_ref, op_ref))
        pl.when(jnp.logical_not(is_prompt))(lambda: finish(xss_ref, g2s_ref, os_ref))


def _final(ew_part, planes_a, planes_b, xs_p, gate2_p, g_final, *, tm, tiles_per_batch, tile_lo, n_prompt_tiles,
           sample=None, y_prev=None):
    prompt_tile = lambda i: tile_lo + jnp.minimum(i, n_prompt_tiles - 1)
    sample_tile = lambda i: jnp.maximum(i - n_prompt_tiles, 0)
    prompt_rows = pl.BlockSpec((tm, D_MODEL), lambda i: (prompt_tile(i), 0))
    sample_rows = pl.BlockSpec((tm, D_MODEL), lambda i: (sample_tile(i), 0))
    planes = pl.BlockSpec((TOP_K, tm, QUARTER_WORDS), lambda i: (0, i, 0))
    has_sample = sample is not None
    has_prev = y_prev is not None
    n_tiles = n_prompt_tiles + (sample[0].shape[0] // tm if has_sample else 0)
    args = [ew_part, planes_a, planes_b, xs_p]
    in_specs = [pl.BlockSpec((tm, TOP_K), lambda i: (i, 0)), planes, planes, prompt_rows]
    if has_sample:
        args.append(sample[0])
        in_specs.append(sample_rows)
    args.append(gate2_p)
    in_specs.append(pl.BlockSpec((None, 1, D_MODEL), lambda i: (prompt_tile(i) // tiles_per_batch, 0, 0)))
    if has_sample:
        args.append(sample[1])
        in_specs.append(pl.BlockSpec((None, tm, D_MODEL), lambda i: (sample_tile(i), 0, 0)))
    args.append(g_final)
    in_specs.append(_const_spec(g_final))
    if has_prev:
        args.append(y_prev)
        in_specs.append(pl.BlockSpec(memory_space=pl.ANY))
    out_specs = [prompt_rows] + ([sample_rows] if has_sample else [])
    out_shape = [jax.ShapeDtypeStruct(xs_p.shape, F32)] + (
        [jax.ShapeDtypeStruct(sample[0].shape, F32)] if has_sample else [])
    outs = pl.pallas_call(
        functools.partial(_final_kernel, tm=tm, n_prompt_tiles=n_prompt_tiles, has_sample=has_sample,
                          has_prev=has_prev),
        grid=(n_tiles,),
        in_specs=in_specs,
        out_specs=out_specs,
        out_shape=out_shape,
        input_output_aliases={len(args) - 1: 0} if has_prev else {},
        compiler_params=_cparams(),
        name="moe_combine_final",
    )(*args)
    return outs if has_sample else outs[0]


def _pick_tile(n, preferred):
    t = min(n, preferred)
    assert n % t == 0, (n, t)
    return t


def kernel(x_prompt, x_sample, c_prompt, c_sample, state_conv, state_short_conv, state_delta, w_ada, b_ada, g_mix, w_in, conv_dw, conv_dw_b, conv_ln_g, conv_ln_b, w_pa, short_conv_w, a_log, dt_bias, dn_norm_g, w_pb, w_o, g_ffn, w_router, e_bias, w_gate, w_up, w_down, ws_gate, ws_up, ws_down, g_final):
    nb, seq, d = x_prompt.shape
    ns = x_sample.shape[0]
    assert d == D_MODEL and w_ada.shape[0] == 1 and x_sample.shape[1] == 1
    assert seq % CHUNK == 0 and seq >= CONV_HALO and ns % SUBLANES == 0
    tp = nb * seq
    row1 = lambda a: a.reshape(1, -1)

    wi = w_in[0].astype(BF16)
    o_q = 2 * CONV_CH
    o_z = o_q + 3 * DN_WIDTH
    o_a = o_z + DN_WIDTH
    o_ga = o_a + 2 * N_HEADS
    w_proj = {
        "glu": wi[:, :o_q], "qkv": wi[:, o_q:o_z], "z": wi[:, o_z:o_a],
        "ab": jnp.pad(wi[:, o_a:o_ga], ((0, 0), (0, LANES - 2 * N_HEADS))),
        "ga": wi[:, o_ga:o_ga + D_MODEL], "gb": wi[:, o_ga + D_MODEL:],
    }
    nea = jnp.pad(-jnp.exp(a_log[0].astype(F32)), (0, LANES - N_HEADS)).reshape(1, LANES)
    dtb = jnp.pad(dt_bias[0].astype(F32), (0, LANES - N_HEADS)).reshape(1, LANES)
    w_post = {
        "pa": w_pa[0].astype(BF16), "pb": w_pb[0].astype(BF16), "o": w_o[0].astype(BF16),
        "router_t": w_router[0].T.astype(BF16),
        "s_gu": jnp.concatenate([ws_gate[0], ws_up[0]], axis=1).astype(BF16),
        "s_down": ws_down[0].astype(BF16),
    }
    ebias_col = e_bias[0].astype(F32).reshape(N_EXPERTS, 1)
    g_mix1, g_ffn1, g_fin1 = row1(g_mix[0]), row1(g_ffn[0]), row1(g_final)
    cdw, cdb, clg, clb = conv_dw[0], row1(conv_dw_b[0]), row1(conv_ln_g[0]), row1(conv_ln_b[0])
    scw, dng = short_conv_w[0], row1(dn_norm_g[0])

    ada = _ada(jnp.concatenate([c_prompt, c_sample], axis=0), w_ada[0].astype(BF16), b_ada[0])
    ada = ada.reshape(nb + ns, 6, D_MODEL)
    mod_p = [ada[:nb, m].reshape(nb, 1, D_MODEL) for m in range(6)]
    mod_s = [ada[nb:, m].reshape(1, ns, D_MODEL) for m in range(6)]
    shift1, scale1, gate1, shift2, scale2, gate2 = range(6)

    tm_p = _pick_tile(seq, 256)
    tpm_p = seq // tm_p
    xp = x_prompt.reshape(tp, D_MODEL)
    u_p, qkv_p, zs_p, ab_p, sga_p, sgb_p = _in_proj(
        xp, mod_p[scale1], mod_p[shift1], g_mix1, w_proj, nea, dtb, tm=tm_p, tiles_per_mod=tpm_p)
    tl = _pick_tile(seq, 256)
    acta_p = _conv_branch(u_p, jnp.zeros((nb, CONV_HALO, CONV_CH), F32), cdw, cdb, clg, clb,
                          tl=tl, tiles_per_batch=seq // tl)
    tc = _pick_tile(seq, 4 * CHUNK)
    actb_p, s_new_p = _delta_branch(
        qkv_p, jnp.zeros((nb, SC_HALO, 3 * DN_WIDTH), F32),
        jnp.zeros((nb, N_HEADS, HEAD_DIM, HEAD_DIM), F32), scw, ab_p, zs_p, dng,
        tc=tc, tiles_per_batch=seq // tc)

    xs_ = x_sample.reshape(ns, D_MODEL)
    u_s, qkv_s, zs_s, ab_s, sga_s, sgb_s = _in_proj(
        xs_, mod_s[scale1], mod_s[shift1], g_mix1, w_proj, nea, dtb, tm=ns, tiles_per_mod=1)
    bt = SUBLANES
    acta_s, nch_s, qkvp_s, nsh_s = _step_front(
        u_s, qkv_s, state_conv[0].astype(F32), state_short_conv[0].astype(F32),
        cdw, cdb, clg, clb, scw, bt=bt)
    actb_s, s_new_s = _step_delta(qkvp_s, ab_s, zs_s, state_delta[0].astype(F32), dng, bt=bt)

    cnt0 = jnp.zeros((N_EXPERTS, LANES), F32)
    xsh_p, h2_p, eidx_p, ew_p, rank_p, cnt_p = _post(
        acta_p, actb_p, sga_p, sgb_p, xp, mod_p[gate1], mod_p[scale2], mod_p[shift2], mod_p[gate2],
        cnt0, g_ffn1, w_post, ebias_col, tm=tm_p, tiles_per_mod=tpm_p)
    xsh_s, h2_s, eidx_s, ew_s, rank_s, cnt_all = _post(
        acta_s, actb_s, sga_s, sgb_s, xs_, mod_s[gate1], mod_s[scale2], mod_s[shift2], mod_s[gate2],
        cnt_p, g_ffn1, w_post, ebias_col, tm=ns, tiles_per_mod=1)

    bm = EXPERT_SUBBLOCKS * EXPERT_SUB_ROWS
    tmd = LANES
    t_all = tp + ns
    n_rows = t_all * TOP_K
    assert seq % tmd == 0 and ns % tmd == 0 and n_rows % bm == 0 and t_all % SC_WINDOW == 0
    n_blocks = n_rows // bm
    counts = cnt_all[:, 0].astype(I32)
    cnt_end = jnp.cumsum(counts)
    cnt_start = cnt_end - counts
    e_idx = jnp.concatenate([eidx_p, eidx_s], axis=1)
    rank = jnp.concatenate([rank_p, rank_s], axis=1)
    dest = _dest(cnt_start, e_idx, rank)
    dest_flat = dest.reshape(1, n_rows)
    seg_start = jnp.sort(jnp.concatenate([jnp.arange(n_blocks, dtype=I32) * bm, cnt_start]))
    seg_end = jnp.concatenate([seg_start[1:], jnp.full((1,), n_rows, I32)])
    seg_blk = jnp.minimum(seg_start // bm, n_blocks - 1)
    seg_lo = seg_start - seg_blk * bm
    seg_hi = seg_end - seg_blk * bm
    seg_exp = jnp.sum((cnt_end[None, :] <= seg_start[:, None]).astype(I32), axis=1)
    seg_exp = jnp.minimum(seg_exp, N_EXPERTS - 1)
    seg_exp = lax.cummax(jnp.where(seg_hi > seg_lo, seg_exp, 0))
    none = jnp.int32(N_EXPERTS)
    in_use_from = lax.cummin(jnp.where(counts > 0, jnp.arange(N_EXPERTS, dtype=I32), none), reverse=True)
    next_in_use = jnp.concatenate([in_use_from[1:], jnp.full((1,), none, I32)])
    next_in_use = jnp.where(next_in_use == none, -1, next_in_use)
    seg_next = next_in_use[seg_exp]
    first_exp = jnp.minimum(in_use_from[:1], N_EXPERTS - 1)

    xs_a, xs_b = (_sc_scatter_rows((h2_p, h2_s), half, dest_flat, n_rows) for half in range(2))
    ys_a, ys_b = _experts(xs_a, xs_b, w_gate[0], w_up[0], w_down[0], seg_blk, seg_exp, seg_lo, seg_hi, seg_next,
                          first_exp, bm=bm)

    ew_tok = jnp.concatenate([ew_p, ew_s], axis=1).T
    n_prompt_tiles = tp // tmd
    tiles_1 = n_prompt_tiles // 2
    t_1 = tiles_1 * tmd

    def gathered(lo, hi):
        idx = dest[:, lo:hi].reshape(1, -1)
        return [_sc_gather_rows(ys, idx).reshape(TOP_K, hi - lo, QUARTER_WORDS) for ys in (ys_a, ys_b)]

    final = functools.partial(_final, xs_p=xsh_p, gate2_p=mod_p[gate2], g_final=g_fin1, tm=tmd,
                              tiles_per_batch=seq // tmd)
    y_p = final(ew_tok[:t_1], *gathered(0, t_1), tile_lo=0, n_prompt_tiles=tiles_1)
    y_p, y_s = final(ew_tok[t_1:], *gathered(t_1, t_all), tile_lo=tiles_1, n_prompt_tiles=n_prompt_tiles - tiles_1,
                     sample=(xsh_s, mod_s[gate2].reshape(ns // tmd, tmd, D_MODEL)), y_prev=y_p)

    hist_rows = CONV_WIDTH - 1
    sc_rows = SHORT_CONV - 1
    y_prompt = y_p.reshape(nb, seq, D_MODEL)
    y_sample = y_s.reshape(ns, 1, D_MODEL)
    new_conv_prompt = u_p.reshape(nb, seq, CONV_CH)[:, seq - hist_rows:][None]
    new_sc_prompt = qkv_p.reshape(nb, seq, 3 * DN_WIDTH)[:, seq - sc_rows:][None]
    new_conv_sample = nch_s[None]
    new_sc_sample = nsh_s[None]
    return (y_prompt, y_sample, new_conv_prompt, new_sc_prompt, s_new_p[None],
            new_conv_sample, new_sc_sample, s_new_s[None])
```

```python
import functools

import jax
import jax.numpy as jnp
from jax import lax
from jax.experimental import pallas as pl
from jax.experimental.pallas import tpu as pltpu
from jax.experimental.pallas import tpu_sc as plsc

F32 = jnp.float32
BF16 = jnp.bfloat16
I32 = jnp.int32
U32 = jnp.uint32

EPS = 1e-6
D_MODEL = 1024
CONV_CH = 512
CONV_WIDTH = 31
N_HEADS = 4
HEAD_DIM = 128
DN_WIDTH = N_HEADS * HEAD_DIM
SHORT_CONV = 4
CHUNK = 64
CHUNK_SHIFT = CHUNK.bit_length() - 1
assert 1 << CHUNK_SHIFT == CHUNK
SUB = 16
SUB_SHIFT = SUB.bit_length() - 1
assert 1 << SUB_SHIFT == SUB and CHUNK % SUB == 0
N_EXPERTS = 256
TOP_K = 8
N_GROUPS = 8
GROUP_SIZE = N_EXPERTS // N_GROUPS
TOPK_GROUPS = 4
D_EXPERT = 256
D_SHARED = 256
ROUTE_SCALE = 2.5

LANES = 128
SUBLANES = 8
CONV_HALO = 32
SC_HALO = 8
VMEM_LIMIT = 48 * 1024 * 1024
EXPERT_SUBBLOCKS = 8
EXPERT_SUB_ROWS = 128
SC_WINDOW = 128
QUARTER_WORDS = D_MODEL // 4

NT_DIMS = (((1,), (1,)), ((), ()))
TN_DIMS = (((0,), (0,)), ((), ()))


def _cparams(n_grid_dims=1):
    return pltpu.CompilerParams(
        dimension_semantics=("arbitrary",) * n_grid_dims,
        vmem_limit_bytes=VMEM_LIMIT)


def _sigmoid(x):
    return jax.nn.sigmoid(x)


def _silu(x):
    return x * jax.nn.sigmoid(x)


def _bdot(a, b):
    return jnp.dot(a.astype(BF16), b.astype(BF16), preferred_element_type=F32)


def _pack_bf16_pairs(x):
    half = x.shape[1] // 2
    bits = lax.bitcast_convert_type(x.astype(BF16).astype(F32), U32)
    return bits[:, half:] | (bits[:, :half] >> 16)


def _unpack_bf16_pairs(w):
    low = lax.bitcast_convert_type(w << 16, F32).astype(BF16)
    high = lax.bitcast_convert_type(w & jnp.uint32(0xFFFF0000), F32).astype(BF16)
    return low, high


def _dot_exact_lhs(a_bf, b):
    hi = b.astype(BF16)
    r1 = b - hi.astype(F32)
    mid = r1.astype(BF16)
    low = (r1 - mid.astype(F32)).astype(BF16)
    dot = lambda p: jnp.dot(a_bf, p, preferred_element_type=F32)
    return dot(hi) + dot(mid) + dot(low)


def _ada_kernel(c_ref, w_ref, b_ref, o_ref):
    o_ref[...] = _bdot(_silu(c_ref[...]), w_ref[...]) + b_ref[...]


def _ada(c_all, w_ada_bf, b_ada):
    n = c_all.shape[0]
    width = w_ada_bf.shape[1]
    tn = 1536
    return pl.pallas_call(
        _ada_kernel,
        grid=(width // tn,),
        in_specs=[pl.BlockSpec((n, D_MODEL), lambda j: (0, 0)),
                  pl.BlockSpec((D_MODEL, tn), lambda j: (0, j)),
                  pl.BlockSpec((1, tn), lambda j: (0, j))],
        out_specs=pl.BlockSpec((n, tn), lambda j: (0, j)),
        out_shape=jax.ShapeDtypeStruct((n, width), F32),
        compiler_params=_cparams(),
        name="ada",
    )(c_all, w_ada_bf, b_ada.reshape(1, width))


def _in_proj_kernel(x_ref, scale_ref, shift_ref, g_ref, wglu_ref, wqkv_ref, wz_ref, wab_ref,
                    wga_ref, wgb_ref, nea_ref, dtb_ref,
                    u_ref, qkv_ref, zs_ref, ab_ref, sga_ref, sgb_ref):
    x = x_ref[...]
    y = x * lax.rsqrt(jnp.mean(x * x, axis=-1, keepdims=True) + EPS) * g_ref[...]
    h = (y * (1.0 + scale_ref[...]) + shift_ref[...]).astype(BF16)
    glu = jnp.dot(h, wglu_ref[...], preferred_element_type=F32)
    u_ref[...] = glu[:, :CONV_CH] * _sigmoid(glu[:, CONV_CH:])
    qkv_ref[...] = jnp.dot(h, wqkv_ref[...], preferred_element_type=F32)
    zs_ref[...] = _silu(jnp.dot(h, wz_ref[...], preferred_element_type=F32))
    ab = jnp.dot(h, wab_ref[...], preferred_element_type=F32)
    sp = ab + dtb_ref[...]
    softplus = jnp.maximum(sp, 0.0) + jnp.log1p(jnp.exp(-jnp.abs(sp)))
    lane = lax.broadcasted_iota(I32, ab.shape, 1)
    ab_ref[...] = jnp.where(lane < N_HEADS, nea_ref[...] * softplus, _sigmoid(ab))
    sga_ref[...] = _sigmoid(jnp.dot(h, wga_ref[...], preferred_element_type=F32))
    sgb_ref[...] = _sigmoid(jnp.dot(h, wgb_ref[...], preferred_element_type=F32))


def _mod_spec(mod, tiles_per_mod):
    return pl.BlockSpec((None,) + mod.shape[1:], lambda i: (i // tiles_per_mod, 0, 0))


def _const_spec(a):
    nd = a.ndim
    return pl.BlockSpec(a.shape, lambda i: (0,) * nd)


def _in_proj(x2d, scale, shift, g_mix, w, nea, dtb, *, tm, tiles_per_mod):
    t = x2d.shape[0]
    row = lambda c: pl.BlockSpec((tm, c), lambda i: (i, 0))
    consts = [g_mix, w["glu"], w["qkv"], w["z"], w["ab"], w["ga"], w["gb"], nea, dtb]
    widths = [CONV_CH, 3 * DN_WIDTH, DN_WIDTH, LANES, D_MODEL, D_MODEL]
    return pl.pallas_call(
        _in_proj_kernel,
        grid=(t // tm,),
        in_specs=[row(D_MODEL), _mod_spec(scale, tiles_per_mod), _mod_spec(shift, tiles_per_mod)]
                 + [_const_spec(a) for a in consts],
        out_specs=[row(c) for c in widths],
        out_shape=[jax.ShapeDtypeStruct((t, c), F32) for c in widths],
        compiler_params=_cparams(),
        name="in_proj",
    )(x2d, scale, shift, *consts)


def _layer_norm_swish(y, g, b):
    mu = jnp.mean(y, axis=-1, keepdims=True)
    yc = y - mu
    var = jnp.mean(yc * yc, axis=-1, keepdims=True)
    return _silu(yc * lax.rsqrt(var + EPS) * g + b)


def _conv_kernel(u_ref, halo_ref, hist_ref, w_ref, b_ref, lg_ref, lb_ref, o_ref, win_ref, shift_ref,
                 *, tl, tiles_per_batch, rows_per_pass):
    i = pl.program_id(0)
    first = (i % tiles_per_batch) == 0

    @pl.when(first)
    def _():
        win_ref[0:CONV_HALO, :] = hist_ref[...]

    @pl.when(jnp.logical_not(first))
    def _():
        win_ref[0:CONV_HALO, :] = halo_ref[...]

    win_ref[CONV_HALO:CONV_HALO + tl, :] = u_ref[...]
    tap0 = CONV_HALO - (CONV_WIDTH - 1)
    n_shifted = CONV_HALO + tl - SUBLANES
    for b in range(1, SUBLANES):
        shift_ref[b - 1] = win_ref[b:b + n_shifted, :]

    for r in range(tl // rows_per_pass):
        base = r * rows_per_pass
        acc = jnp.zeros((rows_per_pass, CONV_CH), F32)
        for j in range(CONV_WIDTH):
            a, b = divmod(tap0 + j, SUBLANES)
            start = base + a * SUBLANES
            rows = (win_ref[start:start + rows_per_pass, :] if b == 0
                    else shift_ref[b - 1, start:start + rows_per_pass, :])
            acc = acc + w_ref[j:j + 1, :] * rows
        act = _layer_norm_swish(acc + b_ref[...], lg_ref[...], lb_ref[...])
        o_ref[base:base + rows_per_pass, :] = act.astype(o_ref.dtype)


def _conv_branch(u2d, hist, conv_dw, conv_dw_b, ln_g, ln_b, *, tl, tiles_per_batch):
    t = u2d.shape[0]
    halo_blocks = tl // CONV_HALO
    consts = [conv_dw, conv_dw_b, ln_g, ln_b]
    return pl.pallas_call(
        functools.partial(_conv_kernel, tl=tl, tiles_per_batch=tiles_per_batch, rows_per_pass=64),
        grid=(t // tl,),
        in_specs=[pl.BlockSpec((tl, CONV_CH), lambda i: (i, 0)),
                  pl.BlockSpec((CONV_HALO, CONV_CH), lambda i: (jnp.maximum(i * halo_blocks - 1, 0), 0)),
                  pl.BlockSpec((None, CONV_HALO, CONV_CH), lambda i: (i // tiles_per_batch, 0, 0))]
                 + [_const_spec(a) for a in consts],
        out_specs=pl.BlockSpec((tl, CONV_CH), lambda i: (i, 0)),
        out_shape=jax.ShapeDtypeStruct((t, CONV_CH), BF16),
        scratch_shapes=[pltpu.VMEM((CONV_HALO + tl, CONV_CH), F32),
                        pltpu.VMEM((SUBLANES - 1, CONV_HALO + tl - SUBLANES, CONV_CH), F32)],
        compiler_params=_cparams(),
        name="conv_branch",
    )(u2d, u2d, hist, *consts)


def _l2n(x):
    return x * lax.rsqrt(jnp.sum(x * x, axis=-1, keepdims=True) + 1e-6)


def _gated_head_norm(o, g, zs):
    return o * lax.rsqrt(jnp.mean(o * o, axis=-1, keepdims=True) + EPS) * g * zs


def _delta_kernel(qkv_ref, halo_ref, hist_ref, scw_ref, ab_ref, zs_ref, s0_ref, dng_ref,
                  o_ref, s_ref, win_ref, c_ref, *, tc, tiles_per_batch):
    i = pl.program_id(0)
    first = (i % tiles_per_batch) == 0

    @pl.when(first)
    def _():
        win_ref[0:SC_HALO, :] = hist_ref[...]
        s_ref[...] = s0_ref[...]

    @pl.when(jnp.logical_not(first))
    def _():
        win_ref[0:SC_HALO, :] = halo_ref[...]

    win_ref[SC_HALO:SC_HALO + tc, :] = qkv_ref[...]
    tap0 = SC_HALO - (SHORT_CONV - 1)
    acc = jnp.zeros((tc, 3 * DN_WIDTH), F32)
    for j in range(SHORT_CONV):
        acc = acc + scw_ref[j:j + 1, :] * win_ref[tap0 + j:tap0 + j + tc, :]
    c_ref[...] = _silu(acc)

    row = lax.broadcasted_iota(I32, (tc, tc), 0)
    col = lax.broadcasted_iota(I32, (tc, tc), 1)
    same_chunk = lax.shift_right_logical(row, CHUNK_SHIFT) == lax.shift_right_logical(col, CHUNK_SHIFT)
    incl = jnp.logical_and(same_chunk, row >= col)
    row_sub = lax.shift_right_logical(row, SUB_SHIFT)
    col_sub = lax.shift_right_logical(col, SUB_SHIFT)
    strict_sub = jnp.logical_and(row_sub == col_sub, row > col)
    below_sub = jnp.logical_and(same_chunk, row_sub > col_sub)
    eye = (row == col).astype(F32)
    ab = ab_ref[...]
    gcum = _dot_exact_lhs(incl.astype(BF16), ab)
    gcum_t = gcum.T
    n_ch = tc // CHUNK

    heads = range(N_HEADS)
    q, k, gc, kk, a_qk, rhs, qd = [], [], [], [], [], [], []
    for h in heads:
        lo = h * HEAD_DIM
        q_h = _l2n(c_ref[:, lo:lo + HEAD_DIM]) * (HEAD_DIM ** -0.5)
        k_h = _l2n(c_ref[:, DN_WIDTH + lo:DN_WIDTH + lo + HEAD_DIM])
        v_h = c_ref[:, 2 * DN_WIDTH + lo:2 * DN_WIDTH + lo + HEAD_DIM]
        beta = ab[:, N_HEADS + h:N_HEADS + h + 1]
        gc_h = gcum[:, h:h + 1]
        gr = gcum_t[h:h + 1, :]
        decay = jnp.exp(jnp.where(incl, gc_h - gr, -jnp.inf))
        kb = k_h * beta
        kq = lax.dot_general(jnp.concatenate([kb, q_h], axis=0).astype(BF16), k_h.astype(BF16),
                             NT_DIMS, preferred_element_type=F32)
        e_gc = jnp.exp(gc_h)
        q.append(q_h)
        k.append(k_h)
        gc.append(gc_h)
        kk.append(kq[:tc] * decay)
        a_qk.append(kq[tc:] * decay)
        rhs.append(jnp.concatenate([v_h * beta, kb * e_gc], axis=1))
        qd.append(q_h * e_gc)

    pw = [jnp.where(strict_sub, kk[h], 0.0) for h in heads]
    d_inv = [eye - pw[h] for h in heads]
    for _ in range(SUB_SHIFT - 1):
        pw = [_bdot(pw[h], pw[h]) for h in heads]
        d_inv = [d_inv[h] + _bdot(d_inv[h], pw[h]) for h in heads]
    c_rhs = [_bdot(d_inv[h], rhs[h]) for h in heads]
    m_off = [_bdot(d_inv[h], jnp.where(below_sub, kk[h], 0.0)) for h in heads]
    sol = c_rhs
    for _ in range(CHUNK // SUB - 1):
        sol = [c_rhs[h] - _bdot(m_off[h], sol[h]) for h in heads]

    s = [s_ref[h] for h in heads]
    u_parts = [[] for _ in heads]
    o_parts = [[] for _ in heads]
    for c in range(n_ch):
        r0 = c * CHUNK
        for h in heads:
            u0_c = sol[h][r0:r0 + CHUNK, :HEAD_DIM]
            w_c = sol[h][r0:r0 + CHUNK, HEAD_DIM:]
            wq_s = _bdot(jnp.concatenate([w_c, qd[h][r0:r0 + CHUNK]], axis=0), s[h])
            u_c = u0_c - wq_s[:CHUNK]
            g_last = gcum[r0 + CHUNK - 1:r0 + CHUNK, h:h + 1]
            k_dec = k[h][r0:r0 + CHUNK] * jnp.exp(g_last - gc[h][r0:r0 + CHUNK])
            s[h] = jnp.exp(g_last) * s[h] + lax.dot_general(
                k_dec.astype(BF16), u_c.astype(BF16), TN_DIMS, preferred_element_type=F32)
            u_parts[h].append(u_c)
            o_parts[h].append(wq_s[CHUNK:])
    for h in heads:
        lo = h * HEAD_DIM
        s_ref[h] = s[h]
        o = jnp.concatenate(o_parts[h], axis=0) + _bdot(a_qk[h], jnp.concatenate(u_parts[h], axis=0))
        o_ref[:, lo:lo + HEAD_DIM] = _gated_head_norm(
            o, dng_ref[...], zs_ref[:, lo:lo + HEAD_DIM]).astype(o_ref.dtype)


def _delta_branch(qkv2d, hist, s0, scw, ab, zs, dng, *, tc, tiles_per_batch):
    t = qkv2d.shape[0]
    nb = s0.shape[0]
    halo_blocks = tc // SC_HALO
    wqkv = 3 * DN_WIDTH
    state_spec = pl.BlockSpec((None, N_HEADS, HEAD_DIM, HEAD_DIM), lambda i: (i // tiles_per_batch, 0, 0, 0))
    return pl.pallas_call(
        functools.partial(_delta_kernel, tc=tc, tiles_per_batch=tiles_per_batch),
        grid=(t // tc,),
        in_specs=[pl.BlockSpec((tc, wqkv), lambda i: (i, 0)),
                  pl.BlockSpec((SC_HALO, wqkv), lambda i: (jnp.maximum(i * halo_blocks - 1, 0), 0)),
                  pl.BlockSpec((None, SC_HALO, wqkv), lambda i: (i // tiles_per_batch, 0, 0)),
                  _const_spec(scw),
                  pl.BlockSpec((tc, LANES), lambda i: (i, 0)),
                  pl.BlockSpec((tc, DN_WIDTH), lambda i: (i, 0)),
                  state_spec,
                  _const_spec(dng)],
        out_specs=[pl.BlockSpec((tc, DN_WIDTH), lambda i: (i, 0)), state_spec],
        out_shape=[jax.ShapeDtypeStruct((t, DN_WIDTH), BF16),
                   jax.ShapeDtypeStruct((nb, N_HEADS, HEAD_DIM, HEAD_DIM), F32)],
        scratch_shapes=[pltpu.VMEM((SC_HALO + tc, wqkv), F32), pltpu.VMEM((tc, wqkv), F32)],
        compiler_params=_cparams(),
        name="delta_branch",
    )(qkv2d, qkv2d, hist, scw, ab, zs, s0, dng)


def _step_front_kernel(u_ref, qkv_ref, ch_ref, sh_ref, cw_ref, cb_ref, lg_ref, lb_ref, scw_ref,
                       acta_ref, nch_ref, qkvp_ref, nsh_ref):
    hist_rows = CONV_WIDTH - 1
    u = u_ref[...]
    acc = cw_ref[hist_rows:hist_rows + 1, :] * u
    for j in range(hist_rows):
        acc = acc + cw_ref[j:j + 1, :] * ch_ref[:, j * CONV_CH:(j + 1) * CONV_CH]
    acta_ref[...] = _layer_norm_swish(acc + cb_ref[...], lg_ref[...], lb_ref[...]).astype(acta_ref.dtype)
    nch_ref[:, :(hist_rows - 1) * CONV_CH] = ch_ref[:, CONV_CH:]
    nch_ref[:, (hist_rows - 1) * CONV_CH:] = u

    wq = 3 * DN_WIDTH
    sc_rows = SHORT_CONV - 1
    qkv = qkv_ref[...]
    acc = scw_ref[sc_rows:sc_rows + 1, :] * qkv
    for j in range(sc_rows):
        acc = acc + scw_ref[j:j + 1, :] * sh_ref[:, j * wq:(j + 1) * wq]
    c = _silu(acc)
    for h in range(N_HEADS):
        lo = h * HEAD_DIM
        qkvp_ref[:, lo:lo + HEAD_DIM] = _l2n(c[:, lo:lo + HEAD_DIM]) * (HEAD_DIM ** -0.5)
        qkvp_ref[:, DN_WIDTH + lo:DN_WIDTH + lo + HEAD_DIM] = _l2n(c[:, DN_WIDTH + lo:DN_WIDTH + lo + HEAD_DIM])
    qkvp_ref[:, 2 * DN_WIDTH:] = c[:, 2 * DN_WIDTH:]
    nsh_ref[:, :(sc_rows - 1) * wq] = sh_ref[:, wq:]
    nsh_ref[:, (sc_rows - 1) * wq:] = qkv


def _step_front(u, qkv, conv_hist_flat, sc_hist_flat, conv_dw, conv_dw_b, ln_g, ln_b, scw, *, bt):
    n = u.shape[0]
    wq = 3 * DN_WIDTH
    consts = [conv_dw, conv_dw_b, ln_g, ln_b, scw]
    row = lambda c: pl.BlockSpec((bt, c), lambda i: (i, 0))
    widths = [CONV_CH, conv_hist_flat.shape[1], wq, sc_hist_flat.shape[1]]
    dtypes = [BF16, F32, F32, F32]
    return pl.pallas_call(
        _step_front_kernel,
        grid=(n // bt,),
        in_specs=[row(CONV_CH), row(wq), row(widths[1]), row(widths[3])] + [_const_spec(a) for a in consts],
        out_specs=[row(c) for c in widths],
        out_shape=[jax.ShapeDtypeStruct((n, c), d) for c, d in zip(widths, dtypes)],
        compiler_params=_cparams(),
        name="step_front",
    )(u, qkv, conv_hist_flat, sc_hist_flat, *consts)


def _rows_to_cols(x):
    pad = jnp.zeros((LANES - x.shape[0], LANES), x.dtype)
    return jnp.concatenate([x, pad], axis=0).T


def _step_delta_kernel(qkvp_ref, ab_ref, zs_ref, s_ref, dng_ref, o_ref, sn_ref, *, bt):
    ab = ab_ref[...]
    for h in range(N_HEADS):
        lo = h * HEAD_DIM
        q = qkvp_ref[:, lo:lo + HEAD_DIM]
        k = qkvp_ref[:, DN_WIDTH + lo:DN_WIDTH + lo + HEAD_DIM]
        v = qkvp_ref[:, 2 * DN_WIDTH + lo:2 * DN_WIDTH + lo + HEAD_DIM]
        q_cols = _rows_to_cols(q)
        k_cols = _rows_to_cols(k)
        qk = jnp.sum(q * k, axis=-1, keepdims=True)
        alpha = jnp.exp(ab[:, h:h + 1])
        beta = ab[:, N_HEADS + h:N_HEADS + h + 1]
        seqs = range(bt)
        kc = [k_cols[:, j:j + 1] for j in seqs]
        a = [alpha[j:j + 1, :] for j in seqs]
        s_k = [jnp.sum(s_ref[j, h] * kc[j], axis=0, keepdims=True) for j in seqs]
        s_q = [jnp.sum(s_ref[j, h] * q_cols[:, j:j + 1], axis=0, keepdims=True) for j in seqs]
        u = [beta[j:j + 1, :] * (v[j:j + 1, :] - a[j] * s_k[j]) for j in seqs]
        for j in seqs:
            sn_ref[j, h] = a[j] * s_ref[j, h] + kc[j] * u[j]
        o = jnp.concatenate([a[j] * s_q[j] + qk[j:j + 1, :] * u[j] for j in seqs], axis=0)
        o_ref[:, lo:lo + HEAD_DIM] = _gated_head_norm(
            o, dng_ref[...], zs_ref[:, lo:lo + HEAD_DIM]).astype(o_ref.dtype)


def _step_delta(qkvp, ab, zs, s0, dng, *, bt):
    n = qkvp.shape[0]
    row = lambda c: pl.BlockSpec((bt, c), lambda i: (i, 0))
    state_spec = pl.BlockSpec((bt, N_HEADS, HEAD_DIM, HEAD_DIM), lambda i: (i, 0, 0, 0))
    return pl.pallas_call(
        functools.partial(_step_delta_kernel, bt=bt),
        grid=(n // bt,),
        in_specs=[row(3 * DN_WIDTH), row(LANES), row(DN_WIDTH), state_spec, _const_spec(dng)],
        out_specs=[row(DN_WIDTH), state_spec],
        out_shape=[jax.ShapeDtypeStruct((n, DN_WIDTH), BF16), jax.ShapeDtypeStruct(s0.shape, F32)],
        compiler_params=_cparams(),
        name="step_delta",
    )(qkvp, ab, zs, s0, dng)


def _post_kernel(acta_ref, actb_ref, sga_ref, sgb_ref, x_ref, gate1_ref, scale2_ref, shift2_ref,
                 gate2_ref, cnt0_ref, gffn_ref, wpa_ref, wpb_ref, wo_ref, wrt_ref, ebias_ref,
                 wsgu_ref, wsd_ref,
                 xs_ref, h2_ref, eidx_ref, ew_ref, rank_ref, cnt_ref, scores_s, *, tm):
    i = pl.program_id(0)

    @pl.when(i == 0)
    def _():
        cnt_ref[...] = cnt0_ref[...]
        scores_s[...] = jnp.zeros(scores_s.shape, F32)

    neg = -jnp.inf
    big = jnp.int32(1 << 30)

    def project():
        y_a = jnp.dot(acta_ref[...], wpa_ref[...], preferred_element_type=F32)
        yield
        y_b = jnp.dot(actb_ref[...], wpb_ref[...], preferred_element_type=F32)
        merged = sga_ref[...] * y_a + sgb_ref[...] * y_b
        yield
        mix = _bdot(merged, wo_ref[...])
        yield
        x1 = x_ref[...] + gate1_ref[...] * mix
        y = x1 * lax.rsqrt(jnp.mean(x1 * x1, axis=-1, keepdims=True) + EPS) * gffn_ref[...]
        h2 = y * (1.0 + scale2_ref[...]) + shift2_ref[...]
        h2_ref[...] = _pack_bf16_pairs(h2)
        h2b = h2.astype(BF16)
        yield
        gu = jnp.dot(h2b, wsgu_ref[...], preferred_element_type=F32)
        yield
        shared = _bdot(_silu(gu[:, :D_SHARED]) * gu[:, D_SHARED:], wsd_ref[...])
        xs_ref[...] = x1 + gate2_ref[...] * shared
        yield
        logits_t = lax.dot_general(wrt_ref[...], h2b, NT_DIMS, preferred_element_type=F32)
        scores_s[...] = _sigmoid(logits_t)

    def first_argmax(vals, rows):
        m = jnp.max(vals, axis=0, keepdims=True)
        return m, jnp.min(jnp.where(vals == m, rows, big), axis=0, keepdims=True)

    def route(scores):
        biased = scores + ebias_ref[...]
        erow = lax.broadcasted_iota(I32, (N_EXPERTS, tm), 0)
        group_scores = []
        for g in range(N_GROUPS):
            vals = biased[g * GROUP_SIZE:(g + 1) * GROUP_SIZE, :]
            rows = lax.broadcasted_iota(I32, (GROUP_SIZE, tm), 0) + g * GROUP_SIZE
            m1, i1 = first_argmax(vals, rows)
            m2 = jnp.max(jnp.where(rows == i1, neg, vals), axis=0, keepdims=True)
            group_scores.append(m1 + m2)
            yield
        group_sel = [jnp.zeros((1, tm), jnp.bool_)] * N_GROUPS
        for _ in range(TOPK_GROUPS):
            best = functools.reduce(jnp.maximum, group_scores)
            gi = functools.reduce(
                jnp.minimum, [jnp.where(group_scores[g] == best, jnp.int32(g), big) for g in range(N_GROUPS)])
            for g in range(N_GROUPS):
                hit = gi == g
                group_sel[g] = jnp.logical_or(group_sel[g], hit)
                group_scores[g] = jnp.where(hit, neg, group_scores[g])
        cand = jnp.concatenate(
            [jnp.where(group_sel[g], biased[g * GROUP_SIZE:(g + 1) * GROUP_SIZE, :], neg)
             for g in range(N_GROUPS)], axis=0)
        yield
        w_rows, hits = [], []
        picked = jnp.zeros((N_EXPERTS, tm), F32)
        for k in range(TOP_K):
            _, ei = first_argmax(cand, erow)
            hit = erow == ei
            eidx_ref[k:k + 1, :] = ei
            w_rows.append(jnp.sum(jnp.where(hit, scores, 0.0), axis=0, keepdims=True))
            hits.append(hit)
            picked = jnp.where(hit, 1.0, picked)
            cand = jnp.where(hit, neg, cand)
            yield
        w_sum = functools.reduce(jnp.add, w_rows)
        for k in range(TOP_K):
            ew_ref[k:k + 1, :] = w_rows[k] / w_sum * ROUTE_SCALE
        yield
        trow = lax.broadcasted_iota(I32, (tm, tm), 0)
        tcol = lax.broadcasted_iota(I32, (tm, tm), 1)
        before = (trow < tcol).astype(BF16)
        prefix = jnp.dot(picked.astype(BF16), before, preferred_element_type=F32) + cnt_ref[:, 0:1]
        for k in range(TOP_K):
            rank_ref[k:k + 1, :] = jnp.sum(jnp.where(hits[k], prefix, 0.0), axis=0, keepdims=True).astype(I32)
            yield
        counted = jnp.where(i > 0, jnp.sum(picked, axis=1, keepdims=True), 0.0)
        cnt_ref[...] = cnt_ref[...] + counted

    jobs = [route(scores_s[...]), project()]
    shares = [4, 1]
    while jobs:
        for job, share in list(zip(jobs, shares)):
            for _ in range(share):
                if next(job, "done") == "done":
                    k = jobs.index(job)
                    del jobs[k], shares[k]
                    break


def _post(acta, actb, sga, sgb, x2d, gate1, scale2, shift2, gate2, cnt0, g_ffn, w, ebias_col,
          *, tm, tiles_per_mod):
    t = x2d.shape[0]
    n_tiles = t // tm
    tile = lambda i: jnp.minimum(i, n_tiles - 1)
    routed_tile = lambda i: jnp.maximum(i - 1, 0)
    row = lambda c: pl.BlockSpec((tm, c), lambda i: (tile(i), 0))
    col = lambda r: pl.BlockSpec((r, tm), lambda i: (0, routed_tile(i)))
    mod = lambda m: pl.BlockSpec((None,) + m.shape[1:], lambda i: (tile(i) // tiles_per_mod, 0, 0))
    mods = [gate1, scale2, shift2, gate2]
    consts = [cnt0, g_ffn, w["pa"], w["pb"], w["o"], w["router_t"], ebias_col, w["s_gu"], w["s_down"]]
    return pl.pallas_call(
        functools.partial(_post_kernel, tm=tm),
        grid=(n_tiles + 1,),
        in_specs=[row(CONV_CH), row(DN_WIDTH), row(D_MODEL), row(D_MODEL), row(D_MODEL)]
                 + [mod(m) for m in mods] + [_const_spec(a) for a in consts],
        out_specs=[row(D_MODEL), row(D_MODEL // 2), col(TOP_K), col(TOP_K), col(TOP_K), _const_spec(cnt0)],
        scratch_shapes=[pltpu.VMEM((N_EXPERTS, tm), F32)],
        out_shape=[jax.ShapeDtypeStruct((t, D_MODEL), F32), jax.ShapeDtypeStruct((t, D_MODEL // 2), U32),
                   jax.ShapeDtypeStruct((TOP_K, t), I32), jax.ShapeDtypeStruct((TOP_K, t), F32),
                   jax.ShapeDtypeStruct((TOP_K, t), I32), jax.ShapeDtypeStruct(cnt0.shape, F32)],
        compiler_params=_cparams(),
        name="post_mixer",
    )(acta, actb, sga, sgb, x2d, *mods, *consts)


def _dest_kernel(start_ref, eidx_ref, rank_ref, o_ref, *, width):
    step = SUBLANES * LANES
    for c0 in range(0, width, step):
        c1 = min(c0 + step, width)
        e = eidx_ref[:, c0:c1]
        base = lax.fori_loop(0, N_EXPERTS, lambda x, acc: jnp.where(e == x, start_ref[x], acc),
                             jnp.zeros(e.shape, I32), unroll=8)
        o_ref[:, c0:c1] = base + rank_ref[:, c0:c1]


def _dest(cnt_start, e_idx, rank):
    k, t = e_idx.shape
    full = pl.BlockSpec((k, t), lambda i, st: (0, 0))
    return pl.pallas_call(
        functools.partial(_dest_kernel, width=t),
        grid_spec=pltpu.PrefetchScalarGridSpec(
            num_scalar_prefetch=1, grid=(1,), in_specs=[full, full], out_specs=full),
        out_shape=jax.ShapeDtypeStruct((k, t), I32),
        compiler_params=_cparams(),
        name="moe_dest",
    )(cnt_start, e_idx, rank)


def _sc_mesh():
    return plsc.VectorSubcoreMesh(core_axis_name="core", subcore_axis_name="subcore")


def _sc_pipeline(body, n_steps, in_specs, out_specs):
    return pltpu.emit_pipeline(body, grid=(n_steps,), in_specs=in_specs, out_specs=out_specs,
                               core_axis_name=("core", "subcore"), dimension_semantics=(pltpu.PARALLEL,))


def _sc_scatter_rows(x_groups, half, dest_flat, n_rows):
    tiles_all = sum(x.shape[0] for x in x_groups) // SC_WINDOW
    n_k = dest_flat.shape[1] // (tiles_all * SC_WINDOW)

    @functools.partial(pl.kernel, mesh=_sc_mesh(), scratch_types=[],
                       out_type=jax.ShapeDtypeStruct((n_rows, QUARTER_WORDS), x_groups[0].dtype))
    def scatter(*refs):
        *x_hbms, i_hbm, o_hbm = refs

        def body(x_vmem, i_vmem):
            pltpu.sync_copy(x_vmem, o_hbm.at[i_vmem.at[0]])

        tile0 = 0
        for x_hbm in x_hbms:
            tiles = x_hbm.shape[0] // SC_WINDOW
            _sc_pipeline(body, n_k * tiles,
                         [pl.BlockSpec((SC_WINDOW, QUARTER_WORDS), lambda i, tiles=tiles: (i % tiles, half)),
                          pl.BlockSpec((1, SC_WINDOW),
                                       lambda i, tiles=tiles, tile0=tile0: (0, (i // tiles) * tiles_all + tile0 + i % tiles))],
                         [])(x_hbm, i_hbm)
            tile0 += tiles

    return scatter(*x_groups, dest_flat)


def _sc_gather_rows(table, idx_flat):
    n = idx_flat.shape[1]

    @functools.partial(pl.kernel, mesh=_sc_mesh(), scratch_types=[],
                       out_type=jax.ShapeDtypeStruct((n, table.shape[1]), table.dtype))
    def gather(t_hbm, i_hbm, o_hbm):
        def body(i_vmem, o_vmem):
            pltpu.sync_copy(t_hbm.at[i_vmem.at[0]], o_vmem)

        _sc_pipeline(body, n // SC_WINDOW,
                     [pl.BlockSpec((1, SC_WINDOW), lambda i: (0, i))],
                     [pl.BlockSpec((SC_WINDOW, table.shape[1]), lambda i: (i, 0))])(i_hbm, o_hbm)

    return gather(table, idx_flat)


def _experts_kernel(sblk_ref, sexp_ref, slo_ref, shi_ref, snext_ref, first_ref,
                    xa_ref, xb_ref, wg_hbm, wu_hbm, wd_hbm, ya_ref, yb_ref,
                    wg_buf, wu_buf, wd_buf, wgu_s, wd_s, state_ref, sems, *, bm):
    p = pl.program_id(0)
    lo = slo_ref[p]
    expert = sexp_ref[p]

    def weight_copies(e, slot):
        return [pltpu.make_async_copy(src.at[e], dst.at[slot], sems.at[slot])
                for src, dst in ((wg_hbm, wg_buf), (wu_hbm, wu_buf), (wd_hbm, wd_buf))]

    @pl.when(p == 0)
    def _():
        state_ref[0] = -1
        state_ref[1] = 0
        for c in weight_copies(first_ref[0], 0):
            c.start()

    @pl.when(shi_ref[p] > lo)
    def _():
        @pl.when(state_ref[0] != expert)
        def _():
            slot = state_ref[1]
            for c in weight_copies(expert, slot):
                c.wait()
            wgu_s[:, :D_EXPERT] = wg_buf[slot].astype(BF16)
            wgu_s[:, D_EXPERT:] = wu_buf[slot].astype(BF16)
            wd_s[...] = wd_buf[slot].astype(BF16)
            state_ref[0] = expert
            state_ref[1] = 1 - slot
            upcoming = snext_ref[p]

            @pl.when(upcoming >= 0)
            def _():
                for c in weight_copies(upcoming, 1 - slot):
                    c.start()

        q = QUARTER_WORDS
        sub = bm // EXPERT_SUBBLOCKS
        first_sub = lo // sub
        last_sub = (shi_ref[p] - 1) // sub

        def run(r0, r1):
            quarters = {}
            quarters[0], quarters[2] = _unpack_bf16_pairs(xa_ref[r0:r1, :])
            quarters[1], quarters[3] = _unpack_bf16_pairs(xb_ref[r0:r1, :])
            gu = functools.reduce(jnp.add, [
                jnp.dot(quarters[c], wgu_s[c * q:(c + 1) * q, :], preferred_element_type=F32) for c in range(4)])
            hb = (_silu(gu[:, :D_EXPERT]) * gu[:, D_EXPERT:]).astype(BF16)
            y = _pack_bf16_pairs(jnp.dot(hb, wd_s[...], preferred_element_type=F32))

            @pl.when(lo <= r0)
            def _():
                ya_ref[r0:r1, :] = y[:, :q]
                yb_ref[r0:r1, :] = y[:, q:]

            @pl.when(lo > r0)
            def _():
                keep = lax.broadcasted_iota(I32, (r1 - r0, 1), 0) + r0 >= lo
                ya_ref[r0:r1, :] = jnp.where(keep, y[:, :q], ya_ref[r0:r1, :])
                yb_ref[r0:r1, :] = jnp.where(keep, y[:, q:], yb_ref[r0:r1, :])

        for a in range(EXPERT_SUBBLOCKS):
            for b in range(a, EXPERT_SUBBLOCKS):
                pl.when(jnp.logical_and(first_sub == a, last_sub == b))(
                    functools.partial(run, a * sub, (b + 1) * sub))


def _experts(xs_a, xs_b, w_gate, w_up, w_down, seg_blk, seg_exp, seg_lo, seg_hi, seg_next, first_exp, *, bm):
    n_rows = xs_a.shape[0]
    n_seg = seg_blk.shape[0]
    ring = 2
    hbm = pl.BlockSpec(memory_space=pl.ANY)
    rows = pl.BlockSpec((bm, QUARTER_WORDS), lambda p, sb, *_: (sb[p], 0))
    grid_spec = pltpu.PrefetchScalarGridSpec(
        num_scalar_prefetch=6,
        grid=(n_seg,),
        in_specs=[rows, rows, hbm, hbm, hbm],
        out_specs=[rows, rows],
        scratch_shapes=[pltpu.VMEM((ring, D_MODEL, D_EXPERT), F32), pltpu.VMEM((ring, D_MODEL, D_EXPERT), F32),
                        pltpu.VMEM((ring, D_EXPERT, D_MODEL), F32),
                        pltpu.VMEM((D_MODEL, 2 * D_EXPERT), BF16), pltpu.VMEM((D_EXPERT, D_MODEL), BF16),
                        pltpu.SMEM((2,), I32), pltpu.SemaphoreType.DMA((ring,))],
    )
    return pl.pallas_call(
        functools.partial(_experts_kernel, bm=bm),
        grid_spec=grid_spec,
        out_shape=[jax.ShapeDtypeStruct((n_rows, QUARTER_WORDS), U32)] * 2,
        compiler_params=_cparams(),
        name="moe_experts",
    )(seg_blk, seg_exp, seg_lo, seg_hi, seg_next, first_exp, xs_a, xs_b, w_gate, w_up, w_down)


def _final_kernel(*refs, tm, n_prompt_tiles, has_sample, has_prev):
    refs = list(refs)
    ew_ref, pa_ref, pb_ref, xsp_ref = refs[:4]
    del refs[:4]
    xss_ref = refs.pop(0) if has_sample else None
    g2p_ref = refs.pop(0)
    g2s_ref = refs.pop(0) if has_sample else None
    gfin_ref = refs.pop(0)
    if has_prev:
        refs.pop(0)
    op_ref = refs.pop(0)
    os_ref = refs.pop(0) if has_sample else None
    ew = ew_ref[...]
    quarters = [jnp.zeros((tm, QUARTER_WORDS), F32)] * 4
    for k in range(TOP_K):
        w_k = ew[:, k:k + 1]
        a_low, a_high = _unpack_bf16_pairs(pa_ref[k])
        b_low, b_high = _unpack_bf16_pairs(pb_ref[k])
        for c, part in enumerate((a_low, b_low, a_high, b_high)):
            quarters[c] = quarters[c] + w_k * part.astype(F32)
    routed = jnp.concatenate(quarters, axis=1)

    def finish(xs_ref, gate2_ref, o_ref):
        x2 = xs_ref[...] + gate2_ref[...] * routed
        o_ref[...] = x2 * lax.rsqrt(jnp.mean(x2 * x2, axis=-1, keepdims=True) + EPS) * gfin_ref[...]

    if not has_sample:
        finish(xsp_ref, g2p_ref, op_ref)
    else:
        is_prompt = pl.program_id(0) < n_prompt_tiles
        pl.when(is_prompt)(lambda: finish(xsp_ref, g2p_ref, op_ref))
        pl.when(jnp.logical_not(is_prompt))(lambda: finish(xss_ref, g2s_ref, os_ref))


def _final(ew_part, planes_a, planes_b, xs_p, gate2_p, g_final, *, tm, tiles_per_batch, tile_lo, n_prompt_tiles,
           sample=None, y_prev=None):
    prompt_tile = lambda i: tile_lo + jnp.minimum(i, n_prompt_tiles - 1)
    sample_tile = lambda i: jnp.maximum(i - n_prompt_tiles, 0)
    prompt_rows = pl.BlockSpec((tm, D_MODEL), lambda i: (prompt_tile(i), 0))
    sample_rows = pl.BlockSpec((tm, D_MODEL), lambda i: (sample_tile(i), 0))
    planes = pl.BlockSpec((TOP_K, tm, QUARTER_WORDS), lambda i: (0, i, 0))
    has_sample = sample is not None
    has_prev = y_prev is not None
    n_tiles = n_prompt_tiles + (sample[0].shape[0] // tm if has_sample else 0)
    args = [ew_part, planes_a, planes_b, xs_p]
    in_specs = [pl.BlockSpec((tm, TOP_K), lambda i: (i, 0)), planes, planes, prompt_rows]
    if has_sample:
        args.append(sample[0])
        in_specs.append(sample_rows)
    args.append(gate2_p)
    in_specs.append(pl.BlockSpec((None, 1, D_MODEL), lambda i: (prompt_tile(i) // tiles_per_batch, 0, 0)))
    if has_sample:
        args.append(sample[1])
        in_specs.append(pl.BlockSpec((None, tm, D_MODEL), lambda i: (sample_tile(i), 0, 0)))
    args.append(g_final)
    in_specs.append(_const_spec(g_final))
    if has_prev:
        args.append(y_prev)
        in_specs.append(pl.BlockSpec(memory_space=pl.ANY))
    out_specs = [prompt_rows] + ([sample_rows] if has_sample else [])
    out_shape = [jax.ShapeDtypeStruct(xs_p.shape, F32)] + (
        [jax.ShapeDtypeStruct(sample[0].shape, F32)] if has_sample else [])
    outs = pl.pallas_call(
        functools.partial(_final_kernel, tm=tm, n_prompt_tiles=n_prompt_tiles, has_sample=has_sample,
                          has_prev=has_prev),
        grid=(n_tiles,),
        in_specs=in_specs,
        out_specs=out_specs,
        out_shape=out_shape,
        input_output_aliases={len(args) - 1: 0} if has_prev else {},
        compiler_params=_cparams(),
        name="moe_combine_final",
    )(*args)
    return outs if has_sample else outs[0]


def _pick_tile(n, preferred):
    t = min(n, preferred)
    assert n % t == 0, (n, t)
    return t


def kernel(x_prompt, x_sample, c_prompt, c_sample, state_conv, state_short_conv, state_delta, w_ada, b_ada, g_mix, w_in, conv_dw, conv_dw_b, conv_ln_g, conv_ln_b, w_pa, short_conv_w, a_log, dt_bias, dn_norm_g, w_pb, w_o, g_ffn, w_router, e_bias, w_gate, w_up, w_down, ws_gate, ws_up, ws_down, g_final):
    nb, seq, d = x_prompt.shape
    ns = x_sample.shape[0]
    assert d == D_MODEL and w_ada.shape[0] == 1 and x_sample.shape[1] == 1
    assert seq % CHUNK == 0 and seq >= CONV_HALO and ns % SUBLANES == 0
    tp = nb * seq
    row1 = lambda a: a.reshape(1, -1)

    wi = w_in[0].astype(BF16)
    o_q = 2 * CONV_CH
    o_z = o_q + 3 * DN_WIDTH
    o_a = o_z + DN_WIDTH
    o_ga = o_a + 2 * N_HEADS
    w_proj = {
        "glu": wi[:, :o_q], "qkv": wi[:, o_q:o_z], "z": wi[:, o_z:o_a],
        "ab": jnp.pad(wi[:, o_a:o_ga], ((0, 0), (0, LANES - 2 * N_HEADS))),
        "ga": wi[:, o_ga:o_ga + D_MODEL], "gb": wi[:, o_ga + D_MODEL:],
    }
    nea = jnp.pad(-jnp.exp(a_log[0].astype(F32)), (0, LANES - N_HEADS)).reshape(1, LANES)
    dtb = jnp.pad(dt_bias[0].astype(F32), (0, LANES - N_HEADS)).reshape(1, LANES)
    w_post = {
        "pa": w_pa[0].astype(BF16), "pb": w_pb[0].astype(BF16), "o": w_o[0].astype(BF16),
        "router_t": w_router[0].T.astype(BF16),
        "s_gu": jnp.concatenate([ws_gate[0], ws_up[0]], axis=1).astype(BF16),
        "s_down": ws_down[0].astype(BF16),
    }
    ebias_col = e_bias[0].astype(F32).reshape(N_EXPERTS, 1)
    g_mix1, g_ffn1, g_fin1 = row1(g_mix[0]), row1(g_ffn[0]), row1(g_final)
    cdw, cdb, clg, clb = conv_dw[0], row1(conv_dw_b[0]), row1(conv_ln_g[0]), row1(conv_ln_b[0])
    scw, dng = short_conv_w[0], row1(dn_norm_g[0])

    ada = _ada(jnp.concatenate([c_prompt, c_sample], axis=0), w_ada[0].astype(BF16), b_ada[0])
    ada = ada.reshape(nb + ns, 6, D_MODEL)
    mod_p = [ada[:nb, m].reshape(nb, 1, D_MODEL) for m in range(6)]
    mod_s = [ada[nb:, m].reshape(1, ns, D_MODEL) for m in range(6)]
    shift1, scale1, gate1, shift2, scale2, gate2 = range(6)

    tm_p = _pick_tile(seq, 256)
    tpm_p = seq // tm_p
    xp = x_prompt.reshape(tp, D_MODEL)
    u_p, qkv_p, zs_p, ab_p, sga_p, sgb_p = _in_proj(
        xp, mod_p[scale1], mod_p[shift1], g_mix1, w_proj, nea, dtb, tm=tm_p, tiles_per_mod=tpm_p)
    tl = _pick_tile(seq, 256)
    acta_p = _conv_branch(u_p, jnp.zeros((nb, CONV_HALO, CONV_CH), F32), cdw, cdb, clg, clb,
                          tl=tl, tiles_per_batch=seq // tl)
    tc = _pick_tile(seq, 4 * CHUNK)
    actb_p, s_new_p = _delta_branch(
        qkv_p, jnp.zeros((nb, SC_HALO, 3 * DN_WIDTH), F32),
        jnp.zeros((nb, N_HEADS, HEAD_DIM, HEAD_DIM), F32), scw, ab_p, zs_p, dng,
        tc=tc, tiles_per_batch=seq // tc)

    xs_ = x_sample.reshape(ns, D_MODEL)
    u_s, qkv_s, zs_s, ab_s, sga_s, sgb_s = _in_proj(
        xs_, mod_s[scale1], mod_s[shift1], g_mix1, w_proj, nea, dtb, tm=ns, tiles_per_mod=1)
    bt = SUBLANES
    acta_s, nch_s, qkvp_s, nsh_s = _step_front(
        u_s, qkv_s, state_conv[0].reshape(ns, -1), state_short_conv[0].reshape(ns, -1),
        cdw, cdb, clg, clb, scw, bt=bt)
    actb_s, s_new_s = _step_delta(qkvp_s, ab_s, zs_s, state_delta[0].astype(F32), dng, bt=bt)

    cnt0 = jnp.zeros((N_EXPERTS, LANES), F32)
    xsh_p, h2_p, eidx_p, ew_p, rank_p, cnt_p = _post(
        acta_p, actb_p, sga_p, sgb_p, xp, mod_p[gate1], mod_p[scale2], mod_p[shift2], mod_p[gate2],
        cnt0, g_ffn1, w_post, ebias_col, tm=tm_p, tiles_per_mod=tpm_p)
    xsh_s, h2_s, eidx_s, ew_s, rank_s, cnt_all = _post(
        acta_s, actb_s, sga_s, sgb_s, xs_, mod_s[gate1], mod_s[scale2], mod_s[shift2], mod_s[gate2],
        cnt_p, g_ffn1, w_post, ebias_col, tm=ns, tiles_per_mod=1)

    bm = EXPERT_SUBBLOCKS * EXPERT_SUB_ROWS
    tmd = LANES
    t_all = tp + ns
    n_rows = t_all * TOP_K
    assert seq % tmd == 0 and ns % tmd == 0 and n_rows % bm == 0 and t_all % SC_WINDOW == 0
    n_blocks = n_rows // bm
    counts = cnt_all[:, 0].astype(I32)
    cnt_end = jnp.cumsum(counts)
    cnt_start = cnt_end - counts
    e_idx = jnp.concatenate([eidx_p, eidx_s], axis=1)
    rank = jnp.concatenate([rank_p, rank_s], axis=1)
    dest = _dest(cnt_start, e_idx, rank)
    dest_flat = dest.reshape(1, n_rows)
    seg_start = jnp.sort(jnp.concatenate([jnp.arange(n_blocks, dtype=I32) * bm, cnt_start]))
    seg_end = jnp.concatenate([seg_start[1:], jnp.full((1,), n_rows, I32)])
    seg_blk = jnp.minimum(seg_start // bm, n_blocks - 1)
    seg_lo = seg_start - seg_blk * bm
    seg_hi = seg_end - seg_blk * bm
    seg_exp = jnp.sum((cnt_end[None, :] <= seg_start[:, None]).astype(I32), axis=1)
    seg_exp = jnp.minimum(seg_exp, N_EXPERTS - 1)
    seg_exp = lax.cummax(jnp.where(seg_hi > seg_lo, seg_exp, 0))
    none = jnp.int32(N_EXPERTS)
    in_use_from = lax.cummin(jnp.where(counts > 0, jnp.arange(N_EXPERTS, dtype=I32), none), reverse=True)
    next_in_use = jnp.concatenate([in_use_from[1:], jnp.full((1,), none, I32)])
    next_in_use = jnp.where(next_in_use == none, -1, next_in_use)
    seg_next = next_in_use[seg_exp]
    first_exp = jnp.minimum(in_use_from[:1], N_EXPERTS - 1)

    xs_a, xs_b = (_sc_scatter_rows((h2_p, h2_s), half, dest_flat, n_rows) for half in range(2))
    ys_a, ys_b = _experts(xs_a, xs_b, w_gate[0], w_up[0], w_down[0], seg_blk, seg_exp, seg_lo, seg_hi, seg_next,
                          first_exp, bm=bm)

    ew_tok = jnp.concatenate([ew_p, ew_s], axis=1).T
    n_prompt_tiles = tp // tmd
    tiles_1 = n_prompt_tiles // 2
    t_1 = tiles_1 * tmd

    def gathered(lo, hi):
        idx = dest[:, lo:hi].reshape(1, -1)
        return [_sc_gather_rows(ys, idx).reshape(TOP_K, hi - lo, QUARTER_WORDS) for ys in (ys_a, ys_b)]

    final = functools.partial(_final, xs_p=xsh_p, gate2_p=mod_p[gate2], g_final=g_fin1, tm=tmd,
                              tiles_per_batch=seq // tmd)
    y_p = final(ew_tok[:t_1], *gathered(0, t_1), tile_lo=0, n_prompt_tiles=tiles_1)
    y_p, y_s = final(ew_tok[t_1:], *gathered(t_1, t_all), tile_lo=tiles_1, n_prompt_tiles=n_prompt_tiles - tiles_1,
                     sample=(xsh_s, mod_s[gate2].reshape(ns // tmd, tmd, D_MODEL)), y_prev=y_p)

    hist_rows = CONV_WIDTH - 1
    sc_rows = SHORT_CONV - 1
    y_prompt = y_p.reshape(nb, seq, D_MODEL)
    y_sample = y_s.reshape(ns, 1, D_MODEL)
    new_conv_prompt = u_p.reshape(nb, seq, CONV_CH)[:, seq - hist_rows:][None]
    new_sc_prompt = qkv_p.reshape(nb, seq, 3 * DN_WIDTH)[:, seq - sc_rows:][None]
    new_conv_sample = nch_s.reshape(ns, hist_rows, CONV_CH)[None]
    new_sc_sample = nsh_s.reshape(ns, sc_rows, 3 * DN_WIDTH)[None]
    return (y_prompt, y_sample, new_conv_prompt, new_sc_prompt, s_new_p[None],
            new_conv_sample, new_sc_sample, s_new_s[None])
```

```python
import functools

import jax
import jax.numpy as jnp
from jax import lax
from jax.experimental import pallas as pl
from jax.experimental.pallas import tpu as pltpu
from jax.experimental.pallas import tpu_sc as plsc

F32 = jnp.float32
BF16 = jnp.bfloat16
I32 = jnp.int32
U32 = jnp.uint32

EPS = 1e-6
D_MODEL = 1024
CONV_CH = 512
CONV_WIDTH = 31
N_HEADS = 4
HEAD_DIM = 128
DN_WIDTH = N_HEADS * HEAD_DIM
SHORT_CONV = 4
CHUNK = 64
CHUNK_SHIFT = CHUNK.bit_length() - 1
assert 1 << CHUNK_SHIFT == CHUNK
SUB = 16
SUB_SHIFT = SUB.bit_length() - 1
assert 1 << SUB_SHIFT == SUB and CHUNK % SUB == 0
N_EXPERTS = 256
TOP_K = 8
N_GROUPS = 8
GROUP_SIZE = N_EXPERTS // N_GROUPS
TOPK_GROUPS = 4
D_EXPERT = 256
D_SHARED = 256
ROUTE_SCALE = 2.5

LANES = 128
SUBLANES = 8
CONV_HALO = 32
SC_HALO = 8
VMEM_LIMIT = 48 * 1024 * 1024
EXPERT_SUBBLOCKS = 8
EXPERT_SUB_ROWS = 128
SC_WINDOW = 128
QUARTER_WORDS = D_MODEL // 4

NT_DIMS = (((1,), (1,)), ((), ()))
TN_DIMS = (((0,), (0,)), ((), ()))


def _cparams(n_grid_dims=1):
    return pltpu.CompilerParams(
        dimension_semantics=("arbitrary",) * n_grid_dims,
        vmem_limit_bytes=VMEM_LIMIT)


def _sigmoid(x):
    return jax.nn.sigmoid(x)


def _silu(x):
    return x * jax.nn.sigmoid(x)


def _bdot(a, b):
    return jnp.dot(a.astype(BF16), b.astype(BF16), preferred_element_type=F32)


def _pack_bf16_pairs(x):
    half = x.shape[1] // 2
    bits = lax.bitcast_convert_type(x.astype(BF16).astype(F32), U32)
    return bits[:, half:] | (bits[:, :half] >> 16)


def _unpack_bf16_pairs(w):
    low = lax.bitcast_convert_type(w << 16, F32).astype(BF16)
    high = lax.bitcast_convert_type(w & jnp.uint32(0xFFFF0000), F32).astype(BF16)
    return low, high


def _dot_exact_lhs(a_bf, b):
    hi = b.astype(BF16)
    r1 = b - hi.astype(F32)
    mid = r1.astype(BF16)
    low = (r1 - mid.astype(F32)).astype(BF16)
    dot = lambda p: jnp.dot(a_bf, p, preferred_element_type=F32)
    return dot(hi) + dot(mid) + dot(low)


def _ada_kernel(c_ref, w_ref, b_ref, o_ref):
    o_ref[...] = _bdot(_silu(c_ref[...]), w_ref[...]) + b_ref[...]


def _ada(c_all, w_ada_bf, b_ada):
    n = c_all.shape[0]
    width = w_ada_bf.shape[1]
    tn = 1536
    return pl.pallas_call(
        _ada_kernel,
        grid=(width // tn,),
        in_specs=[pl.BlockSpec((n, D_MODEL), lambda j: (0, 0)),
                  pl.BlockSpec((D_MODEL, tn), lambda j: (0, j)),
                  pl.BlockSpec((1, tn), lambda j: (0, j))],
        out_specs=pl.BlockSpec((n, tn), lambda j: (0, j)),
        out_shape=jax.ShapeDtypeStruct((n, width), F32),
        compiler_params=_cparams(),
        name="ada",
    )(c_all, w_ada_bf, b_ada.reshape(1, width))


def _in_proj_kernel(x_ref, scale_ref, shift_ref, g_ref, wglu_ref, wqkv_ref, wz_ref, wab_ref,
                    wga_ref, wgb_ref, nea_ref, dtb_ref,
                    u_ref, qkv_ref, zs_ref, ab_ref, sga_ref, sgb_ref):
    x = x_ref[...]
    y = x * lax.rsqrt(jnp.mean(x * x, axis=-1, keepdims=True) + EPS) * g_ref[...]
    h = (y * (1.0 + scale_ref[...]) + shift_ref[...]).astype(BF16)
    glu = jnp.dot(h, wglu_ref[...], preferred_element_type=F32)
    u_ref[...] = glu[:, :CONV_CH] * _sigmoid(glu[:, CONV_CH:])
    qkv_ref[...] = jnp.dot(h, wqkv_ref[...], preferred_element_type=F32)
    zs_ref[...] = _silu(jnp.dot(h, wz_ref[...], preferred_element_type=F32))
    ab = jnp.dot(h, wab_ref[...], preferred_element_type=F32)
    sp = ab + dtb_ref[...]
    softplus = jnp.maximum(sp, 0.0) + jnp.log1p(jnp.exp(-jnp.abs(sp)))
    lane = lax.broadcasted_iota(I32, ab.shape, 1)
    ab_ref[...] = jnp.where(lane < N_HEADS, nea_ref[...] * softplus, _sigmoid(ab))
    sga_ref[...] = _sigmoid(jnp.dot(h, wga_ref[...], preferred_element_type=F32))
    sgb_ref[...] = _sigmoid(jnp.dot(h, wgb_ref[...], preferred_element_type=F32))


def _mod_spec(mod, tiles_per_mod):
    return pl.BlockSpec((None,) + mod.shape[1:], lambda i: (i // tiles_per_mod, 0, 0))


def _const_spec(a):
    nd = a.ndim
    return pl.BlockSpec(a.shape, lambda i: (0,) * nd)


def _in_proj(x2d, scale, shift, g_mix, w, nea, dtb, *, tm, tiles_per_mod):
    t = x2d.shape[0]
    row = lambda c: pl.BlockSpec((tm, c), lambda i: (i, 0))
    consts = [g_mix, w["glu"], w["qkv"], w["z"], w["ab"], w["ga"], w["gb"], nea, dtb]
    widths = [CONV_CH, 3 * DN_WIDTH, DN_WIDTH, LANES, D_MODEL, D_MODEL]
    return pl.pallas_call(
        _in_proj_kernel,
        grid=(t // tm,),
        in_specs=[row(D_MODEL), _mod_spec(scale, tiles_per_mod), _mod_spec(shift, tiles_per_mod)]
                 + [_const_spec(a) for a in consts],
        out_specs=[row(c) for c in widths],
        out_shape=[jax.ShapeDtypeStruct((t, c), F32) for c in widths],
        compiler_params=_cparams(),
        name="in_proj",
    )(x2d, scale, shift, *consts)


def _layer_norm_swish(y, g, b):
    mu = jnp.mean(y, axis=-1, keepdims=True)
    yc = y - mu
    var = jnp.mean(yc * yc, axis=-1, keepdims=True)
    return _silu(yc * lax.rsqrt(var + EPS) * g + b)


def _conv_kernel(u_ref, halo_ref, hist_ref, w_ref, b_ref, lg_ref, lb_ref, o_ref, win_ref, shift_ref,
                 *, tl, tiles_per_batch, rows_per_pass):
    i = pl.program_id(0)
    first = (i % tiles_per_batch) == 0

    @pl.when(first)
    def _():
        win_ref[0:CONV_HALO, :] = hist_ref[...]

    @pl.when(jnp.logical_not(first))
    def _():
        win_ref[0:CONV_HALO, :] = halo_ref[...]

    win_ref[CONV_HALO:CONV_HALO + tl, :] = u_ref[...]
    tap0 = CONV_HALO - (CONV_WIDTH - 1)
    n_shifted = CONV_HALO + tl - SUBLANES
    for b in range(1, SUBLANES):
        shift_ref[b - 1] = win_ref[b:b + n_shifted, :]

    for r in range(tl // rows_per_pass):
        base = r * rows_per_pass
        acc = jnp.zeros((rows_per_pass, CONV_CH), F32)
        for j in range(CONV_WIDTH):
            a, b = divmod(tap0 + j, SUBLANES)
            start = base + a * SUBLANES
            rows = (win_ref[start:start + rows_per_pass, :] if b == 0
                    else shift_ref[b - 1, start:start + rows_per_pass, :])
            acc = acc + w_ref[j:j + 1, :] * rows
        act = _layer_norm_swish(acc + b_ref[...], lg_ref[...], lb_ref[...])
        o_ref[base:base + rows_per_pass, :] = act.astype(o_ref.dtype)


def _conv_branch(u2d, hist, conv_dw, conv_dw_b, ln_g, ln_b, *, tl, tiles_per_batch):
    t = u2d.shape[0]
    halo_blocks = tl // CONV_HALO
    consts = [conv_dw, conv_dw_b, ln_g, ln_b]
    return pl.pallas_call(
        functools.partial(_conv_kernel, tl=tl, tiles_per_batch=tiles_per_batch, rows_per_pass=64),
        grid=(t // tl,),
        in_specs=[pl.BlockSpec((tl, CONV_CH), lambda i: (i, 0)),
                  pl.BlockSpec((CONV_HALO, CONV_CH), lambda i: (jnp.maximum(i * halo_blocks - 1, 0), 0)),
                  pl.BlockSpec((None, CONV_HALO, CONV_CH), lambda i: (i // tiles_per_batch, 0, 0))]
                 + [_const_spec(a) for a in consts],
        out_specs=pl.BlockSpec((tl, CONV_CH), lambda i: (i, 0)),
        out_shape=jax.ShapeDtypeStruct((t, CONV_CH), BF16),
        scratch_shapes=[pltpu.VMEM((CONV_HALO + tl, CONV_CH), F32),
                        pltpu.VMEM((SUBLANES - 1, CONV_HALO + tl - SUBLANES, CONV_CH), F32)],
        compiler_params=_cparams(),
        name="conv_branch",
    )(u2d, u2d, hist, *consts)


def _l2n(x):
    return x * lax.rsqrt(jnp.sum(x * x, axis=-1, keepdims=True) + 1e-6)


def _gated_head_norm(o, g, zs):
    return o * lax.rsqrt(jnp.mean(o * o, axis=-1, keepdims=True) + EPS) * g * zs


def _delta_kernel(qkv_ref, halo_ref, hist_ref, scw_ref, ab_ref, zs_ref, s0_ref, dng_ref,
                  o_ref, s_ref, win_ref, c_ref, *, tc, tiles_per_batch):
    i = pl.program_id(0)
    first = (i % tiles_per_batch) == 0

    @pl.when(first)
    def _():
        win_ref[0:SC_HALO, :] = hist_ref[...]
        s_ref[...] = s0_ref[...]

    @pl.when(jnp.logical_not(first))
    def _():
        win_ref[0:SC_HALO, :] = halo_ref[...]

    win_ref[SC_HALO:SC_HALO + tc, :] = qkv_ref[...]
    tap0 = SC_HALO - (SHORT_CONV - 1)
    acc = jnp.zeros((tc, 3 * DN_WIDTH), F32)
    for j in range(SHORT_CONV):
        acc = acc + scw_ref[j:j + 1, :] * win_ref[tap0 + j:tap0 + j + tc, :]
    c_ref[...] = _silu(acc)

    row = lax.broadcasted_iota(I32, (tc, tc), 0)
    col = lax.broadcasted_iota(I32, (tc, tc), 1)
    same_chunk = lax.shift_right_logical(row, CHUNK_SHIFT) == lax.shift_right_logical(col, CHUNK_SHIFT)
    incl = jnp.logical_and(same_chunk, row >= col)
    row_sub = lax.shift_right_logical(row, SUB_SHIFT)
    col_sub = lax.shift_right_logical(col, SUB_SHIFT)
    strict_sub = jnp.logical_and(row_sub == col_sub, row > col)
    below_sub = jnp.logical_and(same_chunk, row_sub > col_sub)
    eye = (row == col).astype(F32)
    ab = ab_ref[...]
    gcum = _dot_exact_lhs(incl.astype(BF16), ab)
    gcum_t = gcum.T
    n_ch = tc // CHUNK

    heads = range(N_HEADS)
    q, k, gc, kk, a_qk, rhs, qd = [], [], [], [], [], [], []
    for h in heads:
        lo = h * HEAD_DIM
        q_h = _l2n(c_ref[:, lo:lo + HEAD_DIM]) * (HEAD_DIM ** -0.5)
        k_h = _l2n(c_ref[:, DN_WIDTH + lo:DN_WIDTH + lo + HEAD_DIM])
        v_h = c_ref[:, 2 * DN_WIDTH + lo:2 * DN_WIDTH + lo + HEAD_DIM]
        beta = ab[:, N_HEADS + h:N_HEADS + h + 1]
        gc_h = gcum[:, h:h + 1]
        gr = gcum_t[h:h + 1, :]
        decay = jnp.exp(jnp.where(incl, gc_h - gr, -jnp.inf))
        kb = k_h * beta
        kq = lax.dot_general(jnp.concatenate([kb, q_h], axis=0).astype(BF16), k_h.astype(BF16),
                             NT_DIMS, preferred_element_type=F32)
        e_gc = jnp.exp(gc_h)
        q.append(q_h)
        k.append(k_h)
        gc.append(gc_h)
        kk.append(kq[:tc] * decay)
        a_qk.append(kq[tc:] * decay)
        rhs.append(jnp.concatenate([v_h * beta, kb * e_gc], axis=1))
        qd.append(q_h * e_gc)

    pw = [jnp.where(strict_sub, kk[h], 0.0) for h in heads]
    d_inv = [eye - pw[h] for h in heads]
    for _ in range(SUB_SHIFT - 1):
        pw = [_bdot(pw[h], pw[h]) for h in heads]
        d_inv = [d_inv[h] + _bdot(d_inv[h], pw[h]) for h in heads]
    c_rhs = [_bdot(d_inv[h], rhs[h]) for h in heads]
    m_off = [_bdot(d_inv[h], jnp.where(below_sub, kk[h], 0.0)) for h in heads]
    sol = c_rhs
    for _ in range(CHUNK // SUB - 1):
        sol = [c_rhs[h] - _bdot(m_off[h], sol[h]) for h in heads]

    s = [s_ref[h] for h in heads]
    u_parts = [[] for _ in heads]
    o_parts = [[] for _ in heads]
    for c in range(n_ch):
        r0 = c * CHUNK
        for h in heads:
            u0_c = sol[h][r0:r0 + CHUNK, :HEAD_DIM]
            w_c = sol[h][r0:r0 + CHUNK, HEAD_DIM:]
            wq_s = _bdot(jnp.concatenate([w_c, qd[h][r0:r0 + CHUNK]], axis=0), s[h])
            u_c = u0_c - wq_s[:CHUNK]
            g_last = gcum[r0 + CHUNK - 1:r0 + CHUNK, h:h + 1]
            k_dec = k[h][r0:r0 + CHUNK] * jnp.exp(g_last - gc[h][r0:r0 + CHUNK])
            s[h] = jnp.exp(g_last) * s[h] + lax.dot_general(
                k_dec.astype(BF16), u_c.astype(BF16), TN_DIMS, preferred_element_type=F32)
            u_parts[h].append(u_c)
            o_parts[h].append(wq_s[CHUNK:])
    for h in heads:
        lo = h * HEAD_DIM
        s_ref[h] = s[h]
        o = jnp.concatenate(o_parts[h], axis=0) + _bdot(a_qk[h], jnp.concatenate(u_parts[h], axis=0))
        o_ref[:, lo:lo + HEAD_DIM] = _gated_head_norm(
            o, dng_ref[...], zs_ref[:, lo:lo + HEAD_DIM]).astype(o_ref.dtype)


def _delta_branch(qkv2d, hist, s0, scw, ab, zs, dng, *, tc, tiles_per_batch):
    t = qkv2d.shape[0]
    nb = s0.shape[0]
    halo_blocks = tc // SC_HALO
    wqkv = 3 * DN_WIDTH
    state_spec = pl.BlockSpec((None, N_HEADS, HEAD_DIM, HEAD_DIM), lambda i: (i // tiles_per_batch, 0, 0, 0))
    return pl.pallas_call(
        functools.partial(_delta_kernel, tc=tc, tiles_per_batch=tiles_per_batch),
        grid=(t // tc,),
        in_specs=[pl.BlockSpec((tc, wqkv), lambda i: (i, 0)),
                  pl.BlockSpec((SC_HALO, wqkv), lambda i: (jnp.maximum(i * halo_blocks - 1, 0), 0)),
                  pl.BlockSpec((None, SC_HALO, wqkv), lambda i: (i // tiles_per_batch, 0, 0)),
                  _const_spec(scw),
                  pl.BlockSpec((tc, LANES), lambda i: (i, 0)),
                  pl.BlockSpec((tc, DN_WIDTH), lambda i: (i, 0)),
                  state_spec,
                  _const_spec(dng)],
        out_specs=[pl.BlockSpec((tc, DN_WIDTH), lambda i: (i, 0)), state_spec],
        out_shape=[jax.ShapeDtypeStruct((t, DN_WIDTH), BF16),
                   jax.ShapeDtypeStruct((nb, N_HEADS, HEAD_DIM, HEAD_DIM), F32)],
        scratch_shapes=[pltpu.VMEM((SC_HALO + tc, wqkv), F32), pltpu.VMEM((tc, wqkv), F32)],
        compiler_params=_cparams(),
        name="delta_branch",
    )(qkv2d, qkv2d, hist, scw, ab, zs, s0, dng)


def _step_front_kernel(u_ref, qkv_ref, ch_ref, sh_ref, cw_ref, cb_ref, lg_ref, lb_ref, scw_ref,
                       acta_ref, nch_ref, qkvp_ref, nsh_ref):
    hist_rows = CONV_WIDTH - 1
    u = u_ref[...]
    acc = cw_ref[hist_rows:hist_rows + 1, :] * u
    for j in range(hist_rows):
        acc = acc + cw_ref[j:j + 1, :] * ch_ref[:, j * CONV_CH:(j + 1) * CONV_CH]
    acta_ref[...] = _layer_norm_swish(acc + cb_ref[...], lg_ref[...], lb_ref[...]).astype(acta_ref.dtype)
    nch_ref[:, :(hist_rows - 1) * CONV_CH] = ch_ref[:, CONV_CH:]
    nch_ref[:, (hist_rows - 1) * CONV_CH:] = u

    wq = 3 * DN_WIDTH
    sc_rows = SHORT_CONV - 1
    qkv = qkv_ref[...]
    acc = scw_ref[sc_rows:sc_rows + 1, :] * qkv
    for j in range(sc_rows):
        acc = acc + scw_ref[j:j + 1, :] * sh_ref[:, j * wq:(j + 1) * wq]
    c = _silu(acc)
    for h in range(N_HEADS):
        lo = h * HEAD_DIM
        qkvp_ref[:, lo:lo + HEAD_DIM] = _l2n(c[:, lo:lo + HEAD_DIM]) * (HEAD_DIM ** -0.5)
        qkvp_ref[:, DN_WIDTH + lo:DN_WIDTH + lo + HEAD_DIM] = _l2n(c[:, DN_WIDTH + lo:DN_WIDTH + lo + HEAD_DIM])
    qkvp_ref[:, 2 * DN_WIDTH:] = c[:, 2 * DN_WIDTH:]
    nsh_ref[:, :(sc_rows - 1) * wq] = sh_ref[:, wq:]
    nsh_ref[:, (sc_rows - 1) * wq:] = qkv


def _step_front(u, qkv, conv_hist_flat, sc_hist_flat, conv_dw, conv_dw_b, ln_g, ln_b, scw, *, bt):
    n = u.shape[0]
    wq = 3 * DN_WIDTH
    consts = [conv_dw, conv_dw_b, ln_g, ln_b, scw]
    row = lambda c: pl.BlockSpec((bt, c), lambda i: (i, 0))
    widths = [CONV_CH, conv_hist_flat.shape[1], wq, sc_hist_flat.shape[1]]
    dtypes = [BF16, F32, F32, F32]
    return pl.pallas_call(
        _step_front_kernel,
        grid=(n // bt,),
        in_specs=[row(CONV_CH), row(wq), row(widths[1]), row(widths[3])] + [_const_spec(a) for a in consts],
        out_specs=[row(c) for c in widths],
        out_shape=[jax.ShapeDtypeStruct((n, c), d) for c, d in zip(widths, dtypes)],
        compiler_params=_cparams(),
        name="step_front",
    )(u, qkv, conv_hist_flat, sc_hist_flat, *consts)


def _rows_to_cols(x):
    pad = jnp.zeros((LANES - x.shape[0], LANES), x.dtype)
    return jnp.concatenate([x, pad], axis=0).T


def _step_delta_kernel(qkvp_ref, ab_ref, zs_ref, s_ref, dng_ref, o_ref, sn_ref, *, bt):
    ab = ab_ref[...]
    for h in range(N_HEADS):
        lo = h * HEAD_DIM
        q = qkvp_ref[:, lo:lo + HEAD_DIM]
        k = qkvp_ref[:, DN_WIDTH + lo:DN_WIDTH + lo + HEAD_DIM]
        v = qkvp_ref[:, 2 * DN_WIDTH + lo:2 * DN_WIDTH + lo + HEAD_DIM]
        q_cols = _rows_to_cols(q)
        k_cols = _rows_to_cols(k)
        qk = jnp.sum(q * k, axis=-1, keepdims=True)
        alpha = jnp.exp(ab[:, h:h + 1])
        beta = ab[:, N_HEADS + h:N_HEADS + h + 1]
        seqs = range(bt)
        kc = [k_cols[:, j:j + 1] for j in seqs]
        a = [alpha[j:j + 1, :] for j in seqs]
        s_k = [jnp.sum(s_ref[j, h] * kc[j], axis=0, keepdims=True) for j in seqs]
        s_q = [jnp.sum(s_ref[j, h] * q_cols[:, j:j + 1], axis=0, keepdims=True) for j in seqs]
        u = [beta[j:j + 1, :] * (v[j:j + 1, :] - a[j] * s_k[j]) for j in seqs]
        for j in seqs:
            sn_ref[j, h] = a[j] * s_ref[j, h] + kc[j] * u[j]
        o = jnp.concatenate([a[j] * s_q[j] + qk[j:j + 1, :] * u[j] for j in seqs], axis=0)
        o_ref[:, lo:lo + HEAD_DIM] = _gated_head_norm(
            o, dng_ref[...], zs_ref[:, lo:lo + HEAD_DIM]).astype(o_ref.dtype)


def _step_delta(qkvp, ab, zs, s0, dng, *, bt):
    n = qkvp.shape[0]
    row = lambda c: pl.BlockSpec((bt, c), lambda i: (i, 0))
    state_spec = pl.BlockSpec((bt, N_HEADS, HEAD_DIM, HEAD_DIM), lambda i: (i, 0, 0, 0))
    return pl.pallas_call(
        functools.partial(_step_delta_kernel, bt=bt),
        grid=(n // bt,),
        in_specs=[row(3 * DN_WIDTH), row(LANES), row(DN_WIDTH), state_spec, _const_spec(dng)],
        out_specs=[row(DN_WIDTH), state_spec],
        out_shape=[jax.ShapeDtypeStruct((n, DN_WIDTH), BF16), jax.ShapeDtypeStruct(s0.shape, F32)],
        compiler_params=_cparams(),
        name="step_delta",
    )(qkvp, ab, zs, s0, dng)


def _post_kernel(acta_ref, actb_ref, sga_ref, sgb_ref, x_ref, gate1_ref, scale2_ref, shift2_ref,
                 gate2_ref, cnt0_ref, gffn_ref, wpa_ref, wpb_ref, wo_ref, wrt_ref, ebias_ref,
                 wsgu_ref, wsd_ref,
                 xs_ref, h2_ref, eidx_ref, ew_ref, rank_ref, cnt_ref, scores_s, *, tm):
    i = pl.program_id(0)

    @pl.when(i == 0)
    def _():
        cnt_ref[...] = cnt0_ref[...]
        scores_s[...] = jnp.zeros(scores_s.shape, F32)

    neg = -jnp.inf
    big = jnp.int32(1 << 30)

    def project():
        y_a = jnp.dot(acta_ref[...], wpa_ref[...], preferred_element_type=F32)
        yield
        y_b = jnp.dot(actb_ref[...], wpb_ref[...], preferred_element_type=F32)
        merged = sga_ref[...] * y_a + sgb_ref[...] * y_b
        yield
        mix = _bdot(merged, wo_ref[...])
        yield
        x1 = x_ref[...] + gate1_ref[...] * mix
        y = x1 * lax.rsqrt(jnp.mean(x1 * x1, axis=-1, keepdims=True) + EPS) * gffn_ref[...]
        h2 = y * (1.0 + scale2_ref[...]) + shift2_ref[...]
        h2_ref[...] = _pack_bf16_pairs(h2)
        h2b = h2.astype(BF16)
        yield
        gu = jnp.dot(h2b, wsgu_ref[...], preferred_element_type=F32)
        yield
        shared = _bdot(_silu(gu[:, :D_SHARED]) * gu[:, D_SHARED:], wsd_ref[...])
        xs_ref[...] = x1 + gate2_ref[...] * shared
        yield
        logits_t = lax.dot_general(wrt_ref[...], h2b, NT_DIMS, preferred_element_type=F32)
        scores_s[...] = _sigmoid(logits_t)

    def first_argmax(vals, rows):
        m = jnp.max(vals, axis=0, keepdims=True)
        return m, jnp.min(jnp.where(vals == m, rows, big), axis=0, keepdims=True)

    def route(scores):
        biased = scores + ebias_ref[...]
        erow = lax.broadcasted_iota(I32, (N_EXPERTS, tm), 0)
        group_scores = []
        for g in range(N_GROUPS):
            vals = biased[g * GROUP_SIZE:(g + 1) * GROUP_SIZE, :]
            rows = lax.broadcasted_iota(I32, (GROUP_SIZE, tm), 0) + g * GROUP_SIZE
            m1, i1 = first_argmax(vals, rows)
            m2 = jnp.max(jnp.where(rows == i1, neg, vals), axis=0, keepdims=True)
            group_scores.append(m1 + m2)
            yield
        group_sel = [jnp.zeros((1, tm), jnp.bool_)] * N_GROUPS
        for _ in range(TOPK_GROUPS):
            best = functools.reduce(jnp.maximum, group_scores)
            gi = functools.reduce(
                jnp.minimum, [jnp.where(group_scores[g] == best, jnp.int32(g), big) for g in range(N_GROUPS)])
            for g in range(N_GROUPS):
                hit = gi == g
                group_sel[g] = jnp.logical_or(group_sel[g], hit)
                group_scores[g] = jnp.where(hit, neg, group_scores[g])
        cand = jnp.concatenate(
            [jnp.where(group_sel[g], biased[g * GROUP_SIZE:(g + 1) * GROUP_SIZE, :], neg)
             for g in range(N_GROUPS)], axis=0)
        yield
        w_rows, hits = [], []
        picked = jnp.zeros((N_EXPERTS, tm), F32)
        for k in range(TOP_K):
            _, ei = first_argmax(cand, erow)
            hit = erow == ei
            eidx_ref[k:k + 1, :] = ei
            w_rows.append(jnp.sum(jnp.where(hit, scores, 0.0), axis=0, keepdims=True))
            hits.append(hit)
            picked = jnp.where(hit, 1.0, picked)
            cand = jnp.where(hit, neg, cand)
            yield
        w_sum = functools.reduce(jnp.add, w_rows)
        for k in range(TOP_K):
            ew_ref[k:k + 1, :] = w_rows[k] / w_sum * ROUTE_SCALE
        yield
        trow = lax.broadcasted_iota(I32, (tm, tm), 0)
        tcol = lax.broadcasted_iota(I32, (tm, tm), 1)
        before = (trow < tcol).astype(BF16)
        prefix = jnp.dot(picked.astype(BF16), before, preferred_element_type=F32) + cnt_ref[:, 0:1]
        for k in range(TOP_K):
            rank_ref[k:k + 1, :] = jnp.sum(jnp.where(hits[k], prefix, 0.0), axis=0, keepdims=True).astype(I32)
            yield
        counted = jnp.where(i > 0, jnp.sum(picked, axis=1, keepdims=True), 0.0)
        cnt_ref[...] = cnt_ref[...] + counted

    jobs = [route(scores_s[...]), project()]
    shares = [4, 1]
    while jobs:
        for job, share in list(zip(jobs, shares)):
            for _ in range(share):
                if next(job, "done") == "done":
                    k = jobs.index(job)
                    del jobs[k], shares[k]
                    break


def _post(acta, actb, sga, sgb, x2d, gate1, scale2, shift2, gate2, cnt0, g_ffn, w, ebias_col,
          *, tm, tiles_per_mod):
    t = x2d.shape[0]
    n_tiles = t // tm
    tile = lambda i: jnp.minimum(i, n_tiles - 1)
    routed_tile = lambda i: jnp.maximum(i - 1, 0)
    row = lambda c: pl.BlockSpec((tm, c), lambda i: (tile(i), 0))
    col = lambda r: pl.BlockSpec((r, tm), lambda i: (0, routed_tile(i)))
    mod = lambda m: pl.BlockSpec((None,) + m.shape[1:], lambda i: (tile(i) // tiles_per_mod, 0, 0))
    mods = [gate1, scale2, shift2, gate2]
    consts = [cnt0, g_ffn, w["pa"], w["pb"], w["o"], w["router_t"], ebias_col, w["s_gu"], w["s_down"]]
    return pl.pallas_call(
        functools.partial(_post_kernel, tm=tm),
        grid=(n_tiles + 1,),
        in_specs=[row(CONV_CH), row(DN_WIDTH), row(D_MODEL), row(D_MODEL), row(D_MODEL)]
                 + [mod(m) for m in mods] + [_const_spec(a) for a in consts],
        out_specs=[row(D_MODEL), row(D_MODEL // 2), col(TOP_K), col(TOP_K), col(TOP_K), _const_spec(cnt0)],
        scratch_shapes=[pltpu.VMEM((N_EXPERTS, tm), F32)],
        out_shape=[jax.ShapeDtypeStruct((t, D_MODEL), F32), jax.ShapeDtypeStruct((t, D_MODEL // 2), U32),
                   jax.ShapeDtypeStruct((TOP_K, t), I32), jax.ShapeDtypeStruct((TOP_K, t), F32),
                   jax.ShapeDtypeStruct((TOP_K, t), I32), jax.ShapeDtypeStruct(cnt0.shape, F32)],
        compiler_params=_cparams(),
        name="post_mixer",
    )(acta, actb, sga, sgb, x2d, *mods, *consts)


def _dest_kernel(start_ref, eidx_ref, rank_ref, o_ref, *, width):
    step = SUBLANES * LANES
    for c0 in range(0, width, step):
        c1 = min(c0 + step, width)
        e = eidx_ref[:, c0:c1]
        base = lax.fori_loop(0, N_EXPERTS, lambda x, acc: jnp.where(e == x, start_ref[x], acc),
                             jnp.zeros(e.shape, I32), unroll=8)
        o_ref[:, c0:c1] = base + rank_ref[:, c0:c1]


def _dest(cnt_start, e_idx, rank):
    k, t = e_idx.shape
    full = pl.BlockSpec((k, t), lambda i, st: (0, 0))
    return pl.pallas_call(
        functools.partial(_dest_kernel, width=t),
        grid_spec=pltpu.PrefetchScalarGridSpec(
            num_scalar_prefetch=1, grid=(1,), in_specs=[full, full], out_specs=full),
        out_shape=jax.ShapeDtypeStruct((k, t), I32),
        compiler_params=_cparams(),
        name="moe_dest",
    )(cnt_start, e_idx, rank)


def _sc_mesh():
    return plsc.VectorSubcoreMesh(core_axis_name="core", subcore_axis_name="subcore")


def _sc_pipeline(body, n_steps, in_specs, out_specs):
    return pltpu.emit_pipeline(body, grid=(n_steps,), in_specs=in_specs, out_specs=out_specs,
                               core_axis_name=("core", "subcore"), dimension_semantics=(pltpu.PARALLEL,))


def _sc_scatter_rows(x_groups, dest_flat, n_rows):
    tiles_all = sum(x.shape[0] for x in x_groups) // SC_WINDOW
    n_k = dest_flat.shape[1] // (tiles_all * SC_WINDOW)
    n_groups = len(x_groups)

    @functools.partial(pl.kernel, mesh=_sc_mesh(), scratch_types=[],
                       out_type=[jax.ShapeDtypeStruct((n_rows, QUARTER_WORDS), x_groups[0].dtype)] * 2)
    def scatter(*refs):
        x_hbms, i_hbm, o_hbms = refs[:n_groups], refs[n_groups], refs[n_groups + 1:]
        for half, o_hbm in enumerate(o_hbms):
            def body(x_vmem, i_vmem, o_hbm=o_hbm):
                pltpu.sync_copy(x_vmem, o_hbm.at[i_vmem.at[0]])

            tile0 = 0
            for x_hbm in x_hbms:
                tiles = x_hbm.shape[0] // SC_WINDOW
                _sc_pipeline(body, n_k * tiles,
                             [pl.BlockSpec((SC_WINDOW, QUARTER_WORDS),
                                           lambda i, tiles=tiles, half=half: (i % tiles, half)),
                              pl.BlockSpec((1, SC_WINDOW),
                                           lambda i, tiles=tiles, tile0=tile0:
                                           (0, (i // tiles) * tiles_all + tile0 + i % tiles))],
                             [])(x_hbm, i_hbm)
                tile0 += tiles

    return scatter(*x_groups, dest_flat)


def _sc_gather_rows(tables, idx_flat):
    n = idx_flat.shape[1]
    n_tables = len(tables)

    @functools.partial(pl.kernel, mesh=_sc_mesh(), scratch_types=[],
                       out_type=[jax.ShapeDtypeStruct((n, t.shape[1]), t.dtype) for t in tables])
    def gather(*refs):
        t_hbms, i_hbm, o_hbms = refs[:n_tables], refs[n_tables], refs[n_tables + 1:]
        for t_hbm, o_hbm in zip(t_hbms, o_hbms):
            def body(i_vmem, o_vmem, t_hbm=t_hbm):
                pltpu.sync_copy(t_hbm.at[i_vmem.at[0]], o_vmem)

            _sc_pipeline(body, n // SC_WINDOW,
                         [pl.BlockSpec((1, SC_WINDOW), lambda i: (0, i))],
                         [pl.BlockSpec((SC_WINDOW, t_hbm.shape[1]), lambda i: (i, 0))])(i_hbm, o_hbm)

    return gather(*tables, idx_flat)


def _experts_kernel(sblk_ref, sexp_ref, slo_ref, shi_ref, snext_ref, first_ref,
                    xa_ref, xb_ref, wg_hbm, wu_hbm, wd_hbm, ya_ref, yb_ref,
                    wg_buf, wu_buf, wd_buf, wgu_s, wd_s, state_ref, sems, *, bm):
    p = pl.program_id(0)
    lo = slo_ref[p]
    expert = sexp_ref[p]

    def weight_copies(e, slot):
        return [pltpu.make_async_copy(src.at[e], dst.at[slot], sems.at[slot])
                for src, dst in ((wg_hbm, wg_buf), (wu_hbm, wu_buf), (wd_hbm, wd_buf))]

    @pl.when(p == 0)
    def _():
        state_ref[0] = -1
        state_ref[1] = 0
        for c in weight_copies(first_ref[0], 0):
            c.start()

    @pl.when(shi_ref[p] > lo)
    def _():
        @pl.when(state_ref[0] != expert)
        def _():
            slot = state_ref[1]
            for c in weight_copies(expert, slot):
                c.wait()
            wgu_s[:, :D_EXPERT] = wg_buf[slot].astype(BF16)
            wgu_s[:, D_EXPERT:] = wu_buf[slot].astype(BF16)
            wd_s[...] = wd_buf[slot].astype(BF16)
            state_ref[0] = expert
            state_ref[1] = 1 - slot
            upcoming = snext_ref[p]

            @pl.when(upcoming >= 0)
            def _():
                for c in weight_copies(upcoming, 1 - slot):
                    c.start()

        q = QUARTER_WORDS
        sub = bm // EXPERT_SUBBLOCKS
        first_sub = lo // sub
        last_sub = (shi_ref[p] - 1) // sub

        def run(r0, r1):
            quarters = {}
            quarters[0], quarters[2] = _unpack_bf16_pairs(xa_ref[r0:r1, :])
            quarters[1], quarters[3] = _unpack_bf16_pairs(xb_ref[r0:r1, :])
            gu = functools.reduce(jnp.add, [
                jnp.dot(quarters[c], wgu_s[c * q:(c + 1) * q, :], preferred_element_type=F32) for c in range(4)])
            hb = (_silu(gu[:, :D_EXPERT]) * gu[:, D_EXPERT:]).astype(BF16)
            y = _pack_bf16_pairs(jnp.dot(hb, wd_s[...], preferred_element_type=F32))

            @pl.when(lo <= r0)
            def _():
                ya_ref[r0:r1, :] = y[:, :q]
                yb_ref[r0:r1, :] = y[:, q:]

            @pl.when(lo > r0)
            def _():
                keep = lax.broadcasted_iota(I32, (r1 - r0, 1), 0) + r0 >= lo
                ya_ref[r0:r1, :] = jnp.where(keep, y[:, :q], ya_ref[r0:r1, :])
                yb_ref[r0:r1, :] = jnp.where(keep, y[:, q:], yb_ref[r0:r1, :])

        for a in range(EXPERT_SUBBLOCKS):
            for b in range(a, EXPERT_SUBBLOCKS):
                pl.when(jnp.logical_and(first_sub == a, last_sub == b))(
                    functools.partial(run, a * sub, (b + 1) * sub))


def _experts(xs_a, xs_b, w_gate, w_up, w_down, seg_blk, seg_exp, seg_lo, seg_hi, seg_next, first_exp, *, bm):
    n_rows = xs_a.shape[0]
    n_seg = seg_blk.shape[0]
    ring = 2
    hbm = pl.BlockSpec(memory_space=pl.ANY)
    rows = pl.BlockSpec((bm, QUARTER_WORDS), lambda p, sb, *_: (sb[p], 0))
    grid_spec = pltpu.PrefetchScalarGridSpec(
        num_scalar_prefetch=6,
        grid=(n_seg,),
        in_specs=[rows, rows, hbm, hbm, hbm],
        out_specs=[rows, rows],
        scratch_shapes=[pltpu.VMEM((ring, D_MODEL, D_EXPERT), F32), pltpu.VMEM((ring, D_MODEL, D_EXPERT), F32),
                        pltpu.VMEM((ring, D_EXPERT, D_MODEL), F32),
                        pltpu.VMEM((D_MODEL, 2 * D_EXPERT), BF16), pltpu.VMEM((D_EXPERT, D_MODEL), BF16),
                        pltpu.SMEM((2,), I32), pltpu.SemaphoreType.DMA((ring,))],
    )
    return pl.pallas_call(
        functools.partial(_experts_kernel, bm=bm),
        grid_spec=grid_spec,
        out_shape=[jax.ShapeDtypeStruct((n_rows, QUARTER_WORDS), U32)] * 2,
        compiler_params=_cparams(),
        name="moe_experts",
    )(seg_blk, seg_exp, seg_lo, seg_hi, seg_next, first_exp, xs_a, xs_b, w_gate, w_up, w_down)


def _final_kernel(*refs, tm, n_prompt_tiles, has_sample, has_prev):
    refs = list(refs)
    ew_ref, pa_ref, pb_ref, xsp_ref = refs[:4]
    del refs[:4]
    xss_ref = refs.pop(0) if has_sample else None
    g2p_ref = refs.pop(0)
    g2s_ref = refs.pop(0) if has_sample else None
    gfin_ref = refs.pop(0)
    if has_prev:
        refs.pop(0)
    op_ref = refs.pop(0)
    os_ref = refs.pop(0) if has_sample else None
    ew = ew_ref[...]
    quarters = [jnp.zeros((tm, QUARTER_WORDS), F32)] * 4
    for k in range(TOP_K):
        w_k = ew[:, k:k + 1]
        a_low, a_high = _unpack_bf16_pairs(pa_ref[k])
        b_low, b_high = _unpack_bf16_pairs(pb_ref[k])
        for c, part in enumerate((a_low, b_low, a_high, b_high)):
            quarters[c] = quarters[c] + w_k * part.astype(F32)
    routed = jnp.concatenate(quarters, axis=1)

    def finish(xs_ref, gate2_ref, o_ref):
        x2 = xs_ref[...] + gate2_ref[...] * routed
        o_ref[...] = x2 * lax.rsqrt(jnp.mean(x2 * x2, axis=-1, keepdims=True) + EPS) * gfin_ref[...]

    if not has_sample:
        finish(xsp_ref, g2p_ref, op_ref)
    else:
        is_prompt = pl.program_id(0) < n_prompt_tiles
        pl.when(is_prompt)(lambda: finish(xsp_ref, g2p_ref, op_ref))
        pl.when(jnp.logical_not(is_prompt))(lambda: finish(xss_ref, g2s_ref, os_ref))


def _final(ew_part, planes_a, planes_b, xs_p, gate2_p, g_final, *, tm, tiles_per_batch, tile_lo, n_prompt_tiles,
           sample=None, y_prev=None):
    prompt_tile = lambda i: tile_lo + jnp.minimum(i, n_prompt_tiles - 1)
    sample_tile = lambda i: jnp.maximum(i - n_prompt_tiles, 0)
    prompt_rows = pl.BlockSpec((tm, D_MODEL), lambda i: (prompt_tile(i), 0))
    sample_rows = pl.BlockSpec((tm, D_MODEL), lambda i: (sample_tile(i), 0))
    planes = pl.BlockSpec((TOP_K, tm, QUARTER_WORDS), lambda i: (0, i, 0))
    has_sample = sample is not None
    has_prev = y_prev is not None
    n_tiles = n_prompt_tiles + (sample[0].shape[0] // tm if has_sample else 0)
    args = [ew_part, planes_a, planes_b, xs_p]
    in_specs = [pl.BlockSpec((tm, TOP_K), lambda i: (i, 0)), planes, planes, prompt_rows]
    if has_sample:
        args.append(sample[0])
        in_specs.append(sample_rows)
    args.append(gate2_p)
    in_specs.append(pl.BlockSpec((None, 1, D_MODEL), lambda i: (prompt_tile(i) // tiles_per_batch, 0, 0)))
    if has_sample:
        args.append(sample[1])
        in_specs.append(pl.BlockSpec((None, tm, D_MODEL), lambda i: (sample_tile(i), 0, 0)))
    args.append(g_final)
    in_specs.append(_const_spec(g_final))
    if has_prev:
        args.append(y_prev)
        in_specs.append(pl.BlockSpec(memory_space=pl.ANY))
    out_specs = [prompt_rows] + ([sample_rows] if has_sample else [])
    out_shape = [jax.ShapeDtypeStruct(xs_p.shape, F32)] + (
        [jax.ShapeDtypeStruct(sample[0].shape, F32)] if has_sample else [])
    outs = pl.pallas_call(
        functools.partial(_final_kernel, tm=tm, n_prompt_tiles=n_prompt_tiles, has_sample=has_sample,
                          has_prev=has_prev),
        grid=(n_tiles,),
        in_specs=in_specs,
        out_specs=out_specs,
        out_shape=out_shape,
        input_output_aliases={len(args) - 1: 0} if has_prev else {},
        compiler_params=_cparams(),
        name="moe_combine_final",
    )(*args)
    return outs if has_sample else outs[0]


def _pick_tile(n, preferred):
    t = min(n, preferred)
    assert n % t == 0, (n, t)
    return t


def kernel(x_prompt, x_sample, c_prompt, c_sample, state_conv, state_short_conv, state_delta, w_ada, b_ada, g_mix, w_in, conv_dw, conv_dw_b, conv_ln_g, conv_ln_b, w_pa, short_conv_w, a_log, dt_bias, dn_norm_g, w_pb, w_o, g_ffn, w_router, e_bias, w_gate, w_up, w_down, ws_gate, ws_up, ws_down, g_final):
    nb, seq, d = x_prompt.shape
    ns = x_sample.shape[0]
    assert d == D_MODEL and w_ada.shape[0] == 1 and x_sample.shape[1] == 1
    assert seq % CHUNK == 0 and seq >= CONV_HALO and ns % SUBLANES == 0
    tp = nb * seq
    row1 = lambda a: a.reshape(1, -1)

    wi = w_in[0].astype(BF16)
    o_q = 2 * CONV_CH
    o_z = o_q + 3 * DN_WIDTH
    o_a = o_z + DN_WIDTH
    o_ga = o_a + 2 * N_HEADS
    w_proj = {
        "glu": wi[:, :o_q], "qkv": wi[:, o_q:o_z], "z": wi[:, o_z:o_a],
        "ab": jnp.pad(wi[:, o_a:o_ga], ((0, 0), (0, LANES - 2 * N_HEADS))),
        "ga": wi[:, o_ga:o_ga + D_MODEL], "gb": wi[:, o_ga + D_MODEL:],
    }
    nea = jnp.pad(-jnp.exp(a_log[0].astype(F32)), (0, LANES - N_HEADS)).reshape(1, LANES)
    dtb = jnp.pad(dt_bias[0].astype(F32), (0, LANES - N_HEADS)).reshape(1, LANES)
    w_post = {
        "pa": w_pa[0].astype(BF16), "pb": w_pb[0].astype(BF16), "o": w_o[0].astype(BF16),
        "router_t": w_router[0].T.astype(BF16),
        "s_gu": jnp.concatenate([ws_gate[0], ws_up[0]], axis=1).astype(BF16),
        "s_down": ws_down[0].astype(BF16),
    }
    ebias_col = e_bias[0].astype(F32).reshape(N_EXPERTS, 1)
    g_mix1, g_ffn1, g_fin1 = row1(g_mix[0]), row1(g_ffn[0]), row1(g_final)
    cdw, cdb, clg, clb = conv_dw[0], row1(conv_dw_b[0]), row1(conv_ln_g[0]), row1(conv_ln_b[0])
    scw, dng = short_conv_w[0], row1(dn_norm_g[0])

    ada = _ada(jnp.concatenate([c_prompt, c_sample], axis=0), w_ada[0].astype(BF16), b_ada[0])
    ada = ada.reshape(nb + ns, 6, D_MODEL)
    mod_p = [ada[:nb, m].reshape(nb, 1, D_MODEL) for m in range(6)]
    mod_s = [ada[nb:, m].reshape(1, ns, D_MODEL) for m in range(6)]
    shift1, scale1, gate1, shift2, scale2, gate2 = range(6)

    tm_p = _pick_tile(seq, 256)
    tpm_p = seq // tm_p
    xp = x_prompt.reshape(tp, D_MODEL)
    u_p, qkv_p, zs_p, ab_p, sga_p, sgb_p = _in_proj(
        xp, mod_p[scale1], mod_p[shift1], g_mix1, w_proj, nea, dtb, tm=tm_p, tiles_per_mod=tpm_p)
    tl = _pick_tile(seq, 256)
    acta_p = _conv_branch(u_p, jnp.zeros((nb, CONV_HALO, CONV_CH), F32), cdw, cdb, clg, clb,
                          tl=tl, tiles_per_batch=seq // tl)
    tc = _pick_tile(seq, 4 * CHUNK)
    actb_p, s_new_p = _delta_branch(
        qkv_p, jnp.zeros((nb, SC_HALO, 3 * DN_WIDTH), F32),
        jnp.zeros((nb, N_HEADS, HEAD_DIM, HEAD_DIM), F32), scw, ab_p, zs_p, dng,
        tc=tc, tiles_per_batch=seq // tc)

    xs_ = x_sample.reshape(ns, D_MODEL)
    u_s, qkv_s, zs_s, ab_s, sga_s, sgb_s = _in_proj(
        xs_, mod_s[scale1], mod_s[shift1], g_mix1, w_proj, nea, dtb, tm=ns, tiles_per_mod=1)
    bt = SUBLANES
    acta_s, nch_s, qkvp_s, nsh_s = _step_front(
        u_s, qkv_s, state_conv[0].reshape(ns, -1), state_short_conv[0].reshape(ns, -1),
        cdw, cdb, clg, clb, scw, bt=bt)
    actb_s, s_new_s = _step_delta(qkvp_s, ab_s, zs_s, state_delta[0].astype(F32), dng, bt=bt)

    cnt0 = jnp.zeros((N_EXPERTS, LANES), F32)
    xsh_p, h2_p, eidx_p, ew_p, rank_p, cnt_p = _post(
        acta_p, actb_p, sga_p, sgb_p, xp, mod_p[gate1], mod_p[scale2], mod_p[shift2], mod_p[gate2],
        cnt0, g_ffn1, w_post, ebias_col, tm=tm_p, tiles_per_mod=tpm_p)
    xsh_s, h2_s, eidx_s, ew_s, rank_s, cnt_all = _post(
        acta_s, actb_s, sga_s, sgb_s, xs_, mod_s[gate1], mod_s[scale2], mod_s[shift2], mod_s[gate2],
        cnt_p, g_ffn1, w_post, ebias_col, tm=ns, tiles_per_mod=1)

    bm = EXPERT_SUBBLOCKS * EXPERT_SUB_ROWS
    tmd = LANES
    t_all = tp + ns
    n_rows = t_all * TOP_K
    assert seq % tmd == 0 and ns % tmd == 0 and n_rows % bm == 0 and t_all % SC_WINDOW == 0
    n_blocks = n_rows // bm
    counts = cnt_all[:, 0].astype(I32)
    cnt_end = jnp.cumsum(counts)
    cnt_start = cnt_end - counts
    e_idx = jnp.concatenate([eidx_p, eidx_s], axis=1)
    rank = jnp.concatenate([rank_p, rank_s], axis=1)
    dest = _dest(cnt_start, e_idx, rank)
    dest_flat = dest.reshape(1, n_rows)
    seg_start = jnp.sort(jnp.concatenate([jnp.arange(n_blocks, dtype=I32) * bm, cnt_start]))
    seg_end = jnp.concatenate([seg_start[1:], jnp.full((1,), n_rows, I32)])
    seg_blk = jnp.minimum(seg_start // bm, n_blocks - 1)
    seg_lo = seg_start - seg_blk * bm
    seg_hi = seg_end - seg_blk * bm
    seg_exp = jnp.sum((cnt_end[None, :] <= seg_start[:, None]).astype(I32), axis=1)
    seg_exp = jnp.minimum(seg_exp, N_EXPERTS - 1)
    seg_exp = lax.cummax(jnp.where(seg_hi > seg_lo, seg_exp, 0))
    none = jnp.int32(N_EXPERTS)
    in_use_from = lax.cummin(jnp.where(counts > 0, jnp.arange(N_EXPERTS, dtype=I32), none), reverse=True)
    next_in_use = jnp.concatenate([in_use_from[1:], jnp.full((1,), none, I32)])
    next_in_use = jnp.where(next_in_use == none, -1, next_in_use)
    seg_next = next_in_use[seg_exp]
    first_exp = jnp.minimum(in_use_from[:1], N_EXPERTS - 1)

    xs_a, xs_b = _sc_scatter_rows((h2_p, h2_s), dest_flat, n_rows)
    ys_a, ys_b = _experts(xs_a, xs_b, w_gate[0], w_up[0], w_down[0], seg_blk, seg_exp, seg_lo, seg_hi, seg_next,
                          first_exp, bm=bm)

    ew_tok = jnp.concatenate([ew_p, ew_s], axis=1).T
    n_prompt_tiles = tp // tmd
    tiles_1 = n_prompt_tiles // 2
    t_1 = tiles_1 * tmd

    def gathered(lo, hi):
        idx = dest[:, lo:hi].reshape(1, -1)
        return [p.reshape(TOP_K, hi - lo, QUARTER_WORDS) for p in _sc_gather_rows((ys_a, ys_b), idx)]

    final = functools.partial(_final, xs_p=xsh_p, gate2_p=mod_p[gate2], g_final=g_fin1, tm=tmd,
                              tiles_per_batch=seq // tmd)
    y_p = final(ew_tok[:t_1], *gathered(0, t_1), tile_lo=0, n_prompt_tiles=tiles_1)
    y_p, y_s = final(ew_tok[t_1:], *gathered(t_1, t_all), tile_lo=tiles_1, n_prompt_tiles=n_prompt_tiles - tiles_1,
                     sample=(xsh_s, mod_s[gate2].reshape(ns // tmd, tmd, D_MODEL)), y_prev=y_p)

    hist_rows = CONV_WIDTH - 1
    sc_rows = SHORT_CONV - 1
    y_prompt = y_p.reshape(nb, seq, D_MODEL)
    y_sample = y_s.reshape(ns, 1, D_MODEL)
    new_conv_prompt = u_p.reshape(nb, seq, CONV_CH)[:, seq - hist_rows:][None]
    new_sc_prompt = qkv_p.reshape(nb, seq, 3 * DN_WIDTH)[:, seq - sc_rows:][None]
    new_conv_sample = nch_s.reshape(ns, hist_rows, CONV_CH)[None]
    new_sc_sample = nsh_s.reshape(ns, sc_rows, 3 * DN_WIDTH)[None]
    return (y_prompt, y_sample, new_conv_prompt, new_sc_prompt, s_new_p[None],
            new_conv_sample, new_sc_sample, s_new_s[None])
```

```python
import functools

import jax
import jax.numpy as jnp
from jax import lax
from jax.experimental import pallas as pl
from jax.experimental.pallas import tpu as pltpu
from jax.experimental.pallas import tpu_sc as plsc

F32 = jnp.float32
BF16 = jnp.bfloat16
I32 = jnp.int32
U32 = jnp.uint32

EPS = 1e-6
D_MODEL = 1024
CONV_CH = 512
CONV_WIDTH = 31
N_HEADS = 4
HEAD_DIM = 128
DN_WIDTH = N_HEADS * HEAD_DIM
SHORT_CONV = 4
CHUNK = 64
CHUNK_SHIFT = CHUNK.bit_length() - 1
assert 1 << CHUNK_SHIFT == CHUNK
SUB = 16
SUB_SHIFT = SUB.bit_length() - 1
assert 1 << SUB_SHIFT == SUB and CHUNK % SUB == 0
N_EXPERTS = 256
TOP_K = 8
N_GROUPS = 8
GROUP_SIZE = N_EXPERTS // N_GROUPS
TOPK_GROUPS = 4
D_EXPERT = 256
D_SHARED = 256
ROUTE_SCALE = 2.5

LANES = 128
SUBLANES = 8
CONV_HALO = 32
SC_HALO = 8
VMEM_LIMIT = 48 * 1024 * 1024
EXPERT_SUBBLOCKS = 8
EXPERT_SUB_ROWS = 128
SC_WINDOW = 128
QUARTER_WORDS = D_MODEL // 4

NT_DIMS = (((1,), (1,)), ((), ()))
TN_DIMS = (((0,), (0,)), ((), ()))


def _cparams(n_grid_dims=1):
    return pltpu.CompilerParams(
        dimension_semantics=("arbitrary",) * n_grid_dims,
        vmem_limit_bytes=VMEM_LIMIT)


def _sigmoid(x):
    return jax.nn.sigmoid(x)


def _silu(x):
    return x * jax.nn.sigmoid(x)


def _bdot(a, b):
    return jnp.dot(a.astype(BF16), b.astype(BF16), preferred_element_type=F32)


def _pack_bf16_pairs(x):
    half = x.shape[1] // 2
    bits = lax.bitcast_convert_type(x.astype(BF16).astype(F32), U32)
    return bits[:, half:] | (bits[:, :half] >> 16)


def _unpack_bf16_pairs(w):
    low = lax.bitcast_convert_type(w << 16, F32).astype(BF16)
    high = lax.bitcast_convert_type(w & jnp.uint32(0xFFFF0000), F32).astype(BF16)
    return low, high


def _dot_exact_lhs(a_bf, b):
    hi = b.astype(BF16)
    r1 = b - hi.astype(F32)
    mid = r1.astype(BF16)
    low = (r1 - mid.astype(F32)).astype(BF16)
    dot = lambda p: jnp.dot(a_bf, p, preferred_element_type=F32)
    return dot(hi) + dot(mid) + dot(low)


def _ada_kernel(c_ref, w_ref, b_ref, o_ref):
    o_ref[...] = _bdot(_silu(c_ref[...]), w_ref[...]) + b_ref[...]


def _ada(c_all, w_ada_bf, b_ada):
    n = c_all.shape[0]
    width = w_ada_bf.shape[1]
    tn = 1536
    return pl.pallas_call(
        _ada_kernel,
        grid=(width // tn,),
        in_specs=[pl.BlockSpec((n, D_MODEL), lambda j: (0, 0)),
                  pl.BlockSpec((D_MODEL, tn), lambda j: (0, j)),
                  pl.BlockSpec((1, tn), lambda j: (0, j))],
        out_specs=pl.BlockSpec((n, tn), lambda j: (0, j)),
        out_shape=jax.ShapeDtypeStruct((n, width), F32),
        compiler_params=_cparams(),
        name="ada",
    )(c_all, w_ada_bf, b_ada.reshape(1, width))


def _in_proj_kernel(x_ref, scale_ref, shift_ref, g_ref, wglu_ref, wqkv_ref, wz_ref, wab_ref,
                    wga_ref, wgb_ref, nea_ref, dtb_ref,
                    u_ref, qkv_ref, zs_ref, ab_ref, sga_ref, sgb_ref):
    x = x_ref[...]
    y = x * lax.rsqrt(jnp.mean(x * x, axis=-1, keepdims=True) + EPS) * g_ref[...]
    h = (y * (1.0 + scale_ref[...]) + shift_ref[...]).astype(BF16)
    glu = jnp.dot(h, wglu_ref[...], preferred_element_type=F32)
    u_ref[...] = glu[:, :CONV_CH] * _sigmoid(glu[:, CONV_CH:])
    qkv_ref[...] = jnp.dot(h, wqkv_ref[...], preferred_element_type=F32)
    zs_ref[...] = _silu(jnp.dot(h, wz_ref[...], preferred_element_type=F32))
    ab = jnp.dot(h, wab_ref[...], preferred_element_type=F32)
    sp = ab + dtb_ref[...]
    softplus = jnp.maximum(sp, 0.0) + jnp.log1p(jnp.exp(-jnp.abs(sp)))
    lane = lax.broadcasted_iota(I32, ab.shape, 1)
    ab_ref[...] = jnp.where(lane < N_HEADS, nea_ref[...] * softplus, _sigmoid(ab))
    sga_ref[...] = _sigmoid(jnp.dot(h, wga_ref[...], preferred_element_type=F32))
    sgb_ref[...] = _sigmoid(jnp.dot(h, wgb_ref[...], preferred_element_type=F32))


def _mod_spec(mod, tiles_per_mod):
    return pl.BlockSpec((None,) + mod.shape[1:], lambda i: (i // tiles_per_mod, 0, 0))


def _const_spec(a):
    nd = a.ndim
    return pl.BlockSpec(a.shape, lambda i: (0,) * nd)


def _in_proj(x2d, scale, shift, g_mix, w, nea, dtb, *, tm, tiles_per_mod):
    t = x2d.shape[0]
    row = lambda c: pl.BlockSpec((tm, c), lambda i: (i, 0))
    consts = [g_mix, w["glu"], w["qkv"], w["z"], w["ab"], w["ga"], w["gb"], nea, dtb]
    widths = [CONV_CH, 3 * DN_WIDTH, DN_WIDTH, LANES, D_MODEL, D_MODEL]
    return pl.pallas_call(
        _in_proj_kernel,
        grid=(t // tm,),
        in_specs=[row(D_MODEL), _mod_spec(scale, tiles_per_mod), _mod_spec(shift, tiles_per_mod)]
                 + [_const_spec(a) for a in consts],
        out_specs=[row(c) for c in widths],
        out_shape=[jax.ShapeDtypeStruct((t, c), F32) for c in widths],
        compiler_params=_cparams(),
        name="in_proj",
    )(x2d, scale, shift, *consts)


def _layer_norm_swish(y, g, b):
    mu = jnp.mean(y, axis=-1, keepdims=True)
    yc = y - mu
    var = jnp.mean(yc * yc, axis=-1, keepdims=True)
    return _silu(yc * lax.rsqrt(var + EPS) * g + b)


def _conv_kernel(u_ref, halo_ref, hist_ref, w_ref, b_ref, lg_ref, lb_ref, o_ref, win_ref, shift_ref,
                 *, tl, tiles_per_batch, rows_per_pass):
    i = pl.program_id(0)
    first = (i % tiles_per_batch) == 0

    @pl.when(first)
    def _():
        win_ref[0:CONV_HALO, :] = hist_ref[...]

    @pl.when(jnp.logical_not(first))
    def _():
        win_ref[0:CONV_HALO, :] = halo_ref[...]

    win_ref[CONV_HALO:CONV_HALO + tl, :] = u_ref[...]
    tap0 = CONV_HALO - (CONV_WIDTH - 1)
    n_shifted = CONV_HALO + tl - SUBLANES
    for b in range(1, SUBLANES):
        shift_ref[b - 1] = win_ref[b:b + n_shifted, :]

    for r in range(tl // rows_per_pass):
        base = r * rows_per_pass
        acc = jnp.zeros((rows_per_pass, CONV_CH), F32)
        for j in range(CONV_WIDTH):
            a, b = divmod(tap0 + j, SUBLANES)
            start = base + a * SUBLANES
            rows = (win_ref[start:start + rows_per_pass, :] if b == 0
                    else shift_ref[b - 1, start:start + rows_per_pass, :])
            acc = acc + w_ref[j:j + 1, :] * rows
        act = _layer_norm_swish(acc + b_ref[...], lg_ref[...], lb_ref[...])
        o_ref[base:base + rows_per_pass, :] = act.astype(o_ref.dtype)


def _conv_branch(u2d, hist, conv_dw, conv_dw_b, ln_g, ln_b, *, tl, tiles_per_batch):
    t = u2d.shape[0]
    halo_blocks = tl // CONV_HALO
    consts = [conv_dw, conv_dw_b, ln_g, ln_b]
    return pl.pallas_call(
        functools.partial(_conv_kernel, tl=tl, tiles_per_batch=tiles_per_batch, rows_per_pass=64),
        grid=(t // tl,),
        in_specs=[pl.BlockSpec((tl, CONV_CH), lambda i: (i, 0)),
                  pl.BlockSpec((CONV_HALO, CONV_CH), lambda i: (jnp.maximum(i * halo_blocks - 1, 0), 0)),
                  pl.BlockSpec((None, CONV_HALO, CONV_CH), lambda i: (i // tiles_per_batch, 0, 0))]
                 + [_const_spec(a) for a in consts],
        out_specs=pl.BlockSpec((tl, CONV_CH), lambda i: (i, 0)),
        out_shape=jax.ShapeDtypeStruct((t, CONV_CH), BF16),
        scratch_shapes=[pltpu.VMEM((CONV_HALO + tl, CONV_CH), F32),
                        pltpu.VMEM((SUBLANES - 1, CONV_HALO + tl - SUBLANES, CONV_CH), F32)],
        compiler_params=_cparams(),
        name="conv_branch",
    )(u2d, u2d, hist, *consts)


def _l2n(x):
    return x * lax.rsqrt(jnp.sum(x * x, axis=-1, keepdims=True) + 1e-6)


def _gated_head_norm(o, g, zs):
    return o * lax.rsqrt(jnp.mean(o * o, axis=-1, keepdims=True) + EPS) * g * zs


def _delta_kernel(qkv_ref, halo_ref, hist_ref, scw_ref, ab_ref, zs_ref, s0_ref, dng_ref,
                  o_ref, s_ref, win_ref, c_ref, *, tc, tiles_per_batch):
    i = pl.program_id(0)
    first = (i % tiles_per_batch) == 0

    @pl.when(first)
    def _():
        win_ref[0:SC_HALO, :] = hist_ref[...]
        s_ref[...] = s0_ref[...]

    @pl.when(jnp.logical_not(first))
    def _():
        win_ref[0:SC_HALO, :] = halo_ref[...]

    win_ref[SC_HALO:SC_HALO + tc, :] = qkv_ref[...]
    tap0 = SC_HALO - (SHORT_CONV - 1)
    acc = jnp.zeros((tc, 3 * DN_WIDTH), F32)
    for j in range(SHORT_CONV):
        acc = acc + scw_ref[j:j + 1, :] * win_ref[tap0 + j:tap0 + j + tc, :]
    c_ref[...] = _silu(acc)

    row = lax.broadcasted_iota(I32, (tc, tc), 0)
    col = lax.broadcasted_iota(I32, (tc, tc), 1)
    same_chunk = lax.shift_right_logical(row, CHUNK_SHIFT) == lax.shift_right_logical(col, CHUNK_SHIFT)
    incl = jnp.logical_and(same_chunk, row >= col)
    row_sub = lax.shift_right_logical(row, SUB_SHIFT)
    col_sub = lax.shift_right_logical(col, SUB_SHIFT)
    strict_sub = jnp.logical_and(row_sub == col_sub, row > col)
    below_sub = jnp.logical_and(same_chunk, row_sub > col_sub)
    eye = (row == col).astype(F32)
    ab = ab_ref[...]
    gcum = _dot_exact_lhs(incl.astype(BF16), ab)
    gcum_t = gcum.T
    n_ch = tc // CHUNK

    heads = range(N_HEADS)
    q, k, gc, kk, a_qk, rhs, qd = [], [], [], [], [], [], []
    for h in heads:
        lo = h * HEAD_DIM
        q_h = _l2n(c_ref[:, lo:lo + HEAD_DIM]) * (HEAD_DIM ** -0.5)
        k_h = _l2n(c_ref[:, DN_WIDTH + lo:DN_WIDTH + lo + HEAD_DIM])
        v_h = c_ref[:, 2 * DN_WIDTH + lo:2 * DN_WIDTH + lo + HEAD_DIM]
        beta = ab[:, N_HEADS + h:N_HEADS + h + 1]
        gc_h = gcum[:, h:h + 1]
        gr = gcum_t[h:h + 1, :]
        decay = jnp.exp(jnp.where(incl, gc_h - gr, -jnp.inf))
        kb = k_h * beta
        kq = lax.dot_general(jnp.concatenate([kb, q_h], axis=0).astype(BF16), k_h.astype(BF16),
                             NT_DIMS, preferred_element_type=F32)
        e_gc = jnp.exp(gc_h)
        q.append(q_h)
        k.append(k_h)
        gc.append(gc_h)
        kk.append(kq[:tc] * decay)
        a_qk.append(kq[tc:] * decay)
        rhs.append(jnp.concatenate([v_h * beta, kb * e_gc], axis=1))
        qd.append(q_h * e_gc)

    pw = [jnp.where(strict_sub, kk[h], 0.0) for h in heads]
    d_inv = [eye - pw[h] for h in heads]
    for _ in range(SUB_SHIFT - 1):
        pw = [_bdot(pw[h], pw[h]) for h in heads]
        d_inv = [d_inv[h] + _bdot(d_inv[h], pw[h]) for h in heads]
    c_rhs = [_bdot(d_inv[h], rhs[h]) for h in heads]
    m_off = [_bdot(d_inv[h], jnp.where(below_sub, kk[h], 0.0)) for h in heads]
    sol = c_rhs
    for _ in range(CHUNK // SUB - 1):
        sol = [c_rhs[h] - _bdot(m_off[h], sol[h]) for h in heads]

    s = [s_ref[h] for h in heads]
    u_parts = [[] for _ in heads]
    o_parts = [[] for _ in heads]
    for c in range(n_ch):
        r0 = c * CHUNK
        for h in heads:
            u0_c = sol[h][r0:r0 + CHUNK, :HEAD_DIM]
            w_c = sol[h][r0:r0 + CHUNK, HEAD_DIM:]
            wq_s = _bdot(jnp.concatenate([w_c, qd[h][r0:r0 + CHUNK]], axis=0), s[h])
            u_c = u0_c - wq_s[:CHUNK]
            g_last = gcum[r0 + CHUNK - 1:r0 + CHUNK, h:h + 1]
            k_dec = k[h][r0:r0 + CHUNK] * jnp.exp(g_last - gc[h][r0:r0 + CHUNK])
            s[h] = jnp.exp(g_last) * s[h] + lax.dot_general(
                k_dec.astype(BF16), u_c.astype(BF16), TN_DIMS, preferred_element_type=F32)
            u_parts[h].append(u_c)
            o_parts[h].append(wq_s[CHUNK:])
    for h in heads:
        lo = h * HEAD_DIM
        s_ref[h] = s[h]
        o = jnp.concatenate(o_parts[h], axis=0) + _bdot(a_qk[h], jnp.concatenate(u_parts[h], axis=0))
        o_ref[:, lo:lo + HEAD_DIM] = _gated_head_norm(
            o, dng_ref[...], zs_ref[:, lo:lo + HEAD_DIM]).astype(o_ref.dtype)


def _delta_branch(qkv2d, hist, s0, scw, ab, zs, dng, *, tc, tiles_per_batch):
    t = qkv2d.shape[0]
    nb = s0.shape[0]
    halo_blocks = tc // SC_HALO
    wqkv = 3 * DN_WIDTH
    state_spec = pl.BlockSpec((None, N_HEADS, HEAD_DIM, HEAD_DIM), lambda i: (i // tiles_per_batch, 0, 0, 0))
    return pl.pallas_call(
        functools.partial(_delta_kernel, tc=tc, tiles_per_batch=tiles_per_batch),
        grid=(t // tc,),
        in_specs=[pl.BlockSpec((tc, wqkv), lambda i: (i, 0)),
                  pl.BlockSpec((SC_HALO, wqkv), lambda i: (jnp.maximum(i * halo_blocks - 1, 0), 0)),
                  pl.BlockSpec((None, SC_HALO, wqkv), lambda i: (i // tiles_per_batch, 0, 0)),
                  _const_spec(scw),
                  pl.BlockSpec((tc, LANES), lambda i: (i, 0)),
                  pl.BlockSpec((tc, DN_WIDTH), lambda i: (i, 0)),
                  state_spec,
                  _const_spec(dng)],
        out_specs=[pl.BlockSpec((tc, DN_WIDTH), lambda i: (i, 0)), state_spec],
        out_shape=[jax.ShapeDtypeStruct((t, DN_WIDTH), BF16),
                   jax.ShapeDtypeStruct((nb, N_HEADS, HEAD_DIM, HEAD_DIM), F32)],
        scratch_shapes=[pltpu.VMEM((SC_HALO + tc, wqkv), F32), pltpu.VMEM((tc, wqkv), F32)],
        compiler_params=_cparams(),
        name="delta_branch",
    )(qkv2d, qkv2d, hist, scw, ab, zs, s0, dng)


def _step_front_kernel(u_ref, qkv_ref, ch_ref, sh_ref, cw_ref, cb_ref, lg_ref, lb_ref, scw_ref,
                       acta_ref, nch_ref, qkvp_ref, nsh_ref):
    hist_rows = CONV_WIDTH - 1
    u = u_ref[...]
    acc = cw_ref[hist_rows:hist_rows + 1, :] * u
    for j in range(hist_rows):
        acc = acc + cw_ref[j:j + 1, :] * ch_ref[:, j * CONV_CH:(j + 1) * CONV_CH]
    acta_ref[...] = _layer_norm_swish(acc + cb_ref[...], lg_ref[...], lb_ref[...]).astype(acta_ref.dtype)
    nch_ref[:, :(hist_rows - 1) * CONV_CH] = ch_ref[:, CONV_CH:]
    nch_ref[:, (hist_rows - 1) * CONV_CH:] = u

    wq = 3 * DN_WIDTH
    sc_rows = SHORT_CONV - 1
    qkv = qkv_ref[...]
    acc = scw_ref[sc_rows:sc_rows + 1, :] * qkv
    for j in range(sc_rows):
        acc = acc + scw_ref[j:j + 1, :] * sh_ref[:, j * wq:(j + 1) * wq]
    c = _silu(acc)
    for h in range(N_HEADS):
        lo = h * HEAD_DIM
        qkvp_ref[:, lo:lo + HEAD_DIM] = _l2n(c[:, lo:lo + HEAD_DIM]) * (HEAD_DIM ** -0.5)
        qkvp_ref[:, DN_WIDTH + lo:DN_WIDTH + lo + HEAD_DIM] = _l2n(c[:, DN_WIDTH + lo:DN_WIDTH + lo + HEAD_DIM])
    qkvp_ref[:, 2 * DN_WIDTH:] = c[:, 2 * DN_WIDTH:]
    nsh_ref[:, :(sc_rows - 1) * wq] = sh_ref[:, wq:]
    nsh_ref[:, (sc_rows - 1) * wq:] = qkv


def _step_front(u, qkv, conv_hist_flat, sc_hist_flat, conv_dw, conv_dw_b, ln_g, ln_b, scw, *, bt):
    n = u.shape[0]
    wq = 3 * DN_WIDTH
    consts = [conv_dw, conv_dw_b, ln_g, ln_b, scw]
    row = lambda c: pl.BlockSpec((bt, c), lambda i: (i, 0))
    widths = [CONV_CH, conv_hist_flat.shape[1], wq, sc_hist_flat.shape[1]]
    dtypes = [BF16, F32, F32, F32]
    return pl.pallas_call(
        _step_front_kernel,
        grid=(n // bt,),
        in_specs=[row(CONV_CH), row(wq), row(widths[1]), row(widths[3])] + [_const_spec(a) for a in consts],
        out_specs=[row(c) for c in widths],
        out_shape=[jax.ShapeDtypeStruct((n, c), d) for c, d in zip(widths, dtypes)],
        compiler_params=_cparams(),
        name="step_front",
    )(u, qkv, conv_hist_flat, sc_hist_flat, *consts)


def _rows_to_cols(x):
    pad = jnp.zeros((LANES - x.shape[0], LANES), x.dtype)
    return jnp.concatenate([x, pad], axis=0).T


def _step_delta_kernel(qkvp_ref, ab_ref, zs_ref, s_ref, dng_ref, o_ref, sn_ref, *, bt):
    ab = ab_ref[...]
    for h in range(N_HEADS):
        lo = h * HEAD_DIM
        q = qkvp_ref[:, lo:lo + HEAD_DIM]
        k = qkvp_ref[:, DN_WIDTH + lo:DN_WIDTH + lo + HEAD_DIM]
        v = qkvp_ref[:, 2 * DN_WIDTH + lo:2 * DN_WIDTH + lo + HEAD_DIM]
        q_cols = _rows_to_cols(q)
        k_cols = _rows_to_cols(k)
        qk = jnp.sum(q * k, axis=-1, keepdims=True)
        alpha = jnp.exp(ab[:, h:h + 1])
        beta = ab[:, N_HEADS + h:N_HEADS + h + 1]
        seqs = range(bt)
        kc = [k_cols[:, j:j + 1] for j in seqs]
        a = [alpha[j:j + 1, :] for j in seqs]
        s_k = [jnp.sum(s_ref[j, h] * kc[j], axis=0, keepdims=True) for j in seqs]
        s_q = [jnp.sum(s_ref[j, h] * q_cols[:, j:j + 1], axis=0, keepdims=True) for j in seqs]
        u = [beta[j:j + 1, :] * (v[j:j + 1, :] - a[j] * s_k[j]) for j in seqs]
        for j in seqs:
            sn_ref[j, h] = a[j] * s_ref[j, h] + kc[j] * u[j]
        o = jnp.concatenate([a[j] * s_q[j] + qk[j:j + 1, :] * u[j] for j in seqs], axis=0)
        o_ref[:, lo:lo + HEAD_DIM] = _gated_head_norm(
            o, dng_ref[...], zs_ref[:, lo:lo + HEAD_DIM]).astype(o_ref.dtype)


def _step_delta(qkvp, ab, zs, s0, dng, *, bt):
    n = qkvp.shape[0]
    row = lambda c: pl.BlockSpec((bt, c), lambda i: (i, 0))
    state_spec = pl.BlockSpec((bt, N_HEADS, HEAD_DIM, HEAD_DIM), lambda i: (i, 0, 0, 0))
    return pl.pallas_call(
        functools.partial(_step_delta_kernel, bt=bt),
        grid=(n // bt,),
        in_specs=[row(3 * DN_WIDTH), row(LANES), row(DN_WIDTH), state_spec, _const_spec(dng)],
        out_specs=[row(DN_WIDTH), state_spec],
        out_shape=[jax.ShapeDtypeStruct((n, DN_WIDTH), BF16), jax.ShapeDtypeStruct(s0.shape, F32)],
        compiler_params=_cparams(),
        name="step_delta",
    )(qkvp, ab, zs, s0, dng)


def _post_kernel(acta_ref, actb_ref, sga_ref, sgb_ref, x_ref, gate1_ref, scale2_ref, shift2_ref,
                 gate2_ref, cnt0_ref, gffn_ref, wpa_ref, wpb_ref, wo_ref, wrt_ref, ebias_ref,
                 wsgu_ref, wsd_ref,
                 xs_ref, h2_ref, eidx_ref, ew_ref, rank_ref, cnt_ref, scores_s, *, tm):
    i = pl.program_id(0)

    @pl.when(i == 0)
    def _():
        cnt_ref[...] = cnt0_ref[...]
        scores_s[...] = jnp.zeros(scores_s.shape, F32)

    neg = -jnp.inf
    big = jnp.int32(1 << 30)

    def project():
        y_a = jnp.dot(acta_ref[...], wpa_ref[...], preferred_element_type=F32)
        yield
        y_b = jnp.dot(actb_ref[...], wpb_ref[...], preferred_element_type=F32)
        merged = sga_ref[...] * y_a + sgb_ref[...] * y_b
        yield
        mix = _bdot(merged, wo_ref[...])
        yield
        x1 = x_ref[...] + gate1_ref[...] * mix
        y = x1 * lax.rsqrt(jnp.mean(x1 * x1, axis=-1, keepdims=True) + EPS) * gffn_ref[...]
        h2 = y * (1.0 + scale2_ref[...]) + shift2_ref[...]
        h2_ref[...] = _pack_bf16_pairs(h2)
        h2b = h2.astype(BF16)
        yield
        gu = jnp.dot(h2b, wsgu_ref[...], preferred_element_type=F32)
        yield
        shared = _bdot(_silu(gu[:, :D_SHARED]) * gu[:, D_SHARED:], wsd_ref[...])
        xs_ref[...] = x1 + gate2_ref[...] * shared
        yield
        logits_t = lax.dot_general(wrt_ref[...], h2b, NT_DIMS, preferred_element_type=F32)
        scores_s[...] = _sigmoid(logits_t)

    def first_argmax(vals, rows):
        m = jnp.max(vals, axis=0, keepdims=True)
        return m, jnp.min(jnp.where(vals == m, rows, big), axis=0, keepdims=True)

    def route(scores):
        biased = scores + ebias_ref[...]
        erow = lax.broadcasted_iota(I32, (N_EXPERTS, tm), 0)
        group_scores = []
        for g in range(N_GROUPS):
            vals = biased[g * GROUP_SIZE:(g + 1) * GROUP_SIZE, :]
            rows = lax.broadcasted_iota(I32, (GROUP_SIZE, tm), 0) + g * GROUP_SIZE
            m1, i1 = first_argmax(vals, rows)
            m2 = jnp.max(jnp.where(rows == i1, neg, vals), axis=0, keepdims=True)
            group_scores.append(m1 + m2)
            yield
        group_sel = [jnp.zeros((1, tm), jnp.bool_)] * N_GROUPS
        for _ in range(TOPK_GROUPS):
            best = functools.reduce(jnp.maximum, group_scores)
            gi = functools.reduce(
                jnp.minimum, [jnp.where(group_scores[g] == best, jnp.int32(g), big) for g in range(N_GROUPS)])
            for g in range(N_GROUPS):
                hit = gi == g
                group_sel[g] = jnp.logical_or(group_sel[g], hit)
                group_scores[g] = jnp.where(hit, neg, group_scores[g])
        cand = jnp.concatenate(
            [jnp.where(group_sel[g], biased[g * GROUP_SIZE:(g + 1) * GROUP_SIZE, :], neg)
             for g in range(N_GROUPS)], axis=0)
        yield
        w_rows, hits = [], []
        picked = jnp.zeros((N_EXPERTS, tm), F32)
        for k in range(TOP_K):
            _, ei = first_argmax(cand, erow)
            hit = erow == ei
            eidx_ref[k:k + 1, :] = ei
            w_rows.append(jnp.sum(jnp.where(hit, scores, 0.0), axis=0, keepdims=True))
            hits.append(hit)
            picked = jnp.where(hit, 1.0, picked)
            cand = jnp.where(hit, neg, cand)
            yield
        w_sum = functools.reduce(jnp.add, w_rows)
        for k in range(TOP_K):
            ew_ref[k:k + 1, :] = w_rows[k] / w_sum * ROUTE_SCALE
        yield
        trow = lax.broadcasted_iota(I32, (tm, tm), 0)
        tcol = lax.broadcasted_iota(I32, (tm, tm), 1)
        before = (trow < tcol).astype(BF16)
        prefix = jnp.dot(picked.astype(BF16), before, preferred_element_type=F32) + cnt_ref[:, 0:1]
        for k in range(TOP_K):
            rank_ref[k:k + 1, :] = jnp.sum(jnp.where(hits[k], prefix, 0.0), axis=0, keepdims=True).astype(I32)
            yield
        counted = jnp.where(i > 0, jnp.sum(picked, axis=1, keepdims=True), 0.0)
        cnt_ref[...] = cnt_ref[...] + counted

    jobs = [route(scores_s[...]), project()]
    shares = [4, 1]
    while jobs:
        for job, share in list(zip(jobs, shares)):
            for _ in range(share):
                if next(job, "done") == "done":
                    k = jobs.index(job)
                    del jobs[k], shares[k]
                    break


def _post(acta, actb, sga, sgb, x2d, gate1, scale2, shift2, gate2, cnt0, g_ffn, w, ebias_col,
          *, tm, tiles_per_mod):
    t = x2d.shape[0]
    n_tiles = t // tm
    tile = lambda i: jnp.minimum(i, n_tiles - 1)
    routed_tile = lambda i: jnp.maximum(i - 1, 0)
    row = lambda c: pl.BlockSpec((tm, c), lambda i: (tile(i), 0))
    col = lambda r: pl.BlockSpec((r, tm), lambda i: (0, routed_tile(i)))
    mod = lambda m: pl.BlockSpec((None,) + m.shape[1:], lambda i: (tile(i) // tiles_per_mod, 0, 0))
    mods = [gate1, scale2, shift2, gate2]
    consts = [cnt0, g_ffn, w["pa"], w["pb"], w["o"], w["router_t"], ebias_col, w["s_gu"], w["s_down"]]
    return pl.pallas_call(
        functools.partial(_post_kernel, tm=tm),
        grid=(n_tiles + 1,),
        in_specs=[row(CONV_CH), row(DN_WIDTH), row(D_MODEL), row(D_MODEL), row(D_MODEL)]
                 + [mod(m) for m in mods] + [_const_spec(a) for a in consts],
        out_specs=[row(D_MODEL), row(D_MODEL // 2), col(TOP_K), col(TOP_K), col(TOP_K), _const_spec(cnt0)],
        scratch_shapes=[pltpu.VMEM((N_EXPERTS, tm), F32)],
        out_shape=[jax.ShapeDtypeStruct((t, D_MODEL), F32), jax.ShapeDtypeStruct((t, D_MODEL // 2), U32),
                   jax.ShapeDtypeStruct((TOP_K, t), I32), jax.ShapeDtypeStruct((TOP_K, t), F32),
                   jax.ShapeDtypeStruct((TOP_K, t), I32), jax.ShapeDtypeStruct(cnt0.shape, F32)],
        compiler_params=_cparams(),
        name="post_mixer",
    )(acta, actb, sga, sgb, x2d, *mods, *consts)


def _dest_kernel(start_ref, eidx_ref, rank_ref, o_ref, *, width):
    step = SUBLANES * LANES
    for c0 in range(0, width, step):
        c1 = min(c0 + step, width)
        e = eidx_ref[:, c0:c1]
        base = lax.fori_loop(0, N_EXPERTS, lambda x, acc: jnp.where(e == x, start_ref[x], acc),
                             jnp.zeros(e.shape, I32), unroll=8)
        o_ref[:, c0:c1] = base + rank_ref[:, c0:c1]


def _dest(cnt_start, e_idx, rank):
    k, t = e_idx.shape
    full = pl.BlockSpec((k, t), lambda i, st: (0, 0))
    return pl.pallas_call(
        functools.partial(_dest_kernel, width=t),
        grid_spec=pltpu.PrefetchScalarGridSpec(
            num_scalar_prefetch=1, grid=(1,), in_specs=[full, full], out_specs=full),
        out_shape=jax.ShapeDtypeStruct((k, t), I32),
        compiler_params=_cparams(),
        name="moe_dest",
    )(cnt_start, e_idx, rank)


def _sc_mesh():
    return plsc.VectorSubcoreMesh(core_axis_name="core", subcore_axis_name="subcore")


def _sc_pipeline(body, n_steps, in_specs, out_specs):
    return pltpu.emit_pipeline(body, grid=(n_steps,), in_specs=in_specs, out_specs=out_specs,
                               core_axis_name=("core", "subcore"), dimension_semantics=(pltpu.PARALLEL,))


def _sc_scatter_rows(x_groups, dest_flat, n_rows):
    tiles_all = sum(x.shape[0] for x in x_groups) // SC_WINDOW
    n_k = dest_flat.shape[1] // (tiles_all * SC_WINDOW)
    n_groups = len(x_groups)

    @functools.partial(pl.kernel, mesh=_sc_mesh(), scratch_types=[],
                       out_type=[jax.ShapeDtypeStruct((n_rows, QUARTER_WORDS), x_groups[0].dtype)] * 2)
    def scatter(*refs):
        x_hbms, i_hbm, o_hbms = refs[:n_groups], refs[n_groups], refs[n_groups + 1:]
        for half, o_hbm in enumerate(o_hbms):
            def body(x_vmem, i_vmem, o_hbm=o_hbm):
                pltpu.sync_copy(x_vmem, o_hbm.at[i_vmem.at[0]])

            tile0 = 0
            for x_hbm in x_hbms:
                tiles = x_hbm.shape[0] // SC_WINDOW
                _sc_pipeline(body, n_k * tiles,
                             [pl.BlockSpec((SC_WINDOW, QUARTER_WORDS),
                                           lambda i, tiles=tiles, half=half: (i % tiles, half)),
                              pl.BlockSpec((1, SC_WINDOW),
                                           lambda i, tiles=tiles, tile0=tile0:
                                           (0, (i // tiles) * tiles_all + tile0 + i % tiles))],
                             [])(x_hbm, i_hbm)
                tile0 += tiles

    return scatter(*x_groups, dest_flat)


def _sc_gather_rows(tables, idx_flat):
    n = idx_flat.shape[1]
    n_tables = len(tables)

    @functools.partial(pl.kernel, mesh=_sc_mesh(), scratch_types=[],
                       out_type=[jax.ShapeDtypeStruct((n, t.shape[1]), t.dtype) for t in tables])
    def gather(*refs):
        t_hbms, i_hbm, o_hbms = refs[:n_tables], refs[n_tables], refs[n_tables + 1:]
        for t_hbm, o_hbm in zip(t_hbms, o_hbms):
            def body(i_vmem, o_vmem, t_hbm=t_hbm):
                pltpu.sync_copy(t_hbm.at[i_vmem.at[0]], o_vmem)

            _sc_pipeline(body, n // SC_WINDOW,
                         [pl.BlockSpec((1, SC_WINDOW), lambda i: (0, i))],
                         [pl.BlockSpec((SC_WINDOW, t_hbm.shape[1]), lambda i: (i, 0))])(i_hbm, o_hbm)

    return gather(*tables, idx_flat)


def _experts_kernel(sblk_ref, sexp_ref, slo_ref, shi_ref, snext_ref, first_ref,
                    xa_ref, xb_ref, wg_hbm, wu_hbm, wd_hbm, ya_ref, yb_ref,
                    wg_buf, wu_buf, wd_buf, wgu_s, wd_s, state_ref, sems, *, bm):
    p = pl.program_id(0)
    lo = slo_ref[p]
    expert = sexp_ref[p]

    def weight_copies(e, slot):
        return [pltpu.make_async_copy(src.at[e], dst.at[slot], sems.at[slot])
                for src, dst in ((wg_hbm, wg_buf), (wu_hbm, wu_buf), (wd_hbm, wd_buf))]

    @pl.when(p == 0)
    def _():
        state_ref[0] = -1
        state_ref[1] = 0
        for c in weight_copies(first_ref[0], 0):
            c.start()

    @pl.when(shi_ref[p] > lo)
    def _():
        @pl.when(state_ref[0] != expert)
        def _():
            slot = state_ref[1]
            for c in weight_copies(expert, slot):
                c.wait()
            wgu_s[:, :D_EXPERT] = wg_buf[slot].astype(BF16)
            wgu_s[:, D_EXPERT:] = wu_buf[slot].astype(BF16)
            wd_s[...] = wd_buf[slot].astype(BF16)
            state_ref[0] = expert
            state_ref[1] = 1 - slot
            upcoming = snext_ref[p]

            @pl.when(upcoming >= 0)
            def _():
                for c in weight_copies(upcoming, 1 - slot):
                    c.start()

        q = QUARTER_WORDS
        sub = bm // EXPERT_SUBBLOCKS
        first_sub = lo // sub
        last_sub = (shi_ref[p] - 1) // sub

        def run(r0, r1):
            quarters = {}
            quarters[0], quarters[2] = _unpack_bf16_pairs(xa_ref[r0:r1, :])
            quarters[1], quarters[3] = _unpack_bf16_pairs(xb_ref[r0:r1, :])
            gu = functools.reduce(jnp.add, [
                jnp.dot(quarters[c], wgu_s[c * q:(c + 1) * q, :], preferred_element_type=F32) for c in range(4)])
            hb = (_silu(gu[:, :D_EXPERT]) * gu[:, D_EXPERT:]).astype(BF16)
            y = _pack_bf16_pairs(jnp.dot(hb, wd_s[...], preferred_element_type=F32))

            @pl.when(lo <= r0)
            def _():
                ya_ref[r0:r1, :] = y[:, :q]
                yb_ref[r0:r1, :] = y[:, q:]

            @pl.when(lo > r0)
            def _():
                keep = lax.broadcasted_iota(I32, (r1 - r0, 1), 0) + r0 >= lo
                ya_ref[r0:r1, :] = jnp.where(keep, y[:, :q], ya_ref[r0:r1, :])
                yb_ref[r0:r1, :] = jnp.where(keep, y[:, q:], yb_ref[r0:r1, :])

        for a in range(EXPERT_SUBBLOCKS):
            for b in range(a, EXPERT_SUBBLOCKS):
                pl.when(jnp.logical_and(first_sub == a, last_sub == b))(
                    functools.partial(run, a * sub, (b + 1) * sub))


def _experts(xs_a, xs_b, w_gate, w_up, w_down, seg_blk, seg_exp, seg_lo, seg_hi, seg_next, first_exp, *, bm):
    n_rows = xs_a.shape[0]
    n_seg = seg_blk.shape[0]
    ring = 2
    hbm = pl.BlockSpec(memory_space=pl.ANY)
    rows = pl.BlockSpec((bm, QUARTER_WORDS), lambda p, sb, *_: (sb[p], 0))
    grid_spec = pltpu.PrefetchScalarGridSpec(
        num_scalar_prefetch=6,
        grid=(n_seg,),
        in_specs=[rows, rows, hbm, hbm, hbm],
        out_specs=[rows, rows],
        scratch_shapes=[pltpu.VMEM((ring, D_MODEL, D_EXPERT), F32), pltpu.VMEM((ring, D_MODEL, D_EXPERT), F32),
                        pltpu.VMEM((ring, D_EXPERT, D_MODEL), F32),
                        pltpu.VMEM((D_MODEL, 2 * D_EXPERT), BF16), pltpu.VMEM((D_EXPERT, D_MODEL), BF16),
                        pltpu.SMEM((2,), I32), pltpu.SemaphoreType.DMA((ring,))],
    )
    return pl.pallas_call(
        functools.partial(_experts_kernel, bm=bm),
        grid_spec=grid_spec,
        out_shape=[jax.ShapeDtypeStruct((n_rows, QUARTER_WORDS), U32)] * 2,
        compiler_params=_cparams(),
        name="moe_experts",
    )(seg_blk, seg_exp, seg_lo, seg_hi, seg_next, first_exp, xs_a, xs_b, w_gate, w_up, w_down)


def _final_kernel(*refs, tm, n_prompt_tiles, has_sample, has_prev):
    refs = list(refs)
    ew_ref, pa_ref, pb_ref, xsp_ref = refs[:4]
    del refs[:4]
    xss_ref = refs.pop(0) if has_sample else None
    g2p_ref = refs.pop(0)
    g2s_ref = refs.pop(0) if has_sample else None
    gfin_ref = refs.pop(0)
    if has_prev:
        refs.pop(0)
    op_ref = refs.pop(0)
    os_ref = refs.pop(0) if has_sample else None
    ew = ew_ref[...]
    quarters = [jnp.zeros((tm, QUARTER_WORDS), F32)] * 4
    for k in range(TOP_K):
        w_k = ew[:, k:k + 1]
        a_low, a_high = _unpack_bf16_pairs(pa_ref[k])
        b_low, b_high = _unpack_bf16_pairs(pb_ref[k])
        for c, part in enumerate((a_low, b_low, a_high, b_high)):
            quarters[c] = quarters[c] + w_k * part.astype(F32)
    routed = jnp.concatenate(quarters, axis=1)

    def finish(xs_ref, gate2_ref, o_ref):
        x2 = xs_ref[...] + gate2_ref[...] * routed
        o_ref[...] = x2 * lax.rsqrt(jnp.mean(x2 * x2, axis=-1, keepdims=True) + EPS) * gfin_ref[...]

    if not has_sample:
        finish(xsp_ref, g2p_ref, op_ref)
    else:
        is_prompt = pl.program_id(0) < n_prompt_tiles
        pl.when(is_prompt)(lambda: finish(xsp_ref, g2p_ref, op_ref))
        pl.when(jnp.logical_not(is_prompt))(lambda: finish(xss_ref, g2s_ref, os_ref))


def _final(ew_part, planes_a, planes_b, xs_p, gate2_p, g_final, *, tm, tiles_per_batch, tile_lo, n_prompt_tiles,
           sample=None, y_prev=None):
    prompt_tile = lambda i: tile_lo + jnp.minimum(i, n_prompt_tiles - 1)
    sample_tile = lambda i: jnp.maximum(i - n_prompt_tiles, 0)
    prompt_rows = pl.BlockSpec((tm, D_MODEL), lambda i: (prompt_tile(i), 0))
    sample_rows = pl.BlockSpec((tm, D_MODEL), lambda i: (sample_tile(i), 0))
    planes = pl.BlockSpec((TOP_K, tm, QUARTER_WORDS), lambda i: (0, i, 0))
    has_sample = sample is not None
    has_prev = y_prev is not None
    n_tiles = n_prompt_tiles + (sample[0].shape[0] // tm if has_sample else 0)
    args = [ew_part, planes_a, planes_b, xs_p]
    in_specs = [pl.BlockSpec((tm, TOP_K), lambda i: (i, 0)), planes, planes, prompt_rows]
    if has_sample:
        args.append(sample[0])
        in_specs.append(sample_rows)
    args.append(gate2_p)
    in_specs.append(pl.BlockSpec((None, 1, D_MODEL), lambda i: (prompt_tile(i) // tiles_per_batch, 0, 0)))
    if has_sample:
        args.append(sample[1])
        in_specs.append(pl.BlockSpec((None, tm, D_MODEL), lambda i: (sample_tile(i), 0, 0)))
    args.append(g_final)
    in_specs.append(_const_spec(g_final))
    if has_prev:
        args.append(y_prev)
        in_specs.append(pl.BlockSpec(memory_space=pl.ANY))
    out_specs = [prompt_rows] + ([sample_rows] if has_sample else [])
    out_shape = [jax.ShapeDtypeStruct(xs_p.shape, F32)] + (
        [jax.ShapeDtypeStruct(sample[0].shape, F32)] if has_sample else [])
    outs = pl.pallas_call(
        functools.partial(_final_kernel, tm=tm, n_prompt_tiles=n_prompt_tiles, has_sample=has_sample,
                          has_prev=has_prev),
        grid=(n_tiles,),
        in_specs=in_specs,
        out_specs=out_specs,
        out_shape=out_shape,
        input_output_aliases={len(args) - 1: 0} if has_prev else {},
        compiler_params=_cparams(),
        name="moe_combine_final",
    )(*args)
    return outs if has_sample else outs[0]


def _pick_tile(n, preferred):
    t = min(n, preferred)
    assert n % t == 0, (n, t)
    return t


def kernel(x_prompt, x_sample, c_prompt, c_sample, state_conv, state_short_conv, state_delta, w_ada, b_ada, g_mix, w_in, conv_dw, conv_dw_b, conv_ln_g, conv_ln_b, w_pa, short_conv_w, a_log, dt_bias, dn_norm_g, w_pb, w_o, g_ffn, w_router, e_bias, w_gate, w_up, w_down, ws_gate, ws_up, ws_down, g_final):
    nb, seq, d = x_prompt.shape
    ns = x_sample.shape[0]
    assert d == D_MODEL and w_ada.shape[0] == 1 and x_sample.shape[1] == 1
    assert seq % CHUNK == 0 and seq >= CONV_HALO and ns % SUBLANES == 0
    tp = nb * seq
    row1 = lambda a: a.reshape(1, -1)

    wi = w_in[0].astype(BF16)
    o_q = 2 * CONV_CH
    o_z = o_q + 3 * DN_WIDTH
    o_a = o_z + DN_WIDTH
    o_ga = o_a + 2 * N_HEADS
    w_proj = {
        "glu": wi[:, :o_q], "qkv": wi[:, o_q:o_z], "z": wi[:, o_z:o_a],
        "ab": jnp.pad(wi[:, o_a:o_ga], ((0, 0), (0, LANES - 2 * N_HEADS))),
        "ga": wi[:, o_ga:o_ga + D_MODEL], "gb": wi[:, o_ga + D_MODEL:],
    }
    nea = jnp.pad(-jnp.exp(a_log[0].astype(F32)), (0, LANES - N_HEADS)).reshape(1, LANES)
    dtb = jnp.pad(dt_bias[0].astype(F32), (0, LANES - N_HEADS)).reshape(1, LANES)
    w_post = {
        "pa": w_pa[0].astype(BF16), "pb": w_pb[0].astype(BF16), "o": w_o[0].astype(BF16),
        "router_t": w_router[0].T.astype(BF16),
        "s_gu": jnp.concatenate([ws_gate[0], ws_up[0]], axis=1).astype(BF16),
        "s_down": ws_down[0].astype(BF16),
    }
    ebias_col = e_bias[0].astype(F32).reshape(N_EXPERTS, 1)
    g_mix1, g_ffn1, g_fin1 = row1(g_mix[0]), row1(g_ffn[0]), row1(g_final)
    cdw, cdb, clg, clb = conv_dw[0], row1(conv_dw_b[0]), row1(conv_ln_g[0]), row1(conv_ln_b[0])
    scw, dng = short_conv_w[0], row1(dn_norm_g[0])

    ada = _ada(jnp.concatenate([c_prompt, c_sample], axis=0), w_ada[0].astype(BF16), b_ada[0])
    ada = ada.reshape(nb + ns, 6, D_MODEL)
    mod_p = [ada[:nb, m].reshape(nb, 1, D_MODEL) for m in range(6)]
    mod_s = [ada[nb:, m].reshape(1, ns, D_MODEL) for m in range(6)]
    shift1, scale1, gate1, shift2, scale2, gate2 = range(6)

    tm_p = _pick_tile(seq, 256)
    tpm_p = seq // tm_p
    xp = x_prompt.reshape(tp, D_MODEL)
    u_p, qkv_p, zs_p, ab_p, sga_p, sgb_p = _in_proj(
        xp, mod_p[scale1], mod_p[shift1], g_mix1, w_proj, nea, dtb, tm=tm_p, tiles_per_mod=tpm_p)
    tl = _pick_tile(seq, 256)
    acta_p = _conv_branch(u_p, jnp.zeros((nb, CONV_HALO, CONV_CH), F32), cdw, cdb, clg, clb,
                          tl=tl, tiles_per_batch=seq // tl)
    tc = _pick_tile(seq, 4 * CHUNK)
    actb_p, s_new_p = _delta_branch(
        qkv_p, jnp.zeros((nb, SC_HALO, 3 * DN_WIDTH), F32),
        jnp.zeros((nb, N_HEADS, HEAD_DIM, HEAD_DIM), F32), scw, ab_p, zs_p, dng,
        tc=tc, tiles_per_batch=seq // tc)

    xs_ = x_sample.reshape(ns, D_MODEL)
    u_s, qkv_s, zs_s, ab_s, sga_s, sgb_s = _in_proj(
        xs_, mod_s[scale1], mod_s[shift1], g_mix1, w_proj, nea, dtb, tm=ns, tiles_per_mod=1)
    bt = SUBLANES
    acta_s, nch_s, qkvp_s, nsh_s = _step_front(
        u_s, qkv_s, state_conv[0].reshape(ns, -1), state_short_conv[0].reshape(ns, -1),
        cdw, cdb, clg, clb, scw, bt=bt)
    actb_s, s_new_s = _step_delta(qkvp_s, ab_s, zs_s, state_delta[0].astype(F32), dng, bt=bt)

    cnt0 = jnp.zeros((N_EXPERTS, LANES), F32)
    xsh_p, h2_p, eidx_p, ew_p, rank_p, cnt_p = _post(
        acta_p, actb_p, sga_p, sgb_p, xp, mod_p[gate1], mod_p[scale2], mod_p[shift2], mod_p[gate2],
        cnt0, g_ffn1, w_post, ebias_col, tm=tm_p, tiles_per_mod=tpm_p)
    xsh_s, h2_s, eidx_s, ew_s, rank_s, cnt_all = _post(
        acta_s, actb_s, sga_s, sgb_s, xs_, mod_s[gate1], mod_s[scale2], mod_s[shift2], mod_s[gate2],
        cnt_p, g_ffn1, w_post, ebias_col, tm=ns, tiles_per_mod=1)

    bm = EXPERT_SUBBLOCKS * EXPERT_SUB_ROWS
    tmd = LANES
    t_all = tp + ns
    n_rows = t_all * TOP_K
    assert seq % tmd == 0 and ns % tmd == 0 and n_rows % bm == 0 and t_all % SC_WINDOW == 0
    n_blocks = n_rows // bm
    counts = cnt_all[:, 0].astype(I32)
    cnt_end = jnp.cumsum(counts)
    cnt_start = cnt_end - counts
    e_idx = jnp.concatenate([eidx_p, eidx_s], axis=1)
    rank = jnp.concatenate([rank_p, rank_s], axis=1)
    dest = _dest(cnt_start, e_idx, rank)
    dest_flat = dest.reshape(1, n_rows)
    seg_start = jnp.sort(jnp.concatenate([jnp.arange(n_blocks, dtype=I32) * bm, cnt_start]))
    seg_end = jnp.concatenate([seg_start[1:], jnp.full((1,), n_rows, I32)])
    seg_blk = jnp.minimum(seg_start // bm, n_blocks - 1)
    seg_lo = seg_start - seg_blk * bm
    seg_hi = seg_end - seg_blk * bm
    seg_exp = jnp.sum((cnt_end[None, :] <= seg_start[:, None]).astype(I32), axis=1)
    seg_exp = jnp.minimum(seg_exp, N_EXPERTS - 1)
    seg_exp = lax.cummax(jnp.where(seg_hi > seg_lo, seg_exp, 0))
    none = jnp.int32(N_EXPERTS)
    in_use_from = lax.cummin(jnp.where(counts > 0, jnp.arange(N_EXPERTS, dtype=I32), none), reverse=True)
    next_in_use = jnp.concatenate([in_use_from[1:], jnp.full((1,), none, I32)])
    next_in_use = jnp.where(next_in_use == none, -1, next_in_use)
    seg_next = next_in_use[seg_exp]
    first_exp = jnp.minimum(in_use_from[:1], N_EXPERTS - 1)

    xs_a, xs_b = _sc_scatter_rows((h2_p, h2_s), dest_flat, n_rows)
    ys_a, ys_b = _experts(xs_a, xs_b, w_gate[0], w_up[0], w_down[0], seg_blk, seg_exp, seg_lo, seg_hi, seg_next,
                          first_exp, bm=bm)

    ew_tok = jnp.concatenate([ew_p, ew_s], axis=1).T
    n_prompt_tiles = tp // tmd
    tiles_1 = n_prompt_tiles // 2
    t_1 = tiles_1 * tmd

    def gathered(lo, hi):
        idx = dest[:, lo:hi].reshape(1, -1)
        return [p.reshape(TOP_K, hi - lo, QUARTER_WORDS) for p in _sc_gather_rows((ys_a, ys_b), idx)]

    final = functools.partial(_final, xs_p=xsh_p, gate2_p=mod_p[gate2], g_final=g_fin1)
    tm_1 = 2 * tmd if tiles_1 % 2 == 0 and seq % (2 * tmd) == 0 else tmd
    y_p = final(ew_tok[:t_1], *gathered(0, t_1), tm=tm_1, tiles_per_batch=seq // tm_1, tile_lo=0,
                n_prompt_tiles=t_1 // tm_1)
    y_p, y_s = final(ew_tok[t_1:], *gathered(t_1, t_all), tm=tmd, tiles_per_batch=seq // tmd, tile_lo=tiles_1,
                     n_prompt_tiles=n_prompt_tiles - tiles_1,
                     sample=(xsh_s, mod_s[gate2].reshape(ns // tmd, tmd, D_MODEL)), y_prev=y_p)

    hist_rows = CONV_WIDTH - 1
    sc_rows = SHORT_CONV - 1
    y_prompt = y_p.reshape(nb, seq, D_MODEL)
    y_sample = y_s.reshape(ns, 1, D_MODEL)
    new_conv_prompt = u_p.reshape(nb, seq, CONV_CH)[:, seq - hist_rows:][None]
    new_sc_prompt = qkv_p.reshape(nb, seq, 3 * DN_WIDTH)[:, seq - sc_rows:][None]
    new_conv_sample = nch_s.reshape(ns, hist_rows, CONV_CH)[None]
    new_sc_sample = nsh_s.reshape(ns, sc_rows, 3 * DN_WIDTH)[None]
    return (y_prompt, y_sample, new_conv_prompt, new_sc_prompt, s_new_p[None],
            new_conv_sample, new_sc_sample, s_new_s[None])
```
